```python
import math
import jax, jax.numpy as jnp
from jax import lax
import numpy as np

D_MODEL = 1024
BATCH = 16
SEQ = 2048
DEPTH = 1

N_MEM = 256
HEAD_DIM = 64
ATT_HEADS = 8
ATT_KV_HEADS = 2
ATT_WIDTH = ATT_HEADS * HEAD_DIM
KV_WIDTH = ATT_KV_HEADS * HEAD_DIM
IDX_HEADS = 8
IDX_DIM = 32
TOPK_MAX = 256
Q_BLOCK = 128
SSM_WIDTH = D_MODEL // 4
SSM_GROUP = 16
SSM_GROUPS = SSM_WIDTH // SSM_GROUP
SSM_STATE = 64
DT_MIN = 0.001
DT_MAX = 0.1
MEM_HEADS = 4
MEM_WIDTH = MEM_HEADS * HEAD_DIM
MIX_WIDTH = ATT_WIDTH + SSM_WIDTH + MEM_WIDTH
ROPE_THETA = 500000.0
ROPE_FRAC = 4
LN_EPS = 1e-5
DN_ALPHA = (2.0 * DEPTH) ** 0.25
DN_BETA = (8.0 * DEPTH) ** -0.25

SPLITS = [
    ATT_WIDTH,
    KV_WIDTH,
    KV_WIDTH,
    IDX_HEADS * IDX_DIM,
    IDX_DIM,
    IDX_HEADS,
    ATT_WIDTH,
    SSM_WIDTH,
    SSM_WIDTH,
    MEM_WIDTH,
    MEM_WIDTH,
]
IN_WIDTH = int(sum(SPLITS))
SPLIT_POINTS = [int(s) for s in np.cumsum(SPLITS)[:-1]]

kernel_name = "hymba_dsa_s5_memxattn_deepnorm"


def rope_partial(x, pos):
    d = x.shape[-1]
    r = d // ROPE_FRAC
    half = r // 2
    inv = ROPE_THETA ** (-jnp.arange(0, half, dtype=jnp.float32) * 2.0 / r)
    ang = pos.astype(jnp.float32)[:, None] * inv[None, :]
    cos = jnp.cos(ang)[:, None, :]
    sin = jnp.sin(ang)[:, None, :]
    xf = x.astype(jnp.float32)
    x1, x2, xp = xf[..., :half], xf[..., half:r], xf[..., r:]
    out = jnp.concatenate([x1 * cos - x2 * sin, x2 * cos + x1 * sin, xp], axis=-1)
    return out.astype(x.dtype)


def dsa_attention(q, k, v, q_idx, k_idx, w_idx):
    B, L = q.shape[0], q.shape[1]
    topk = min(TOPK_MAX, L // 4)
    nb = L // Q_BLOCK
    rep = ATT_HEADS // ATT_KV_HEADS
    key_pos = jnp.arange(L)
    k_idx_f = k_idx.astype(jnp.float32)
    gather = jax.vmap(lambda a, i: a[i])

    def to_blocks(a):
        return a.reshape((B, nb, Q_BLOCK) + a.shape[2:]).swapaxes(0, 1)

    def block(args):
        qb, qib, wb, start = args
        qpos = start + jnp.arange(Q_BLOCK)
        s = jnp.einsum('bqhd,bsd->bqhs', qib.astype(jnp.float32), k_idx_f)
        score = jnp.einsum('bqh,bqhs->bqs', wb.astype(jnp.float32), jax.nn.relu(s))
        causal = key_pos[None, :] <= qpos[:, None]
        score = jnp.where(causal[None], score, -jnp.inf)
        _, idx = lax.top_k(score, topk)
        valid = idx <= qpos[None, :, None]
        kg = gather(k, idx)
        vg = gather(v, idx)
        qg = qb.reshape(B, Q_BLOCK, ATT_KV_HEADS, rep, HEAD_DIM)
        logits = jnp.einsum('bqgrd,bqkgd->bqgrk', qg, kg).astype(jnp.float32) * (HEAD_DIM ** -0.5)
        logits = jnp.where(valid[:, :, None, None, :], logits, -jnp.inf)
        p = jax.nn.softmax(logits, axis=-1).astype(v.dtype)
        o = jnp.einsum('bqgrk,bqkgd->bqgrd', p, vg)
        return o.reshape(B, Q_BLOCK, ATT_WIDTH)

    starts = jnp.arange(nb) * Q_BLOCK
    out = lax.map(block, (to_blocks(q), to_blocks(q_idx), to_blocks(w_idx), starts))
    return out.swapaxes(0, 1).reshape(B, L, ATT_WIDTH)


def s5_scan(u, lam_re, lam_im, log_dt, b_re, b_im, c_re, c_im, d_skip):
    f32 = jnp.float32
    lam = lax.complex(lam_re.astype(f32), lam_im.astype(f32))
    dt = jnp.exp(log_dt.astype(f32))[:, None]
    lam_bar = jnp.exp(lam * dt)
    b = lax.complex(b_re.astype(f32), b_im.astype(f32))
    b_bar = ((lam_bar - 1.0) / lam)[:, :, None] * b
    uf = u.astype(f32)
    bu = jnp.einsum('gpn,blgn->blgp', b_bar, uf.astype(jnp.complex64))
    a = jnp.broadcast_to(lam_bar, bu.shape)

    def combine(e1, e2):
        a1, b1 = e1
        a2, b2 = e2
        return a1 * a2, a2 * b1 + b2

    _, states = lax.associative_scan(combine, (a, bu), axis=1)
    c = lax.complex(c_re.astype(f32), c_im.astype(f32))
    y = jnp.einsum('gnp,blgp->blgn', c, states).real + d_skip.astype(f32) * uf
    return y.astype(u.dtype)


def layer_norm(h, g, b):
    hf = h.astype(jnp.float32)
    mu = jnp.mean(hf, axis=-1, keepdims=True)
    var = jnp.mean(jnp.square(hf - mu), axis=-1, keepdims=True)
    return ((hf - mu) * lax.rsqrt(var + LN_EPS) * g.astype(jnp.float32) + b.astype(jnp.float32))


def setup_inputs(seed: int = 0) -> dict:
    key = jax.random.key(seed)
    ks = jax.random.split(key, 20)
    f32 = jnp.float32
    G, P, N = SSM_GROUPS, SSM_STATE, SSM_GROUP
    x = jax.random.normal(ks[0], (BATCH, SEQ, D_MODEL), f32)
    mem = jax.random.normal(ks[1], (BATCH, N_MEM, D_MODEL), f32)
    w_in = jax.random.normal(ks[2], (D_MODEL, IN_WIDTH), f32) * D_MODEL ** -0.5
    w_mem_kv = jax.random.normal(ks[3], (D_MODEL, 2 * MEM_WIDTH), f32) * D_MODEL ** -0.5
    n = jnp.arange(P, dtype=f32)
    lam_re = -0.5 + 0.01 * jax.random.normal(ks[4], (G, P), f32)
    lam_im = math.pi * n[None, :] + 0.01 * jax.random.normal(ks[5], (G, P), f32)
    log_dt = jax.random.uniform(ks[6], (G,), f32, math.log(DT_MIN), math.log(DT_MAX))
    b_re = jax.random.normal(ks[7], (G, P, N), f32) * (2.0 * N) ** -0.5
    b_im = jax.random.normal(ks[8], (G, P, N), f32) * (2.0 * N) ** -0.5
    c_re = jax.random.normal(ks[9], (G, N, P), f32) * (2.0 * P) ** -0.5
    c_im = jax.random.normal(ks[10], (G, N, P), f32) * (2.0 * P) ** -0.5
    d_skip = jax.random.normal(ks[11], (G, N), f32)
    w_glu = jax.random.normal(ks[12], (SSM_WIDTH, SSM_WIDTH), f32) * SSM_WIDTH ** -0.5
    b_glu = 0.01 * jax.random.normal(ks[13], (SSM_WIDTH,), f32)
    w_out = jax.random.normal(ks[14], (MIX_WIDTH, D_MODEL), f32) * (MIX_WIDTH ** -0.5) * DN_BETA
    ln_g = 1.0 + 0.02 * jax.random.normal(ks[15], (D_MODEL,), f32)
    ln_b = 0.02 * jax.random.normal(ks[16], (D_MODEL,), f32)
    return {"x": x, "mem": mem, "w_in": w_in, "w_mem_kv": w_mem_kv,
            "lam_re": lam_re, "lam_im": lam_im, "log_dt": log_dt,
            "b_re": b_re, "b_im": b_im, "c_re": c_re, "c_im": c_im, "d_skip": d_skip,
            "w_glu": w_glu, "b_glu": b_glu, "w_out": w_out, "ln_g": ln_g, "ln_b": ln_b}


def reference(x, mem, w_in, w_mem_kv, lam_re, lam_im, log_dt, b_re, b_im, c_re, c_im,
              d_skip, w_glu, b_glu, w_out, ln_g, ln_b):
    B, L, _ = x.shape
    pos = jnp.arange(L)
    mkv = jnp.einsum('bmd,de->bme', mem, w_mem_kv)
    m_k, m_v = jnp.split(mkv, 2, axis=-1)
    m_k = m_k.reshape(B, N_MEM, MEM_HEADS, HEAD_DIM)
    m_v = m_v.reshape(B, N_MEM, MEM_HEADS, HEAD_DIM)
    h = x
    for _ in range(DEPTH):
        z = jnp.einsum('bld,de->ble', h, w_in)
        (q, k, v, q_idx, k_idx, w_idx, g_att, u, g_ssm, q_mem, g_mem) = jnp.split(z, SPLIT_POINTS, axis=-1)

        q = rope_partial(q.reshape(B, L, ATT_HEADS, HEAD_DIM), pos)
        k = rope_partial(k.reshape(B, L, ATT_KV_HEADS, HEAD_DIM), pos)
        v = v.reshape(B, L, ATT_KV_HEADS, HEAD_DIM)
        q_idx = rope_partial(q_idx.reshape(B, L, IDX_HEADS, IDX_DIM), pos)
        k_idx = rope_partial(k_idx[:, :, None, :], pos)[:, :, 0, :]
        w_idx = w_idx * (IDX_HEADS ** -0.5 * IDX_DIM ** -0.5)
        o_att = dsa_attention(q, k, v, q_idx, k_idx, w_idx)

        y = s5_scan(u.reshape(B, L, SSM_GROUPS, SSM_GROUP), lam_re, lam_im, log_dt,
                    b_re, b_im, c_re, c_im, d_skip).reshape(B, L, SSM_WIDTH)
        y = jax.nn.gelu(y)
        o_ssm = y * jax.nn.sigmoid(jnp.einsum('blc,ce->ble', y, w_glu) + b_glu)

        qm = q_mem.reshape(B, L, MEM_HEADS, HEAD_DIM)
        sm = jnp.einsum('blhd,bmhd->bhlm', qm, m_k).astype(jnp.float32) * (HEAD_DIM ** -0.5)
        pm = jax.nn.softmax(sm, axis=-1).astype(m_v.dtype)
        o_mem = jnp.einsum('bhlm,bmhd->blhd', pm, m_v).reshape(B, L, MEM_WIDTH)

        cat = jnp.concatenate([o_att * jax.nn.silu(g_att),
                               o_ssm * jax.nn.silu(g_ssm),
                               o_mem * jax.nn.silu(g_mem)], axis=-1)
        sub = jnp.einsum('ble,ed->bld', cat, w_out)
        h = layer_norm(DN_ALPHA * h.astype(jnp.float32) + sub.astype(jnp.float32), ln_g, ln_b).astype(x.dtype)
    return h
```

```python
import functools

import jax
import jax.numpy as jnp
import numpy as np
from jax import lax
from jax.experimental import pallas as pl
from jax.experimental.pallas import tpu as pltpu

F32 = jnp.float32
BF16 = jnp.bfloat16
I32 = jnp.int32

D_MODEL = 1024
N_MEM = 256
HEAD_DIM = 64
ATT_HEADS = 8
ATT_KV_HEADS = 2
ATT_REP = ATT_HEADS // ATT_KV_HEADS
ATT_WIDTH = ATT_HEADS * HEAD_DIM
KV_WIDTH = ATT_KV_HEADS * HEAD_DIM
IDX_HEADS = 8
IDX_DIM = 32
IDX_WIDTH = IDX_HEADS * IDX_DIM
TOPK_MAX = 256
SSM_WIDTH = D_MODEL // 4
SSM_GROUP = 16
SSM_GROUPS = SSM_WIDTH // SSM_GROUP
SSM_STATE = 64
SSM_LANES = SSM_GROUPS * SSM_STATE
MEM_HEADS = 4
MEM_WIDTH = MEM_HEADS * HEAD_DIM
ROPE_THETA = 500000.0
ROPE_FRAC = 4
LN_EPS = 1e-5
DEPTH = 1
DN_ALPHA = (2.0 * DEPTH) ** 0.25
ATT_SCALE = HEAD_DIM ** -0.5
IDX_SCALE = IDX_HEADS ** -0.5 * IDX_DIM ** -0.5

LANES = 128
SUBLANES = 8
VMEM_LIMIT_BYTES = 48 * 1024 * 1024
TOKEN_TILE = 512
Q_TILE = 128
KEY_CHUNK = 256
TIME_CHUNK = 64
SCAN_LANES = 256
NEG_BIG = -1e30

INT_MIN = -2147483648
KEY_POS_INF = 0x7F800000
KEY_NEG_INF = INT_MIN + 0x7FFFFF

_G_Q, _G_K, _G_V, _G_QI, _G_KI, _G_W, _G_GA, _G_U, _G_GS, _G_QM, _G_GM = range(11)
_GROUP_WIDTHS = [ATT_WIDTH, KV_WIDTH, KV_WIDTH, IDX_WIDTH, LANES, LANES, ATT_WIDTH,
                 SSM_WIDTH, SSM_WIDTH, MEM_WIDTH, MEM_WIDTH]
_GROUP_OFFS = [int(v) for v in np.cumsum([0] + _GROUP_WIDTHS)]
IN_COLS = _GROUP_OFFS[-1]


def _att_perm():
    cols = []
    for j in range(ATT_REP):
        for g in range(ATT_KV_HEADS):
            h = g * ATT_REP + j
            cols.extend(range(h * HEAD_DIM, (h + 1) * HEAD_DIM))
    return np.asarray(cols, dtype=np.int32)


def _rope_tables(seq, period, dtype=F32):
    r = period // ROPE_FRAC
    half = r // 2
    inv = ROPE_THETA ** (-jnp.arange(0, half, dtype=F32) * 2.0 / r)
    ang = jnp.arange(seq).astype(F32)[:, None] * inv[None, :]
    cos, sin = jnp.cos(ang), jnp.sin(ang)
    ones = jnp.ones((seq, period - r), F32)
    zeros = jnp.zeros((seq, period - r), F32)
    c = jnp.concatenate([cos, cos, ones], axis=1)
    s = jnp.concatenate([-sin, sin, zeros], axis=1)
    reps = LANES // period
    return jnp.tile(c, (1, reps)).astype(dtype), jnp.tile(s, (1, reps)).astype(dtype)


def _rope(z, cos_t, sin_t, period):
    width = z.shape[1]
    half = period // ROPE_FRAC // 2
    reps = width // LANES
    if reps > 1:
        cos_t = jnp.concatenate([cos_t] * reps, axis=1)
        sin_t = jnp.concatenate([sin_t] * reps, axis=1)
    lane = lax.broadcasted_iota(I32, z.shape, 1)
    first = (lane & (period - 1)) < half
    up = pltpu.roll(z, half, 1)
    down = pltpu.roll(z, width - half, 1)
    return z * cos_t + jnp.where(first, down, up) * sin_t


def _memkv_kernel(mem_ref, w_ref, mk_ref, mv_ref):
    z = jnp.dot(mem_ref[...].astype(BF16), w_ref[...], preferred_element_type=F32)
    mk_ref[...] = z[:, :MEM_WIDTH].astype(BF16)
    mv_ref[...] = z[:, MEM_WIDTH:].astype(BF16)


def _memkv(mem2d, w_bf16):
    rows = mem2d.shape[0]
    tile = TOKEN_TILE
    return pl.pallas_call(
        _memkv_kernel,
        grid=(rows // tile,),
        in_specs=[pl.BlockSpec((tile, D_MODEL), lambda i: (i, 0)),
                  pl.BlockSpec((D_MODEL, 2 * MEM_WIDTH), lambda i: (0, 0))],
        out_specs=[pl.BlockSpec((tile, MEM_WIDTH), lambda i: (i, 0)),
                   pl.BlockSpec((tile, MEM_WIDTH), lambda i: (i, 0))],
        out_shape=[jax.ShapeDtypeStruct((rows, MEM_WIDTH), BF16),
                   jax.ShapeDtypeStruct((rows, MEM_WIDTH), BF16)],
        compiler_params=pltpu.CompilerParams(dimension_semantics=("arbitrary",),
                                             vmem_limit_bytes=VMEM_LIMIT_BYTES),
        name="memkv",
    )(mem2d, w_bf16)


def _inproj_kernel(x_ref, w_ref, ca_ref, sa_ref, ci_ref, si_ref,
                   q_ref, k_ref, vt_ref, qi_ref, ki_ref, wi_ref, ga_ref, u_ref, gs_ref, qm_ref, gm_ref):
    tile = x_ref.shape[1]
    pos0 = pl.multiple_of(pl.program_id(1) * tile, tile)
    xb = x_ref[0].astype(BF16)

    def proj(group):
        lo, hi = _GROUP_OFFS[group], _GROUP_OFFS[group + 1]
        return jnp.dot(xb, w_ref[:, lo:hi], preferred_element_type=F32)

    ca = ca_ref[pl.ds(pos0, tile), :]
    sa = sa_ref[pl.ds(pos0, tile), :]
    ci = ci_ref[pl.ds(pos0, tile), :]
    si = si_ref[pl.ds(pos0, tile), :]

    q_ref[0] = (_rope(proj(_G_Q), ca, sa, HEAD_DIM) * ATT_SCALE).astype(BF16)
    k_ref[0] = _rope(proj(_G_K), ca, sa, HEAD_DIM).astype(BF16)
    v = proj(_G_V)
    for c in range(tile // KEY_CHUNK):
        vt_ref[0, c] = v[c * KEY_CHUNK:(c + 1) * KEY_CHUNK, :].T.astype(BF16)
    qi_ref[0] = _rope(proj(_G_QI), ci, si, IDX_DIM).astype(BF16)
    ki_ref[0] = _rope(proj(_G_KI), ci, si, IDX_DIM).astype(BF16)
    wi_ref[0] = proj(_G_W) * IDX_SCALE
    ga_ref[0] = proj(_G_GA)
    u_ref[0] = proj(_G_U)
    gs_ref[0] = proj(_G_GS)
    qm_ref[0] = (proj(_G_QM) * ATT_SCALE).astype(BF16)
    gm_ref[0] = proj(_G_GM)


def _inproj(x, w_bf16, tabs):
    batch, seq, _ = x.shape
    tile = TOKEN_TILE
    widths = _GROUP_WIDTHS
    dtypes = [BF16, BF16, BF16, BF16, BF16, F32, F32, F32, F32, BF16, F32]
    tab_spec = pl.BlockSpec((seq, LANES), lambda b, j: (0, 0))
    out_specs = [pl.BlockSpec((1, tile, w), lambda b, j: (b, j, 0)) for w in widths]
    out_shape = [jax.ShapeDtypeStruct((batch, seq, w), d) for w, d in zip(widths, dtypes)]
    out_specs[_G_V] = pl.BlockSpec((1, tile // KEY_CHUNK, KV_WIDTH, KEY_CHUNK), lambda b, j: (b, j, 0, 0))
    out_shape[_G_V] = jax.ShapeDtypeStruct((batch, seq // KEY_CHUNK, KV_WIDTH, KEY_CHUNK), BF16)
    return pl.pallas_call(
        _inproj_kernel,
        grid=(batch, seq // tile),
        in_specs=[pl.BlockSpec((1, tile, D_MODEL), lambda b, j: (b, j, 0)),
                  pl.BlockSpec((D_MODEL, IN_COLS), lambda b, j: (0, 0)),
                  tab_spec, tab_spec, tab_spec, tab_spec],
        out_specs=out_specs,
        out_shape=out_shape,
        compiler_params=pltpu.CompilerParams(dimension_semantics=("arbitrary", "arbitrary"),
                                             vmem_limit_bytes=VMEM_LIMIT_BYTES),
        name="inproj",
    )(x, w_bf16, *tabs)


def _key_to_float(key):
    bits = jnp.where(key >= 0, key, key ^ 0x7FFFFFFF)
    val = lax.bitcast_convert_type(bits, F32)
    val = jnp.where(key <= KEY_NEG_INF, -jnp.inf, val)
    return jnp.where(key >= KEY_POS_INF, jnp.inf, val)


def _dsa_kernel(q_ref, qi_ref, wi_ref, k_ref, vt_ref, ki_ref, o_ref,
                score_scr, bias_scr, lim_scr, s_scr, acc_scr, og_scr, *, topk):
    i = pl.program_id(1)
    n128 = i + 1
    n256 = (i + 2) // 2
    q0 = i * Q_TILE
    kf = float(topk)
    seq = k_ref.shape[1]
    att_cols = ATT_REP * Q_TILE

    sub = lax.broadcasted_iota(I32, (LANES, Q_TILE), 0)
    qpos = q0 + lax.broadcasted_iota(I32, (LANES, Q_TILE), 1)

    wi_t = wi_ref[0].T
    qi = qi_ref[0].astype(F32)
    per_tile = LANES // IDX_DIM
    qi_t = [qi[:, t * LANES:(t + 1) * LANES].T for t in range(IDX_WIDTH // LANES)]

    def head_cols(h):
        lo = (h % per_tile) * IDX_DIM
        keep = (sub >= lo) & (sub < lo + IDX_DIM)
        return jnp.where(keep, qi_t[h // per_tile], 0.0).astype(BF16)

    rhs_pairs = [jnp.concatenate([head_cols(2 * p), head_cols(2 * p + 1)], axis=1)
                 for p in range(IDX_HEADS // 2)]
    w_rows = [wi_t[h:h + 1, :] for h in range(IDX_HEADS)]
    krow = lax.broadcasted_iota(I32, (KEY_CHUNK, Q_TILE), 0)
    kq = q0 + lax.broadcasted_iota(I32, (KEY_CHUNK, Q_TILE), 1)

    def score_chunk(c, carry):
        ks = pl.multiple_of(c * KEY_CHUNK, KEY_CHUNK)
        kc = ki_ref[0, pl.ds(ks, KEY_CHUNK), :]
        acc = jnp.zeros((KEY_CHUNK, Q_TILE), F32)
        for p in range(IDX_HEADS // 2):
            s2 = jnp.dot(kc, rhs_pairs[p], preferred_element_type=F32)
            acc = acc + w_rows[2 * p] * jnp.maximum(s2[:, :Q_TILE], 0.0)
            acc = acc + w_rows[2 * p + 1] * jnp.maximum(s2[:, Q_TILE:], 0.0)
        score_scr[pl.ds(ks, KEY_CHUNK), :] = jnp.where(ks + krow <= kq, acc, -jnp.inf)
        return carry

    lax.fori_loop(0, n256, score_chunk, 0)

    def count(pred):
        def body(c, acc):
            r0 = pl.multiple_of(c * LANES, LANES)
            return acc + jnp.where(pred(score_scr[pl.ds(r0, LANES), :], r0), 1.0, 0.0)
        acc = lax.fori_loop(0, n128, body, jnp.zeros((LANES, Q_TILE), F32))
        return jnp.sum(acc, axis=0, keepdims=True)

    def count_ge(cand_key):
        cb = jnp.broadcast_to(_key_to_float(cand_key), (LANES, Q_TILE))
        return count(lambda sc, r0: sc >= cb)

    key0 = jnp.where(count_ge(jnp.zeros((1, Q_TILE), I32)) >= kf, 0, INT_MIN).astype(I32)

    def bit_step(b, key):
        cand = key | jnp.left_shift(jnp.int32(1), 30 - b)
        return jnp.where(count_ge(cand) >= kf, cand, key)

    thr_key = lax.fori_loop(0, 31, bit_step, key0)
    thr_b = jnp.broadcast_to(_key_to_float(thr_key), (LANES, Q_TILE))

    n_gt = count(lambda sc, r0: sc > thr_b)
    n_ge = count(lambda sc, r0: sc >= thr_b)
    need = kf - n_gt
    has_tie = (n_ge > kf) & (thr_key != INT_MIN)
    lim_scr[...] = jnp.full(lim_scr.shape, seq, I32)

    @pl.when(jnp.max(jnp.where(has_tie, 1.0, 0.0)) > 0.0)
    def _():
        def idx_step(b, lim):
            cand = lim | jnp.left_shift(jnp.int32(1), 10 - b)
            cb = jnp.broadcast_to(cand, (LANES, Q_TILE))
            below = count(lambda sc, r0: (sc == thr_b) & ((r0 + sub) < cb))
            return jnp.where(below < need, cand, lim)

        lim = lax.fori_loop(0, 11, idx_step, jnp.zeros((1, Q_TILE), I32))
        lim_scr[...] = jnp.broadcast_to(lim, lim_scr.shape)

    lim_b = jnp.broadcast_to(lim_scr[0:1, :], (LANES, Q_TILE))

    def bias_chunk(c, carry):
        r0 = pl.multiple_of(c * LANES, LANES)
        sc = score_scr[pl.ds(r0, LANES), :]
        kidx = r0 + sub
        chosen = (sc > thr_b) | ((sc == thr_b) & (kidx <= lim_b))
        bias_scr[pl.ds(r0, LANES), :] = jnp.where(chosen & (kidx <= qpos), 0.0, NEG_BIG)
        return carry

    lax.fori_loop(0, 2 * n256, bias_chunk, 0)

    qf = q_ref[0].astype(F32)
    q_t = [qf[:, j * LANES:(j + 1) * LANES].T for j in range(ATT_REP)]
    groups = KEY_CHUNK // SUBLANES
    for g in range(ATT_KV_HEADS):
        keep = (sub >= g * HEAD_DIM) & (sub < (g + 1) * HEAD_DIM)
        qg_t = jnp.concatenate([jnp.where(keep, q_t[j], 0.0) for j in range(ATT_REP)], axis=1).astype(BF16)

        def logits_chunk(c, m_acc):
            ks = pl.multiple_of(c * KEY_CHUNK, KEY_CHUNK)
            s = jnp.dot(k_ref[0, pl.ds(ks, KEY_CHUNK), :], qg_t, preferred_element_type=F32)
            bias = bias_scr[pl.ds(ks, KEY_CHUNK), :]
            s = s + jnp.concatenate([bias] * ATT_REP, axis=1)
            s_scr[pl.ds(ks, KEY_CHUNK), :] = s
            return jnp.maximum(m_acc, jnp.max(s.reshape(groups, SUBLANES, att_cols), axis=0))

        m_acc = lax.fori_loop(0, n256, logits_chunk, jnp.full((SUBLANES, att_cols), NEG_BIG, F32))
        m = jnp.max(m_acc, axis=0, keepdims=True)
        acc_scr[...] = jnp.zeros(acc_scr.shape, F32)

        def pv_chunk(c, l_acc):
            ks = pl.multiple_of(c * KEY_CHUNK, KEY_CHUNK)
            p = jnp.exp(s_scr[pl.ds(ks, KEY_CHUNK), :] - m)
            acc_scr[...] += jnp.dot(vt_ref[0, c], p.astype(BF16), preferred_element_type=F32)
            return l_acc + jnp.sum(p.reshape(groups, SUBLANES, att_cols), axis=0)

        l_acc = lax.fori_loop(0, n256, pv_chunk, jnp.zeros((SUBLANES, att_cols), F32))
        denom = jnp.sum(l_acc, axis=0, keepdims=True)
        og_scr[g * HEAD_DIM:(g + 1) * HEAD_DIM, :] = acc_scr[g * HEAD_DIM:(g + 1) * HEAD_DIM, :] / denom

    for j in range(ATT_REP):
        o_ref[0, :, j * LANES:(j + 1) * LANES] = og_scr[:, j * Q_TILE:(j + 1) * Q_TILE].T


def _dsa(q, qi, wi, k, vt, ki):
    batch, seq, _ = q.shape
    topk = min(TOPK_MAX, seq // 4)
    nq = seq // Q_TILE
    att_cols = ATT_REP * Q_TILE
    per_q = lambda w: pl.BlockSpec((1, Q_TILE, w), lambda b, i: (b, i, 0))
    per_b = lambda w: pl.BlockSpec((1, seq, w), lambda b, i: (b, 0, 0))
    vt_spec = pl.BlockSpec((1, seq // KEY_CHUNK, KV_WIDTH, KEY_CHUNK), lambda b, i: (b, 0, 0, 0))
    return pl.pallas_call(
        functools.partial(_dsa_kernel, topk=topk),
        grid=(batch, nq),
        in_specs=[per_q(ATT_WIDTH), per_q(IDX_WIDTH), per_q(LANES), per_b(KV_WIDTH), vt_spec, per_b(LANES)],
        out_specs=per_q(ATT_WIDTH),
        out_shape=jax.ShapeDtypeStruct((batch, seq, ATT_WIDTH), F32),
        scratch_shapes=[pltpu.VMEM((seq, Q_TILE), F32),
                        pltpu.VMEM((seq, Q_TILE), F32),
                        pltpu.VMEM((SUBLANES, Q_TILE), I32),
                        pltpu.VMEM((seq, att_cols), F32),
                        pltpu.VMEM((KV_WIDTH, att_cols), F32),
                        pltpu.VMEM((KV_WIDTH, att_cols), F32)],
        compiler_params=pltpu.CompilerParams(dimension_semantics=("arbitrary", "arbitrary"),
                                             vmem_limit_bytes=VMEM_LIMIT_BYTES),
        name="dsa",
    )(q, qi, wi, k, vt, ki)


def _s5_kernel(u_ref, bm_ref, cm_ref, are_ref, aim_ref, dsk_ref, wg_ref, bg_ref, o_ref, st_scr, carry_scr):
    tc, batch, width = u_ref.shape
    rows = tc * batch

    @pl.when(pl.program_id(0) == 0)
    def _():
        carry_scr[...] = jnp.zeros(carry_scr.shape, F32)

    u = u_ref[...].reshape(rows, width)
    st_scr[...] = jnp.dot(u.astype(BF16), bm_ref[...], preferred_element_type=F32)

    for part in range(SSM_LANES // SCAN_LANES):
        re_lo = part * SCAN_LANES
        im_lo = SSM_LANES + re_lo
        a_re = jnp.broadcast_to(are_ref[:, re_lo:re_lo + SCAN_LANES], (batch, SCAN_LANES))
        a_im = jnp.broadcast_to(aim_ref[:, re_lo:re_lo + SCAN_LANES], (batch, SCAN_LANES))

        def step(t, state):
            x_re, x_im = state
            r0 = pl.multiple_of(t * batch, batch)
            n_re = a_re * x_re - a_im * x_im + st_scr[pl.ds(r0, batch), re_lo:re_lo + SCAN_LANES]
            n_im = a_re * x_im + a_im * x_re + st_scr[pl.ds(r0, batch), im_lo:im_lo + SCAN_LANES]
            st_scr[pl.ds(r0, batch), re_lo:re_lo + SCAN_LANES] = n_re
            st_scr[pl.ds(r0, batch), im_lo:im_lo + SCAN_LANES] = n_im
            return n_re, n_im

        x_re, x_im = lax.fori_loop(
            0, tc, step,
            (carry_scr[:, re_lo:re_lo + SCAN_LANES], carry_scr[:, im_lo:im_lo + SCAN_LANES]), unroll=4)
        carry_scr[:, re_lo:re_lo + SCAN_LANES] = x_re
        carry_scr[:, im_lo:im_lo + SCAN_LANES] = x_im

    y = jnp.dot(st_scr[...].astype(BF16), cm_ref[...], preferred_element_type=F32) + dsk_ref[...] * u
    y = jax.nn.gelu(y)
    gate = jax.nn.sigmoid(jnp.dot(y.astype(BF16), wg_ref[...], preferred_element_type=F32) + bg_ref[...])
    o_ref[...] = (y * gate).reshape(tc, batch, width)


def _s5(u_tm, bmat, cmat, a_re, a_im, dskip, w_glu, b_glu):
    seq, batch, width = u_tm.shape
    tc = TIME_CHUNK
    full = lambda shape: pl.BlockSpec(shape, lambda t: tuple(0 for _ in shape))
    return pl.pallas_call(
        _s5_kernel,
        grid=(seq // tc,),
        in_specs=[pl.BlockSpec((tc, batch, width), lambda t: (t, 0, 0)),
                  full(bmat.shape), full(cmat.shape), full(a_re.shape), full(a_im.shape),
                  full(dskip.shape), full(w_glu.shape), full(b_glu.shape)],
        out_specs=pl.BlockSpec((tc, batch, width), lambda t: (t, 0, 0)),
        out_shape=jax.ShapeDtypeStruct((seq, batch, width), F32),
        scratch_shapes=[pltpu.VMEM((tc * batch, 2 * SSM_LANES), F32),
                        pltpu.VMEM((batch, 2 * SSM_LANES), F32)],
        compiler_params=pltpu.CompilerParams(dimension_semantics=("arbitrary",),
                                             vmem_limit_bytes=VMEM_LIMIT_BYTES),
        name="s5",
    )(u_tm, bmat, cmat, a_re, a_im, dskip, w_glu, b_glu)


def _s5_params(lam_re, lam_im, log_dt, b_re, b_im, c_re, c_im):
    dt = jnp.exp(log_dt.astype(F32))[:, None]
    mag = jnp.exp(lam_re.astype(F32) * dt)
    ang = lam_im.astype(F32) * dt
    lb_re, lb_im = mag * jnp.cos(ang), mag * jnp.sin(ang)
    den = lam_re * lam_re + lam_im * lam_im
    k_re = ((lb_re - 1.0) * lam_re + lb_im * lam_im) / den
    k_im = (lb_im * lam_re - (lb_re - 1.0) * lam_im) / den
    bb_re = k_re[:, :, None] * b_re - k_im[:, :, None] * b_im
    bb_im = k_re[:, :, None] * b_im + k_im[:, :, None] * b_re
    eye = jnp.eye(SSM_GROUPS, dtype=F32)
    drive = lambda bb: jnp.einsum('gpn,gh->gnhp', bb, eye).reshape(SSM_WIDTH, SSM_LANES)
    read = lambda cc: jnp.einsum('gnp,gh->gphn', cc, eye).reshape(SSM_LANES, SSM_WIDTH)
    bmat = jnp.concatenate([drive(bb_re), drive(bb_im)], axis=1)
    cmat = jnp.concatenate([read(c_re.astype(F32)), read(-c_im.astype(F32))], axis=0)
    return (bmat.astype(BF16), cmat.astype(BF16),
            lb_re.reshape(1, SSM_LANES), lb_im.reshape(1, SSM_LANES))


def _combine_kernel(x_ref, oa_ref, ga_ref, os_ref, gs_ref, qm_ref, gm_ref, mk_ref, mv_ref,
                    wo_ref, lg_ref, lb_ref, out_ref):
    tile = x_ref.shape[1]
    nt_dims = (((1,), (1,)), ((), ()))
    lane = lax.broadcasted_iota(I32, (tile, LANES), 1)
    low = lane < HEAD_DIM

    qm = qm_ref[0].astype(F32)
    mem_tiles = []
    for t in range(MEM_WIDTH // LANES):
        qt = qm[:, t * LANES:(t + 1) * LANES]
        mk = mk_ref[0, :, t * LANES:(t + 1) * LANES]
        mv = mv_ref[0, :, t * LANES:(t + 1) * LANES]
        halves = []
        for first in (True, False):
            qh = jnp.where(low if first else ~low, qt, 0.0).astype(BF16)
            s = lax.dot_general(qh, mk, nt_dims, preferred_element_type=F32)
            p = jnp.exp(s - jnp.max(s, axis=1, keepdims=True))
            o = jnp.dot(p.astype(BF16), mv, preferred_element_type=F32)
            halves.append(o / jnp.sum(p, axis=1, keepdims=True))
        mem_tiles.append(jnp.where(low, halves[0], halves[1]))
    o_mem = jnp.concatenate(mem_tiles, axis=1)

    c_att = (oa_ref[0] * jax.nn.silu(ga_ref[0])).astype(BF16)
    c_ssm = (os_ref[0] * jax.nn.silu(gs_ref[0])).astype(BF16)
    c_mem = (o_mem * jax.nn.silu(gm_ref[0])).astype(BF16)
    a1, a2 = ATT_WIDTH, ATT_WIDTH + SSM_WIDTH
    sub = (jnp.dot(c_att, wo_ref[:a1, :], preferred_element_type=F32)
           + jnp.dot(c_ssm, wo_ref[a1:a2, :], preferred_element_type=F32)
           + jnp.dot(c_mem, wo_ref[a2:, :], preferred_element_type=F32))
    h = DN_ALPHA * x_ref[0] + sub
    mu = jnp.mean(h, axis=1, keepdims=True)
    d = h - mu
    var = jnp.mean(d * d, axis=1, keepdims=True)
    out_ref[0] = d * lax.rsqrt(var + LN_EPS) * lg_ref[...] + lb_ref[...]


def _combine(x, o_att, g_att, o_ssm, g_ssm, qm, g_mem, mk, mv, wo_bf16, ln_g, ln_b):
    batch, seq, _ = x.shape
    tile = TOKEN_TILE
    tok = lambda w: pl.BlockSpec((1, tile, w), lambda b, j: (b, j, 0))
    per_b = pl.BlockSpec((1, N_MEM, MEM_WIDTH), lambda b, j: (b, 0, 0))
    const = lambda shape: pl.BlockSpec(shape, lambda b, j: tuple(0 for _ in shape))
    return pl.pallas_call(
        _combine_kernel,
        grid=(batch, seq // tile),
        in_specs=[tok(D_MODEL), tok(ATT_WIDTH), tok(ATT_WIDTH), tok(SSM_WIDTH), tok(SSM_WIDTH),
                  tok(MEM_WIDTH), tok(MEM_WIDTH), per_b, per_b,
                  const((D_MODEL, D_MODEL)), const((1, D_MODEL)), const((1, D_MODEL))],
        out_specs=tok(D_MODEL),
        out_shape=jax.ShapeDtypeStruct((batch, seq, D_MODEL), x.dtype),
        compiler_params=pltpu.CompilerParams(dimension_semantics=("arbitrary", "arbitrary"),
                                             vmem_limit_bytes=VMEM_LIMIT_BYTES),
        name="combine",
    )(x, o_att, g_att, o_ssm, g_ssm, qm, g_mem, mk, mv, wo_bf16, ln_g, ln_b)


def _reordered_w_in(w_in):
    offs = np.cumsum([0, ATT_WIDTH, KV_WIDTH, KV_WIDTH, IDX_WIDTH, IDX_DIM, IDX_HEADS,
                      ATT_WIDTH, SSM_WIDTH, SSM_WIDTH, MEM_WIDTH, MEM_WIDTH])
    part = lambda n: w_in[:, int(offs[n]):int(offs[n + 1])]
    perm = _att_perm()
    k_idx = part(4)
    w_idx = part(5)
    zeros = jnp.zeros((D_MODEL, LANES - IDX_HEADS), w_in.dtype)
    cols = [part(0)[:, perm], part(1), part(2), part(3),
            jnp.tile(k_idx, (1, LANES // IDX_DIM)),
            jnp.concatenate([w_idx, zeros], axis=1),
            part(6)[:, perm], part(7), part(8), part(9), part(10)]
    return jnp.concatenate(cols, axis=1).astype(BF16)


def kernel(x, mem, w_in, w_mem_kv, lam_re, lam_im, log_dt, b_re, b_im, c_re, c_im, d_skip, w_glu, b_glu,
           w_out, ln_g, ln_b):
    batch, seq, _ = x.shape
    assert seq % TOKEN_TILE == 0 and seq % Q_TILE == 0 and seq % TIME_CHUNK == 0

    w_all = _reordered_w_in(w_in)
    perm = _att_perm()
    wo = jnp.concatenate([w_out[:ATT_WIDTH][perm], w_out[ATT_WIDTH:]], axis=0).astype(BF16)
    bmat, cmat, a_re, a_im = _s5_params(lam_re, lam_im, log_dt, b_re, b_im, c_re, c_im)
    tabs = _rope_tables(seq, HEAD_DIM) + _rope_tables(seq, IDX_DIM)

    mk, mv = _memkv(mem.reshape(batch * N_MEM, D_MODEL), w_mem_kv.astype(BF16))
    mk = mk.reshape(batch, N_MEM, MEM_WIDTH)
    mv = mv.reshape(batch, N_MEM, MEM_WIDTH)

    q, k, vt, qi, ki, wi, g_att, u, g_ssm, qm, g_mem = _inproj(x, w_all, tabs)
    o_att = _dsa(q, qi, wi, k, vt, ki)

    u_tm = jnp.swapaxes(u, 0, 1)
    o_ssm_tm = _s5(u_tm, bmat, cmat, a_re, a_im,
                   d_skip.reshape(1, SSM_WIDTH).astype(F32), w_glu.astype(BF16),
                   b_glu.reshape(1, SSM_WIDTH).astype(F32))
    o_ssm = jnp.swapaxes(o_ssm_tm, 0, 1)

    return _combine(x, o_att, g_att, o_ssm, g_ssm, qm, g_mem, mk, mv, wo,
                    ln_g.reshape(1, D_MODEL).astype(F32), ln_b.reshape(1, D_MODEL).astype(F32))
```

```python
import functools

import jax
import jax.numpy as jnp
import numpy as np
from jax import lax
from jax.experimental import pallas as pl
from jax.experimental.pallas import tpu as pltpu

F32 = jnp.float32
BF16 = jnp.bfloat16
I32 = jnp.int32

D_MODEL = 1024
N_MEM = 256
HEAD_DIM = 64
ATT_HEADS = 8
ATT_KV_HEADS = 2
ATT_REP = ATT_HEADS // ATT_KV_HEADS
ATT_WIDTH = ATT_HEADS * HEAD_DIM
KV_WIDTH = ATT_KV_HEADS * HEAD_DIM
IDX_HEADS = 8
IDX_DIM = 32
IDX_WIDTH = IDX_HEADS * IDX_DIM
TOPK_MAX = 256
SSM_WIDTH = D_MODEL // 4
SSM_GROUP = 16
SSM_GROUPS = SSM_WIDTH // SSM_GROUP
SSM_STATE = 64
SSM_LANES = SSM_GROUPS * SSM_STATE
MEM_HEADS = 4
MEM_WIDTH = MEM_HEADS * HEAD_DIM
ROPE_THETA = 500000.0
ROPE_FRAC = 4
LN_EPS = 1e-5
DEPTH = 1
DN_ALPHA = (2.0 * DEPTH) ** 0.25
ATT_SCALE = HEAD_DIM ** -0.5
IDX_SCALE = IDX_HEADS ** -0.5 * IDX_DIM ** -0.5

LANES = 128
SUBLANES = 8
VMEM_LIMIT_BYTES = 48 * 1024 * 1024
TOKEN_TILE = 512
Q_TILE = 128
KEY_CHUNK = 256
TIME_CHUNK = 64
SCAN_LANES = 256
NEG_BIG = -1e30
SEARCH_MAX_PASSES = 24
SEARCH_GROUP = 4
SEARCH_BISECT_PASSES = 2
SEARCH_CLIP = 0.1

INT_MIN = -2147483648
KEY_POS_INF = 0x7F800000
KEY_NEG_INF = INT_MIN + 0x7FFFFF

_G_Q, _G_K, _G_V, _G_QI, _G_KI, _G_W, _G_GA, _G_U, _G_GS, _G_QM, _G_GM = range(11)
_GROUP_WIDTHS = [ATT_WIDTH, KV_WIDTH, KV_WIDTH, IDX_WIDTH, LANES, LANES, ATT_WIDTH,
                 SSM_WIDTH, SSM_WIDTH, MEM_WIDTH, MEM_WIDTH]
_GROUP_OFFS = [int(v) for v in np.cumsum([0] + _GROUP_WIDTHS)]
IN_COLS = _GROUP_OFFS[-1]


def _att_perm():
    cols = []
    for j in range(ATT_REP):
        for g in range(ATT_KV_HEADS):
            h = g * ATT_REP + j
            cols.extend(range(h * HEAD_DIM, (h + 1) * HEAD_DIM))
    return np.asarray(cols, dtype=np.int32)


def _rope_tables(seq, period, dtype=F32):
    r = period // ROPE_FRAC
    half = r // 2
    inv = ROPE_THETA ** (-jnp.arange(0, half, dtype=F32) * 2.0 / r)
    ang = jnp.arange(seq).astype(F32)[:, None] * inv[None, :]
    cos, sin = jnp.cos(ang), jnp.sin(ang)
    ones = jnp.ones((seq, period - r), F32)
    zeros = jnp.zeros((seq, period - r), F32)
    c = jnp.concatenate([cos, cos, ones], axis=1)
    s = jnp.concatenate([-sin, sin, zeros], axis=1)
    reps = LANES // period
    return jnp.tile(c, (1, reps)).astype(dtype), jnp.tile(s, (1, reps)).astype(dtype)


def _rope(z, cos_t, sin_t, period):
    width = z.shape[1]
    half = period // ROPE_FRAC // 2
    reps = width // LANES
    if reps > 1:
        cos_t = jnp.concatenate([cos_t] * reps, axis=1)
        sin_t = jnp.concatenate([sin_t] * reps, axis=1)
    lane = lax.broadcasted_iota(I32, z.shape, 1)
    first = (lane & (period - 1)) < half
    up = pltpu.roll(z, half, 1)
    down = pltpu.roll(z, width - half, 1)
    return z * cos_t + jnp.where(first, down, up) * sin_t


def _memkv_kernel(mem_ref, w_ref, mk_ref, mv_ref):
    z = jnp.dot(mem_ref[...].astype(BF16), w_ref[...], preferred_element_type=F32)
    mk_ref[...] = z[:, :MEM_WIDTH].astype(BF16)
    mv_ref[...] = z[:, MEM_WIDTH:].astype(BF16)


def _memkv(mem2d, w_bf16):
    rows = mem2d.shape[0]
    tile = TOKEN_TILE
    return pl.pallas_call(
        _memkv_kernel,
        grid=(rows // tile,),
        in_specs=[pl.BlockSpec((tile, D_MODEL), lambda i: (i, 0)),
                  pl.BlockSpec((D_MODEL, 2 * MEM_WIDTH), lambda i: (0, 0))],
        out_specs=[pl.BlockSpec((tile, MEM_WIDTH), lambda i: (i, 0)),
                   pl.BlockSpec((tile, MEM_WIDTH), lambda i: (i, 0))],
        out_shape=[jax.ShapeDtypeStruct((rows, MEM_WIDTH), BF16),
                   jax.ShapeDtypeStruct((rows, MEM_WIDTH), BF16)],
        compiler_params=pltpu.CompilerParams(dimension_semantics=("arbitrary",),
                                             vmem_limit_bytes=VMEM_LIMIT_BYTES),
        name="memkv",
    )(mem2d, w_bf16)


def _inproj_kernel(x_ref, w_ref, ca_ref, sa_ref, ci_ref, si_ref,
                   q_ref, k_ref, vt_ref, qi_ref, ki_ref, wi_ref, ga_ref, u_ref, gs_ref, qm_ref, gm_ref):
    tile = x_ref.shape[1]
    pos0 = pl.multiple_of(pl.program_id(1) * tile, tile)
    xb = x_ref[0].astype(BF16)

    def proj(group):
        lo, hi = _GROUP_OFFS[group], _GROUP_OFFS[group + 1]
        return jnp.dot(xb, w_ref[:, lo:hi], preferred_element_type=F32)

    ca = ca_ref[pl.ds(pos0, tile), :]
    sa = sa_ref[pl.ds(pos0, tile), :]
    ci = ci_ref[pl.ds(pos0, tile), :]
    si = si_ref[pl.ds(pos0, tile), :]

    q_ref[0] = (_rope(proj(_G_Q), ca, sa, HEAD_DIM) * ATT_SCALE).astype(BF16)
    k_ref[0] = _rope(proj(_G_K), ca, sa, HEAD_DIM).astype(BF16)
    v = proj(_G_V)
    for c in range(tile // KEY_CHUNK):
        vt_ref[0, c] = v[c * KEY_CHUNK:(c + 1) * KEY_CHUNK, :].T.astype(BF16)
    qi_ref[0] = _rope(proj(_G_QI), ci, si, IDX_DIM).astype(BF16)
    ki_ref[0] = _rope(proj(_G_KI), ci, si, IDX_DIM).astype(BF16)
    wi_ref[0] = proj(_G_W) * IDX_SCALE
    ga_ref[0] = proj(_G_GA)
    u_ref[0] = proj(_G_U)
    gs_ref[0] = proj(_G_GS)
    qm_ref[0] = (proj(_G_QM) * ATT_SCALE).astype(BF16)
    gm_ref[0] = proj(_G_GM)


def _inproj(x, w_bf16, tabs):
    batch, seq, _ = x.shape
    tile = TOKEN_TILE
    widths = _GROUP_WIDTHS
    dtypes = [BF16, BF16, BF16, BF16, BF16, F32, F32, F32, F32, BF16, F32]
    tab_spec = pl.BlockSpec((seq, LANES), lambda b, j: (0, 0))
    out_specs = [pl.BlockSpec((1, tile, w), lambda b, j: (b, j, 0)) for w in widths]
    out_shape = [jax.ShapeDtypeStruct((batch, seq, w), d) for w, d in zip(widths, dtypes)]
    out_specs[_G_V] = pl.BlockSpec((1, tile // KEY_CHUNK, KV_WIDTH, KEY_CHUNK), lambda b, j: (b, j, 0, 0))
    out_shape[_G_V] = jax.ShapeDtypeStruct((batch, seq // KEY_CHUNK, KV_WIDTH, KEY_CHUNK), BF16)
    return pl.pallas_call(
        _inproj_kernel,
        grid=(batch, seq // tile),
        in_specs=[pl.BlockSpec((1, tile, D_MODEL), lambda b, j: (b, j, 0)),
                  pl.BlockSpec((D_MODEL, IN_COLS), lambda b, j: (0, 0)),
                  tab_spec, tab_spec, tab_spec, tab_spec],
        out_specs=out_specs,
        out_shape=out_shape,
        compiler_params=pltpu.CompilerParams(dimension_semantics=("arbitrary", "arbitrary"),
                                             vmem_limit_bytes=VMEM_LIMIT_BYTES),
        name="inproj",
    )(x, w_bf16, *tabs)


def _key_to_float(key):
    bits = jnp.where(key >= 0, key, key ^ 0x7FFFFFFF)
    val = lax.bitcast_convert_type(bits, F32)
    val = jnp.where(key <= KEY_NEG_INF, -jnp.inf, val)
    return jnp.where(key >= KEY_POS_INF, jnp.inf, val)


def _dsa_kernel(q_ref, qi_ref, wi_ref, k_ref, vt_ref, ki_ref, o_ref,
                score_scr, bias_scr, row_scr, lim_scr, s_scr, acc_scr, og_scr, *, topk):
    i = pl.program_id(1)
    n256 = (i + 2) // 2
    q0 = i * Q_TILE
    kf = float(topk)
    seq = k_ref.shape[1]
    att_cols = ATT_REP * Q_TILE
    groups = KEY_CHUNK // SUBLANES

    sub = lax.broadcasted_iota(I32, (LANES, Q_TILE), 0)
    qpos = q0 + lax.broadcasted_iota(I32, (LANES, Q_TILE), 1)

    def chunk_start(c):
        return pl.multiple_of(c * KEY_CHUNK, KEY_CHUNK)

    def fold(v, op):
        return op(v.reshape(groups, SUBLANES, v.shape[-1]), axis=0)

    wi_t = wi_ref[0].T
    qi = qi_ref[0].astype(F32)
    per_tile = LANES // IDX_DIM
    qi_t = [qi[:, t * LANES:(t + 1) * LANES].T for t in range(IDX_WIDTH // LANES)]

    def head_cols(h):
        lo = (h % per_tile) * IDX_DIM
        keep = (sub >= lo) & (sub < lo + IDX_DIM)
        return jnp.where(keep, qi_t[h // per_tile], 0.0).astype(BF16)

    rhs_pairs = [jnp.concatenate([head_cols(2 * p), head_cols(2 * p + 1)], axis=1)
                 for p in range(IDX_HEADS // 2)]
    w_rows = [wi_t[h:h + 1, :] for h in range(IDX_HEADS)]
    qf = q_ref[0].astype(F32)
    q_t = [qf[:, j * LANES:(j + 1) * LANES].T for j in range(ATT_REP)]
    qg_t = []
    for g in range(ATT_KV_HEADS):
        keep = (sub >= g * HEAD_DIM) & (sub < (g + 1) * HEAD_DIM)
        qg_t.append(jnp.concatenate([jnp.where(keep, q_t[j], 0.0) for j in range(ATT_REP)],
                                    axis=1).astype(BF16))
    krow = lax.broadcasted_iota(I32, (KEY_CHUNK, Q_TILE), 0)
    kq = q0 + lax.broadcasted_iota(I32, (KEY_CHUNK, Q_TILE), 1)

    def matmul_chunk(c, stats):
        mx_a, mn_a, ge_a, gt_a = stats
        ks = chunk_start(c)
        kc = ki_ref[0, pl.ds(ks, KEY_CHUNK), :]
        acc = jnp.zeros((KEY_CHUNK, Q_TILE), F32)
        for p in range(IDX_HEADS // 2):
            s2 = jnp.dot(kc, rhs_pairs[p], preferred_element_type=F32)
            acc = acc + w_rows[2 * p] * jnp.maximum(s2[:, :Q_TILE], 0.0)
            acc = acc + w_rows[2 * p + 1] * jnp.maximum(s2[:, Q_TILE:], 0.0)
        causal = ks + krow <= kq
        sc = jnp.where(causal, acc, -jnp.inf)
        score_scr[pl.ds(ks, KEY_CHUNK), :] = sc
        kk = k_ref[0, pl.ds(ks, KEY_CHUNK), :]
        for g in range(ATT_KV_HEADS):
            s_scr[g, pl.ds(ks, KEY_CHUNK), :] = jnp.dot(kk, qg_t[g], preferred_element_type=F32)
        return (jnp.maximum(mx_a, fold(sc, jnp.max)),
                jnp.minimum(mn_a, fold(jnp.where(causal, acc, jnp.inf), jnp.min)),
                ge_a + fold(jnp.where(sc >= 0.0, 1.0, 0.0), jnp.sum),
                gt_a + fold(jnp.where(sc > 0.0, 1.0, 0.0), jnp.sum))

    stat0 = lambda v: jnp.full((SUBLANES, Q_TILE), v, F32)
    mx_a, mn_a, ge_a, gt_a = lax.fori_loop(0, n256, matmul_chunk,
                                           (stat0(-jnp.inf), stat0(jnp.inf), stat0(0.0), stat0(0.0)))
    s_max = jnp.max(mx_a, axis=0, keepdims=True)
    s_min = jnp.min(mn_a, axis=0, keepdims=True)
    n_ge0 = jnp.sum(ge_a, axis=0, keepdims=True)
    n_gt0 = jnp.sum(gt_a, axis=0, keepdims=True)
    n_causal = (qpos[0:1, :] + 1).astype(F32)

    def count(pred):
        def body(c, acc):
            ks = chunk_start(c)
            for part in range(KEY_CHUNK // LANES):
                r0 = ks + part * LANES
                acc = acc + jnp.where(pred(score_scr[pl.ds(r0, LANES), :], r0), 1.0, 0.0)
            return acc
        acc = lax.fori_loop(0, n256, body, jnp.zeros((LANES, Q_TILE), F32))
        rows = LANES
        while rows > SUBLANES:
            rows //= 2
            acc = acc[:rows] + acc[rows:]
        return jnp.sum(acc, axis=0, keepdims=True)

    def count_ge(value):
        vb = jnp.broadcast_to(value, (LANES, Q_TILE))
        return count(lambda sc, r0: sc >= vb)

    zero_thr = (n_gt0 < kf) & (n_ge0 >= kf)
    keep_all = n_causal <= kf
    positive = n_gt0 >= kf
    settled = zero_thr | keep_all
    lo0 = jnp.where(settled, 0.0, jnp.where(positive, 0.0, s_min))
    hi0 = jnp.where(settled, 0.0, jnp.where(positive, s_max, 0.0))
    c_lo0 = jnp.where(positive, n_gt0, n_causal)
    c_hi0 = jnp.where(positive, 0.0, n_ge0)
    thr0 = jnp.where(keep_all, -jnp.inf, 0.0)
    done0 = jnp.where(settled, 1.0, 0.0)

    def search_pass(it, rows):
        lo, hi, c_lo, c_hi, thr, done = rows
        frac = jnp.clip((kf - c_hi + 0.5) / (c_lo - c_hi + 1.0), SEARCH_CLIP, 1.0 - SEARCH_CLIP)
        frac = jnp.where(it < SEARCH_BISECT_PASSES, 0.5, frac)
        x = hi - (hi - lo) * frac
        c = count_ge(x)
        live = done == 0.0
        above = live & (c >= kf)
        below = live & (c < kf)
        hit = live & (c == kf)
        return (jnp.where(above, x, lo), jnp.where(below, x, hi),
                jnp.where(above, c, c_lo), jnp.where(below, c, c_hi),
                jnp.where(hit, x, thr), jnp.where(hit, 1.0, done))

    def search_cond(state):
        return (state[0] < SEARCH_MAX_PASSES) & (state[1] > 0.0)

    def search_body(state):
        it, _, rows = state
        for p in range(SEARCH_GROUP):
            rows = search_pass(it + p, rows)
        return it + SEARCH_GROUP, jnp.sum(1.0 - rows[5]), rows

    state = lax.while_loop(search_cond, search_body,
                           (jnp.int32(0), jnp.sum(1.0 - done0), (lo0, hi0, c_lo0, c_hi0, thr0, done0)))
    pending, thr_s, done_s = state[1], state[2][4], state[2][5]
    row_scr[0:1, :] = thr_s
    row_scr[1:2, :] = jnp.where(zero_thr, n_gt0, kf)
    row_scr[2:3, :] = jnp.where(zero_thr, n_ge0, kf)

    @pl.when(pending > 0.0)
    def _():
        def count_ge_key(key):
            return count_ge(_key_to_float(key))

        key0 = jnp.where(count_ge_key(jnp.zeros((1, Q_TILE), I32)) >= kf, 0, INT_MIN).astype(I32)

        def bit_step(b, key):
            cand = key | jnp.left_shift(jnp.int32(1), 30 - b)
            return jnp.where(count_ge_key(cand) >= kf, cand, key)

        thr_f = _key_to_float(lax.fori_loop(0, 31, bit_step, key0))
        thr_fb = jnp.broadcast_to(thr_f, (LANES, Q_TILE))
        open_row = done_s == 0.0
        row_scr[0:1, :] = jnp.where(open_row, thr_f, thr_s)
        row_scr[1:2, :] = jnp.where(open_row, count(lambda sc, r0: sc > thr_fb), row_scr[1:2, :])
        row_scr[2:3, :] = jnp.where(open_row, count(lambda sc, r0: sc >= thr_fb), row_scr[2:3, :])

    thr = row_scr[0:1, :]
    thr_b = jnp.broadcast_to(thr, (LANES, Q_TILE))
    need = kf - row_scr[1:2, :]

    has_tie = (row_scr[2:3, :] > kf) & (thr > -jnp.inf)
    lim_scr[...] = jnp.full(lim_scr.shape, seq, I32)

    @pl.when(jnp.max(jnp.where(has_tie, 1.0, 0.0)) > 0.0)
    def _():
        def idx_step(b, lim):
            cand = lim | jnp.left_shift(jnp.int32(1), 10 - b)
            cb = jnp.broadcast_to(cand, (LANES, Q_TILE))
            below = count(lambda sc, r0: (sc == thr_b) & ((r0 + sub) < cb))
            return jnp.where(below < need, cand, lim)

        lim = lax.fori_loop(0, 11, idx_step, jnp.zeros((1, Q_TILE), I32))
        lim_scr[...] = jnp.broadcast_to(lim, lim_scr.shape)

    lim_b = jnp.broadcast_to(lim_scr[0:1, :], (LANES, Q_TILE))

    def masked_logits(c, g):
        ks = chunk_start(c)
        bias = bias_scr[pl.ds(ks, KEY_CHUNK), :]
        return s_scr[g, pl.ds(ks, KEY_CHUNK), :] + jnp.concatenate([bias] * ATT_REP, axis=1)

    def mask_chunk(c, m_acc):
        ks = chunk_start(c)
        for part in range(KEY_CHUNK // LANES):
            r0 = ks + part * LANES
            sc = score_scr[pl.ds(r0, LANES), :]
            kidx = r0 + sub
            chosen = (sc > thr_b) | ((sc == thr_b) & (kidx <= lim_b))
            bias_scr[pl.ds(r0, LANES), :] = jnp.where(chosen & (kidx <= qpos), 0.0, NEG_BIG)
        return tuple(jnp.maximum(m_acc[g], fold(masked_logits(c, g), jnp.max)) for g in range(ATT_KV_HEADS))

    m_acc = lax.fori_loop(0, n256, mask_chunk,
                          tuple(jnp.full((SUBLANES, att_cols), NEG_BIG, F32) for _ in range(ATT_KV_HEADS)))
    m = [jnp.max(m_acc[g], axis=0, keepdims=True) for g in range(ATT_KV_HEADS)]

    acc_scr[...] = jnp.zeros(acc_scr.shape, F32)

    def pv_chunk(c, l_acc):
        out = []
        for g in range(ATT_KV_HEADS):
            p = jnp.exp(masked_logits(c, g) - m[g])
            acc_scr[g] += jnp.dot(vt_ref[0, c], p.astype(BF16), preferred_element_type=F32)
            out.append(l_acc[g] + fold(p, jnp.sum))
        return tuple(out)

    l_acc = lax.fori_loop(0, n256, pv_chunk,
                          tuple(jnp.zeros((SUBLANES, att_cols), F32) for _ in range(ATT_KV_HEADS)))
    for g in range(ATT_KV_HEADS):
        denom = jnp.sum(l_acc[g], axis=0, keepdims=True)
        og_scr[g * HEAD_DIM:(g + 1) * HEAD_DIM, :] = acc_scr[g, g * HEAD_DIM:(g + 1) * HEAD_DIM, :] / denom

    for j in range(ATT_REP):
        o_ref[0, :, j * LANES:(j + 1) * LANES] = og_scr[:, j * Q_TILE:(j + 1) * Q_TILE].T


def _dsa(q, qi, wi, k, vt, ki):
    batch, seq, _ = q.shape
    topk = min(TOPK_MAX, seq // 4)
    nq = seq // Q_TILE
    att_cols = ATT_REP * Q_TILE
    per_q = lambda w: pl.BlockSpec((1, Q_TILE, w), lambda b, i: (b, i, 0))
    per_b = lambda w: pl.BlockSpec((1, seq, w), lambda b, i: (b, 0, 0))
    vt_spec = pl.BlockSpec((1, seq // KEY_CHUNK, KV_WIDTH, KEY_CHUNK), lambda b, i: (b, 0, 0, 0))
    return pl.pallas_call(
        functools.partial(_dsa_kernel, topk=topk),
        grid=(batch, nq),
        in_specs=[per_q(ATT_WIDTH), per_q(IDX_WIDTH), per_q(LANES), per_b(KV_WIDTH), vt_spec, per_b(LANES)],
        out_specs=per_q(ATT_WIDTH),
        out_shape=jax.ShapeDtypeStruct((batch, seq, ATT_WIDTH), F32),
        scratch_shapes=[pltpu.VMEM((seq, Q_TILE), F32),
                        pltpu.VMEM((seq, Q_TILE), F32),
                        pltpu.VMEM((SUBLANES, Q_TILE), F32),
                        pltpu.VMEM((SUBLANES, Q_TILE), I32),
                        pltpu.VMEM((ATT_KV_HEADS, seq, att_cols), F32),
                        pltpu.VMEM((ATT_KV_HEADS, KV_WIDTH, att_cols), F32),
                        pltpu.VMEM((KV_WIDTH, att_cols), F32)],
        compiler_params=pltpu.CompilerParams(dimension_semantics=("arbitrary", "arbitrary"),
                                             vmem_limit_bytes=VMEM_LIMIT_BYTES),
        name="dsa",
    )(q, qi, wi, k, vt, ki)


def _s5_kernel(u_ref, bm_ref, cm_ref, are_ref, aim_ref, dsk_ref, wg_ref, bg_ref, o_ref, st_scr, carry_scr):
    tc, batch, width = u_ref.shape
    rows = tc * batch

    @pl.when(pl.program_id(0) == 0)
    def _():
        carry_scr[...] = jnp.zeros(carry_scr.shape, F32)

    u = u_ref[...].reshape(rows, width)
    st_scr[...] = jnp.dot(u.astype(BF16), bm_ref[...], preferred_element_type=F32)

    for part in range(SSM_LANES // SCAN_LANES):
        re_lo = part * SCAN_LANES
        im_lo = SSM_LANES + re_lo
        a_re = jnp.broadcast_to(are_ref[:, re_lo:re_lo + SCAN_LANES], (batch, SCAN_LANES))
        a_im = jnp.broadcast_to(aim_ref[:, re_lo:re_lo + SCAN_LANES], (batch, SCAN_LANES))

        def step(t, state):
            x_re, x_im = state
            r0 = pl.multiple_of(t * batch, batch)
            n_re = a_re * x_re - a_im * x_im + st_scr[pl.ds(r0, batch), re_lo:re_lo + SCAN_LANES]
            n_im = a_re * x_im + a_im * x_re + st_scr[pl.ds(r0, batch), im_lo:im_lo + SCAN_LANES]
            st_scr[pl.ds(r0, batch), re_lo:re_lo + SCAN_LANES] = n_re
            st_scr[pl.ds(r0, batch), im_lo:im_lo + SCAN_LANES] = n_im
            return n_re, n_im

        x_re, x_im = lax.fori_loop(
            0, tc, step,
            (carry_scr[:, re_lo:re_lo + SCAN_LANES], carry_scr[:, im_lo:im_lo + SCAN_LANES]), unroll=4)
        carry_scr[:, re_lo:re_lo + SCAN_LANES] = x_re
        carry_scr[:, im_lo:im_lo + SCAN_LANES] = x_im

    y = jnp.dot(st_scr[...].astype(BF16), cm_ref[...], preferred_element_type=F32) + dsk_ref[...] * u
    y = jax.nn.gelu(y)
    gate = jax.nn.sigmoid(jnp.dot(y.astype(BF16), wg_ref[...], preferred_element_type=F32) + bg_ref[...])
    o_ref[...] = (y * gate).reshape(tc, batch, width)


def _s5(u_tm, bmat, cmat, a_re, a_im, dskip, w_glu, b_glu):
    seq, batch, width = u_tm.shape
    tc = TIME_CHUNK
    full = lambda shape: pl.BlockSpec(shape, lambda t: tuple(0 for _ in shape))
    return pl.pallas_call(
        _s5_kernel,
        grid=(seq // tc,),
        in_specs=[pl.BlockSpec((tc, batch, width), lambda t: (t, 0, 0)),
                  full(bmat.shape), full(cmat.shape), full(a_re.shape), full(a_im.shape),
                  full(dskip.shape), full(w_glu.shape), full(b_glu.shape)],
        out_specs=pl.BlockSpec((tc, batch, width), lambda t: (t, 0, 0)),
        out_shape=jax.ShapeDtypeStruct((seq, batch, width), F32),
        scratch_shapes=[pltpu.VMEM((tc * batch, 2 * SSM_LANES), F32),
                        pltpu.VMEM((batch, 2 * SSM_LANES), F32)],
        compiler_params=pltpu.CompilerParams(dimension_semantics=("arbitrary",),
                                             vmem_limit_bytes=VMEM_LIMIT_BYTES),
        name="s5",
    )(u_tm, bmat, cmat, a_re, a_im, dskip, w_glu, b_glu)


def _s5_params(lam_re, lam_im, log_dt, b_re, b_im, c_re, c_im):
    dt = jnp.exp(log_dt.astype(F32))[:, None]
    mag = jnp.exp(lam_re.astype(F32) * dt)
    ang = lam_im.astype(F32) * dt
    lb_re, lb_im = mag * jnp.cos(ang), mag * jnp.sin(ang)
    den = lam_re * lam_re + lam_im * lam_im
    k_re = ((lb_re - 1.0) * lam_re + lb_im * lam_im) / den
    k_im = (lb_im * lam_re - (lb_re - 1.0) * lam_im) / den
    bb_re = k_re[:, :, None] * b_re - k_im[:, :, None] * b_im
    bb_im = k_re[:, :, None] * b_im + k_im[:, :, None] * b_re
    eye = jnp.eye(SSM_GROUPS, dtype=F32)
    drive = lambda bb: jnp.einsum('gpn,gh->gnhp', bb, eye).reshape(SSM_WIDTH, SSM_LANES)
    read = lambda cc: jnp.einsum('gnp,gh->gphn', cc, eye).reshape(SSM_LANES, SSM_WIDTH)
    bmat = jnp.concatenate([drive(bb_re), drive(bb_im)], axis=1)
    cmat = jnp.concatenate([read(c_re.astype(F32)), read(-c_im.astype(F32))], axis=0)
    return (bmat.astype(BF16), cmat.astype(BF16),
            lb_re.reshape(1, SSM_LANES), lb_im.reshape(1, SSM_LANES))


def _combine_kernel(x_ref, oa_ref, ga_ref, os_ref, gs_ref, qm_ref, gm_ref, mk_ref, mv_ref,
                    wo_ref, lg_ref, lb_ref, out_ref):
    tile = x_ref.shape[1]
    nt_dims = (((1,), (1,)), ((), ()))
    lane = lax.broadcasted_iota(I32, (tile, LANES), 1)
    low = lane < HEAD_DIM

    qm = qm_ref[0].astype(F32)
    mem_tiles = []
    for t in range(MEM_WIDTH // LANES):
        qt = qm[:, t * LANES:(t + 1) * LANES]
        mk = mk_ref[0, :, t * LANES:(t + 1) * LANES]
        mv = mv_ref[0, :, t * LANES:(t + 1) * LANES]
        halves = []
        for first in (True, False):
            qh = jnp.where(low if first else ~low, qt, 0.0).astype(BF16)
            s = lax.dot_general(qh, mk, nt_dims, preferred_element_type=F32)
            p = jnp.exp(s - jnp.max(s, axis=1, keepdims=True))
            o = jnp.dot(p.astype(BF16), mv, preferred_element_type=F32)
            halves.append(o / jnp.sum(p, axis=1, keepdims=True))
        mem_tiles.append(jnp.where(low, halves[0], halves[1]))
    o_mem = jnp.concatenate(mem_tiles, axis=1)

    c_att = (oa_ref[0] * jax.nn.silu(ga_ref[0])).astype(BF16)
    c_ssm = (os_ref[0] * jax.nn.silu(gs_ref[0])).astype(BF16)
    c_mem = (o_mem * jax.nn.silu(gm_ref[0])).astype(BF16)
    a1, a2 = ATT_WIDTH, ATT_WIDTH + SSM_WIDTH
    sub = (jnp.dot(c_att, wo_ref[:a1, :], preferred_element_type=F32)
           + jnp.dot(c_ssm, wo_ref[a1:a2, :], preferred_element_type=F32)
           + jnp.dot(c_mem, wo_ref[a2:, :], preferred_element_type=F32))
    h = DN_ALPHA * x_ref[0] + sub
    mu = jnp.mean(h, axis=1, keepdims=True)
    d = h - mu
    var = jnp.mean(d * d, axis=1, keepdims=True)
    out_ref[0] = d * lax.rsqrt(var + LN_EPS) * lg_ref[...] + lb_ref[...]


def _combine(x, o_att, g_att, o_ssm, g_ssm, qm, g_mem, mk, mv, wo_bf16, ln_g, ln_b):
    batch, seq, _ = x.shape
    tile = TOKEN_TILE
    tok = lambda w: pl.BlockSpec((1, tile, w), lambda b, j: (b, j, 0))
    per_b = pl.BlockSpec((1, N_MEM, MEM_WIDTH), lambda b, j: (b, 0, 0))
    const = lambda shape: pl.BlockSpec(shape, lambda b, j: tuple(0 for _ in shape))
    return pl.pallas_call(
        _combine_kernel,
        grid=(batch, seq // tile),
        in_specs=[tok(D_MODEL), tok(ATT_WIDTH), tok(ATT_WIDTH), tok(SSM_WIDTH), tok(SSM_WIDTH),
                  tok(MEM_WIDTH), tok(MEM_WIDTH), per_b, per_b,
                  const((D_MODEL, D_MODEL)), const((1, D_MODEL)), const((1, D_MODEL))],
        out_specs=tok(D_MODEL),
        out_shape=jax.ShapeDtypeStruct((batch, seq, D_MODEL), x.dtype),
        compiler_params=pltpu.CompilerParams(dimension_semantics=("arbitrary", "arbitrary"),
                                             vmem_limit_bytes=VMEM_LIMIT_BYTES),
        name="combine",
    )(x, o_att, g_att, o_ssm, g_ssm, qm, g_mem, mk, mv, wo_bf16, ln_g, ln_b)


def _reordered_w_in(w_in):
    offs = np.cumsum([0, ATT_WIDTH, KV_WIDTH, KV_WIDTH, IDX_WIDTH, IDX_DIM, IDX_HEADS,
                      ATT_WIDTH, SSM_WIDTH, SSM_WIDTH, MEM_WIDTH, MEM_WIDTH])
    part = lambda n: w_in[:, int(offs[n]):int(offs[n + 1])]
    perm = _att_perm()
    k_idx = part(4)
    w_idx = part(5)
    zeros = jnp.zeros((D_MODEL, LANES - IDX_HEADS), w_in.dtype)
    cols = [part(0)[:, perm], part(1), part(2), part(3),
            jnp.tile(k_idx, (1, LANES // IDX_DIM)),
            jnp.concatenate([w_idx, zeros], axis=1),
            part(6)[:, perm], part(7), part(8), part(9), part(10)]
    return jnp.concatenate(cols, axis=1).astype(BF16)


def kernel(x, mem, w_in, w_mem_kv, lam_re, lam_im, log_dt, b_re, b_im, c_re, c_im, d_skip, w_glu, b_glu,
           w_out, ln_g, ln_b):
    batch, seq, _ = x.shape
    assert seq % TOKEN_TILE == 0 and seq % Q_TILE == 0 and seq % TIME_CHUNK == 0

    w_all = _reordered_w_in(w_in)
    perm = _att_perm()
    wo = jnp.concatenate([w_out[:ATT_WIDTH][perm], w_out[ATT_WIDTH:]], axis=0).astype(BF16)
    bmat, cmat, a_re, a_im = _s5_params(lam_re, lam_im, log_dt, b_re, b_im, c_re, c_im)
    tabs = _rope_tables(seq, HEAD_DIM) + _rope_tables(seq, IDX_DIM)

    mk, mv = _memkv(mem.reshape(batch * N_MEM, D_MODEL), w_mem_kv.astype(BF16))
    mk = mk.reshape(batch, N_MEM, MEM_WIDTH)
    mv = mv.reshape(batch, N_MEM, MEM_WIDTH)

    q, k, vt, qi, ki, wi, g_att, u, g_ssm, qm, g_mem = _inproj(x, w_all, tabs)
    o_att = _dsa(q, qi, wi, k, vt, ki)

    u_tm = jnp.swapaxes(u, 0, 1)
    o_ssm_tm = _s5(u_tm, bmat, cmat, a_re, a_im,
                   d_skip.reshape(1, SSM_WIDTH).astype(F32), w_glu.astype(BF16),
                   b_glu.reshape(1, SSM_WIDTH).astype(F32))
    o_ssm = jnp.swapaxes(o_ssm_tm, 0, 1)

    return _combine(x, o_att, g_att, o_ssm, g_ssm, qm, g_mem, mk, mv, wo,
                    ln_g.reshape(1, D_MODEL).astype(F32), ln_b.reshape(1, D_MODEL).astype(F32))
```

```python
import functools
import math

import jax
import jax.numpy as jnp
import numpy as np
from jax import lax
from jax.experimental import pallas as pl
from jax.experimental.pallas import tpu as pltpu

F32 = jnp.float32
BF16 = jnp.bfloat16
I32 = jnp.int32

D_MODEL = 1024
N_MEM = 256
HEAD_DIM = 64
ATT_HEADS = 8
ATT_KV_HEADS = 2
ATT_REP = ATT_HEADS // ATT_KV_HEADS
ATT_WIDTH = ATT_HEADS * HEAD_DIM
KV_WIDTH = ATT_KV_HEADS * HEAD_DIM
IDX_HEADS = 8
IDX_DIM = 32
IDX_WIDTH = IDX_HEADS * IDX_DIM
TOPK_MAX = 256
SSM_WIDTH = D_MODEL // 4
SSM_GROUP = 16
SSM_GROUPS = SSM_WIDTH // SSM_GROUP
SSM_STATE = 64
SSM_LANES = SSM_GROUPS * SSM_STATE
MEM_HEADS = 4
MEM_WIDTH = MEM_HEADS * HEAD_DIM
ROPE_THETA = 500000.0
ROPE_FRAC = 4
LN_EPS = 1e-5
DEPTH = 1
DN_ALPHA = (2.0 * DEPTH) ** 0.25
ATT_SCALE = HEAD_DIM ** -0.5
IDX_SCALE = IDX_HEADS ** -0.5 * IDX_DIM ** -0.5
LOG2_E = math.log2(math.e)

LANES = 128
SUBLANES = 8
BF16_ROWS = 16
VMEM_LIMIT_BYTES = 48 * 1024 * 1024
TOKEN_TILE = 512
Q_TILE = 256
KEY_CHUNK = 256
COUNT_ROWS = 64
VT_ROWS = KV_WIDTH + BF16_ROWS
TIME_CHUNK = 64
SCAN_LANES = 256
NEG_BIG = -1e30
SEARCH_MAX_PASSES = 24
SEARCH_GROUP = 4
SEARCH_BISECT_PASSES = 2
SEARCH_CLIP = 0.1

INT_MIN = -2147483648
KEY_POS_INF = 0x7F800000
KEY_NEG_INF = INT_MIN + 0x7FFFFF

_G_Q, _G_K, _G_V, _G_QI, _G_KI, _G_W, _G_GA, _G_U, _G_GS, _G_QM, _G_GM = range(11)
_GROUP_WIDTHS = [ATT_WIDTH, KV_WIDTH, KV_WIDTH, IDX_WIDTH, LANES, LANES, ATT_WIDTH,
                 SSM_WIDTH, SSM_WIDTH, MEM_WIDTH, MEM_WIDTH]
_GROUP_OFFS = [int(v) for v in np.cumsum([0] + _GROUP_WIDTHS)]
IN_COLS = _GROUP_OFFS[-1]


def _att_perm():
    cols = []
    for j in range(ATT_REP):
        for g in range(ATT_KV_HEADS):
            h = g * ATT_REP + j
            cols.extend(range(h * HEAD_DIM, (h + 1) * HEAD_DIM))
    return np.asarray(cols, dtype=np.int32)


def _rope_tables(seq, period, dtype=F32):
    r = period // ROPE_FRAC
    half = r // 2
    inv = ROPE_THETA ** (-jnp.arange(0, half, dtype=F32) * 2.0 / r)
    ang = jnp.arange(seq).astype(F32)[:, None] * inv[None, :]
    cos, sin = jnp.cos(ang), jnp.sin(ang)
    ones = jnp.ones((seq, period - r), F32)
    zeros = jnp.zeros((seq, period - r), F32)
    c = jnp.concatenate([cos, cos, ones], axis=1)
    s = jnp.concatenate([-sin, sin, zeros], axis=1)
    reps = LANES // period
    return jnp.tile(c, (1, reps)).astype(dtype), jnp.tile(s, (1, reps)).astype(dtype)


def _rope(z, cos_t, sin_t, period):
    width = z.shape[1]
    half = period // ROPE_FRAC // 2
    reps = width // LANES
    if reps > 1:
        cos_t = jnp.concatenate([cos_t] * reps, axis=1)
        sin_t = jnp.concatenate([sin_t] * reps, axis=1)
    lane = lax.broadcasted_iota(I32, z.shape, 1)
    first = (lane & (period - 1)) < half
    up = pltpu.roll(z, half, 1)
    down = pltpu.roll(z, width - half, 1)
    return z * cos_t + jnp.where(first, down, up) * sin_t


def _memkv_kernel(mem_ref, w_ref, mk_ref, mv_ref):
    z = jnp.dot(mem_ref[...].astype(BF16), w_ref[...], preferred_element_type=F32)
    mk_ref[...] = z[:, :MEM_WIDTH].astype(BF16)
    mv_ref[...] = z[:, MEM_WIDTH:].astype(BF16)


def _memkv(mem2d, w_bf16):
    rows = mem2d.shape[0]
    tile = TOKEN_TILE
    return pl.pallas_call(
        _memkv_kernel,
        grid=(rows // tile,),
        in_specs=[pl.BlockSpec((tile, D_MODEL), lambda i: (i, 0)),
                  pl.BlockSpec((D_MODEL, 2 * MEM_WIDTH), lambda i: (0, 0))],
        out_specs=[pl.BlockSpec((tile, MEM_WIDTH), lambda i: (i, 0)),
                   pl.BlockSpec((tile, MEM_WIDTH), lambda i: (i, 0))],
        out_shape=[jax.ShapeDtypeStruct((rows, MEM_WIDTH), BF16),
                   jax.ShapeDtypeStruct((rows, MEM_WIDTH), BF16)],
        compiler_params=pltpu.CompilerParams(dimension_semantics=("arbitrary",),
                                             vmem_limit_bytes=VMEM_LIMIT_BYTES),
        name="memkv",
    )(mem2d, w_bf16)


def _inproj_kernel(x_ref, w_ref, ca_ref, sa_ref, ci_ref, si_ref,
                   q_ref, k_ref, vt_ref, qi_ref, ki_ref, wi_ref, ga_ref, u_ref, gs_ref, qm_ref, gm_ref):
    tile = x_ref.shape[1]
    pos0 = pl.multiple_of(pl.program_id(1) * tile, tile)
    xb = x_ref[0].astype(BF16)

    def proj(group):
        lo, hi = _GROUP_OFFS[group], _GROUP_OFFS[group + 1]
        return jnp.dot(xb, w_ref[:, lo:hi], preferred_element_type=F32)

    ca = ca_ref[pl.ds(pos0, tile), :]
    sa = sa_ref[pl.ds(pos0, tile), :]
    ci = ci_ref[pl.ds(pos0, tile), :]
    si = si_ref[pl.ds(pos0, tile), :]

    q_ref[0] = (_rope(proj(_G_Q), ca, sa, HEAD_DIM) * (ATT_SCALE * LOG2_E)).astype(BF16)
    k_ref[0] = _rope(proj(_G_K), ca, sa, HEAD_DIM).astype(BF16)
    v = proj(_G_V)
    for c in range(tile // KEY_CHUNK):
        vt_ref[0, c, :KV_WIDTH, :] = v[c * KEY_CHUNK:(c + 1) * KEY_CHUNK, :].T.astype(BF16)
        vt_ref[0, c, KV_WIDTH:, :] = jnp.ones((BF16_ROWS, KEY_CHUNK), BF16)
    qi_ref[0] = _rope(proj(_G_QI), ci, si, IDX_DIM).astype(BF16)
    ki_ref[0] = _rope(proj(_G_KI), ci, si, IDX_DIM).astype(BF16)
    wi_ref[0] = proj(_G_W) * IDX_SCALE
    ga_ref[0] = jax.nn.silu(proj(_G_GA))
    u_ref[0] = proj(_G_U)
    gs_ref[0] = jax.nn.silu(proj(_G_GS))
    qm_ref[0] = (proj(_G_QM) * ATT_SCALE).astype(BF16)
    gm_ref[0] = jax.nn.silu(proj(_G_GM))


def _inproj(x, w_bf16, tabs):
    batch, seq, _ = x.shape
    tile = TOKEN_TILE
    widths = _GROUP_WIDTHS
    dtypes = [BF16, BF16, BF16, BF16, BF16, F32, F32, F32, F32, BF16, F32]
    tab_spec = pl.BlockSpec((seq, LANES), lambda b, j: (0, 0))
    out_specs = [pl.BlockSpec((1, tile, w), lambda b, j: (b, j, 0)) for w in widths]
    out_shape = [jax.ShapeDtypeStruct((batch, seq, w), d) for w, d in zip(widths, dtypes)]
    out_specs[_G_V] = pl.BlockSpec((1, tile // KEY_CHUNK, VT_ROWS, KEY_CHUNK), lambda b, j: (b, j, 0, 0))
    out_shape[_G_V] = jax.ShapeDtypeStruct((batch, seq // KEY_CHUNK, VT_ROWS, KEY_CHUNK), BF16)
    return pl.pallas_call(
        _inproj_kernel,
        grid=(batch, seq // tile),
        in_specs=[pl.BlockSpec((1, tile, D_MODEL), lambda b, j: (b, j, 0)),
                  pl.BlockSpec((D_MODEL, IN_COLS), lambda b, j: (0, 0)),
                  tab_spec, tab_spec, tab_spec, tab_spec],
        out_specs=out_specs,
        out_shape=out_shape,
        compiler_params=pltpu.CompilerParams(dimension_semantics=("arbitrary", "arbitrary"),
                                             vmem_limit_bytes=VMEM_LIMIT_BYTES),
        name="inproj",
    )(x, w_bf16, *tabs)


def _key_to_float(key):
    bits = jnp.where(key >= 0, key, key ^ 0x7FFFFFFF)
    val = lax.bitcast_convert_type(bits, F32)
    val = jnp.where(key <= KEY_NEG_INF, -jnp.inf, val)
    return jnp.where(key >= KEY_POS_INF, jnp.inf, val)


def _dsa_kernel(q_ref, qi_ref, wi_ref, k_ref, vt_ref, ki_ref, o_ref,
                score_scr, bias_scr, row_scr, s_scr, acc_scr, og_scr, *, topk):
    i = pl.program_id(1)
    n_chunks = ((i + 1) * Q_TILE + KEY_CHUNK - 1) // KEY_CHUNK
    q0 = i * Q_TILE
    kf = float(topk)
    att_cols = ATT_REP * Q_TILE
    groups = KEY_CHUNK // SUBLANES

    sub = lax.broadcasted_iota(I32, (LANES, Q_TILE), 0)
    qpos = q0 + lax.broadcasted_iota(I32, (LANES, Q_TILE), 1)

    def chunk_start(c):
        return pl.multiple_of(c * KEY_CHUNK, KEY_CHUNK)

    def fold(v, op):
        return op(v.reshape(groups, SUBLANES, v.shape[-1]), axis=0)

    wi_t = wi_ref[0].T
    qi = qi_ref[0].astype(F32)
    per_tile = LANES // IDX_DIM
    qi_t = [qi[:, t * LANES:(t + 1) * LANES].T for t in range(IDX_WIDTH // LANES)]

    def head_cols(h):
        lo = (h % per_tile) * IDX_DIM
        keep = (sub >= lo) & (sub < lo + IDX_DIM)
        return jnp.where(keep, qi_t[h // per_tile], 0.0).astype(BF16)

    rhs_pairs = [jnp.concatenate([head_cols(2 * p), head_cols(2 * p + 1)], axis=1)
                 for p in range(IDX_HEADS // 2)]
    w_rows = [wi_t[h:h + 1, :] for h in range(IDX_HEADS)]
    qf = q_ref[0].astype(F32)
    q_t = [qf[:, j * LANES:(j + 1) * LANES].T for j in range(ATT_REP)]
    qg_t = []
    for g in range(ATT_KV_HEADS):
        keep = (sub >= g * HEAD_DIM) & (sub < (g + 1) * HEAD_DIM)
        qg_t.append(jnp.concatenate([jnp.where(keep, q_t[j], 0.0) for j in range(ATT_REP)],
                                    axis=1).astype(BF16))
    krow = lax.broadcasted_iota(I32, (KEY_CHUNK, Q_TILE), 0)
    kq = q0 + lax.broadcasted_iota(I32, (KEY_CHUNK, Q_TILE), 1)

    def matmul_chunk(c, stats):
        mx_a, mn_a, ge_a, gt_a = stats
        ks = chunk_start(c)
        kc = ki_ref[0, pl.ds(ks, KEY_CHUNK), :]
        acc = jnp.zeros((KEY_CHUNK, Q_TILE), F32)
        for p in range(IDX_HEADS // 2):
            s2 = jnp.dot(kc, rhs_pairs[p], preferred_element_type=F32)
            acc = acc + w_rows[2 * p] * jnp.maximum(s2[:, :Q_TILE], 0.0)
            acc = acc + w_rows[2 * p + 1] * jnp.maximum(s2[:, Q_TILE:], 0.0)
        causal = ks + krow <= kq
        sc = jnp.where(causal, acc, -jnp.inf)
        score_scr[pl.ds(ks, KEY_CHUNK), :] = sc
        kk = k_ref[0, pl.ds(ks, KEY_CHUNK), :]
        for g in range(ATT_KV_HEADS):
            s_scr[g, pl.ds(ks, KEY_CHUNK), :] = jnp.dot(kk, qg_t[g], preferred_element_type=F32)
        return (jnp.maximum(mx_a, fold(sc, jnp.max)),
                jnp.minimum(mn_a, fold(jnp.where(causal, acc, jnp.inf), jnp.min)),
                ge_a + fold(jnp.where(sc >= 0.0, 1.0, 0.0), jnp.sum),
                gt_a + fold(jnp.where(sc > 0.0, 1.0, 0.0), jnp.sum))

    stat0 = lambda v: jnp.full((SUBLANES, Q_TILE), v, F32)
    mx_a, mn_a, ge_a, gt_a = lax.fori_loop(0, n_chunks, matmul_chunk,
                                           (stat0(-jnp.inf), stat0(jnp.inf), stat0(0.0), stat0(0.0)))
    s_max = jnp.max(mx_a, axis=0, keepdims=True)
    s_min = jnp.min(mn_a, axis=0, keepdims=True)
    n_ge0 = jnp.sum(ge_a, axis=0, keepdims=True)
    n_gt0 = jnp.sum(gt_a, axis=0, keepdims=True)
    n_causal = (qpos[0:1, :] + 1).astype(F32)

    def count(pred):
        def body(c, acc):
            ks = chunk_start(c)
            for part in range(KEY_CHUNK // LANES):
                hit = jnp.where(pred(score_scr[pl.ds(ks + part * LANES, LANES), :]), 1.0, 0.0)
                acc = acc + (hit[:COUNT_ROWS] + hit[COUNT_ROWS:])
            return acc
        acc = lax.fori_loop(0, n_chunks, body, jnp.zeros((COUNT_ROWS, Q_TILE), F32))
        rows = COUNT_ROWS
        while rows > SUBLANES:
            rows //= 2
            acc = acc[:rows] + acc[rows:]
        return jnp.sum(acc, axis=0, keepdims=True)

    def count_ge(value):
        vb = jnp.broadcast_to(value, (LANES, Q_TILE))
        return count(lambda sc: sc >= vb)

    zero_thr = (n_gt0 < kf) & (n_ge0 >= kf)
    keep_all = n_causal <= kf
    positive = n_gt0 >= kf
    settled = zero_thr | keep_all
    lo0 = jnp.where(settled, 0.0, jnp.where(positive, 0.0, s_min))
    hi0 = jnp.where(settled, 0.0, jnp.where(positive, s_max, 0.0))
    c_lo0 = jnp.where(positive, n_gt0, n_causal)
    c_hi0 = jnp.where(positive, 0.0, n_ge0)
    thr0 = jnp.where(keep_all, -jnp.inf, 0.0)
    done0 = jnp.where(settled, 1.0, 0.0)

    def search_pass(it, rows):
        lo, hi, c_lo, c_hi, thr, done = rows
        frac = jnp.clip((kf - c_hi + 0.5) / (c_lo - c_hi + 1.0), SEARCH_CLIP, 1.0 - SEARCH_CLIP)
        frac = jnp.where(it < SEARCH_BISECT_PASSES, 0.5, frac)
        x = hi - (hi - lo) * frac
        c = count_ge(x)
        live = done == 0.0
        above = live & (c >= kf)
        below = live & (c < kf)
        hit = live & (c == kf)
        return (jnp.where(above, x, lo), jnp.where(below, x, hi),
                jnp.where(above, c, c_lo), jnp.where(below, c, c_hi),
                jnp.where(hit, x, thr), jnp.where(hit, 1.0, done))

    def search_cond(state):
        return (state[0] < SEARCH_MAX_PASSES) & (state[1] > 0.0)

    def search_body(state):
        it, _, rows = state
        for p in range(SEARCH_GROUP):
            rows = search_pass(it + p, rows)
        return it + SEARCH_GROUP, jnp.sum(1.0 - rows[5]), rows

    state = lax.while_loop(search_cond, search_body,
                           (jnp.int32(0), jnp.sum(1.0 - done0), (lo0, hi0, c_lo0, c_hi0, thr0, done0)))
    pending, thr_s, done_s = state[1], state[2][4], state[2][5]
    row_scr[0:1, :] = thr_s
    row_scr[1:2, :] = jnp.where(zero_thr, n_gt0, 0.0)
    row_scr[2:3, :] = jnp.where(zero_thr, n_ge0, kf)

    @pl.when(pending > 0.0)
    def _():
        def count_ge_key(key):
            return count_ge(_key_to_float(key))

        key0 = jnp.where(count_ge_key(jnp.zeros((1, Q_TILE), I32)) >= kf, 0, INT_MIN).astype(I32)

        def bit_step(b, key):
            cand = key | jnp.left_shift(jnp.int32(1), 30 - b)
            return jnp.where(count_ge_key(cand) >= kf, cand, key)

        thr_f = _key_to_float(lax.fori_loop(0, 31, bit_step, key0))
        thr_fb = jnp.broadcast_to(thr_f, (LANES, Q_TILE))
        open_row = done_s == 0.0
        row_scr[0:1, :] = jnp.where(open_row, thr_f, thr_s)
        row_scr[1:2, :] = jnp.where(open_row, count(lambda sc: sc > thr_fb), row_scr[1:2, :])
        row_scr[2:3, :] = jnp.where(open_row, count(lambda sc: sc >= thr_fb), row_scr[2:3, :])

    thr = row_scr[0:1, :]
    thr_b = jnp.broadcast_to(thr, (LANES, Q_TILE))
    thr_c = jnp.broadcast_to(thr, (KEY_CHUNK, Q_TILE))
    has_tie = (row_scr[2:3, :] > kf) & (thr > -jnp.inf)

    def masked_logits(c, g):
        ks = chunk_start(c)
        bias = bias_scr[pl.ds(ks, KEY_CHUNK), :]
        return s_scr[g, pl.ds(ks, KEY_CHUNK), :] + jnp.concatenate([bias] * ATT_REP, axis=1)

    def logit_max(c, m_acc):
        return tuple(jnp.maximum(m_acc[g], fold(masked_logits(c, g), jnp.max)) for g in range(ATT_KV_HEADS))

    m_init = tuple(jnp.full((SUBLANES, att_cols), NEG_BIG, F32) for _ in range(ATT_KV_HEADS))
    any_tie = jnp.max(jnp.where(has_tie, 1.0, 0.0)) > 0.0

    @pl.when(jnp.logical_not(any_tie))
    def _():
        def mask_chunk(c, m_acc):
            ks = chunk_start(c)
            for part in range(KEY_CHUNK // LANES):
                r0 = ks + part * LANES
                keep = (score_scr[pl.ds(r0, LANES), :] >= thr_b) & ((r0 + sub) <= qpos)
                bias_scr[pl.ds(r0, LANES), :] = jnp.where(keep, 0.0, NEG_BIG)
            return logit_max(c, m_acc)

        m_acc = lax.fori_loop(0, n_chunks, mask_chunk, m_init)
        for g in range(ATT_KV_HEADS):
            og_scr[g * SUBLANES:(g + 1) * SUBLANES, :] = m_acc[g]

    @pl.when(any_tie)
    def _():
        need = jnp.broadcast_to(kf - row_scr[1:2, :], (KEY_CHUNK, Q_TILE))
        lower = (lax.broadcasted_iota(I32, (KEY_CHUNK, KEY_CHUNK), 1)
                 <= lax.broadcasted_iota(I32, (KEY_CHUNK, KEY_CHUNK), 0))
        prefix_mat = jnp.where(lower, 1.0, 0.0).astype(BF16)

        def mask_chunk(c, carry):
            seen, m_acc = carry
            ks = chunk_start(c)
            sc = score_scr[pl.ds(ks, KEY_CHUNK), :]
            tied = sc == thr_c
            rank = seen + jnp.dot(prefix_mat, jnp.where(tied, 1.0, 0.0).astype(BF16),
                                  preferred_element_type=F32)
            chosen = (sc > thr_c) | (tied & (rank <= need))
            bias_scr[pl.ds(ks, KEY_CHUNK), :] = jnp.where(chosen & (ks + krow <= kq), 0.0, NEG_BIG)
            return rank[KEY_CHUNK - 1:KEY_CHUNK, :], logit_max(c, m_acc)

        _, m_acc = lax.fori_loop(0, n_chunks, mask_chunk, (jnp.zeros((1, Q_TILE), F32), m_init))
        for g in range(ATT_KV_HEADS):
            og_scr[g * SUBLANES:(g + 1) * SUBLANES, :] = m_acc[g]

    m = [jnp.max(og_scr[g * SUBLANES:(g + 1) * SUBLANES, :], axis=0, keepdims=True)
         for g in range(ATT_KV_HEADS)]

    acc_scr[...] = jnp.zeros(acc_scr.shape, F32)

    def pv_chunk(c, carry):
        for g in range(ATT_KV_HEADS):
            p = jnp.exp2(masked_logits(c, g) - m[g])
            acc_scr[g] += jnp.dot(vt_ref[0, c], p.astype(BF16), preferred_element_type=F32)
        return carry

    lax.fori_loop(0, n_chunks, pv_chunk, 0)
    for g in range(ATT_KV_HEADS):
        denom = acc_scr[g, KV_WIDTH:KV_WIDTH + 1, :]
        og_scr[g * HEAD_DIM:(g + 1) * HEAD_DIM, :] = acc_scr[g, g * HEAD_DIM:(g + 1) * HEAD_DIM, :] / denom

    for j in range(ATT_REP):
        o_ref[0, :, j * LANES:(j + 1) * LANES] = og_scr[:, j * Q_TILE:(j + 1) * Q_TILE].T


def _dsa(q, qi, wi, k, vt, ki):
    batch, seq, _ = q.shape
    topk = min(TOPK_MAX, seq // 4)
    att_cols = ATT_REP * Q_TILE
    per_q = lambda w: pl.BlockSpec((1, Q_TILE, w), lambda b, i: (b, i, 0))
    per_b = lambda w: pl.BlockSpec((1, seq, w), lambda b, i: (b, 0, 0))
    vt_spec = pl.BlockSpec((1, seq // KEY_CHUNK, VT_ROWS, KEY_CHUNK), lambda b, i: (b, 0, 0, 0))
    return pl.pallas_call(
        functools.partial(_dsa_kernel, topk=topk),
        grid=(batch, seq // Q_TILE),
        in_specs=[per_q(ATT_WIDTH), per_q(IDX_WIDTH), per_q(LANES), per_b(KV_WIDTH), vt_spec, per_b(LANES)],
        out_specs=per_q(ATT_WIDTH),
        out_shape=jax.ShapeDtypeStruct((batch, seq, ATT_WIDTH), F32),
        scratch_shapes=[pltpu.VMEM((seq, Q_TILE), F32),
                        pltpu.VMEM((seq, Q_TILE), F32),
                        pltpu.VMEM((SUBLANES, Q_TILE), F32),
                        pltpu.VMEM((ATT_KV_HEADS, seq, att_cols), F32),
                        pltpu.VMEM((ATT_KV_HEADS, VT_ROWS, att_cols), F32),
                        pltpu.VMEM((KV_WIDTH, att_cols), F32)],
        compiler_params=pltpu.CompilerParams(dimension_semantics=("arbitrary", "arbitrary"),
                                             vmem_limit_bytes=VMEM_LIMIT_BYTES),
        name="dsa",
    )(q, qi, wi, k, vt, ki)


def _s5_kernel(u_ref, bm_ref, cm_ref, are_ref, aim_ref, dsk_ref, wg_ref, bg_ref, o_ref, st_scr, carry_scr):
    tc, batch, width = u_ref.shape
    rows = tc * batch

    @pl.when(pl.program_id(0) == 0)
    def _():
        carry_scr[...] = jnp.zeros(carry_scr.shape, F32)

    u = u_ref[...].reshape(rows, width)
    st_scr[...] = jnp.dot(u.astype(BF16), bm_ref[...], preferred_element_type=F32)

    for part in range(SSM_LANES // SCAN_LANES):
        re_lo = part * SCAN_LANES
        im_lo = SSM_LANES + re_lo
        a_re = jnp.broadcast_to(are_ref[:, re_lo:re_lo + SCAN_LANES], (batch, SCAN_LANES))
        a_im = jnp.broadcast_to(aim_ref[:, re_lo:re_lo + SCAN_LANES], (batch, SCAN_LANES))

        def step(t, state):
            x_re, x_im = state
            r0 = pl.multiple_of(t * batch, batch)
            n_re = a_re * x_re - a_im * x_im + st_scr[pl.ds(r0, batch), re_lo:re_lo + SCAN_LANES]
            n_im = a_re * x_im + a_im * x_re + st_scr[pl.ds(r0, batch), im_lo:im_lo + SCAN_LANES]
            st_scr[pl.ds(r0, batch), re_lo:re_lo + SCAN_LANES] = n_re
            st_scr[pl.ds(r0, batch), im_lo:im_lo + SCAN_LANES] = n_im
            return n_re, n_im

        x_re, x_im = lax.fori_loop(
            0, tc, step,
            (carry_scr[:, re_lo:re_lo + SCAN_LANES], carry_scr[:, im_lo:im_lo + SCAN_LANES]), unroll=4)
        carry_scr[:, re_lo:re_lo + SCAN_LANES] = x_re
        carry_scr[:, im_lo:im_lo + SCAN_LANES] = x_im

    y = jnp.dot(st_scr[...].astype(BF16), cm_ref[...], preferred_element_type=F32) + dsk_ref[...] * u
    y = jax.nn.gelu(y)
    gate = jax.nn.sigmoid(jnp.dot(y.astype(BF16), wg_ref[...], preferred_element_type=F32) + bg_ref[...])
    o_ref[...] = (y * gate).reshape(tc, batch, width)


def _s5(u_tm, bmat, cmat, a_re, a_im, dskip, w_glu, b_glu):
    seq, batch, width = u_tm.shape
    tc = TIME_CHUNK
    full = lambda shape: pl.BlockSpec(shape, lambda t: tuple(0 for _ in shape))
    return pl.pallas_call(
        _s5_kernel,
        grid=(seq // tc,),
        in_specs=[pl.BlockSpec((tc, batch, width), lambda t: (t, 0, 0)),
                  full(bmat.shape), full(cmat.shape), full(a_re.shape), full(a_im.shape),
                  full(dskip.shape), full(w_glu.shape), full(b_glu.shape)],
        out_specs=pl.BlockSpec((tc, batch, width), lambda t: (t, 0, 0)),
        out_shape=jax.ShapeDtypeStruct((seq, batch, width), F32),
        scratch_shapes=[pltpu.VMEM((tc * batch, 2 * SSM_LANES), F32),
                        pltpu.VMEM((batch, 2 * SSM_LANES), F32)],
        compiler_params=pltpu.CompilerParams(dimension_semantics=("arbitrary",),
                                             vmem_limit_bytes=VMEM_LIMIT_BYTES),
        name="s5",
    )(u_tm, bmat, cmat, a_re, a_im, dskip, w_glu, b_glu)


def _s5_params(lam_re, lam_im, log_dt, b_re, b_im, c_re, c_im):
    dt = jnp.exp(log_dt.astype(F32))[:, None]
    mag = jnp.exp(lam_re.astype(F32) * dt)
    ang = lam_im.astype(F32) * dt
    lb_re, lb_im = mag * jnp.cos(ang), mag * jnp.sin(ang)
    den = lam_re * lam_re + lam_im * lam_im
    k_re = ((lb_re - 1.0) * lam_re + lb_im * lam_im) / den
    k_im = (lb_im * lam_re - (lb_re - 1.0) * lam_im) / den
    bb_re = k_re[:, :, None] * b_re - k_im[:, :, None] * b_im
    bb_im = k_re[:, :, None] * b_im + k_im[:, :, None] * b_re
    eye = jnp.eye(SSM_GROUPS, dtype=F32)
    drive = lambda bb: jnp.einsum('gpn,gh->gnhp', bb, eye).reshape(SSM_WIDTH, SSM_LANES)
    read = lambda cc: jnp.einsum('gnp,gh->gphn', cc, eye).reshape(SSM_LANES, SSM_WIDTH)
    bmat = jnp.concatenate([drive(bb_re), drive(bb_im)], axis=1)
    cmat = jnp.concatenate([read(c_re.astype(F32)), read(-c_im.astype(F32))], axis=0)
    return (bmat.astype(BF16), cmat.astype(BF16),
            lb_re.reshape(1, SSM_LANES), lb_im.reshape(1, SSM_LANES))


def _combine_kernel(x_ref, oa_ref, ga_ref, os_ref, gs_ref, qm_ref, gm_ref, mk_ref, mv_ref,
                    wo_ref, lg_ref, lb_ref, out_ref):
    tile = x_ref.shape[1]
    nt_dims = (((1,), (1,)), ((), ()))
    lane = lax.broadcasted_iota(I32, (tile, LANES), 1)
    low = lane < HEAD_DIM

    qm = qm_ref[0].astype(F32)
    mem_tiles = []
    for t in range(MEM_WIDTH // LANES):
        qt = qm[:, t * LANES:(t + 1) * LANES]
        mk = mk_ref[0, :, t * LANES:(t + 1) * LANES]
        mv = mv_ref[0, :, t * LANES:(t + 1) * LANES]
        halves = []
        for first in (True, False):
            qh = jnp.where(low if first else ~low, qt, 0.0).astype(BF16)
            s = lax.dot_general(qh, mk, nt_dims, preferred_element_type=F32)
            p = jnp.exp(s - jnp.max(s, axis=1, keepdims=True))
            o = jnp.dot(p.astype(BF16), mv, preferred_element_type=F32)
            halves.append(o / jnp.sum(p, axis=1, keepdims=True))
        mem_tiles.append(jnp.where(low, halves[0], halves[1]))
    o_mem = jnp.concatenate(mem_tiles, axis=1)

    c_att = (oa_ref[0] * ga_ref[0]).astype(BF16)
    c_ssm = (os_ref[0] * gs_ref[0]).astype(BF16)
    c_mem = (o_mem * gm_ref[0]).astype(BF16)
    a1, a2 = ATT_WIDTH, ATT_WIDTH + SSM_WIDTH
    sub = (jnp.dot(c_att, wo_ref[:a1, :], preferred_element_type=F32)
           + jnp.dot(c_ssm, wo_ref[a1:a2, :], preferred_element_type=F32)
           + jnp.dot(c_mem, wo_ref[a2:, :], preferred_element_type=F32))
    h = DN_ALPHA * x_ref[0] + sub
    mu = jnp.mean(h, axis=1, keepdims=True)
    d = h - mu
    var = jnp.mean(d * d, axis=1, keepdims=True)
    out_ref[0] = d * lax.rsqrt(var + LN_EPS) * lg_ref[...] + lb_ref[...]


def _combine(x, o_att, g_att, o_ssm, g_ssm, qm, g_mem, mk, mv, wo_bf16, ln_g, ln_b):
    batch, seq, _ = x.shape
    tile = TOKEN_TILE
    tok = lambda w: pl.BlockSpec((1, tile, w), lambda b, j: (b, j, 0))
    per_b = pl.BlockSpec((1, N_MEM, MEM_WIDTH), lambda b, j: (b, 0, 0))
    const = lambda shape: pl.BlockSpec(shape, lambda b, j: tuple(0 for _ in shape))
    return pl.pallas_call(
        _combine_kernel,
        grid=(batch, seq // tile),
        in_specs=[tok(D_MODEL), tok(ATT_WIDTH), tok(ATT_WIDTH), tok(SSM_WIDTH), tok(SSM_WIDTH),
                  tok(MEM_WIDTH), tok(MEM_WIDTH), per_b, per_b,
                  const((D_MODEL, D_MODEL)), const((1, D_MODEL)), const((1, D_MODEL))],
        out_specs=tok(D_MODEL),
        out_shape=jax.ShapeDtypeStruct((batch, seq, D_MODEL), x.dtype),
        compiler_params=pltpu.CompilerParams(dimension_semantics=("arbitrary", "arbitrary"),
                                             vmem_limit_bytes=VMEM_LIMIT_BYTES),
        name="combine",
    )(x, o_att, g_att, o_ssm, g_ssm, qm, g_mem, mk, mv, wo_bf16, ln_g, ln_b)


def _reordered_w_in(w_in):
    offs = np.cumsum([0, ATT_WIDTH, KV_WIDTH, KV_WIDTH, IDX_WIDTH, IDX_DIM, IDX_HEADS,
                      ATT_WIDTH, SSM_WIDTH, SSM_WIDTH, MEM_WIDTH, MEM_WIDTH])
    part = lambda n: w_in[:, int(offs[n]):int(offs[n + 1])]
    perm = _att_perm()
    k_idx = part(4)
    w_idx = part(5)
    zeros = jnp.zeros((D_MODEL, LANES - IDX_HEADS), w_in.dtype)
    cols = [part(0)[:, perm], part(1), part(2), part(3),
            jnp.tile(k_idx, (1, LANES // IDX_DIM)),
            jnp.concatenate([w_idx, zeros], axis=1),
            part(6)[:, perm], part(7), part(8), part(9), part(10)]
    return jnp.concatenate(cols, axis=1).astype(BF16)


def kernel(x, mem, w_in, w_mem_kv, lam_re, lam_im, log_dt, b_re, b_im, c_re, c_im, d_skip, w_glu, b_glu,
           w_out, ln_g, ln_b):
    batch, seq, _ = x.shape
    assert seq % TOKEN_TILE == 0 and seq % Q_TILE == 0 and seq % TIME_CHUNK == 0

    w_all = _reordered_w_in(w_in)
    perm = _att_perm()
    wo = jnp.concatenate([w_out[:ATT_WIDTH][perm], w_out[ATT_WIDTH:]], axis=0).astype(BF16)
    bmat, cmat, a_re, a_im = _s5_params(lam_re, lam_im, log_dt, b_re, b_im, c_re, c_im)
    tabs = _rope_tables(seq, HEAD_DIM) + _rope_tables(seq, IDX_DIM)

    mk, mv = _memkv(mem.reshape(batch * N_MEM, D_MODEL), w_mem_kv.astype(BF16))
    mk = mk.reshape(batch, N_MEM, MEM_WIDTH)
    mv = mv.reshape(batch, N_MEM, MEM_WIDTH)

    q, k, vt, qi, ki, wi, g_att, u, g_ssm, qm, g_mem = _inproj(x, w_all, tabs)
    o_att = _dsa(q, qi, wi, k, vt, ki)

    u_tm = jnp.swapaxes(u, 0, 1)
    o_ssm_tm = _s5(u_tm, bmat, cmat, a_re, a_im,
                   d_skip.reshape(1, SSM_WIDTH).astype(F32), w_glu.astype(BF16),
                   b_glu.reshape(1, SSM_WIDTH).astype(F32))
    o_ssm = jnp.swapaxes(o_ssm_tm, 0, 1)

    return _combine(x, o_att, g_att, o_ssm, g_ssm, qm, g_mem, mk, mv, wo,
                    ln_g.reshape(1, D_MODEL).astype(F32), ln_b.reshape(1, D_MODEL).astype(F32))
```

```python
import functools
import math

import jax
import jax.numpy as jnp
import numpy as np
from jax import lax
from jax.experimental import pallas as pl
from jax.experimental.pallas import tpu as pltpu

F32 = jnp.float32
BF16 = jnp.bfloat16
I32 = jnp.int32

D_MODEL = 1024
N_MEM = 256
HEAD_DIM = 64
ATT_HEADS = 8
ATT_KV_HEADS = 2
ATT_REP = ATT_HEADS // ATT_KV_HEADS
ATT_WIDTH = ATT_HEADS * HEAD_DIM
KV_WIDTH = ATT_KV_HEADS * HEAD_DIM
IDX_HEADS = 8
IDX_DIM = 32
IDX_WIDTH = IDX_HEADS * IDX_DIM
TOPK_MAX = 256
SSM_WIDTH = D_MODEL // 4
SSM_GROUP = 16
SSM_GROUPS = SSM_WIDTH // SSM_GROUP
SSM_STATE = 64
SSM_LANES = SSM_GROUPS * SSM_STATE
MEM_HEADS = 4
MEM_WIDTH = MEM_HEADS * HEAD_DIM
ROPE_THETA = 500000.0
ROPE_FRAC = 4
LN_EPS = 1e-5
DEPTH = 1
DN_ALPHA = (2.0 * DEPTH) ** 0.25
ATT_SCALE = HEAD_DIM ** -0.5
IDX_SCALE = IDX_HEADS ** -0.5 * IDX_DIM ** -0.5
LOG2_E = math.log2(math.e)

LANES = 128
SUBLANES = 8
BF16_ROWS = 16
VMEM_LIMIT_BYTES = 48 * 1024 * 1024
TOKEN_TILE = 512
Q_TILE = 256
KEY_CHUNK = 256
COUNT_ROWS = 64
VT_ROWS = KV_WIDTH + BF16_ROWS
TIME_CHUNK = 64
SCAN_LANES = 256
NEG_BIG = -1e30
DENOM_FLOOR = 2.0 ** -100
SEARCH_MAX_PASSES = 24
SEARCH_GROUP = 4
SEARCH_BISECT_PASSES = 2
SEARCH_CLIP = 0.1

INT_MIN = -2147483648
KEY_POS_INF = 0x7F800000
KEY_NEG_INF = INT_MIN + 0x7FFFFF

_G_Q, _G_K, _G_V, _G_QI, _G_KI, _G_W, _G_GA, _G_U, _G_GS, _G_QM, _G_GM = range(11)
_GROUP_WIDTHS = [ATT_WIDTH, KV_WIDTH, KV_WIDTH, IDX_WIDTH, LANES, LANES, ATT_WIDTH,
                 SSM_WIDTH, SSM_WIDTH, MEM_WIDTH, MEM_WIDTH]
_GROUP_OFFS = [int(v) for v in np.cumsum([0] + _GROUP_WIDTHS)]
IN_COLS = _GROUP_OFFS[-1]


def _att_perm():
    cols = []
    for j in range(ATT_REP):
        for g in range(ATT_KV_HEADS):
            h = g * ATT_REP + j
            cols.extend(range(h * HEAD_DIM, (h + 1) * HEAD_DIM))
    return np.asarray(cols, dtype=np.int32)


def _rope_tables(seq, period, dtype=F32):
    r = period // ROPE_FRAC
    half = r // 2
    inv = ROPE_THETA ** (-jnp.arange(0, half, dtype=F32) * 2.0 / r)
    ang = jnp.arange(seq).astype(F32)[:, None] * inv[None, :]
    cos, sin = jnp.cos(ang), jnp.sin(ang)
    ones = jnp.ones((seq, period - r), F32)
    zeros = jnp.zeros((seq, period - r), F32)
    c = jnp.concatenate([cos, cos, ones], axis=1)
    s = jnp.concatenate([-sin, sin, zeros], axis=1)
    reps = LANES // period
    return jnp.tile(c, (1, reps)).astype(dtype), jnp.tile(s, (1, reps)).astype(dtype)


def _rope(z, cos_t, sin_t, period):
    width = z.shape[1]
    half = period // ROPE_FRAC // 2
    reps = width // LANES
    if reps > 1:
        cos_t = jnp.concatenate([cos_t] * reps, axis=1)
        sin_t = jnp.concatenate([sin_t] * reps, axis=1)
    lane = lax.broadcasted_iota(I32, z.shape, 1)
    first = (lane & (period - 1)) < half
    up = pltpu.roll(z, half, 1)
    down = pltpu.roll(z, width - half, 1)
    return z * cos_t + jnp.where(first, down, up) * sin_t


def _memkv_kernel(mem_ref, w_ref, mk_ref, mv_ref):
    z = jnp.dot(mem_ref[...].astype(BF16), w_ref[...], preferred_element_type=F32)
    mk_ref[...] = z[:, :MEM_WIDTH].astype(BF16)
    mv_ref[...] = z[:, MEM_WIDTH:].astype(BF16)


def _memkv(mem2d, w_bf16):
    rows = mem2d.shape[0]
    tile = TOKEN_TILE
    return pl.pallas_call(
        _memkv_kernel,
        grid=(rows // tile,),
        in_specs=[pl.BlockSpec((tile, D_MODEL), lambda i: (i, 0)),
                  pl.BlockSpec((D_MODEL, 2 * MEM_WIDTH), lambda i: (0, 0))],
        out_specs=[pl.BlockSpec((tile, MEM_WIDTH), lambda i: (i, 0)),
                   pl.BlockSpec((tile, MEM_WIDTH), lambda i: (i, 0))],
        out_shape=[jax.ShapeDtypeStruct((rows, MEM_WIDTH), BF16),
                   jax.ShapeDtypeStruct((rows, MEM_WIDTH), BF16)],
        compiler_params=pltpu.CompilerParams(dimension_semantics=("arbitrary",),
                                             vmem_limit_bytes=VMEM_LIMIT_BYTES),
        name="memkv",
    )(mem2d, w_bf16)


def _inproj_kernel(x_ref, w_ref, ca_ref, sa_ref, ci_ref, si_ref,
                   q_ref, k_ref, vt_ref, qi_ref, ki_ref, wi_ref, ga_ref, u_ref, gs_ref, qm_ref, gm_ref):
    tile = x_ref.shape[1]
    pos0 = pl.multiple_of(pl.program_id(1) * tile, tile)
    xb = x_ref[0].astype(BF16)

    def proj(group):
        lo, hi = _GROUP_OFFS[group], _GROUP_OFFS[group + 1]
        return jnp.dot(xb, w_ref[:, lo:hi], preferred_element_type=F32)

    ca = ca_ref[pl.ds(pos0, tile), :]
    sa = sa_ref[pl.ds(pos0, tile), :]
    ci = ci_ref[pl.ds(pos0, tile), :]
    si = si_ref[pl.ds(pos0, tile), :]

    def store_transposed(ref, z):
        for qb in range(tile // Q_TILE):
            for t in range(z.shape[1] // LANES):
                zt = z[qb * Q_TILE:(qb + 1) * Q_TILE, t * LANES:(t + 1) * LANES].T
                ref[0, qb, :, t * Q_TILE:(t + 1) * Q_TILE] = zt.astype(ref.dtype)

    store_transposed(q_ref, _rope(proj(_G_Q), ca, sa, HEAD_DIM) * (ATT_SCALE * LOG2_E))
    k_ref[0] = _rope(proj(_G_K), ca, sa, HEAD_DIM).astype(BF16)
    v = proj(_G_V)
    for c in range(tile // KEY_CHUNK):
        vt_ref[0, c, :KV_WIDTH, :] = v[c * KEY_CHUNK:(c + 1) * KEY_CHUNK, :].T.astype(BF16)
        vt_ref[0, c, KV_WIDTH:, :] = jnp.ones((BF16_ROWS, KEY_CHUNK), BF16)
    store_transposed(qi_ref, _rope(proj(_G_QI), ci, si, IDX_DIM))
    ki_ref[0] = _rope(proj(_G_KI), ci, si, IDX_DIM).astype(BF16)
    wi = proj(_G_W) * IDX_SCALE
    for qb in range(tile // Q_TILE):
        wi_ref[0, qb] = wi[qb * Q_TILE:(qb + 1) * Q_TILE, :].T[:IDX_HEADS, :]
    ga_ref[0] = jax.nn.silu(proj(_G_GA))
    u_ref[0] = proj(_G_U)
    gs_ref[0] = jax.nn.silu(proj(_G_GS))
    qm_ref[0] = (proj(_G_QM) * ATT_SCALE).astype(BF16)
    gm_ref[0] = jax.nn.silu(proj(_G_GM))


def _inproj(x, w_bf16, tabs):
    batch, seq, _ = x.shape
    tile = TOKEN_TILE
    widths = _GROUP_WIDTHS
    dtypes = [BF16, BF16, BF16, BF16, BF16, F32, F32, F32, F32, BF16, F32]
    tab_spec = pl.BlockSpec((seq, LANES), lambda b, j: (0, 0))
    out_specs = [pl.BlockSpec((1, tile, w), lambda b, j: (b, j, 0)) for w in widths]
    out_shape = [jax.ShapeDtypeStruct((batch, seq, w), d) for w, d in zip(widths, dtypes)]
    out_specs[_G_V] = pl.BlockSpec((1, tile // KEY_CHUNK, VT_ROWS, KEY_CHUNK), lambda b, j: (b, j, 0, 0))
    out_shape[_G_V] = jax.ShapeDtypeStruct((batch, seq // KEY_CHUNK, VT_ROWS, KEY_CHUNK), BF16)
    for group, rows, dtype in ((_G_Q, LANES, BF16), (_G_QI, LANES, BF16), (_G_W, IDX_HEADS, F32)):
        cols = Q_TILE * max(widths[group] // LANES, 1)
        out_specs[group] = pl.BlockSpec((1, tile // Q_TILE, rows, cols), lambda b, j: (b, j, 0, 0))
        out_shape[group] = jax.ShapeDtypeStruct((batch, seq // Q_TILE, rows, cols), dtype)
    return pl.pallas_call(
        _inproj_kernel,
        grid=(batch, seq // tile),
        in_specs=[pl.BlockSpec((1, tile, D_MODEL), lambda b, j: (b, j, 0)),
                  pl.BlockSpec((D_MODEL, IN_COLS), lambda b, j: (0, 0)),
                  tab_spec, tab_spec, tab_spec, tab_spec],
        out_specs=out_specs,
        out_shape=out_shape,
        compiler_params=pltpu.CompilerParams(dimension_semantics=("arbitrary", "arbitrary"),
                                             vmem_limit_bytes=VMEM_LIMIT_BYTES),
        name="inproj",
    )(x, w_bf16, *tabs)


def _key_to_float(key):
    bits = jnp.where(key >= 0, key, key ^ 0x7FFFFFFF)
    val = lax.bitcast_convert_type(bits, F32)
    val = jnp.where(key <= KEY_NEG_INF, -jnp.inf, val)
    return jnp.where(key >= KEY_POS_INF, jnp.inf, val)


def _dsa_kernel(q_ref, qi_ref, wi_ref, k_ref, vt_ref, ki_ref, o_ref,
                score_scr, bias_scr, row_scr, s_scr, acc_scr, og_scr, *, topk):
    i = pl.program_id(1)
    n_chunks = ((i + 1) * Q_TILE + KEY_CHUNK - 1) // KEY_CHUNK
    q0 = i * Q_TILE
    kf = float(topk)
    att_cols = ATT_REP * Q_TILE
    groups = KEY_CHUNK // SUBLANES

    def chunk_start(c):
        return pl.multiple_of(c * KEY_CHUNK, KEY_CHUNK)

    def fold(v, op):
        return op(v.reshape(groups, SUBLANES, v.shape[-1]), axis=0)

    def only_rows(x, lo, n):
        parts = [jnp.zeros((lo, x.shape[1]), x.dtype)] if lo else []
        parts.append(x[lo:lo + n])
        if lo + n < x.shape[0]:
            parts.append(jnp.zeros((x.shape[0] - lo - n, x.shape[1]), x.dtype))
        return jnp.concatenate(parts, axis=0)

    qi_t = qi_ref[0, 0]
    per_tile = LANES // IDX_DIM

    def head_cols(h):
        t = h // per_tile
        return only_rows(qi_t[:, t * Q_TILE:(t + 1) * Q_TILE], (h % per_tile) * IDX_DIM, IDX_DIM)

    rhs_pairs = [jnp.concatenate([head_cols(2 * p), head_cols(2 * p + 1)], axis=1)
                 for p in range(IDX_HEADS // 2)]
    w_rows = [wi_ref[0, 0, h:h + 1, :] for h in range(IDX_HEADS)]
    q_t = q_ref[0, 0]
    qg_t = [only_rows(q_t, g * HEAD_DIM, HEAD_DIM) for g in range(ATT_KV_HEADS)]
    krow = lax.broadcasted_iota(I32, (KEY_CHUNK, Q_TILE), 0)
    kq = q0 + lax.broadcasted_iota(I32, (KEY_CHUNK, Q_TILE), 1)

    def matmul_chunk(c, stats):
        mx_a, mn_a, ge_a, gt_a, top_a = stats
        ks = chunk_start(c)
        kc = ki_ref[0, pl.ds(ks, KEY_CHUNK), :]
        acc = jnp.zeros((KEY_CHUNK, Q_TILE), F32)
        for p in range(IDX_HEADS // 2):
            s2 = jnp.dot(kc, rhs_pairs[p], preferred_element_type=F32)
            acc = acc + w_rows[2 * p] * jnp.maximum(s2[:, :Q_TILE], 0.0)
            acc = acc + w_rows[2 * p + 1] * jnp.maximum(s2[:, Q_TILE:], 0.0)
        causal = ks + krow <= kq
        sc = jnp.where(causal, acc, -jnp.inf)
        score_scr[pl.ds(ks, KEY_CHUNK), :] = sc
        kk = k_ref[0, pl.ds(ks, KEY_CHUNK), :]
        tops = []
        for g in range(ATT_KV_HEADS):
            s = jnp.dot(kk, qg_t[g], preferred_element_type=F32)
            s_scr[g, pl.ds(ks, KEY_CHUNK), :] = s
            tops.append(jnp.maximum(top_a[g], fold(s, jnp.max)))
        return (jnp.maximum(mx_a, fold(sc, jnp.max)),
                jnp.minimum(mn_a, fold(jnp.where(causal, acc, jnp.inf), jnp.min)),
                ge_a + fold(jnp.where(sc >= 0.0, 1.0, 0.0), jnp.sum),
                gt_a + fold(jnp.where(sc > 0.0, 1.0, 0.0), jnp.sum),
                tuple(tops))

    stat0 = lambda v: jnp.full((SUBLANES, Q_TILE), v, F32)
    top0 = tuple(jnp.full((SUBLANES, att_cols), NEG_BIG, F32) for _ in range(ATT_KV_HEADS))
    mx_a, mn_a, ge_a, gt_a, top_a = lax.fori_loop(
        0, n_chunks, matmul_chunk, (stat0(-jnp.inf), stat0(jnp.inf), stat0(0.0), stat0(0.0), top0))
    m_top = [jnp.max(top_a[g], axis=0, keepdims=True) for g in range(ATT_KV_HEADS)]
    s_max = jnp.max(mx_a, axis=0, keepdims=True)
    s_min = jnp.min(mn_a, axis=0, keepdims=True)
    n_ge0 = jnp.sum(ge_a, axis=0, keepdims=True)
    n_gt0 = jnp.sum(gt_a, axis=0, keepdims=True)
    n_causal = (q0 + 1 + lax.broadcasted_iota(I32, (1, Q_TILE), 1)).astype(F32)

    def count(pred):
        def body(c, acc):
            ks = chunk_start(c)
            for part in range(KEY_CHUNK // LANES):
                hit = jnp.where(pred(score_scr[pl.ds(ks + part * LANES, LANES), :]), 1.0, 0.0)
                acc = acc + (hit[:COUNT_ROWS] + hit[COUNT_ROWS:])
            return acc
        acc = lax.fori_loop(0, n_chunks, body, jnp.zeros((COUNT_ROWS, Q_TILE), F32))
        rows = COUNT_ROWS
        while rows > SUBLANES:
            rows //= 2
            acc = acc[:rows] + acc[rows:]
        return jnp.sum(acc, axis=0, keepdims=True)

    def count_ge(value):
        vb = jnp.broadcast_to(value, (LANES, Q_TILE))
        return count(lambda sc: sc >= vb)

    zero_thr = (n_gt0 < kf) & (n_ge0 >= kf)
    keep_all = n_causal <= kf
    positive = n_gt0 >= kf
    settled = zero_thr | keep_all
    lo0 = jnp.where(settled, 0.0, jnp.where(positive, 0.0, s_min))
    hi0 = jnp.where(settled, 0.0, jnp.where(positive, s_max, 0.0))
    c_lo0 = jnp.where(positive, n_gt0, n_causal)
    c_hi0 = jnp.where(positive, 0.0, n_ge0)
    thr0 = jnp.where(keep_all, -jnp.inf, 0.0)
    done0 = jnp.where(settled, 1.0, 0.0)

    def search_pass(it, rows):
        lo, hi, c_lo, c_hi, thr, done = rows
        frac = jnp.clip((kf - c_hi + 0.5) / (c_lo - c_hi + 1.0), SEARCH_CLIP, 1.0 - SEARCH_CLIP)
        frac = jnp.where(it < SEARCH_BISECT_PASSES, 0.5, frac)
        x = hi - (hi - lo) * frac
        c = count_ge(x)
        live = done == 0.0
        above = live & (c >= kf)
        below = live & (c < kf)
        hit = live & (c == kf)
        return (jnp.where(above, x, lo), jnp.where(below, x, hi),
                jnp.where(above, c, c_lo), jnp.where(below, c, c_hi),
                jnp.where(hit, x, thr), jnp.where(hit, 1.0, done))

    def search_cond(state):
        return (state[0] < SEARCH_MAX_PASSES) & (state[1] > 0.0)

    def search_body(state):
        it, _, rows = state
        for p in range(SEARCH_GROUP):
            rows = search_pass(it + p, rows)
        return it + SEARCH_GROUP, jnp.sum(1.0 - rows[5]), rows

    state = lax.while_loop(search_cond, search_body,
                           (jnp.int32(0), jnp.sum(1.0 - done0), (lo0, hi0, c_lo0, c_hi0, thr0, done0)))
    pending, thr_s, done_s = state[1], state[2][4], state[2][5]
    row_scr[0:1, :] = thr_s
    row_scr[1:2, :] = jnp.where(zero_thr, n_gt0, 0.0)
    row_scr[2:3, :] = jnp.where(zero_thr, n_ge0, kf)

    @pl.when(pending > 0.0)
    def _():
        def count_ge_key(key):
            return count_ge(_key_to_float(key))

        key0 = jnp.where(count_ge_key(jnp.zeros((1, Q_TILE), I32)) >= kf, 0, INT_MIN).astype(I32)

        def bit_step(b, key):
            cand = key | jnp.left_shift(jnp.int32(1), 30 - b)
            return jnp.where(count_ge_key(cand) >= kf, cand, key)

        thr_f = _key_to_float(lax.fori_loop(0, 31, bit_step, key0))
        thr_fb = jnp.broadcast_to(thr_f, (LANES, Q_TILE))
        open_row = done_s == 0.0
        row_scr[0:1, :] = jnp.where(open_row, thr_f, thr_s)
        row_scr[1:2, :] = jnp.where(open_row, count(lambda sc: sc > thr_fb), row_scr[1:2, :])
        row_scr[2:3, :] = jnp.where(open_row, count(lambda sc: sc >= thr_fb), row_scr[2:3, :])

    thr = row_scr[0:1, :]
    thr_c = jnp.broadcast_to(thr, (KEY_CHUNK, Q_TILE))
    has_tie = (row_scr[2:3, :] > kf) & (thr > -jnp.inf)
    any_tie = jnp.max(jnp.where(has_tie, 1.0, 0.0)) > 0.0

    def plain_bias(c, carry):
        ks = chunk_start(c)
        keep = (score_scr[pl.ds(ks, KEY_CHUNK), :] >= thr_c) & (ks + krow <= kq)
        return jnp.where(keep, 0.0, NEG_BIG), carry

    def make_tie_bias():
        need = jnp.broadcast_to(kf - row_scr[1:2, :], (KEY_CHUNK, Q_TILE))
        lower = (lax.broadcasted_iota(I32, (KEY_CHUNK, KEY_CHUNK), 1)
                 <= lax.broadcasted_iota(I32, (KEY_CHUNK, KEY_CHUNK), 0))
        prefix_mat = jnp.where(lower, 1.0, 0.0).astype(BF16)

        def tie_bias(c, seen):
            ks = chunk_start(c)
            sc = score_scr[pl.ds(ks, KEY_CHUNK), :]
            tied = sc == thr_c
            rank = seen + jnp.dot(prefix_mat, jnp.where(tied, 1.0, 0.0).astype(BF16),
                                  preferred_element_type=F32)
            chosen = (sc > thr_c) | (tied & (rank <= need))
            return jnp.where(chosen & (ks + krow <= kq), 0.0, NEG_BIG), rank[KEY_CHUNK - 1:KEY_CHUNK, :]

        return tie_bias

    no_ties_seen = jnp.zeros((1, Q_TILE), F32)

    def pv_pass(bias_of, m, carry0):
        acc_scr[...] = jnp.zeros(acc_scr.shape, F32)

        def pv_chunk(c, carry):
            bias, carry = bias_of(c, carry)
            bias4 = jnp.concatenate([bias] * ATT_REP, axis=1)
            for g in range(ATT_KV_HEADS):
                p = jnp.exp2(s_scr[g, pl.ds(chunk_start(c), KEY_CHUNK), :] + bias4 - m[g])
                acc_scr[g] += jnp.dot(vt_ref[0, c], p.astype(BF16), preferred_element_type=F32)
            return carry

        lax.fori_loop(0, n_chunks, pv_chunk, carry0)

    @pl.when(jnp.logical_not(any_tie))
    def _():
        pv_pass(plain_bias, m_top, 0)

    @pl.when(any_tie)
    def _():
        pv_pass(make_tie_bias(), m_top, no_ties_seen)

    d_min = jnp.minimum(jnp.min(acc_scr[0, KV_WIDTH:KV_WIDTH + 1, :]), jnp.min(acc_scr[1, KV_WIDTH:KV_WIDTH + 1, :]))

    @pl.when(jnp.logical_not(d_min >= DENOM_FLOOR))
    def _():
        tie_bias = make_tie_bias()

        def mask_chunk(c, carry):
            seen, m_acc = carry
            ks = chunk_start(c)
            bias, seen = tie_bias(c, seen)
            bias_scr[pl.ds(ks, KEY_CHUNK), :] = bias
            bias4 = jnp.concatenate([bias] * ATT_REP, axis=1)
            return seen, tuple(jnp.maximum(m_acc[g], fold(s_scr[g, pl.ds(ks, KEY_CHUNK), :] + bias4, jnp.max))
                               for g in range(ATT_KV_HEADS))

        m_init = tuple(jnp.full((SUBLANES, att_cols), NEG_BIG, F32) for _ in range(ATT_KV_HEADS))
        _, m_acc = lax.fori_loop(0, n_chunks, mask_chunk, (no_ties_seen, m_init))
        m_sel = [jnp.max(m_acc[g], axis=0, keepdims=True) for g in range(ATT_KV_HEADS)]
        pv_pass(lambda c, carry: (bias_scr[pl.ds(chunk_start(c), KEY_CHUNK), :], carry), m_sel, 0)

    for g in range(ATT_KV_HEADS):
        denom = acc_scr[g, KV_WIDTH:KV_WIDTH + 1, :]
        og_scr[g * HEAD_DIM:(g + 1) * HEAD_DIM, :] = acc_scr[g, g * HEAD_DIM:(g + 1) * HEAD_DIM, :] / denom

    for j in range(ATT_REP):
        o_ref[0, :, j * LANES:(j + 1) * LANES] = og_scr[:, j * Q_TILE:(j + 1) * Q_TILE].T


def _dsa(q_t, qi_t, wi_t, k, vt, ki):
    batch, seq, _ = k.shape
    topk = min(TOPK_MAX, seq // 4)
    att_cols = ATT_REP * Q_TILE
    per_q = lambda a: pl.BlockSpec((1, 1) + a.shape[2:], lambda b, i: (b, i, 0, 0))
    per_b = lambda w: pl.BlockSpec((1, seq, w), lambda b, i: (b, 0, 0))
    vt_spec = pl.BlockSpec((1, seq // KEY_CHUNK, VT_ROWS, KEY_CHUNK), lambda b, i: (b, 0, 0, 0))
    return pl.pallas_call(
        functools.partial(_dsa_kernel, topk=topk),
        grid=(batch, seq // Q_TILE),
        in_specs=[per_q(q_t), per_q(qi_t), per_q(wi_t), per_b(KV_WIDTH), vt_spec, per_b(LANES)],
        out_specs=pl.BlockSpec((1, Q_TILE, ATT_WIDTH), lambda b, i: (b, i, 0)),
        out_shape=jax.ShapeDtypeStruct((batch, seq, ATT_WIDTH), F32),
        scratch_shapes=[pltpu.VMEM((seq, Q_TILE), F32),
                        pltpu.VMEM((seq, Q_TILE), F32),
                        pltpu.VMEM((SUBLANES, Q_TILE), F32),
                        pltpu.VMEM((ATT_KV_HEADS, seq, att_cols), F32),
                        pltpu.VMEM((ATT_KV_HEADS, VT_ROWS, att_cols), F32),
                        pltpu.VMEM((KV_WIDTH, att_cols), F32)],
        compiler_params=pltpu.CompilerParams(dimension_semantics=("arbitrary", "arbitrary"),
                                             vmem_limit_bytes=VMEM_LIMIT_BYTES),
        name="dsa",
    )(q_t, qi_t, wi_t, k, vt, ki)


def _s5_kernel(u_ref, bm_ref, cm_ref, are_ref, aim_ref, dsk_ref, wg_ref, bg_ref, o_ref,
               st_scr, xb_scr, carry_scr):
    tc, batch, width = u_ref.shape
    rows = tc * batch

    @pl.when(pl.program_id(0) == 0)
    def _():
        carry_scr[...] = jnp.zeros(carry_scr.shape, F32)

    u = u_ref[...].reshape(rows, width)
    st_scr[...] = jnp.dot(u.astype(BF16), bm_ref[...], preferred_element_type=F32)

    for part in range(SSM_LANES // SCAN_LANES):
        re_lo = part * SCAN_LANES
        im_lo = SSM_LANES + re_lo
        a_re = jnp.broadcast_to(are_ref[:, re_lo:re_lo + SCAN_LANES], (batch, SCAN_LANES))
        a_im = jnp.broadcast_to(aim_ref[:, re_lo:re_lo + SCAN_LANES], (batch, SCAN_LANES))

        def step(t, state):
            x_re, x_im = state
            r0 = pl.multiple_of(t * batch, batch)
            n_re = a_re * x_re - a_im * x_im + st_scr[pl.ds(r0, batch), re_lo:re_lo + SCAN_LANES]
            n_im = a_re * x_im + a_im * x_re + st_scr[pl.ds(r0, batch), im_lo:im_lo + SCAN_LANES]
            xb_scr[pl.ds(r0, batch), re_lo:re_lo + SCAN_LANES] = n_re.astype(BF16)
            xb_scr[pl.ds(r0, batch), im_lo:im_lo + SCAN_LANES] = n_im.astype(BF16)
            return n_re, n_im

        x_re, x_im = lax.fori_loop(
            0, tc, step,
            (carry_scr[:, re_lo:re_lo + SCAN_LANES], carry_scr[:, im_lo:im_lo + SCAN_LANES]), unroll=4)
        carry_scr[:, re_lo:re_lo + SCAN_LANES] = x_re
        carry_scr[:, im_lo:im_lo + SCAN_LANES] = x_im

    y = jnp.dot(xb_scr[...], cm_ref[...], preferred_element_type=F32) + dsk_ref[...] * u
    y = jax.nn.gelu(y)
    gate = jax.nn.sigmoid(jnp.dot(y.astype(BF16), wg_ref[...], preferred_element_type=F32) + bg_ref[...])
    o_ref[...] = (y * gate).reshape(tc, batch, width)


def _s5(u_tm, bmat, cmat, a_re, a_im, dskip, w_glu, b_glu):
    seq, batch, width = u_tm.shape
    tc = TIME_CHUNK
    full = lambda shape: pl.BlockSpec(shape, lambda t: tuple(0 for _ in shape))
    return pl.pallas_call(
        _s5_kernel,
        grid=(seq // tc,),
        in_specs=[pl.BlockSpec((tc, batch, width), lambda t: (t, 0, 0)),
                  full(bmat.shape), full(cmat.shape), full(a_re.shape), full(a_im.shape),
                  full(dskip.shape), full(w_glu.shape), full(b_glu.shape)],
        out_specs=pl.BlockSpec((tc, batch, width), lambda t: (t, 0, 0)),
        out_shape=jax.ShapeDtypeStruct((seq, batch, width), F32),
        scratch_shapes=[pltpu.VMEM((tc * batch, 2 * SSM_LANES), F32),
                        pltpu.VMEM((tc * batch, 2 * SSM_LANES), BF16),
                        pltpu.VMEM((batch, 2 * SSM_LANES), F32)],
        compiler_params=pltpu.CompilerParams(dimension_semantics=("arbitrary",),
                                             vmem_limit_bytes=VMEM_LIMIT_BYTES),
        name="s5",
    )(u_tm, bmat, cmat, a_re, a_im, dskip, w_glu, b_glu)


def _s5_params(lam_re, lam_im, log_dt, b_re, b_im, c_re, c_im):
    dt = jnp.exp(log_dt.astype(F32))[:, None]
    mag = jnp.exp(lam_re.astype(F32) * dt)
    ang = lam_im.astype(F32) * dt
    lb_re, lb_im = mag * jnp.cos(ang), mag * jnp.sin(ang)
    den = lam_re * lam_re + lam_im * lam_im
    k_re = ((lb_re - 1.0) * lam_re + lb_im * lam_im) / den
    k_im = (lb_im * lam_re - (lb_re - 1.0) * lam_im) / den
    bb_re = k_re[:, :, None] * b_re - k_im[:, :, None] * b_im
    bb_im = k_re[:, :, None] * b_im + k_im[:, :, None] * b_re
    eye = jnp.eye(SSM_GROUPS, dtype=F32)
    drive = lambda bb: jnp.einsum('gpn,gh->gnhp', bb, eye).reshape(SSM_WIDTH, SSM_LANES)
    read = lambda cc: jnp.einsum('gnp,gh->gphn', cc, eye).reshape(SSM_LANES, SSM_WIDTH)
    bmat = jnp.concatenate([drive(bb_re), drive(bb_im)], axis=1)
    cmat = jnp.concatenate([read(c_re.astype(F32)), read(-c_im.astype(F32))], axis=0)
    return (bmat.astype(BF16), cmat.astype(BF16),
            lb_re.reshape(1, SSM_LANES), lb_im.reshape(1, SSM_LANES))


def _combine_kernel(x_ref, oa_ref, ga_ref, os_ref, gs_ref, qm_ref, gm_ref, mk_ref, mv_ref,
                    wo_ref, lg_ref, lb_ref, out_ref):
    tile = x_ref.shape[1]
    nt_dims = (((1,), (1,)), ((), ()))
    lane = lax.broadcasted_iota(I32, (tile, LANES), 1)
    low = lane < HEAD_DIM

    qm = qm_ref[0].astype(F32)
    mem_tiles = []
    for t in range(MEM_WIDTH // LANES):
        qt = qm[:, t * LANES:(t + 1) * LANES]
        mk = mk_ref[0, :, t * LANES:(t + 1) * LANES]
        mv = mv_ref[0, :, t * LANES:(t + 1) * LANES]
        halves = []
        for first in (True, False):
            qh = jnp.where(low if first else ~low, qt, 0.0).astype(BF16)
            s = lax.dot_general(qh, mk, nt_dims, preferred_element_type=F32)
            p = jnp.exp(s - jnp.max(s, axis=1, keepdims=True))
            o = jnp.dot(p.astype(BF16), mv, preferred_element_type=F32)
            halves.append(o / jnp.sum(p, axis=1, keepdims=True))
        mem_tiles.append(jnp.where(low, halves[0], halves[1]))
    o_mem = jnp.concatenate(mem_tiles, axis=1)

    c_att = (oa_ref[0] * ga_ref[0]).astype(BF16)
    c_ssm = (os_ref[0] * gs_ref[0]).astype(BF16)
    c_mem = (o_mem * gm_ref[0]).astype(BF16)
    a1, a2 = ATT_WIDTH, ATT_WIDTH + SSM_WIDTH
    sub = (jnp.dot(c_att, wo_ref[:a1, :], preferred_element_type=F32)
           + jnp.dot(c_ssm, wo_ref[a1:a2, :], preferred_element_type=F32)
           + jnp.dot(c_mem, wo_ref[a2:, :], preferred_element_type=F32))
    h = DN_ALPHA * x_ref[0] + sub
    mu = jnp.mean(h, axis=1, keepdims=True)
    d = h - mu
    var = jnp.mean(d * d, axis=1, keepdims=True)
    out_ref[0] = d * lax.rsqrt(var + LN_EPS) * lg_ref[...] + lb_ref[...]


def _combine(x, o_att, g_att, o_ssm, g_ssm, qm, g_mem, mk, mv, wo_bf16, ln_g, ln_b):
    batch, seq, _ = x.shape
    tile = TOKEN_TILE
    tok = lambda w: pl.BlockSpec((1, tile, w), lambda b, j: (b, j, 0))
    per_b = pl.BlockSpec((1, N_MEM, MEM_WIDTH), lambda b, j: (b, 0, 0))
    const = lambda shape: pl.BlockSpec(shape, lambda b, j: tuple(0 for _ in shape))
    return pl.pallas_call(
        _combine_kernel,
        grid=(batch, seq // tile),
        in_specs=[tok(D_MODEL), tok(ATT_WIDTH), tok(ATT_WIDTH), tok(SSM_WIDTH), tok(SSM_WIDTH),
                  tok(MEM_WIDTH), tok(MEM_WIDTH), per_b, per_b,
                  const((D_MODEL, D_MODEL)), const((1, D_MODEL)), const((1, D_MODEL))],
        out_specs=tok(D_MODEL),
        out_shape=jax.ShapeDtypeStruct((batch, seq, D_MODEL), x.dtype),
        compiler_params=pltpu.CompilerParams(dimension_semantics=("arbitrary", "arbitrary"),
                                             vmem_limit_bytes=VMEM_LIMIT_BYTES),
        name="combine",
    )(x, o_att, g_att, o_ssm, g_ssm, qm, g_mem, mk, mv, wo_bf16, ln_g, ln_b)


def _reordered_w_in(w_in):
    offs = np.cumsum([0, ATT_WIDTH, KV_WIDTH, KV_WIDTH, IDX_WIDTH, IDX_DIM, IDX_HEADS,
                      ATT_WIDTH, SSM_WIDTH, SSM_WIDTH, MEM_WIDTH, MEM_WIDTH])
    part = lambda n: w_in[:, int(offs[n]):int(offs[n + 1])]
    perm = _att_perm()
    k_idx = part(4)
    w_idx = part(5)
    zeros = jnp.zeros((D_MODEL, LANES - IDX_HEADS), w_in.dtype)
    cols = [part(0)[:, perm], part(1), part(2), part(3),
            jnp.tile(k_idx, (1, LANES // IDX_DIM)),
            jnp.concatenate([w_idx, zeros], axis=1),
            part(6)[:, perm], part(7), part(8), part(9), part(10)]
    return jnp.concatenate(cols, axis=1).astype(BF16)


def kernel(x, mem, w_in, w_mem_kv, lam_re, lam_im, log_dt, b_re, b_im, c_re, c_im, d_skip, w_glu, b_glu,
           w_out, ln_g, ln_b):
    batch, seq, _ = x.shape
    assert seq % TOKEN_TILE == 0 and seq % Q_TILE == 0 and seq % TIME_CHUNK == 0

    w_all = _reordered_w_in(w_in)
    perm = _att_perm()
    wo = jnp.concatenate([w_out[:ATT_WIDTH][perm], w_out[ATT_WIDTH:]], axis=0).astype(BF16)
    bmat, cmat, a_re, a_im = _s5_params(lam_re, lam_im, log_dt, b_re, b_im, c_re, c_im)
    tabs = _rope_tables(seq, HEAD_DIM) + _rope_tables(seq, IDX_DIM)

    mk, mv = _memkv(mem.reshape(batch * N_MEM, D_MODEL), w_mem_kv.astype(BF16))
    mk = mk.reshape(batch, N_MEM, MEM_WIDTH)
    mv = mv.reshape(batch, N_MEM, MEM_WIDTH)

    q, k, vt, qi, ki, wi, g_att, u, g_ssm, qm, g_mem = _inproj(x, w_all, tabs)
    o_att = _dsa(q, qi, wi, k, vt, ki)

    u_tm = jnp.swapaxes(u, 0, 1)
    o_ssm_tm = _s5(u_tm, bmat, cmat, a_re, a_im,
                   d_skip.reshape(1, SSM_WIDTH).astype(F32), w_glu.astype(BF16),
                   b_glu.reshape(1, SSM_WIDTH).astype(F32))
    o_ssm = jnp.swapaxes(o_ssm_tm, 0, 1)

    return _combine(x, o_att, g_att, o_ssm, g_ssm, qm, g_mem, mk, mv, wo,
                    ln_g.reshape(1, D_MODEL).astype(F32), ln_b.reshape(1, D_MODEL).astype(F32))
```

```python
import functools
import math

import jax
import jax.numpy as jnp
import numpy as np
from jax import lax
from jax.experimental import pallas as pl
from jax.experimental.pallas import tpu as pltpu

F32 = jnp.float32
BF16 = jnp.bfloat16
I32 = jnp.int32

D_MODEL = 1024
N_MEM = 256
HEAD_DIM = 64
ATT_HEADS = 8
ATT_KV_HEADS = 2
ATT_REP = ATT_HEADS // ATT_KV_HEADS
ATT_WIDTH = ATT_HEADS * HEAD_DIM
KV_WIDTH = ATT_KV_HEADS * HEAD_DIM
IDX_HEADS = 8
IDX_DIM = 32
IDX_WIDTH = IDX_HEADS * IDX_DIM
TOPK_MAX = 256
SSM_WIDTH = D_MODEL // 4
SSM_GROUP = 16
SSM_GROUPS = SSM_WIDTH // SSM_GROUP
SSM_STATE = 64
SSM_LANES = SSM_GROUPS * SSM_STATE
MEM_HEADS = 4
MEM_WIDTH = MEM_HEADS * HEAD_DIM
ROPE_THETA = 500000.0
ROPE_FRAC = 4
LN_EPS = 1e-5
DEPTH = 1
DN_ALPHA = (2.0 * DEPTH) ** 0.25
ATT_SCALE = HEAD_DIM ** -0.5
IDX_SCALE = IDX_HEADS ** -0.5 * IDX_DIM ** -0.5
LOG2_E = math.log2(math.e)

LANES = 128
SUBLANES = 8
BF16_ROWS = 16
VMEM_LIMIT_BYTES = 48 * 1024 * 1024
TOKEN_TILE = 512
Q_TILE = 256
KEY_CHUNK = 256
COUNT_ROWS = 64
VT_ROWS = KV_WIDTH + BF16_ROWS
TIME_CHUNK = 64
SCAN_LANES = 256
NEG_BIG = -1e30
DENOM_FLOOR = 2.0 ** -100
SEARCH_MAX_PASSES = 24
SEARCH_GROUP = 4
SEARCH_BISECT_PASSES = 2
SEARCH_CLIP = 0.1

INT_MIN = -2147483648
KEY_POS_INF = 0x7F800000
KEY_NEG_INF = INT_MIN + 0x7FFFFF

_G_Q, _G_K, _G_V, _G_QI, _G_KI, _G_W, _G_GA, _G_U, _G_GS, _G_QM, _G_GM = range(11)
_GROUP_WIDTHS = [ATT_WIDTH, KV_WIDTH, KV_WIDTH, IDX_WIDTH, LANES, LANES, ATT_WIDTH,
                 SSM_WIDTH, SSM_WIDTH, MEM_WIDTH, MEM_WIDTH]
_GROUP_OFFS = [int(v) for v in np.cumsum([0] + _GROUP_WIDTHS)]
IN_COLS = _GROUP_OFFS[-1]


def _att_perm():
    cols = []
    for j in range(ATT_REP):
        for g in range(ATT_KV_HEADS):
            h = g * ATT_REP + j
            cols.extend(range(h * HEAD_DIM, (h + 1) * HEAD_DIM))
    return np.asarray(cols, dtype=np.int32)


def _rope_tables(seq, period, dtype=F32):
    r = period // ROPE_FRAC
    half = r // 2
    inv = ROPE_THETA ** (-jnp.arange(0, half, dtype=F32) * 2.0 / r)
    ang = jnp.arange(seq).astype(F32)[:, None] * inv[None, :]
    cos, sin = jnp.cos(ang), jnp.sin(ang)
    ones = jnp.ones((seq, period - r), F32)
    zeros = jnp.zeros((seq, period - r), F32)
    c = jnp.concatenate([cos, cos, ones], axis=1)
    s = jnp.concatenate([-sin, sin, zeros], axis=1)
    reps = LANES // period
    return jnp.tile(c, (1, reps)).astype(dtype), jnp.tile(s, (1, reps)).astype(dtype)


def _rope(z, cos_t, sin_t, period):
    width = z.shape[1]
    half = period // ROPE_FRAC // 2
    reps = width // LANES
    if reps > 1:
        cos_t = jnp.concatenate([cos_t] * reps, axis=1)
        sin_t = jnp.concatenate([sin_t] * reps, axis=1)
    lane = lax.broadcasted_iota(I32, z.shape, 1)
    first = (lane & (period - 1)) < half
    up = pltpu.roll(z, half, 1)
    down = pltpu.roll(z, width - half, 1)
    return z * cos_t + jnp.where(first, down, up) * sin_t


def _memkv_kernel(mem_ref, w_ref, mk_ref, mv_ref):
    z = jnp.dot(mem_ref[...].astype(BF16), w_ref[...], preferred_element_type=F32)
    mk_ref[...] = z[:, :MEM_WIDTH].astype(BF16)
    mv_ref[...] = z[:, MEM_WIDTH:].astype(BF16)


def _memkv(mem2d, w_bf16):
    rows = mem2d.shape[0]
    tile = TOKEN_TILE
    return pl.pallas_call(
        _memkv_kernel,
        grid=(rows // tile,),
        in_specs=[pl.BlockSpec((tile, D_MODEL), lambda i: (i, 0)),
                  pl.BlockSpec((D_MODEL, 2 * MEM_WIDTH), lambda i: (0, 0))],
        out_specs=[pl.BlockSpec((tile, MEM_WIDTH), lambda i: (i, 0)),
                   pl.BlockSpec((tile, MEM_WIDTH), lambda i: (i, 0))],
        out_shape=[jax.ShapeDtypeStruct((rows, MEM_WIDTH), BF16),
                   jax.ShapeDtypeStruct((rows, MEM_WIDTH), BF16)],
        compiler_params=pltpu.CompilerParams(dimension_semantics=("arbitrary",),
                                             vmem_limit_bytes=VMEM_LIMIT_BYTES),
        name="memkv",
    )(mem2d, w_bf16)


def _inproj_kernel(x_ref, w_ref, ca_ref, sa_ref, ci_ref, si_ref,
                   q_ref, k_ref, vt_ref, qi_ref, ki_ref, wi_ref, ga_ref, u_ref, gs_ref, qm_ref, gm_ref):
    tile = x_ref.shape[1]
    pos0 = pl.multiple_of(pl.program_id(1) * tile, tile)
    xb = x_ref[0].astype(BF16)

    def proj(group):
        lo, hi = _GROUP_OFFS[group], _GROUP_OFFS[group + 1]
        return jnp.dot(xb, w_ref[:, lo:hi], preferred_element_type=F32)

    ca = ca_ref[pl.ds(pos0, tile), :]
    sa = sa_ref[pl.ds(pos0, tile), :]
    ci = ci_ref[pl.ds(pos0, tile), :]
    si = si_ref[pl.ds(pos0, tile), :]

    def store_transposed(ref, z):
        for qb in range(tile // Q_TILE):
            for t in range(z.shape[1] // LANES):
                zt = z[qb * Q_TILE:(qb + 1) * Q_TILE, t * LANES:(t + 1) * LANES].T
                ref[0, qb, :, t * Q_TILE:(t + 1) * Q_TILE] = zt.astype(ref.dtype)

    store_transposed(q_ref, _rope(proj(_G_Q), ca, sa, HEAD_DIM) * (ATT_SCALE * LOG2_E))
    k_ref[0] = _rope(proj(_G_K), ca, sa, HEAD_DIM).astype(BF16)
    v = proj(_G_V)
    for c in range(tile // KEY_CHUNK):
        vt_ref[0, c, :KV_WIDTH, :] = v[c * KEY_CHUNK:(c + 1) * KEY_CHUNK, :].T.astype(BF16)
        vt_ref[0, c, KV_WIDTH:, :] = jnp.ones((BF16_ROWS, KEY_CHUNK), BF16)
    store_transposed(qi_ref, _rope(proj(_G_QI), ci, si, IDX_DIM))
    ki_ref[0] = _rope(proj(_G_KI), ci, si, IDX_DIM).astype(BF16)
    wi = proj(_G_W) * IDX_SCALE
    for qb in range(tile // Q_TILE):
        wi_ref[0, qb] = wi[qb * Q_TILE:(qb + 1) * Q_TILE, :].T[:IDX_HEADS, :]
    ga_ref[0] = jax.nn.silu(proj(_G_GA))
    u_ref[...] = proj(_G_U)
    gs_ref[0] = jax.nn.silu(proj(_G_GS))
    qm_ref[0] = (proj(_G_QM) * ATT_SCALE).astype(BF16)
    gm_ref[0] = jax.nn.silu(proj(_G_GM))


def _inproj(x, w_bf16, tabs):
    batch, seq, _ = x.shape
    tile = TOKEN_TILE
    widths = _GROUP_WIDTHS
    dtypes = [BF16, BF16, BF16, BF16, BF16, F32, F32, F32, F32, BF16, F32]
    tab_spec = pl.BlockSpec((seq, LANES), lambda b, j: (0, 0))
    out_specs = [pl.BlockSpec((1, tile, w), lambda b, j: (b, j, 0)) for w in widths]
    out_shape = [jax.ShapeDtypeStruct((batch, seq, w), d) for w, d in zip(widths, dtypes)]
    out_specs[_G_V] = pl.BlockSpec((1, tile // KEY_CHUNK, VT_ROWS, KEY_CHUNK), lambda b, j: (b, j, 0, 0))
    out_shape[_G_V] = jax.ShapeDtypeStruct((batch, seq // KEY_CHUNK, VT_ROWS, KEY_CHUNK), BF16)
    for group, rows, dtype in ((_G_Q, LANES, BF16), (_G_QI, LANES, BF16), (_G_W, IDX_HEADS, F32)):
        cols = Q_TILE * max(widths[group] // LANES, 1)
        out_specs[group] = pl.BlockSpec((1, tile // Q_TILE, rows, cols), lambda b, j: (b, j, 0, 0))
        out_shape[group] = jax.ShapeDtypeStruct((batch, seq // Q_TILE, rows, cols), dtype)
    out_specs[_G_U] = pl.BlockSpec((tile, SSM_WIDTH), lambda b, j: (j, b))
    out_shape[_G_U] = jax.ShapeDtypeStruct((seq, batch * SSM_WIDTH), F32)
    return pl.pallas_call(
        _inproj_kernel,
        grid=(batch, seq // tile),
        in_specs=[pl.BlockSpec((1, tile, D_MODEL), lambda b, j: (b, j, 0)),
                  pl.BlockSpec((D_MODEL, IN_COLS), lambda b, j: (0, 0)),
                  tab_spec, tab_spec, tab_spec, tab_spec],
        out_specs=out_specs,
        out_shape=out_shape,
        compiler_params=pltpu.CompilerParams(dimension_semantics=("arbitrary", "arbitrary"),
                                             vmem_limit_bytes=VMEM_LIMIT_BYTES),
        name="inproj",
    )(x, w_bf16, *tabs)


def _key_to_float(key):
    bits = jnp.where(key >= 0, key, key ^ 0x7FFFFFFF)
    val = lax.bitcast_convert_type(bits, F32)
    val = jnp.where(key <= KEY_NEG_INF, -jnp.inf, val)
    return jnp.where(key >= KEY_POS_INF, jnp.inf, val)


def _dsa_kernel(q_ref, qi_ref, wi_ref, k_ref, vt_ref, ki_ref, ga_ref, o_ref,
                score_scr, bias_scr, row_scr, s_scr, acc_scr, og_scr, *, topk):
    i = pl.program_id(1)
    n_chunks = ((i + 1) * Q_TILE + KEY_CHUNK - 1) // KEY_CHUNK
    q0 = i * Q_TILE
    kf = float(topk)
    att_cols = ATT_REP * Q_TILE
    groups = KEY_CHUNK // SUBLANES

    def chunk_start(c):
        return pl.multiple_of(c * KEY_CHUNK, KEY_CHUNK)

    def fold(v, op):
        return op(v.reshape(groups, SUBLANES, v.shape[-1]), axis=0)

    def only_rows(x, lo, n):
        parts = [jnp.zeros((lo, x.shape[1]), x.dtype)] if lo else []
        parts.append(x[lo:lo + n])
        if lo + n < x.shape[0]:
            parts.append(jnp.zeros((x.shape[0] - lo - n, x.shape[1]), x.dtype))
        return jnp.concatenate(parts, axis=0)

    qi_t = qi_ref[0, 0]
    per_tile = LANES // IDX_DIM

    def head_cols(h):
        t = h // per_tile
        return only_rows(qi_t[:, t * Q_TILE:(t + 1) * Q_TILE], (h % per_tile) * IDX_DIM, IDX_DIM)

    rhs_pairs = [jnp.concatenate([head_cols(2 * p), head_cols(2 * p + 1)], axis=1)
                 for p in range(IDX_HEADS // 2)]
    w_rows = [wi_ref[0, 0, h:h + 1, :] for h in range(IDX_HEADS)]
    q_t = q_ref[0, 0]
    qg_t = [only_rows(q_t, g * HEAD_DIM, HEAD_DIM) for g in range(ATT_KV_HEADS)]
    krow = lax.broadcasted_iota(I32, (KEY_CHUNK, Q_TILE), 0)
    kq = q0 + lax.broadcasted_iota(I32, (KEY_CHUNK, Q_TILE), 1)

    def matmul_chunk(c, stats):
        mx_a, mn_a, ge_a, gt_a, top_a = stats
        ks = chunk_start(c)
        kc = ki_ref[0, pl.ds(ks, KEY_CHUNK), :]
        acc = jnp.zeros((KEY_CHUNK, Q_TILE), F32)
        for p in range(IDX_HEADS // 2):
            s2 = jnp.dot(kc, rhs_pairs[p], preferred_element_type=F32)
            acc = acc + w_rows[2 * p] * jnp.maximum(s2[:, :Q_TILE], 0.0)
            acc = acc + w_rows[2 * p + 1] * jnp.maximum(s2[:, Q_TILE:], 0.0)
        causal = ks + krow <= kq
        sc = jnp.where(causal, acc, -jnp.inf)
        score_scr[pl.ds(ks, KEY_CHUNK), :] = sc
        kk = k_ref[0, pl.ds(ks, KEY_CHUNK), :]
        tops = []
        for g in range(ATT_KV_HEADS):
            s = jnp.dot(kk, qg_t[g], preferred_element_type=F32)
            s_scr[g, pl.ds(ks, KEY_CHUNK), :] = s
            tops.append(jnp.maximum(top_a[g], fold(s, jnp.max)))
        return (jnp.maximum(mx_a, fold(sc, jnp.max)),
                jnp.minimum(mn_a, fold(jnp.where(causal, acc, jnp.inf), jnp.min)),
                ge_a + fold(jnp.where(sc >= 0.0, 1.0, 0.0), jnp.sum),
                gt_a + fold(jnp.where(sc > 0.0, 1.0, 0.0), jnp.sum),
                tuple(tops))

    stat0 = lambda v: jnp.full((SUBLANES, Q_TILE), v, F32)
    top0 = tuple(jnp.full((SUBLANES, att_cols), NEG_BIG, F32) for _ in range(ATT_KV_HEADS))
    mx_a, mn_a, ge_a, gt_a, top_a = lax.fori_loop(
        0, n_chunks, matmul_chunk, (stat0(-jnp.inf), stat0(jnp.inf), stat0(0.0), stat0(0.0), top0))
    m_top = [jnp.max(top_a[g], axis=0, keepdims=True) for g in range(ATT_KV_HEADS)]
    s_max = jnp.max(mx_a, axis=0, keepdims=True)
    s_min = jnp.min(mn_a, axis=0, keepdims=True)
    n_ge0 = jnp.sum(ge_a, axis=0, keepdims=True)
    n_gt0 = jnp.sum(gt_a, axis=0, keepdims=True)
    n_causal = (q0 + 1 + lax.broadcasted_iota(I32, (1, Q_TILE), 1)).astype(F32)

    def count(pred):
        def body(c, acc):
            ks = chunk_start(c)
            for part in range(KEY_CHUNK // LANES):
                hit = jnp.where(pred(score_scr[pl.ds(ks + part * LANES, LANES), :]), 1.0, 0.0)
                acc = acc + (hit[:COUNT_ROWS] + hit[COUNT_ROWS:])
            return acc
        acc = lax.fori_loop(0, n_chunks, body, jnp.zeros((COUNT_ROWS, Q_TILE), F32))
        rows = COUNT_ROWS
        while rows > SUBLANES:
            rows //= 2
            acc = acc[:rows] + acc[rows:]
        return jnp.sum(acc, axis=0, keepdims=True)

    def count_ge(value):
        vb = jnp.broadcast_to(value, (LANES, Q_TILE))
        return count(lambda sc: sc >= vb)

    zero_thr = (n_gt0 < kf) & (n_ge0 >= kf)
    keep_all = n_causal <= kf
    positive = n_gt0 >= kf
    settled = zero_thr | keep_all
    lo0 = jnp.where(settled, 0.0, jnp.where(positive, 0.0, s_min))
    hi0 = jnp.where(settled, 0.0, jnp.where(positive, s_max, 0.0))
    c_lo0 = jnp.where(positive, n_gt0, n_causal)
    c_hi0 = jnp.where(positive, 0.0, n_ge0)
    thr0 = jnp.where(keep_all, -jnp.inf, 0.0)
    done0 = jnp.where(settled, 1.0, 0.0)

    def search_pass(it, rows):
        lo, hi, c_lo, c_hi, thr, done = rows
        frac = jnp.clip((kf - c_hi + 0.5) / (c_lo - c_hi + 1.0), SEARCH_CLIP, 1.0 - SEARCH_CLIP)
        frac = jnp.where(it < SEARCH_BISECT_PASSES, 0.5, frac)
        x = hi - (hi - lo) * frac
        c = count_ge(x)
        live = done == 0.0
        above = live & (c >= kf)
        below = live & (c < kf)
        hit = live & (c == kf)
        return (jnp.where(above, x, lo), jnp.where(below, x, hi),
                jnp.where(above, c, c_lo), jnp.where(below, c, c_hi),
                jnp.where(hit, x, thr), jnp.where(hit, 1.0, done))

    def search_cond(state):
        return (state[0] < SEARCH_MAX_PASSES) & (state[1] > 0.0)

    def search_body(state):
        it, _, rows = state
        for p in range(SEARCH_GROUP):
            rows = search_pass(it + p, rows)
        return it + SEARCH_GROUP, jnp.sum(1.0 - rows[5]), rows

    state = lax.while_loop(search_cond, search_body,
                           (jnp.int32(0), jnp.sum(1.0 - done0), (lo0, hi0, c_lo0, c_hi0, thr0, done0)))
    pending, thr_s, done_s = state[1], state[2][4], state[2][5]
    row_scr[0:1, :] = thr_s
    row_scr[1:2, :] = jnp.where(zero_thr, n_gt0, 0.0)
    row_scr[2:3, :] = jnp.where(zero_thr, n_ge0, kf)

    @pl.when(pending > 0.0)
    def _():
        def count_ge_key(key):
            return count_ge(_key_to_float(key))

        key0 = jnp.where(count_ge_key(jnp.zeros((1, Q_TILE), I32)) >= kf, 0, INT_MIN).astype(I32)

        def bit_step(b, key):
            cand = key | jnp.left_shift(jnp.int32(1), 30 - b)
            return jnp.where(count_ge_key(cand) >= kf, cand, key)

        thr_f = _key_to_float(lax.fori_loop(0, 31, bit_step, key0))
        thr_fb = jnp.broadcast_to(thr_f, (LANES, Q_TILE))
        open_row = done_s == 0.0
        row_scr[0:1, :] = jnp.where(open_row, thr_f, thr_s)
        row_scr[1:2, :] = jnp.where(open_row, count(lambda sc: sc > thr_fb), row_scr[1:2, :])
        row_scr[2:3, :] = jnp.where(open_row, count(lambda sc: sc >= thr_fb), row_scr[2:3, :])

    thr = row_scr[0:1, :]
    thr_c = jnp.broadcast_to(thr, (KEY_CHUNK, Q_TILE))
    has_tie = (row_scr[2:3, :] > kf) & (thr > -jnp.inf)
    any_tie = jnp.max(jnp.where(has_tie, 1.0, 0.0)) > 0.0

    def plain_bias(c, carry):
        ks = chunk_start(c)
        keep = (score_scr[pl.ds(ks, KEY_CHUNK), :] >= thr_c) & (ks + krow <= kq)
        return jnp.where(keep, 0.0, NEG_BIG), carry

    def make_tie_bias():
        need = jnp.broadcast_to(kf - row_scr[1:2, :], (KEY_CHUNK, Q_TILE))
        lower = (lax.broadcasted_iota(I32, (KEY_CHUNK, KEY_CHUNK), 1)
                 <= lax.broadcasted_iota(I32, (KEY_CHUNK, KEY_CHUNK), 0))
        prefix_mat = jnp.where(lower, 1.0, 0.0).astype(BF16)

        def tie_bias(c, seen):
            ks = chunk_start(c)
            sc = score_scr[pl.ds(ks, KEY_CHUNK), :]
            tied = sc == thr_c
            rank = seen + jnp.dot(prefix_mat, jnp.where(tied, 1.0, 0.0).astype(BF16),
                                  preferred_element_type=F32)
            chosen = (sc > thr_c) | (tied & (rank <= need))
            return jnp.where(chosen & (ks + krow <= kq), 0.0, NEG_BIG), rank[KEY_CHUNK - 1:KEY_CHUNK, :]

        return tie_bias

    no_ties_seen = jnp.zeros((1, Q_TILE), F32)

    def pv_pass(bias_of, m, carry0):
        acc_scr[...] = jnp.zeros(acc_scr.shape, F32)

        def pv_chunk(c, carry):
            bias, carry = bias_of(c, carry)
            bias4 = jnp.concatenate([bias] * ATT_REP, axis=1)
            for g in range(ATT_KV_HEADS):
                p = jnp.exp2(s_scr[g, pl.ds(chunk_start(c), KEY_CHUNK), :] + bias4 - m[g])
                acc_scr[g] += jnp.dot(vt_ref[0, c], p.astype(BF16), preferred_element_type=F32)
            return carry

        lax.fori_loop(0, n_chunks, pv_chunk, carry0)

    @pl.when(jnp.logical_not(any_tie))
    def _():
        pv_pass(plain_bias, m_top, 0)

    @pl.when(any_tie)
    def _():
        pv_pass(make_tie_bias(), m_top, no_ties_seen)

    d_min = jnp.minimum(jnp.min(acc_scr[0, KV_WIDTH:KV_WIDTH + 1, :]), jnp.min(acc_scr[1, KV_WIDTH:KV_WIDTH + 1, :]))

    @pl.when(jnp.logical_not(d_min >= DENOM_FLOOR))
    def _():
        tie_bias = make_tie_bias()

        def mask_chunk(c, carry):
            seen, m_acc = carry
            ks = chunk_start(c)
            bias, seen = tie_bias(c, seen)
            bias_scr[pl.ds(ks, KEY_CHUNK), :] = bias
            bias4 = jnp.concatenate([bias] * ATT_REP, axis=1)
            return seen, tuple(jnp.maximum(m_acc[g], fold(s_scr[g, pl.ds(ks, KEY_CHUNK), :] + bias4, jnp.max))
                               for g in range(ATT_KV_HEADS))

        m_init = tuple(jnp.full((SUBLANES, att_cols), NEG_BIG, F32) for _ in range(ATT_KV_HEADS))
        _, m_acc = lax.fori_loop(0, n_chunks, mask_chunk, (no_ties_seen, m_init))
        m_sel = [jnp.max(m_acc[g], axis=0, keepdims=True) for g in range(ATT_KV_HEADS)]
        pv_pass(lambda c, carry: (bias_scr[pl.ds(chunk_start(c), KEY_CHUNK), :], carry), m_sel, 0)

    for g in range(ATT_KV_HEADS):
        denom = acc_scr[g, KV_WIDTH:KV_WIDTH + 1, :]
        og_scr[g * HEAD_DIM:(g + 1) * HEAD_DIM, :] = acc_scr[g, g * HEAD_DIM:(g + 1) * HEAD_DIM, :] / denom

    for j in range(ATT_REP):
        gated = og_scr[:, j * Q_TILE:(j + 1) * Q_TILE].T * ga_ref[0, :, j * LANES:(j + 1) * LANES]
        o_ref[0, :, j * LANES:(j + 1) * LANES] = gated.astype(o_ref.dtype)


def _dsa(q_t, qi_t, wi_t, k, vt, ki, gate):
    batch, seq, _ = k.shape
    topk = min(TOPK_MAX, seq // 4)
    att_cols = ATT_REP * Q_TILE
    per_q = lambda a: pl.BlockSpec((1, 1) + a.shape[2:], lambda b, i: (b, i, 0, 0))
    per_b = lambda w: pl.BlockSpec((1, seq, w), lambda b, i: (b, 0, 0))
    vt_spec = pl.BlockSpec((1, seq // KEY_CHUNK, VT_ROWS, KEY_CHUNK), lambda b, i: (b, 0, 0, 0))
    tok_spec = pl.BlockSpec((1, Q_TILE, ATT_WIDTH), lambda b, i: (b, i, 0))
    return pl.pallas_call(
        functools.partial(_dsa_kernel, topk=topk),
        grid=(batch, seq // Q_TILE),
        in_specs=[per_q(q_t), per_q(qi_t), per_q(wi_t), per_b(KV_WIDTH), vt_spec, per_b(LANES), tok_spec],
        out_specs=tok_spec,
        out_shape=jax.ShapeDtypeStruct((batch, seq, ATT_WIDTH), BF16),
        scratch_shapes=[pltpu.VMEM((seq, Q_TILE), F32),
                        pltpu.VMEM((seq, Q_TILE), F32),
                        pltpu.VMEM((SUBLANES, Q_TILE), F32),
                        pltpu.VMEM((ATT_KV_HEADS, seq, att_cols), F32),
                        pltpu.VMEM((ATT_KV_HEADS, VT_ROWS, att_cols), F32),
                        pltpu.VMEM((KV_WIDTH, att_cols), F32)],
        compiler_params=pltpu.CompilerParams(dimension_semantics=("arbitrary", "arbitrary"),
                                             vmem_limit_bytes=VMEM_LIMIT_BYTES),
        name="dsa",
    )(q_t, qi_t, wi_t, k, vt, ki, gate)


def _s5_kernel(u_ref, bm_ref, cm_ref, are_ref, aim_ref, dsk_ref, wg_ref, bg_ref, o_ref,
               st_scr, xb_scr, carry_scr):
    tc, batch, width = u_ref.shape
    rows = tc * batch

    @pl.when(pl.program_id(0) == 0)
    def _():
        carry_scr[...] = jnp.zeros(carry_scr.shape, F32)

    u = u_ref[...].reshape(rows, width)
    st_scr[...] = jnp.dot(u.astype(BF16), bm_ref[...], preferred_element_type=F32)

    for part in range(SSM_LANES // SCAN_LANES):
        re_lo = part * SCAN_LANES
        im_lo = SSM_LANES + re_lo
        a_re = jnp.broadcast_to(are_ref[:, re_lo:re_lo + SCAN_LANES], (batch, SCAN_LANES))
        a_im = jnp.broadcast_to(aim_ref[:, re_lo:re_lo + SCAN_LANES], (batch, SCAN_LANES))

        def step(t, state):
            x_re, x_im = state
            r0 = pl.multiple_of(t * batch, batch)
            n_re = a_re * x_re - a_im * x_im + st_scr[pl.ds(r0, batch), re_lo:re_lo + SCAN_LANES]
            n_im = a_re * x_im + a_im * x_re + st_scr[pl.ds(r0, batch), im_lo:im_lo + SCAN_LANES]
            xb_scr[pl.ds(r0, batch), re_lo:re_lo + SCAN_LANES] = n_re.astype(BF16)
            xb_scr[pl.ds(r0, batch), im_lo:im_lo + SCAN_LANES] = n_im.astype(BF16)
            return n_re, n_im

        x_re, x_im = lax.fori_loop(
            0, tc, step,
            (carry_scr[:, re_lo:re_lo + SCAN_LANES], carry_scr[:, im_lo:im_lo + SCAN_LANES]), unroll=4)
        carry_scr[:, re_lo:re_lo + SCAN_LANES] = x_re
        carry_scr[:, im_lo:im_lo + SCAN_LANES] = x_im

    y = jnp.dot(xb_scr[...], cm_ref[...], preferred_element_type=F32) + dsk_ref[...] * u
    y = jax.nn.gelu(y)
    gate = jax.nn.sigmoid(jnp.dot(y.astype(BF16), wg_ref[...], preferred_element_type=F32) + bg_ref[...])
    o_ref[...] = (y * gate).reshape(tc, batch, width)


def _s5(u_tm, bmat, cmat, a_re, a_im, dskip, w_glu, b_glu):
    seq, batch, width = u_tm.shape
    tc = TIME_CHUNK
    full = lambda shape: pl.BlockSpec(shape, lambda t: tuple(0 for _ in shape))
    return pl.pallas_call(
        _s5_kernel,
        grid=(seq // tc,),
        in_specs=[pl.BlockSpec((tc, batch, width), lambda t: (t, 0, 0)),
                  full(bmat.shape), full(cmat.shape), full(a_re.shape), full(a_im.shape),
                  full(dskip.shape), full(w_glu.shape), full(b_glu.shape)],
        out_specs=pl.BlockSpec((tc, batch, width), lambda t: (t, 0, 0)),
        out_shape=jax.ShapeDtypeStruct((seq, batch, width), F32),
        scratch_shapes=[pltpu.VMEM((tc * batch, 2 * SSM_LANES), F32),
                        pltpu.VMEM((tc * batch, 2 * SSM_LANES), BF16),
                        pltpu.VMEM((batch, 2 * SSM_LANES), F32)],
        compiler_params=pltpu.CompilerParams(dimension_semantics=("arbitrary",),
                                             vmem_limit_bytes=VMEM_LIMIT_BYTES),
        name="s5",
    )(u_tm, bmat, cmat, a_re, a_im, dskip, w_glu, b_glu)


def _s5_params(lam_re, lam_im, log_dt, b_re, b_im, c_re, c_im):
    dt = jnp.exp(log_dt.astype(F32))[:, None]
    mag = jnp.exp(lam_re.astype(F32) * dt)
    ang = lam_im.astype(F32) * dt
    lb_re, lb_im = mag * jnp.cos(ang), mag * jnp.sin(ang)
    den = lam_re * lam_re + lam_im * lam_im
    k_re = ((lb_re - 1.0) * lam_re + lb_im * lam_im) / den
    k_im = (lb_im * lam_re - (lb_re - 1.0) * lam_im) / den
    bb_re = k_re[:, :, None] * b_re - k_im[:, :, None] * b_im
    bb_im = k_re[:, :, None] * b_im + k_im[:, :, None] * b_re
    eye = jnp.eye(SSM_GROUPS, dtype=F32)
    drive = lambda bb: jnp.einsum('gpn,gh->gnhp', bb, eye).reshape(SSM_WIDTH, SSM_LANES)
    read = lambda cc: jnp.einsum('gnp,gh->gphn', cc, eye).reshape(SSM_LANES, SSM_WIDTH)
    bmat = jnp.concatenate([drive(bb_re), drive(bb_im)], axis=1)
    cmat = jnp.concatenate([read(c_re.astype(F32)), read(-c_im.astype(F32))], axis=0)
    return (bmat.astype(BF16), cmat.astype(BF16),
            lb_re.reshape(1, SSM_LANES), lb_im.reshape(1, SSM_LANES))


def _combine_kernel(x_ref, ca_ref, os_ref, gs_ref, qm_ref, gm_ref, mk_ref, mv_ref,
                    wo_ref, lg_ref, lb_ref, out_ref):
    tile = x_ref.shape[1]
    nt_dims = (((1,), (1,)), ((), ()))
    lane = lax.broadcasted_iota(I32, (tile, LANES), 1)
    low = lane < HEAD_DIM

    qm = qm_ref[0].astype(F32)
    mem_tiles = []
    for t in range(MEM_WIDTH // LANES):
        qt = qm[:, t * LANES:(t + 1) * LANES]
        mk = mk_ref[0, :, t * LANES:(t + 1) * LANES]
        mv = mv_ref[0, :, t * LANES:(t + 1) * LANES]
        halves = []
        for first in (True, False):
            qh = jnp.where(low if first else ~low, qt, 0.0).astype(BF16)
            s = lax.dot_general(qh, mk, nt_dims, preferred_element_type=F32)
            p = jnp.exp(s - jnp.max(s, axis=1, keepdims=True))
            o = jnp.dot(p.astype(BF16), mv, preferred_element_type=F32)
            halves.append(o / jnp.sum(p, axis=1, keepdims=True))
        mem_tiles.append(jnp.where(low, halves[0], halves[1]))
    o_mem = jnp.concatenate(mem_tiles, axis=1)

    c_att = ca_ref[0]
    c_ssm = (os_ref[...] * gs_ref[0]).astype(BF16)
    c_mem = (o_mem * gm_ref[0]).astype(BF16)
    a1, a2 = ATT_WIDTH, ATT_WIDTH + SSM_WIDTH
    sub = (jnp.dot(c_att, wo_ref[:a1, :], preferred_element_type=F32)
           + jnp.dot(c_ssm, wo_ref[a1:a2, :], preferred_element_type=F32)
           + jnp.dot(c_mem, wo_ref[a2:, :], preferred_element_type=F32))
    h = DN_ALPHA * x_ref[0] + sub
    mu = jnp.mean(h, axis=1, keepdims=True)
    d = h - mu
    var = jnp.mean(d * d, axis=1, keepdims=True)
    out_ref[0] = d * lax.rsqrt(var + LN_EPS) * lg_ref[...] + lb_ref[...]


def _combine(x, c_att, o_ssm_tm, g_ssm, qm, g_mem, mk, mv, wo_bf16, ln_g, ln_b):
    batch, seq, _ = x.shape
    tile = TOKEN_TILE
    tok = lambda w: pl.BlockSpec((1, tile, w), lambda b, j: (b, j, 0))
    time_major = pl.BlockSpec((tile, SSM_WIDTH), lambda b, j: (j, b))
    per_b = pl.BlockSpec((1, N_MEM, MEM_WIDTH), lambda b, j: (b, 0, 0))
    const = lambda shape: pl.BlockSpec(shape, lambda b, j: tuple(0 for _ in shape))
    return pl.pallas_call(
        _combine_kernel,
        grid=(batch, seq // tile),
        in_specs=[tok(D_MODEL), tok(ATT_WIDTH), time_major, tok(SSM_WIDTH),
                  tok(MEM_WIDTH), tok(MEM_WIDTH), per_b, per_b,
                  const((D_MODEL, D_MODEL)), const((1, D_MODEL)), const((1, D_MODEL))],
        out_specs=tok(D_MODEL),
        out_shape=jax.ShapeDtypeStruct((batch, seq, D_MODEL), x.dtype),
        compiler_params=pltpu.CompilerParams(dimension_semantics=("arbitrary", "arbitrary"),
                                             vmem_limit_bytes=VMEM_LIMIT_BYTES),
        name="combine",
    )(x, c_att, o_ssm_tm, g_ssm, qm, g_mem, mk, mv, wo_bf16, ln_g, ln_b)


def _reordered_w_in(w_in):
    offs = np.cumsum([0, ATT_WIDTH, KV_WIDTH, KV_WIDTH, IDX_WIDTH, IDX_DIM, IDX_HEADS,
                      ATT_WIDTH, SSM_WIDTH, SSM_WIDTH, MEM_WIDTH, MEM_WIDTH])
    part = lambda n: w_in[:, int(offs[n]):int(offs[n + 1])]
    perm = _att_perm()
    k_idx = part(4)
    w_idx = part(5)
    zeros = jnp.zeros((D_MODEL, LANES - IDX_HEADS), w_in.dtype)
    cols = [part(0)[:, perm], part(1), part(2), part(3),
            jnp.tile(k_idx, (1, LANES // IDX_DIM)),
            jnp.concatenate([w_idx, zeros], axis=1),
            part(6)[:, perm], part(7), part(8), part(9), part(10)]
    return jnp.concatenate(cols, axis=1).astype(BF16)


def kernel(x, mem, w_in, w_mem_kv, lam_re, lam_im, log_dt, b_re, b_im, c_re, c_im, d_skip, w_glu, b_glu,
           w_out, ln_g, ln_b):
    batch, seq, _ = x.shape
    assert seq % TOKEN_TILE == 0 and seq % Q_TILE == 0 and seq % TIME_CHUNK == 0

    w_all = _reordered_w_in(w_in)
    perm = _att_perm()
    wo = jnp.concatenate([w_out[:ATT_WIDTH][perm], w_out[ATT_WIDTH:]], axis=0).astype(BF16)
    bmat, cmat, a_re, a_im = _s5_params(lam_re, lam_im, log_dt, b_re, b_im, c_re, c_im)
    tabs = _rope_tables(seq, HEAD_DIM) + _rope_tables(seq, IDX_DIM)

    mk, mv = _memkv(mem.reshape(batch * N_MEM, D_MODEL), w_mem_kv.astype(BF16))
    mk = mk.reshape(batch, N_MEM, MEM_WIDTH)
    mv = mv.reshape(batch, N_MEM, MEM_WIDTH)

    q_t, k, vt, qi_t, ki, wi_t, g_att, u_tm, g_ssm, qm, g_mem = _inproj(x, w_all, tabs)
    c_att = _dsa(q_t, qi_t, wi_t, k, vt, ki, g_att)

    o_ssm_tm = _s5(u_tm.reshape(seq, batch, SSM_WIDTH), bmat, cmat, a_re, a_im,
                   d_skip.reshape(1, SSM_WIDTH).astype(F32), w_glu.astype(BF16),
                   b_glu.reshape(1, SSM_WIDTH).astype(F32))
    o_ssm = o_ssm_tm.reshape(seq, batch * SSM_WIDTH)

    return _combine(x, c_att, o_ssm, g_ssm, qm, g_mem, mk, mv, wo,
                    ln_g.reshape(1, D_MODEL).astype(F32), ln_b.reshape(1, D_MODEL).astype(F32))
```

```python
import functools
import math

import jax
import jax.numpy as jnp
import numpy as np
from jax import lax
from jax.experimental import pallas as pl
from jax.experimental.pallas import tpu as pltpu

F32 = jnp.float32
BF16 = jnp.bfloat16
I32 = jnp.int32

D_MODEL = 1024
N_MEM = 256
HEAD_DIM = 64
ATT_HEADS = 8
ATT_KV_HEADS = 2
ATT_REP = ATT_HEADS // ATT_KV_HEADS
ATT_WIDTH = ATT_HEADS * HEAD_DIM
KV_WIDTH = ATT_KV_HEADS * HEAD_DIM
IDX_HEADS = 8
IDX_DIM = 32
IDX_WIDTH = IDX_HEADS * IDX_DIM
TOPK_MAX = 256
SSM_WIDTH = D_MODEL // 4
SSM_GROUP = 16
SSM_GROUPS = SSM_WIDTH // SSM_GROUP
SSM_STATE = 64
SSM_LANES = SSM_GROUPS * SSM_STATE
MEM_HEADS = 4
MEM_WIDTH = MEM_HEADS * HEAD_DIM
ROPE_THETA = 500000.0
ROPE_FRAC = 4
LN_EPS = 1e-5
DEPTH = 1
DN_ALPHA = (2.0 * DEPTH) ** 0.25
ATT_SCALE = HEAD_DIM ** -0.5
IDX_SCALE = IDX_HEADS ** -0.5 * IDX_DIM ** -0.5
LOG2_E = math.log2(math.e)

LANES = 128
SUBLANES = 8
BF16_ROWS = 16
VMEM_LIMIT_BYTES = 48 * 1024 * 1024
TOKEN_TILE = 512
Q_TILE = 256
KEY_CHUNK = 256
COUNT_ROWS = 64
VT_ROWS = KV_WIDTH + BF16_ROWS
TIME_CHUNK = 64
SCAN_LANES = 256
NEG_BIG = -1e30
DENOM_FLOOR = 2.0 ** -100
SEARCH_MAX_PASSES = 24
SEARCH_GROUP = 4
SEARCH_BISECT_PASSES = 2
SEARCH_CLIP = 0.1

INT_MIN = -2147483648
KEY_POS_INF = 0x7F800000
KEY_NEG_INF = INT_MIN + 0x7FFFFF

_G_Q, _G_K, _G_V, _G_QI, _G_KI, _G_W, _G_GA, _G_U, _G_GS, _G_QM, _G_GM = range(11)
_GROUP_WIDTHS = [ATT_WIDTH, KV_WIDTH, KV_WIDTH, IDX_WIDTH, LANES, LANES, ATT_WIDTH,
                 SSM_WIDTH, SSM_WIDTH, MEM_WIDTH, MEM_WIDTH]
_GROUP_OFFS = [int(v) for v in np.cumsum([0] + _GROUP_WIDTHS)]
IN_COLS = _GROUP_OFFS[-1]


def _att_perm():
    cols = []
    for j in range(ATT_REP):
        for g in range(ATT_KV_HEADS):
            h = g * ATT_REP + j
            cols.extend(range(h * HEAD_DIM, (h + 1) * HEAD_DIM))
    return np.asarray(cols, dtype=np.int32)


def _rope_tables(seq, period, dtype=F32):
    r = period // ROPE_FRAC
    half = r // 2
    inv = ROPE_THETA ** (-jnp.arange(0, half, dtype=F32) * 2.0 / r)
    ang = jnp.arange(seq).astype(F32)[:, None] * inv[None, :]
    cos, sin = jnp.cos(ang), jnp.sin(ang)
    ones = jnp.ones((seq, period - r), F32)
    zeros = jnp.zeros((seq, period - r), F32)
    c = jnp.concatenate([cos, cos, ones], axis=1)
    s = jnp.concatenate([-sin, sin, zeros], axis=1)
    reps = LANES // period
    return jnp.tile(c, (1, reps)).astype(dtype), jnp.tile(s, (1, reps)).astype(dtype)


def _rope(z, cos_t, sin_t, period):
    width = z.shape[1]
    half = period // ROPE_FRAC // 2
    reps = width // LANES
    if reps > 1:
        cos_t = jnp.concatenate([cos_t] * reps, axis=1)
        sin_t = jnp.concatenate([sin_t] * reps, axis=1)
    lane = lax.broadcasted_iota(I32, z.shape, 1)
    first = (lane & (period - 1)) < half
    up = pltpu.roll(z, half, 1)
    down = pltpu.roll(z, width - half, 1)
    return z * cos_t + jnp.where(first, down, up) * sin_t


def _memkv_kernel(mem_ref, w_ref, mk_ref, mv_ref):
    z = jnp.dot(mem_ref[...].astype(BF16), w_ref[...], preferred_element_type=F32)
    mk_ref[...] = z[:, :MEM_WIDTH].astype(BF16)
    mv_ref[...] = z[:, MEM_WIDTH:].astype(BF16)


def _memkv(mem2d, w_bf16):
    rows = mem2d.shape[0]
    tile = TOKEN_TILE
    return pl.pallas_call(
        _memkv_kernel,
        grid=(rows // tile,),
        in_specs=[pl.BlockSpec((tile, D_MODEL), lambda i: (i, 0)),
                  pl.BlockSpec((D_MODEL, 2 * MEM_WIDTH), lambda i: (0, 0))],
        out_specs=[pl.BlockSpec((tile, MEM_WIDTH), lambda i: (i, 0)),
                   pl.BlockSpec((tile, MEM_WIDTH), lambda i: (i, 0))],
        out_shape=[jax.ShapeDtypeStruct((rows, MEM_WIDTH), BF16),
                   jax.ShapeDtypeStruct((rows, MEM_WIDTH), BF16)],
        compiler_params=pltpu.CompilerParams(dimension_semantics=("arbitrary",),
                                             vmem_limit_bytes=VMEM_LIMIT_BYTES),
        name="memkv",
    )(mem2d, w_bf16)


def _inproj_kernel(x_ref, w_ref, ca_ref, sa_ref, ci_ref, si_ref,
                   q_ref, k_ref, vt_ref, qi_ref, ki_ref, wi_ref, ga_ref, u_ref, gs_ref, qm_ref, gm_ref):
    tile = x_ref.shape[1]
    pos0 = pl.multiple_of(pl.program_id(1) * tile, tile)
    xb = x_ref[0].astype(BF16)

    def proj(group):
        lo, hi = _GROUP_OFFS[group], _GROUP_OFFS[group + 1]
        return jnp.dot(xb, w_ref[:, lo:hi], preferred_element_type=F32)

    ca = ca_ref[pl.ds(pos0, tile), :]
    sa = sa_ref[pl.ds(pos0, tile), :]
    ci = ci_ref[pl.ds(pos0, tile), :]
    si = si_ref[pl.ds(pos0, tile), :]

    def store_transposed(ref, z):
        for qb in range(tile // Q_TILE):
            for t in range(z.shape[1] // LANES):
                zt = z[qb * Q_TILE:(qb + 1) * Q_TILE, t * LANES:(t + 1) * LANES].T
                ref[0, qb, :, t * Q_TILE:(t + 1) * Q_TILE] = zt.astype(ref.dtype)

    store_transposed(q_ref, _rope(proj(_G_Q), ca, sa, HEAD_DIM) * (ATT_SCALE * LOG2_E))
    k_ref[0] = _rope(proj(_G_K), ca, sa, HEAD_DIM).astype(BF16)
    v = proj(_G_V)
    for c in range(tile // KEY_CHUNK):
        vt_ref[0, c, :KV_WIDTH, :] = v[c * KEY_CHUNK:(c + 1) * KEY_CHUNK, :].T.astype(BF16)
        vt_ref[0, c, KV_WIDTH:, :] = jnp.ones((BF16_ROWS, KEY_CHUNK), BF16)
    store_transposed(qi_ref, _rope(proj(_G_QI), ci, si, IDX_DIM))
    ki_ref[0] = _rope(proj(_G_KI), ci, si, IDX_DIM).astype(BF16)
    wi = proj(_G_W) * IDX_SCALE
    for qb in range(tile // Q_TILE):
        wi_ref[0, qb] = wi[qb * Q_TILE:(qb + 1) * Q_TILE, :].T[:IDX_HEADS, :]
    ga_ref[0] = jax.nn.silu(proj(_G_GA))
    u_ref[...] = proj(_G_U)
    gs_ref[0] = jax.nn.silu(proj(_G_GS))
    qm_ref[0] = (proj(_G_QM) * ATT_SCALE).astype(BF16)
    gm_ref[0] = jax.nn.silu(proj(_G_GM))


def _inproj(x, w_bf16, tabs):
    batch, seq, _ = x.shape
    tile = TOKEN_TILE
    widths = _GROUP_WIDTHS
    dtypes = [BF16, BF16, BF16, BF16, BF16, F32, F32, F32, F32, BF16, F32]
    tab_spec = pl.BlockSpec((seq, LANES), lambda b, j: (0, 0))
    out_specs = [pl.BlockSpec((1, tile, w), lambda b, j: (b, j, 0)) for w in widths]
    out_shape = [jax.ShapeDtypeStruct((batch, seq, w), d) for w, d in zip(widths, dtypes)]
    out_specs[_G_V] = pl.BlockSpec((1, tile // KEY_CHUNK, VT_ROWS, KEY_CHUNK), lambda b, j: (b, j, 0, 0))
    out_shape[_G_V] = jax.ShapeDtypeStruct((batch, seq // KEY_CHUNK, VT_ROWS, KEY_CHUNK), BF16)
    for group, rows, dtype in ((_G_Q, LANES, BF16), (_G_QI, LANES, BF16), (_G_W, IDX_HEADS, F32)):
        cols = Q_TILE * max(widths[group] // LANES, 1)
        out_specs[group] = pl.BlockSpec((1, tile // Q_TILE, rows, cols), lambda b, j: (b, j, 0, 0))
        out_shape[group] = jax.ShapeDtypeStruct((batch, seq // Q_TILE, rows, cols), dtype)
    out_specs[_G_U] = pl.BlockSpec((tile, SSM_WIDTH), lambda b, j: (j, b))
    out_shape[_G_U] = jax.ShapeDtypeStruct((seq, batch * SSM_WIDTH), F32)
    return pl.pallas_call(
        _inproj_kernel,
        grid=(batch, seq // tile),
        in_specs=[pl.BlockSpec((1, tile, D_MODEL), lambda b, j: (b, j, 0)),
                  pl.BlockSpec((D_MODEL, IN_COLS), lambda b, j: (0, 0)),
                  tab_spec, tab_spec, tab_spec, tab_spec],
        out_specs=out_specs,
        out_shape=out_shape,
        compiler_params=pltpu.CompilerParams(dimension_semantics=("arbitrary", "arbitrary"),
                                             vmem_limit_bytes=VMEM_LIMIT_BYTES),
        name="inproj",
    )(x, w_bf16, *tabs)


def _key_to_float(key):
    bits = jnp.where(key >= 0, key, key ^ 0x7FFFFFFF)
    val = lax.bitcast_convert_type(bits, F32)
    val = jnp.where(key <= KEY_NEG_INF, -jnp.inf, val)
    return jnp.where(key >= KEY_POS_INF, jnp.inf, val)


def _dsa_kernel(q_ref, qi_ref, wi_ref, k_ref, vt_ref, ki_ref, ga_ref, o_ref,
                score_scr, bias_scr, row_scr, s_scr, acc_scr, og_scr, *, topk):
    i = pl.program_id(1)
    n_chunks = ((i + 1) * Q_TILE + KEY_CHUNK - 1) // KEY_CHUNK
    q0 = i * Q_TILE
    kf = float(topk)
    att_cols = ATT_REP * Q_TILE
    groups = KEY_CHUNK // SUBLANES

    def chunk_start(c):
        return pl.multiple_of(c * KEY_CHUNK, KEY_CHUNK)

    def over_chunk_pairs(body, init):
        def pair(c2, carry):
            return body(2 * c2 + 1, body(2 * c2, carry))
        carry = lax.fori_loop(0, n_chunks // 2, pair, init)
        return lax.cond(n_chunks % 2 == 1, lambda cr: body(n_chunks - 1, cr), lambda cr: cr, carry)

    def fold(v, op):
        return op(v.reshape(groups, SUBLANES, v.shape[-1]), axis=0)

    def only_rows(x, lo, n):
        parts = [jnp.zeros((lo, x.shape[1]), x.dtype)] if lo else []
        parts.append(x[lo:lo + n])
        if lo + n < x.shape[0]:
            parts.append(jnp.zeros((x.shape[0] - lo - n, x.shape[1]), x.dtype))
        return jnp.concatenate(parts, axis=0)

    qi_t = qi_ref[0, 0]
    per_tile = LANES // IDX_DIM

    def head_cols(h):
        t = h // per_tile
        return only_rows(qi_t[:, t * Q_TILE:(t + 1) * Q_TILE], (h % per_tile) * IDX_DIM, IDX_DIM)

    rhs_pairs = [jnp.concatenate([head_cols(2 * p), head_cols(2 * p + 1)], axis=1)
                 for p in range(IDX_HEADS // 2)]
    w_rows = [wi_ref[0, 0, h:h + 1, :] for h in range(IDX_HEADS)]
    q_t = q_ref[0, 0]
    qg_t = [only_rows(q_t, g * HEAD_DIM, HEAD_DIM) for g in range(ATT_KV_HEADS)]
    krow = lax.broadcasted_iota(I32, (KEY_CHUNK, Q_TILE), 0)
    kq = q0 + lax.broadcasted_iota(I32, (KEY_CHUNK, Q_TILE), 1)

    def matmul_chunk(c, stats):
        mx_a, mn_a, ge_a, gt_a, top_a = stats
        ks = chunk_start(c)
        kc = ki_ref[0, pl.ds(ks, KEY_CHUNK), :]
        acc = jnp.zeros((KEY_CHUNK, Q_TILE), F32)
        for p in range(IDX_HEADS // 2):
            s2 = jnp.dot(kc, rhs_pairs[p], preferred_element_type=F32)
            acc = acc + w_rows[2 * p] * jnp.maximum(s2[:, :Q_TILE], 0.0)
            acc = acc + w_rows[2 * p + 1] * jnp.maximum(s2[:, Q_TILE:], 0.0)
        causal = ks + krow <= kq
        sc = jnp.where(causal, acc, -jnp.inf)
        score_scr[pl.ds(ks, KEY_CHUNK), :] = sc
        kk = k_ref[0, pl.ds(ks, KEY_CHUNK), :]
        tops = []
        for g in range(ATT_KV_HEADS):
            s = jnp.dot(kk, qg_t[g], preferred_element_type=F32)
            s_scr[g, pl.ds(ks, KEY_CHUNK), :] = s
            tops.append(jnp.maximum(top_a[g], fold(s, jnp.max)))
        return (jnp.maximum(mx_a, fold(sc, jnp.max)),
                jnp.minimum(mn_a, fold(jnp.where(causal, acc, jnp.inf), jnp.min)),
                ge_a + fold(jnp.where(sc >= 0.0, 1.0, 0.0), jnp.sum),
                gt_a + fold(jnp.where(sc > 0.0, 1.0, 0.0), jnp.sum),
                tuple(tops))

    stat0 = lambda v: jnp.full((SUBLANES, Q_TILE), v, F32)
    top0 = tuple(jnp.full((SUBLANES, att_cols), NEG_BIG, F32) for _ in range(ATT_KV_HEADS))
    mx_a, mn_a, ge_a, gt_a, top_a = over_chunk_pairs(
        matmul_chunk, (stat0(-jnp.inf), stat0(jnp.inf), stat0(0.0), stat0(0.0), top0))
    m_top = [jnp.max(top_a[g], axis=0, keepdims=True) for g in range(ATT_KV_HEADS)]
    s_max = jnp.max(mx_a, axis=0, keepdims=True)
    s_min = jnp.min(mn_a, axis=0, keepdims=True)
    n_ge0 = jnp.sum(ge_a, axis=0, keepdims=True)
    n_gt0 = jnp.sum(gt_a, axis=0, keepdims=True)
    n_causal = (q0 + 1 + lax.broadcasted_iota(I32, (1, Q_TILE), 1)).astype(F32)

    def count(pred):
        def body(c, acc):
            ks = chunk_start(c)
            for part in range(KEY_CHUNK // LANES):
                hit = jnp.where(pred(score_scr[pl.ds(ks + part * LANES, LANES), :]), 1.0, 0.0)
                acc = acc + (hit[:COUNT_ROWS] + hit[COUNT_ROWS:])
            return acc
        acc = lax.fori_loop(0, n_chunks, body, jnp.zeros((COUNT_ROWS, Q_TILE), F32))
        rows = COUNT_ROWS
        while rows > SUBLANES:
            rows //= 2
            acc = acc[:rows] + acc[rows:]
        return jnp.sum(acc, axis=0, keepdims=True)

    def count_ge(value):
        vb = jnp.broadcast_to(value, (LANES, Q_TILE))
        return count(lambda sc: sc >= vb)

    zero_thr = (n_gt0 < kf) & (n_ge0 >= kf)
    keep_all = n_causal <= kf
    positive = n_gt0 >= kf
    settled = zero_thr | keep_all
    lo0 = jnp.where(settled, 0.0, jnp.where(positive, 0.0, s_min))
    hi0 = jnp.where(settled, 0.0, jnp.where(positive, s_max, 0.0))
    c_lo0 = jnp.where(positive, n_gt0, n_causal)
    c_hi0 = jnp.where(positive, 0.0, n_ge0)
    thr0 = jnp.where(keep_all, -jnp.inf, 0.0)
    done0 = jnp.where(settled, 1.0, 0.0)

    def search_pass(it, rows):
        lo, hi, c_lo, c_hi, thr, done = rows
        frac = jnp.clip((kf - c_hi + 0.5) / (c_lo - c_hi + 1.0), SEARCH_CLIP, 1.0 - SEARCH_CLIP)
        frac = jnp.where(it < SEARCH_BISECT_PASSES, 0.5, frac)
        x = hi - (hi - lo) * frac
        c = count_ge(x)
        live = done == 0.0
        above = live & (c >= kf)
        below = live & (c < kf)
        hit = live & (c == kf)
        return (jnp.where(above, x, lo), jnp.where(below, x, hi),
                jnp.where(above, c, c_lo), jnp.where(below, c, c_hi),
                jnp.where(hit, x, thr), jnp.where(hit, 1.0, done))

    def search_cond(state):
        return (state[0] < SEARCH_MAX_PASSES) & (state[1] > 0.0)

    def search_body(state):
        it, _, rows = state
        for p in range(SEARCH_GROUP):
            rows = search_pass(it + p, rows)
        return it + SEARCH_GROUP, jnp.sum(1.0 - rows[5]), rows

    state = lax.while_loop(search_cond, search_body,
                           (jnp.int32(0), jnp.sum(1.0 - done0), (lo0, hi0, c_lo0, c_hi0, thr0, done0)))
    pending, thr_s, done_s = state[1], state[2][4], state[2][5]
    row_scr[0:1, :] = thr_s
    row_scr[1:2, :] = jnp.where(zero_thr, n_gt0, 0.0)
    row_scr[2:3, :] = jnp.where(zero_thr, n_ge0, kf)

    @pl.when(pending > 0.0)
    def _():
        def count_ge_key(key):
            return count_ge(_key_to_float(key))

        key0 = jnp.where(count_ge_key(jnp.zeros((1, Q_TILE), I32)) >= kf, 0, INT_MIN).astype(I32)

        def bit_step(b, key):
            cand = key | jnp.left_shift(jnp.int32(1), 30 - b)
            return jnp.where(count_ge_key(cand) >= kf, cand, key)

        thr_f = _key_to_float(lax.fori_loop(0, 31, bit_step, key0))
        thr_fb = jnp.broadcast_to(thr_f, (LANES, Q_TILE))
        open_row = done_s == 0.0
        row_scr[0:1, :] = jnp.where(open_row, thr_f, thr_s)
        row_scr[1:2, :] = jnp.where(open_row, count(lambda sc: sc > thr_fb), row_scr[1:2, :])
        row_scr[2:3, :] = jnp.where(open_row, count(lambda sc: sc >= thr_fb), row_scr[2:3, :])

    thr = row_scr[0:1, :]
    thr_c = jnp.broadcast_to(thr, (KEY_CHUNK, Q_TILE))
    has_tie = (row_scr[2:3, :] > kf) & (thr > -jnp.inf)
    any_tie = jnp.max(jnp.where(has_tie, 1.0, 0.0)) > 0.0

    def plain_bias(c, carry):
        ks = chunk_start(c)
        keep = (score_scr[pl.ds(ks, KEY_CHUNK), :] >= thr_c) & (ks + krow <= kq)
        return jnp.where(keep, 0.0, NEG_BIG), carry

    def make_tie_bias():
        need = jnp.broadcast_to(kf - row_scr[1:2, :], (KEY_CHUNK, Q_TILE))
        lower = (lax.broadcasted_iota(I32, (KEY_CHUNK, KEY_CHUNK), 1)
                 <= lax.broadcasted_iota(I32, (KEY_CHUNK, KEY_CHUNK), 0))
        prefix_mat = jnp.where(lower, 1.0, 0.0).astype(BF16)

        def tie_bias(c, seen):
            ks = chunk_start(c)
            sc = score_scr[pl.ds(ks, KEY_CHUNK), :]
            tied = sc == thr_c
            rank = seen + jnp.dot(prefix_mat, jnp.where(tied, 1.0, 0.0).astype(BF16),
                                  preferred_element_type=F32)
            chosen = (sc > thr_c) | (tied & (rank <= need))
            return jnp.where(chosen & (ks + krow <= kq), 0.0, NEG_BIG), rank[KEY_CHUNK - 1:KEY_CHUNK, :]

        return tie_bias

    no_ties_seen = jnp.zeros((1, Q_TILE), F32)

    def pv_pass(bias_of, m, carry0):
        acc_scr[...] = jnp.zeros(acc_scr.shape, F32)

        def pv_chunk(c, carry):
            bias, carry = bias_of(c, carry)
            bias4 = jnp.concatenate([bias] * ATT_REP, axis=1)
            for g in range(ATT_KV_HEADS):
                p = jnp.exp2(s_scr[g, pl.ds(chunk_start(c), KEY_CHUNK), :] + bias4 - m[g])
                acc_scr[g] += jnp.dot(vt_ref[0, c], p.astype(BF16), preferred_element_type=F32)
            return carry

        over_chunk_pairs(pv_chunk, carry0)

    @pl.when(jnp.logical_not(any_tie))
    def _():
        pv_pass(plain_bias, m_top, jnp.int32(0))

    @pl.when(any_tie)
    def _():
        pv_pass(make_tie_bias(), m_top, no_ties_seen)

    d_min = jnp.minimum(jnp.min(acc_scr[0, KV_WIDTH:KV_WIDTH + 1, :]), jnp.min(acc_scr[1, KV_WIDTH:KV_WIDTH + 1, :]))

    @pl.when(jnp.logical_not(d_min >= DENOM_FLOOR))
    def _():
        tie_bias = make_tie_bias()

        def mask_chunk(c, carry):
            seen, m_acc = carry
            ks = chunk_start(c)
            bias, seen = tie_bias(c, seen)
            bias_scr[pl.ds(ks, KEY_CHUNK), :] = bias
            bias4 = jnp.concatenate([bias] * ATT_REP, axis=1)
            return seen, tuple(jnp.maximum(m_acc[g], fold(s_scr[g, pl.ds(ks, KEY_CHUNK), :] + bias4, jnp.max))
                               for g in range(ATT_KV_HEADS))

        m_init = tuple(jnp.full((SUBLANES, att_cols), NEG_BIG, F32) for _ in range(ATT_KV_HEADS))
        _, m_acc = lax.fori_loop(0, n_chunks, mask_chunk, (no_ties_seen, m_init))
        m_sel = [jnp.max(m_acc[g], axis=0, keepdims=True) for g in range(ATT_KV_HEADS)]
        pv_pass(lambda c, carry: (bias_scr[pl.ds(chunk_start(c), KEY_CHUNK), :], carry), m_sel, jnp.int32(0))

    for g in range(ATT_KV_HEADS):
        denom = acc_scr[g, KV_WIDTH:KV_WIDTH + 1, :]
        og_scr[g * HEAD_DIM:(g + 1) * HEAD_DIM, :] = acc_scr[g, g * HEAD_DIM:(g + 1) * HEAD_DIM, :] / denom

    for j in range(ATT_REP):
        gated = og_scr[:, j * Q_TILE:(j + 1) * Q_TILE].T * ga_ref[0, :, j * LANES:(j + 1) * LANES]
        o_ref[0, :, j * LANES:(j + 1) * LANES] = gated.astype(o_ref.dtype)


def _dsa(q_t, qi_t, wi_t, k, vt, ki, gate):
    batch, seq, _ = k.shape
    topk = min(TOPK_MAX, seq // 4)
    att_cols = ATT_REP * Q_TILE
    per_q = lambda a: pl.BlockSpec((1, 1) + a.shape[2:], lambda b, i: (b, i, 0, 0))
    per_b = lambda w: pl.BlockSpec((1, seq, w), lambda b, i: (b, 0, 0))
    vt_spec = pl.BlockSpec((1, seq // KEY_CHUNK, VT_ROWS, KEY_CHUNK), lambda b, i: (b, 0, 0, 0))
    tok_spec = pl.BlockSpec((1, Q_TILE, ATT_WIDTH), lambda b, i: (b, i, 0))
    return pl.pallas_call(
        functools.partial(_dsa_kernel, topk=topk),
        grid=(batch, seq // Q_TILE),
        in_specs=[per_q(q_t), per_q(qi_t), per_q(wi_t), per_b(KV_WIDTH), vt_spec, per_b(LANES), tok_spec],
        out_specs=tok_spec,
        out_shape=jax.ShapeDtypeStruct((batch, seq, ATT_WIDTH), BF16),
        scratch_shapes=[pltpu.VMEM((seq, Q_TILE), F32),
                        pltpu.VMEM((seq, Q_TILE), F32),
                        pltpu.VMEM((SUBLANES, Q_TILE), F32),
                        pltpu.VMEM((ATT_KV_HEADS, seq, att_cols), F32),
                        pltpu.VMEM((ATT_KV_HEADS, VT_ROWS, att_cols), F32),
                        pltpu.VMEM((KV_WIDTH, att_cols), F32)],
        compiler_params=pltpu.CompilerParams(dimension_semantics=("arbitrary", "arbitrary"),
                                             vmem_limit_bytes=VMEM_LIMIT_BYTES),
        name="dsa",
    )(q_t, qi_t, wi_t, k, vt, ki, gate)


def _s5_kernel(u_ref, bm_ref, cm_ref, are_ref, aim_ref, dsk_ref, wg_ref, bg_ref, o_ref,
               st_scr, xb_scr, carry_scr):
    tc, batch, width = u_ref.shape
    rows = tc * batch

    @pl.when(pl.program_id(0) == 0)
    def _():
        carry_scr[...] = jnp.zeros(carry_scr.shape, F32)

    u = u_ref[...].reshape(rows, width)
    st_scr[...] = jnp.dot(u.astype(BF16), bm_ref[...], preferred_element_type=F32)

    for part in range(SSM_LANES // SCAN_LANES):
        re_lo = part * SCAN_LANES
        im_lo = SSM_LANES + re_lo
        a_re = jnp.broadcast_to(are_ref[:, re_lo:re_lo + SCAN_LANES], (batch, SCAN_LANES))
        a_im = jnp.broadcast_to(aim_ref[:, re_lo:re_lo + SCAN_LANES], (batch, SCAN_LANES))

        def step(t, state):
            x_re, x_im = state
            r0 = pl.multiple_of(t * batch, batch)
            n_re = a_re * x_re - a_im * x_im + st_scr[pl.ds(r0, batch), re_lo:re_lo + SCAN_LANES]
            n_im = a_re * x_im + a_im * x_re + st_scr[pl.ds(r0, batch), im_lo:im_lo + SCAN_LANES]
            xb_scr[pl.ds(r0, batch), re_lo:re_lo + SCAN_LANES] = n_re.astype(BF16)
            xb_scr[pl.ds(r0, batch), im_lo:im_lo + SCAN_LANES] = n_im.astype(BF16)
            return n_re, n_im

        x_re, x_im = lax.fori_loop(
            0, tc, step,
            (carry_scr[:, re_lo:re_lo + SCAN_LANES], carry_scr[:, im_lo:im_lo + SCAN_LANES]), unroll=4)
        carry_scr[:, re_lo:re_lo + SCAN_LANES] = x_re
        carry_scr[:, im_lo:im_lo + SCAN_LANES] = x_im

    y = jnp.dot(xb_scr[...], cm_ref[...], preferred_element_type=F32) + dsk_ref[...] * u
    y = jax.nn.gelu(y)
    gate = jax.nn.sigmoid(jnp.dot(y.astype(BF16), wg_ref[...], preferred_element_type=F32) + bg_ref[...])
    o_ref[...] = (y * gate).reshape(tc, batch, width)


def _s5(u_tm, bmat, cmat, a_re, a_im, dskip, w_glu, b_glu):
    seq, batch, width = u_tm.shape
    tc = TIME_CHUNK
    full = lambda shape: pl.BlockSpec(shape, lambda t: tuple(0 for _ in shape))
    return pl.pallas_call(
        _s5_kernel,
        grid=(seq // tc,),
        in_specs=[pl.BlockSpec((tc, batch, width), lambda t: (t, 0, 0)),
                  full(bmat.shape), full(cmat.shape), full(a_re.shape), full(a_im.shape),
                  full(dskip.shape), full(w_glu.shape), full(b_glu.shape)],
        out_specs=pl.BlockSpec((tc, batch, width), lambda t: (t, 0, 0)),
        out_shape=jax.ShapeDtypeStruct((seq, batch, width), F32),
        scratch_shapes=[pltpu.VMEM((tc * batch, 2 * SSM_LANES), F32),
                        pltpu.VMEM((tc * batch, 2 * SSM_LANES), BF16),
                        pltpu.VMEM((batch, 2 * SSM_LANES), F32)],
        compiler_params=pltpu.CompilerParams(dimension_semantics=("arbitrary",),
                                             vmem_limit_bytes=VMEM_LIMIT_BYTES),
        name="s5",
    )(u_tm, bmat, cmat, a_re, a_im, dskip, w_glu, b_glu)


def _s5_params(lam_re, lam_im, log_dt, b_re, b_im, c_re, c_im):
    dt = jnp.exp(log_dt.astype(F32))[:, None]
    mag = jnp.exp(lam_re.astype(F32) * dt)
    ang = lam_im.astype(F32) * dt
    lb_re, lb_im = mag * jnp.cos(ang), mag * jnp.sin(ang)
    den = lam_re * lam_re + lam_im * lam_im
    k_re = ((lb_re - 1.0) * lam_re + lb_im * lam_im) / den
    k_im = (lb_im * lam_re - (lb_re - 1.0) * lam_im) / den
    bb_re = k_re[:, :, None] * b_re - k_im[:, :, None] * b_im
    bb_im = k_re[:, :, None] * b_im + k_im[:, :, None] * b_re
    eye = jnp.eye(SSM_GROUPS, dtype=F32)
    drive = lambda bb: jnp.einsum('gpn,gh->gnhp', bb, eye).reshape(SSM_WIDTH, SSM_LANES)
    read = lambda cc: jnp.einsum('gnp,gh->gphn', cc, eye).reshape(SSM_LANES, SSM_WIDTH)
    bmat = jnp.concatenate([drive(bb_re), drive(bb_im)], axis=1)
    cmat = jnp.concatenate([read(c_re.astype(F32)), read(-c_im.astype(F32))], axis=0)
    return (bmat.astype(BF16), cmat.astype(BF16),
            lb_re.reshape(1, SSM_LANES), lb_im.reshape(1, SSM_LANES))


def _combine_kernel(x_ref, ca_ref, os_ref, gs_ref, qm_ref, gm_ref, mk_ref, mv_ref,
                    wo_ref, lg_ref, lb_ref, out_ref):
    tile = x_ref.shape[1]
    nt_dims = (((1,), (1,)), ((), ()))
    lane = lax.broadcasted_iota(I32, (tile, LANES), 1)
    low = lane < HEAD_DIM

    qm = qm_ref[0].astype(F32)
    mem_tiles = []
    for t in range(MEM_WIDTH // LANES):
        qt = qm[:, t * LANES:(t + 1) * LANES]
        mk = mk_ref[0, :, t * LANES:(t + 1) * LANES]
        mv = mv_ref[0, :, t * LANES:(t + 1) * LANES]
        halves = []
        for first in (True, False):
            qh = jnp.where(low if first else ~low, qt, 0.0).astype(BF16)
            s = lax.dot_general(qh, mk, nt_dims, preferred_element_type=F32)
            p = jnp.exp(s - jnp.max(s, axis=1, keepdims=True))
            o = jnp.dot(p.astype(BF16), mv, preferred_element_type=F32)
            halves.append(o / jnp.sum(p, axis=1, keepdims=True))
        mem_tiles.append(jnp.where(low, halves[0], halves[1]))
    o_mem = jnp.concatenate(mem_tiles, axis=1)

    c_att = ca_ref[0]
    c_ssm = (os_ref[...] * gs_ref[0]).astype(BF16)
    c_mem = (o_mem * gm_ref[0]).astype(BF16)
    a1, a2 = ATT_WIDTH, ATT_WIDTH + SSM_WIDTH
    sub = (jnp.dot(c_att, wo_ref[:a1, :], preferred_element_type=F32)
           + jnp.dot(c_ssm, wo_ref[a1:a2, :], preferred_element_type=F32)
           + jnp.dot(c_mem, wo_ref[a2:, :], preferred_element_type=F32))
    h = DN_ALPHA * x_ref[0] + sub
    mu = jnp.mean(h, axis=1, keepdims=True)
    d = h - mu
    var = jnp.mean(d * d, axis=1, keepdims=True)
    out_ref[0] = d * lax.rsqrt(var + LN_EPS) * lg_ref[...] + lb_ref[...]


def _combine(x, c_att, o_ssm_tm, g_ssm, qm, g_mem, mk, mv, wo_bf16, ln_g, ln_b):
    batch, seq, _ = x.shape
    tile = TOKEN_TILE
    tok = lambda w: pl.BlockSpec((1, tile, w), lambda b, j: (b, j, 0))
    time_major = pl.BlockSpec((tile, SSM_WIDTH), lambda b, j: (j, b))
    per_b = pl.BlockSpec((1, N_MEM, MEM_WIDTH), lambda b, j: (b, 0, 0))
    const = lambda shape: pl.BlockSpec(shape, lambda b, j: tuple(0 for _ in shape))
    return pl.pallas_call(
        _combine_kernel,
        grid=(batch, seq // tile),
        in_specs=[tok(D_MODEL), tok(ATT_WIDTH), time_major, tok(SSM_WIDTH),
                  tok(MEM_WIDTH), tok(MEM_WIDTH), per_b, per_b,
                  const((D_MODEL, D_MODEL)), const((1, D_MODEL)), const((1, D_MODEL))],
        out_specs=tok(D_MODEL),
        out_shape=jax.ShapeDtypeStruct((batch, seq, D_MODEL), x.dtype),
        compiler_params=pltpu.CompilerParams(dimension_semantics=("arbitrary", "arbitrary"),
                                             vmem_limit_bytes=VMEM_LIMIT_BYTES),
        name="combine",
    )(x, c_att, o_ssm_tm, g_ssm, qm, g_mem, mk, mv, wo_bf16, ln_g, ln_b)


def _reordered_w_in(w_in):
    offs = np.cumsum([0, ATT_WIDTH, KV_WIDTH, KV_WIDTH, IDX_WIDTH, IDX_DIM, IDX_HEADS,
                      ATT_WIDTH, SSM_WIDTH, SSM_WIDTH, MEM_WIDTH, MEM_WIDTH])
    part = lambda n: w_in[:, int(offs[n]):int(offs[n + 1])]
    perm = _att_perm()
    k_idx = part(4)
    w_idx = part(5)
    zeros = jnp.zeros((D_MODEL, LANES - IDX_HEADS), w_in.dtype)
    cols = [part(0)[:, perm], part(1), part(2), part(3),
            jnp.tile(k_idx, (1, LANES // IDX_DIM)),
            jnp.concatenate([w_idx, zeros], axis=1),
            part(6)[:, perm], part(7), part(8), part(9), part(10)]
    return jnp.concatenate(cols, axis=1).astype(BF16)


def kernel(x, mem, w_in, w_mem_kv, lam_re, lam_im, log_dt, b_re, b_im, c_re, c_im, d_skip, w_glu, b_glu,
           w_out, ln_g, ln_b):
    batch, seq, _ = x.shape
    assert seq % TOKEN_TILE == 0 and seq % Q_TILE == 0 and seq % TIME_CHUNK == 0

    w_all = _reordered_w_in(w_in)
    perm = _att_perm()
    wo = jnp.concatenate([w_out[:ATT_WIDTH][perm], w_out[ATT_WIDTH:]], axis=0).astype(BF16)
    bmat, cmat, a_re, a_im = _s5_params(lam_re, lam_im, log_dt, b_re, b_im, c_re, c_im)
    tabs = _rope_tables(seq, HEAD_DIM) + _rope_tables(seq, IDX_DIM)

    mk, mv = _memkv(mem.reshape(batch * N_MEM, D_MODEL), w_mem_kv.astype(BF16))
    mk = mk.reshape(batch, N_MEM, MEM_WIDTH)
    mv = mv.reshape(batch, N_MEM, MEM_WIDTH)

    q_t, k, vt, qi_t, ki, wi_t, g_att, u_tm, g_ssm, qm, g_mem = _inproj(x, w_all, tabs)
    c_att = _dsa(q_t, qi_t, wi_t, k, vt, ki, g_att)

    o_ssm_tm = _s5(u_tm.reshape(seq, batch, SSM_WIDTH), bmat, cmat, a_re, a_im,
                   d_skip.reshape(1, SSM_WIDTH).astype(F32), w_glu.astype(BF16),
                   b_glu.reshape(1, SSM_WIDTH).astype(F32))
    o_ssm = o_ssm_tm.reshape(seq, batch * SSM_WIDTH)

    return _combine(x, c_att, o_ssm, g_ssm, qm, g_mem, mk, mv, wo,
                    ln_g.reshape(1, D_MODEL).astype(F32), ln_b.reshape(1, D_MODEL).astype(F32))
```

```python
import functools
import math

import jax
import jax.numpy as jnp
import numpy as np
from jax import lax
from jax.experimental import pallas as pl
from jax.experimental.pallas import tpu as pltpu

F32 = jnp.float32
BF16 = jnp.bfloat16
I32 = jnp.int32

D_MODEL = 1024
N_MEM = 256
HEAD_DIM = 64
ATT_HEADS = 8
ATT_KV_HEADS = 2
ATT_REP = ATT_HEADS // ATT_KV_HEADS
ATT_WIDTH = ATT_HEADS * HEAD_DIM
KV_WIDTH = ATT_KV_HEADS * HEAD_DIM
IDX_HEADS = 8
IDX_DIM = 32
IDX_WIDTH = IDX_HEADS * IDX_DIM
TOPK_MAX = 256
SSM_WIDTH = D_MODEL // 4
SSM_GROUP = 16
SSM_GROUPS = SSM_WIDTH // SSM_GROUP
SSM_STATE = 64
SSM_LANES = SSM_GROUPS * SSM_STATE
MEM_HEADS = 4
MEM_WIDTH = MEM_HEADS * HEAD_DIM
ROPE_THETA = 500000.0
ROPE_FRAC = 4
LN_EPS = 1e-5
DEPTH = 1
DN_ALPHA = (2.0 * DEPTH) ** 0.25
ATT_SCALE = HEAD_DIM ** -0.5
IDX_SCALE = IDX_HEADS ** -0.5 * IDX_DIM ** -0.5
LOG2_E = math.log2(math.e)

LANES = 128
SUBLANES = 8
BF16_ROWS = 16
VMEM_LIMIT_BYTES = 48 * 1024 * 1024
TOKEN_TILE = 512
Q_TILE = 256
KEY_CHUNK = 256
COUNT_ROWS = 64
VT_ROWS = KV_WIDTH + BF16_ROWS
TIME_CHUNK = 64
SCAN_LANES = 256
NEG_BIG = -1e30
DENOM_FLOOR = 2.0 ** -100
SEARCH_MAX_PASSES = 24
SEARCH_FIRST_PASSES = 12
SEARCH_GROUP = 2
SEARCH_BISECT_PASSES = 2
SEARCH_CLIP = 0.1

INT_MIN = -2147483648
KEY_POS_INF = 0x7F800000
KEY_NEG_INF = INT_MIN + 0x7FFFFF

_G_Q, _G_K, _G_V, _G_QI, _G_KI, _G_W, _G_GA, _G_U, _G_GS, _G_QM, _G_GM = range(11)
_GROUP_WIDTHS = [ATT_WIDTH, KV_WIDTH, KV_WIDTH, IDX_WIDTH, LANES, LANES, ATT_WIDTH,
                 SSM_WIDTH, SSM_WIDTH, MEM_WIDTH, MEM_WIDTH]
_GROUP_OFFS = [int(v) for v in np.cumsum([0] + _GROUP_WIDTHS)]
IN_COLS = _GROUP_OFFS[-1]


def _att_perm():
    cols = []
    for j in range(ATT_REP):
        for g in range(ATT_KV_HEADS):
            h = g * ATT_REP + j
            cols.extend(range(h * HEAD_DIM, (h + 1) * HEAD_DIM))
    return np.asarray(cols, dtype=np.int32)


def _rope_tables(seq, period, dtype=F32):
    r = period // ROPE_FRAC
    half = r // 2
    inv = ROPE_THETA ** (-jnp.arange(0, half, dtype=F32) * 2.0 / r)
    ang = jnp.arange(seq).astype(F32)[:, None] * inv[None, :]
    cos, sin = jnp.cos(ang), jnp.sin(ang)
    ones = jnp.ones((seq, period - r), F32)
    zeros = jnp.zeros((seq, period - r), F32)
    c = jnp.concatenate([cos, cos, ones], axis=1)
    s = jnp.concatenate([-sin, sin, zeros], axis=1)
    reps = LANES // period
    return jnp.tile(c, (1, reps)).astype(dtype), jnp.tile(s, (1, reps)).astype(dtype)


def _rope(z, cos_t, sin_t, period):
    width = z.shape[1]
    half = period // ROPE_FRAC // 2
    reps = width // LANES
    if reps > 1:
        cos_t = jnp.concatenate([cos_t] * reps, axis=1)
        sin_t = jnp.concatenate([sin_t] * reps, axis=1)
    lane = lax.broadcasted_iota(I32, z.shape, 1)
    first = (lane & (period - 1)) < half
    up = pltpu.roll(z, half, 1)
    down = pltpu.roll(z, width - half, 1)
    return z * cos_t + jnp.where(first, down, up) * sin_t


def _memkv_kernel(mem_ref, w_ref, mk_ref, mv_ref):
    z = jnp.dot(mem_ref[...].astype(BF16), w_ref[...], preferred_element_type=F32)
    mk_ref[...] = z[:, :MEM_WIDTH].astype(BF16)
    mv_ref[...] = z[:, MEM_WIDTH:].astype(BF16)


def _memkv(mem2d, w_bf16):
    rows = mem2d.shape[0]
    tile = TOKEN_TILE
    return pl.pallas_call(
        _memkv_kernel,
        grid=(rows // tile,),
        in_specs=[pl.BlockSpec((tile, D_MODEL), lambda i: (i, 0)),
                  pl.BlockSpec((D_MODEL, 2 * MEM_WIDTH), lambda i: (0, 0))],
        out_specs=[pl.BlockSpec((tile, MEM_WIDTH), lambda i: (i, 0)),
                   pl.BlockSpec((tile, MEM_WIDTH), lambda i: (i, 0))],
        out_shape=[jax.ShapeDtypeStruct((rows, MEM_WIDTH), BF16),
                   jax.ShapeDtypeStruct((rows, MEM_WIDTH), BF16)],
        compiler_params=pltpu.CompilerParams(dimension_semantics=("arbitrary",),
                                             vmem_limit_bytes=VMEM_LIMIT_BYTES),
        name="memkv",
    )(mem2d, w_bf16)


def _inproj_kernel(x_ref, w_ref, ca_ref, sa_ref, ci_ref, si_ref,
                   q_ref, k_ref, vt_ref, qi_ref, ki_ref, wi_ref, ga_ref, u_ref, gs_ref, qm_ref, gm_ref):
    tile = x_ref.shape[1]
    pos0 = pl.multiple_of(pl.program_id(1) * tile, tile)
    xb = x_ref[0].astype(BF16)

    def proj(group):
        lo, hi = _GROUP_OFFS[group], _GROUP_OFFS[group + 1]
        return jnp.dot(xb, w_ref[:, lo:hi], preferred_element_type=F32)

    ca = ca_ref[pl.ds(pos0, tile), :]
    sa = sa_ref[pl.ds(pos0, tile), :]
    ci = ci_ref[pl.ds(pos0, tile), :]
    si = si_ref[pl.ds(pos0, tile), :]

    def store_transposed(ref, z):
        for qb in range(tile // Q_TILE):
            for t in range(z.shape[1] // LANES):
                zt = z[qb * Q_TILE:(qb + 1) * Q_TILE, t * LANES:(t + 1) * LANES].T
                ref[0, qb, :, t * Q_TILE:(t + 1) * Q_TILE] = zt.astype(ref.dtype)

    store_transposed(q_ref, _rope(proj(_G_Q), ca, sa, HEAD_DIM) * (ATT_SCALE * LOG2_E))
    k_ref[0] = _rope(proj(_G_K), ca, sa, HEAD_DIM).astype(BF16)
    v = proj(_G_V)
    for c in range(tile // KEY_CHUNK):
        vt_ref[0, c, :KV_WIDTH, :] = v[c * KEY_CHUNK:(c + 1) * KEY_CHUNK, :].T.astype(BF16)
        vt_ref[0, c, KV_WIDTH:, :] = jnp.ones((BF16_ROWS, KEY_CHUNK), BF16)
    store_transposed(qi_ref, _rope(proj(_G_QI), ci, si, IDX_DIM))
    ki_ref[0] = _rope(proj(_G_KI), ci, si, IDX_DIM).astype(BF16)
    wi = proj(_G_W) * IDX_SCALE
    for qb in range(tile // Q_TILE):
        wi_ref[0, qb] = wi[qb * Q_TILE:(qb + 1) * Q_TILE, :].T[:IDX_HEADS, :]
    ga_ref[0] = jax.nn.silu(proj(_G_GA))
    u_ref[...] = proj(_G_U)
    gs_ref[0] = jax.nn.silu(proj(_G_GS))
    qm_ref[0] = (proj(_G_QM) * ATT_SCALE).astype(BF16)
    gm_ref[0] = jax.nn.silu(proj(_G_GM))


def _inproj(x, w_bf16, tabs):
    batch, seq, _ = x.shape
    tile = TOKEN_TILE
    widths = _GROUP_WIDTHS
    dtypes = [BF16, BF16, BF16, BF16, BF16, F32, F32, F32, F32, BF16, F32]
    tab_spec = pl.BlockSpec((seq, LANES), lambda b, j: (0, 0))
    out_specs = [pl.BlockSpec((1, tile, w), lambda b, j: (b, j, 0)) for w in widths]
    out_shape = [jax.ShapeDtypeStruct((batch, seq, w), d) for w, d in zip(widths, dtypes)]
    out_specs[_G_V] = pl.BlockSpec((1, tile // KEY_CHUNK, VT_ROWS, KEY_CHUNK), lambda b, j: (b, j, 0, 0))
    out_shape[_G_V] = jax.ShapeDtypeStruct((batch, seq // KEY_CHUNK, VT_ROWS, KEY_CHUNK), BF16)
    for group, rows, dtype in ((_G_Q, LANES, BF16), (_G_QI, LANES, BF16), (_G_W, IDX_HEADS, F32)):
        cols = Q_TILE * max(widths[group] // LANES, 1)
        out_specs[group] = pl.BlockSpec((1, tile // Q_TILE, rows, cols), lambda b, j: (b, j, 0, 0))
        out_shape[group] = jax.ShapeDtypeStruct((batch, seq // Q_TILE, rows, cols), dtype)
    out_specs[_G_U] = pl.BlockSpec((tile, SSM_WIDTH), lambda b, j: (j, b))
    out_shape[_G_U] = jax.ShapeDtypeStruct((seq, batch * SSM_WIDTH), F32)
    return pl.pallas_call(
        _inproj_kernel,
        grid=(batch, seq // tile),
        in_specs=[pl.BlockSpec((1, tile, D_MODEL), lambda b, j: (b, j, 0)),
                  pl.BlockSpec((D_MODEL, IN_COLS), lambda b, j: (0, 0)),
                  tab_spec, tab_spec, tab_spec, tab_spec],
        out_specs=out_specs,
        out_shape=out_shape,
        compiler_params=pltpu.CompilerParams(dimension_semantics=("arbitrary", "arbitrary"),
                                             vmem_limit_bytes=VMEM_LIMIT_BYTES),
        name="inproj",
    )(x, w_bf16, *tabs)


def _key_to_float(key):
    bits = jnp.where(key >= 0, key, key ^ 0x7FFFFFFF)
    val = lax.bitcast_convert_type(bits, F32)
    val = jnp.where(key <= KEY_NEG_INF, -jnp.inf, val)
    return jnp.where(key >= KEY_POS_INF, jnp.inf, val)


def _dsa_kernel(q_ref, qi_ref, wi_ref, k_ref, vt_ref, ki_ref, ga_ref, o_ref,
                score_scr, bias_scr, row_scr, s_scr, acc_scr, og_scr, *, topk):
    i = pl.program_id(1)
    n_chunks = ((i + 1) * Q_TILE + KEY_CHUNK - 1) // KEY_CHUNK
    q0 = i * Q_TILE
    kf = float(topk)
    att_cols = ATT_REP * Q_TILE
    groups = KEY_CHUNK // SUBLANES

    def chunk_start(c):
        return pl.multiple_of(c * KEY_CHUNK, KEY_CHUNK)

    def over_chunk_pairs(body, init):
        def pair(c2, carry):
            return body(2 * c2 + 1, body(2 * c2, carry))
        carry = lax.fori_loop(0, n_chunks // 2, pair, init)
        return lax.cond(n_chunks % 2 == 1, lambda cr: body(n_chunks - 1, cr), lambda cr: cr, carry)

    def fold(v, op):
        return op(v.reshape(groups, SUBLANES, v.shape[-1]), axis=0)

    def only_rows(x, lo, n):
        parts = [jnp.zeros((lo, x.shape[1]), x.dtype)] if lo else []
        parts.append(x[lo:lo + n])
        if lo + n < x.shape[0]:
            parts.append(jnp.zeros((x.shape[0] - lo - n, x.shape[1]), x.dtype))
        return jnp.concatenate(parts, axis=0)

    qi_t = qi_ref[0, 0]
    per_tile = LANES // IDX_DIM

    def head_cols(h):
        t = h // per_tile
        return only_rows(qi_t[:, t * Q_TILE:(t + 1) * Q_TILE], (h % per_tile) * IDX_DIM, IDX_DIM)

    rhs_pairs = [jnp.concatenate([head_cols(2 * p), head_cols(2 * p + 1)], axis=1)
                 for p in range(IDX_HEADS // 2)]
    w_rows = [wi_ref[0, 0, h:h + 1, :] for h in range(IDX_HEADS)]
    q_t = q_ref[0, 0]
    qg_t = [only_rows(q_t, g * HEAD_DIM, HEAD_DIM) for g in range(ATT_KV_HEADS)]
    krow = lax.broadcasted_iota(I32, (KEY_CHUNK, Q_TILE), 0)
    kq = q0 + lax.broadcasted_iota(I32, (KEY_CHUNK, Q_TILE), 1)

    def matmul_chunk(c, stats):
        mx_a, mn_a, ge_a, gt_a, top_a = stats
        ks = chunk_start(c)
        kc = ki_ref[0, pl.ds(ks, KEY_CHUNK), :]
        acc = jnp.zeros((KEY_CHUNK, Q_TILE), F32)
        for p in range(IDX_HEADS // 2):
            s2 = jnp.dot(kc, rhs_pairs[p], preferred_element_type=F32)
            acc = acc + w_rows[2 * p] * jnp.maximum(s2[:, :Q_TILE], 0.0)
            acc = acc + w_rows[2 * p + 1] * jnp.maximum(s2[:, Q_TILE:], 0.0)
        causal = ks + krow <= kq
        sc = jnp.where(causal, acc, -jnp.inf)
        score_scr[pl.ds(ks, KEY_CHUNK), :] = sc
        kk = k_ref[0, pl.ds(ks, KEY_CHUNK), :]
        tops = []
        for g in range(ATT_KV_HEADS):
            s = jnp.dot(kk, qg_t[g], preferred_element_type=F32)
            s_scr[g, pl.ds(ks, KEY_CHUNK), :] = s
            tops.append(jnp.maximum(top_a[g], fold(s, jnp.max)))
        return (jnp.maximum(mx_a, fold(sc, jnp.max)),
                jnp.minimum(mn_a, fold(jnp.where(causal, acc, jnp.inf), jnp.min)),
                ge_a + fold(jnp.where(sc >= 0.0, 1.0, 0.0), jnp.sum),
                gt_a + fold(jnp.where(sc > 0.0, 1.0, 0.0), jnp.sum),
                tuple(tops))

    stat0 = lambda v: jnp.full((SUBLANES, Q_TILE), v, F32)
    top0 = tuple(jnp.full((SUBLANES, att_cols), NEG_BIG, F32) for _ in range(ATT_KV_HEADS))
    mx_a, mn_a, ge_a, gt_a, top_a = over_chunk_pairs(
        matmul_chunk, (stat0(-jnp.inf), stat0(jnp.inf), stat0(0.0), stat0(0.0), top0))
    m_top = [jnp.max(top_a[g], axis=0, keepdims=True) for g in range(ATT_KV_HEADS)]
    s_max = jnp.max(mx_a, axis=0, keepdims=True)
    s_min = jnp.min(mn_a, axis=0, keepdims=True)
    n_ge0 = jnp.sum(ge_a, axis=0, keepdims=True)
    n_gt0 = jnp.sum(gt_a, axis=0, keepdims=True)
    n_causal = (q0 + 1 + lax.broadcasted_iota(I32, (1, Q_TILE), 1)).astype(F32)

    def count(pred):
        def body(c, acc):
            ks = chunk_start(c)
            for part in range(KEY_CHUNK // LANES):
                hit = jnp.where(pred(score_scr[pl.ds(ks + part * LANES, LANES), :]), 1.0, 0.0)
                acc = acc + (hit[:COUNT_ROWS] + hit[COUNT_ROWS:])
            return acc
        acc = lax.fori_loop(0, n_chunks, body, jnp.zeros((COUNT_ROWS, Q_TILE), F32))
        rows = COUNT_ROWS
        while rows > SUBLANES:
            rows //= 2
            acc = acc[:rows] + acc[rows:]
        return jnp.sum(acc, axis=0, keepdims=True)

    def count_ge(value):
        vb = jnp.broadcast_to(value, (LANES, Q_TILE))
        return count(lambda sc: sc >= vb)

    zero_thr = (n_gt0 < kf) & (n_ge0 >= kf)
    keep_all = n_causal <= kf
    positive = n_gt0 >= kf
    settled = zero_thr | keep_all
    lo0 = jnp.where(settled, 0.0, jnp.where(positive, 0.0, s_min))
    hi0 = jnp.where(settled, 0.0, jnp.where(positive, s_max, 0.0))
    c_lo0 = jnp.where(positive, n_gt0, n_causal)
    c_hi0 = jnp.where(positive, 0.0, n_ge0)
    thr0 = jnp.where(keep_all, -jnp.inf, 0.0)
    done0 = jnp.where(settled, 1.0, 0.0)

    def search_pass(bisect, rows):
        lo, hi, c_lo, c_hi, thr, done = rows
        if bisect:
            frac = 0.5
        else:
            frac = jnp.clip((kf - c_hi + 0.5) / (c_lo - c_hi + 1.0), SEARCH_CLIP, 1.0 - SEARCH_CLIP)
        x = hi - (hi - lo) * frac
        c = count_ge(x)
        live = done == 0.0
        above = live & (c >= kf)
        below = live & (c < kf)
        hit = live & (c == kf)
        return (jnp.where(above, x, lo), jnp.where(below, x, hi),
                jnp.where(above, c, c_lo), jnp.where(below, c, c_hi),
                jnp.where(hit, x, thr), jnp.where(hit, 1.0, done))

    def search_cond(state):
        return (state[0] < SEARCH_MAX_PASSES) & (state[1] > 0.0)

    def search_body(state):
        it, _, rows = state
        for _ in range(SEARCH_GROUP):
            rows = search_pass(False, rows)
        return it + SEARCH_GROUP, jnp.sum(1.0 - rows[5]), rows

    def first_passes(rows):
        for p in range(SEARCH_FIRST_PASSES):
            rows = search_pass(p < SEARCH_BISECT_PASSES, rows)
        return rows

    rows0 = (lo0, hi0, c_lo0, c_hi0, thr0, done0)
    rows1 = lax.cond(jnp.sum(1.0 - done0) > 0.0, first_passes, lambda rows: rows, rows0)
    state = lax.while_loop(search_cond, search_body,
                           (jnp.int32(SEARCH_FIRST_PASSES), jnp.sum(1.0 - rows1[5]), rows1))
    pending, thr_s, done_s = state[1], state[2][4], state[2][5]
    row_scr[0:1, :] = thr_s
    row_scr[1:2, :] = jnp.where(zero_thr, n_gt0, 0.0)
    row_scr[2:3, :] = jnp.where(zero_thr, n_ge0, kf)

    @pl.when(pending > 0.0)
    def _():
        def count_ge_key(key):
            return count_ge(_key_to_float(key))

        key0 = jnp.where(count_ge_key(jnp.zeros((1, Q_TILE), I32)) >= kf, 0, INT_MIN).astype(I32)

        def bit_step(b, key):
            cand = key | jnp.left_shift(jnp.int32(1), 30 - b)
            return jnp.where(count_ge_key(cand) >= kf, cand, key)

        thr_f = _key_to_float(lax.fori_loop(0, 31, bit_step, key0))
        thr_fb = jnp.broadcast_to(thr_f, (LANES, Q_TILE))
        open_row = done_s == 0.0
        row_scr[0:1, :] = jnp.where(open_row, thr_f, thr_s)
        row_scr[1:2, :] = jnp.where(open_row, count(lambda sc: sc > thr_fb), row_scr[1:2, :])
        row_scr[2:3, :] = jnp.where(open_row, count(lambda sc: sc >= thr_fb), row_scr[2:3, :])

    thr = row_scr[0:1, :]
    thr_c = jnp.broadcast_to(thr, (KEY_CHUNK, Q_TILE))
    has_tie = (row_scr[2:3, :] > kf) & (thr > -jnp.inf)
    any_tie = jnp.max(jnp.where(has_tie, 1.0, 0.0)) > 0.0

    def plain_bias(c, carry):
        ks = chunk_start(c)
        keep = (score_scr[pl.ds(ks, KEY_CHUNK), :] >= thr_c) & (ks + krow <= kq)
        return jnp.where(keep, 0.0, NEG_BIG), carry

    def make_tie_bias():
        need = jnp.broadcast_to(kf - row_scr[1:2, :], (KEY_CHUNK, Q_TILE))
        lower = (lax.broadcasted_iota(I32, (KEY_CHUNK, KEY_CHUNK), 1)
                 <= lax.broadcasted_iota(I32, (KEY_CHUNK, KEY_CHUNK), 0))
        prefix_mat = jnp.where(lower, 1.0, 0.0).astype(BF16)

        def tie_bias(c, seen):
            ks = chunk_start(c)
            sc = score_scr[pl.ds(ks, KEY_CHUNK), :]
            tied = sc == thr_c
            rank = seen + jnp.dot(prefix_mat, jnp.where(tied, 1.0, 0.0).astype(BF16),
                                  preferred_element_type=F32)
            chosen = (sc > thr_c) | (tied & (rank <= need))
            return jnp.where(chosen & (ks + krow <= kq), 0.0, NEG_BIG), rank[KEY_CHUNK - 1:KEY_CHUNK, :]

        return tie_bias

    no_ties_seen = jnp.zeros((1, Q_TILE), F32)

    def pv_pass(bias_of, m, carry0):
        acc_scr[...] = jnp.zeros(acc_scr.shape, F32)

        def pv_chunk(c, carry):
            bias, carry = bias_of(c, carry)
            bias4 = jnp.concatenate([bias] * ATT_REP, axis=1)
            for g in range(ATT_KV_HEADS):
                p = jnp.exp2(s_scr[g, pl.ds(chunk_start(c), KEY_CHUNK), :] + bias4 - m[g])
                acc_scr[g] += jnp.dot(vt_ref[0, c], p.astype(BF16), preferred_element_type=F32)
            return carry

        over_chunk_pairs(pv_chunk, carry0)

    @pl.when(jnp.logical_not(any_tie))
    def _():
        pv_pass(plain_bias, m_top, jnp.int32(0))

    @pl.when(any_tie)
    def _():
        pv_pass(make_tie_bias(), m_top, no_ties_seen)

    d_min = jnp.min(jnp.minimum(acc_scr[0, KV_WIDTH:KV_WIDTH + 1, :], acc_scr[1, KV_WIDTH:KV_WIDTH + 1, :]))

    @pl.when(jnp.logical_not(d_min >= DENOM_FLOOR))
    def _():
        tie_bias = make_tie_bias()

        def mask_chunk(c, carry):
            seen, m_acc = carry
            ks = chunk_start(c)
            bias, seen = tie_bias(c, seen)
            bias_scr[pl.ds(ks, KEY_CHUNK), :] = bias
            bias4 = jnp.concatenate([bias] * ATT_REP, axis=1)
            return seen, tuple(jnp.maximum(m_acc[g], fold(s_scr[g, pl.ds(ks, KEY_CHUNK), :] + bias4, jnp.max))
                               for g in range(ATT_KV_HEADS))

        m_init = tuple(jnp.full((SUBLANES, att_cols), NEG_BIG, F32) for _ in range(ATT_KV_HEADS))
        _, m_acc = lax.fori_loop(0, n_chunks, mask_chunk, (no_ties_seen, m_init))
        m_sel = [jnp.max(m_acc[g], axis=0, keepdims=True) for g in range(ATT_KV_HEADS)]
        pv_pass(lambda c, carry: (bias_scr[pl.ds(chunk_start(c), KEY_CHUNK), :], carry), m_sel, jnp.int32(0))

    for g in range(ATT_KV_HEADS):
        denom = acc_scr[g, KV_WIDTH:KV_WIDTH + 1, :]
        og_scr[g * HEAD_DIM:(g + 1) * HEAD_DIM, :] = acc_scr[g, g * HEAD_DIM:(g + 1) * HEAD_DIM, :] / denom

    for j in range(ATT_REP):
        gated = og_scr[:, j * Q_TILE:(j + 1) * Q_TILE].T * ga_ref[0, :, j * LANES:(j + 1) * LANES]
        o_ref[0, :, j * LANES:(j + 1) * LANES] = gated.astype(o_ref.dtype)


def _dsa(q_t, qi_t, wi_t, k, vt, ki, gate):
    batch, seq, _ = k.shape
    topk = min(TOPK_MAX, seq // 4)
    att_cols = ATT_REP * Q_TILE
    per_q = lambda a: pl.BlockSpec((1, 1) + a.shape[2:], lambda b, i: (b, i, 0, 0))
    per_b = lambda w: pl.BlockSpec((1, seq, w), lambda b, i: (b, 0, 0))
    vt_spec = pl.BlockSpec((1, seq // KEY_CHUNK, VT_ROWS, KEY_CHUNK), lambda b, i: (b, 0, 0, 0))
    tok_spec = pl.BlockSpec((1, Q_TILE, ATT_WIDTH), lambda b, i: (b, i, 0))
    return pl.pallas_call(
        functools.partial(_dsa_kernel, topk=topk),
        grid=(batch, seq // Q_TILE),
        in_specs=[per_q(q_t), per_q(qi_t), per_q(wi_t), per_b(KV_WIDTH), vt_spec, per_b(LANES), tok_spec],
        out_specs=tok_spec,
        out_shape=jax.ShapeDtypeStruct((batch, seq, ATT_WIDTH), BF16),
        scratch_shapes=[pltpu.VMEM((seq, Q_TILE), F32),
                        pltpu.VMEM((seq, Q_TILE), F32),
                        pltpu.VMEM((SUBLANES, Q_TILE), F32),
                        pltpu.VMEM((ATT_KV_HEADS, seq, att_cols), F32),
                        pltpu.VMEM((ATT_KV_HEADS, VT_ROWS, att_cols), F32),
                        pltpu.VMEM((KV_WIDTH, att_cols), F32)],
        compiler_params=pltpu.CompilerParams(dimension_semantics=("arbitrary", "arbitrary"),
                                             vmem_limit_bytes=VMEM_LIMIT_BYTES),
        name="dsa",
    )(q_t, qi_t, wi_t, k, vt, ki, gate)


def _s5_kernel(u_ref, bm_ref, cm_ref, are_ref, aim_ref, dsk_ref, wg_ref, bg_ref, o_ref,
               st_scr, xb_scr, carry_scr):
    tc, batch, width = u_ref.shape
    rows = tc * batch

    @pl.when(pl.program_id(0) == 0)
    def _():
        carry_scr[...] = jnp.zeros(carry_scr.shape, F32)

    u = u_ref[...].reshape(rows, width)
    st_scr[...] = jnp.dot(u.astype(BF16), bm_ref[...], preferred_element_type=F32)

    for part in range(SSM_LANES // SCAN_LANES):
        re_lo = part * SCAN_LANES
        im_lo = SSM_LANES + re_lo
        a_re = jnp.broadcast_to(are_ref[:, re_lo:re_lo + SCAN_LANES], (batch, SCAN_LANES))
        a_im = jnp.broadcast_to(aim_ref[:, re_lo:re_lo + SCAN_LANES], (batch, SCAN_LANES))

        def step(t, state):
            x_re, x_im = state
            r0 = pl.multiple_of(t * batch, batch)
            n_re = a_re * x_re - a_im * x_im + st_scr[pl.ds(r0, batch), re_lo:re_lo + SCAN_LANES]
            n_im = a_re * x_im + a_im * x_re + st_scr[pl.ds(r0, batch), im_lo:im_lo + SCAN_LANES]
            xb_scr[pl.ds(r0, batch), re_lo:re_lo + SCAN_LANES] = n_re.astype(BF16)
            xb_scr[pl.ds(r0, batch), im_lo:im_lo + SCAN_LANES] = n_im.astype(BF16)
            return n_re, n_im

        x_re, x_im = lax.fori_loop(
            0, tc, step,
            (carry_scr[:, re_lo:re_lo + SCAN_LANES], carry_scr[:, im_lo:im_lo + SCAN_LANES]), unroll=4)
        carry_scr[:, re_lo:re_lo + SCAN_LANES] = x_re
        carry_scr[:, im_lo:im_lo + SCAN_LANES] = x_im

    y = jnp.dot(xb_scr[...], cm_ref[...], preferred_element_type=F32) + dsk_ref[...] * u
    y = jax.nn.gelu(y)
    gate = jax.nn.sigmoid(jnp.dot(y.astype(BF16), wg_ref[...], preferred_element_type=F32) + bg_ref[...])
    o_ref[...] = (y * gate).reshape(tc, batch, width)


def _s5(u_tm, bmat, cmat, a_re, a_im, dskip, w_glu, b_glu):
    seq, batch, width = u_tm.shape
    tc = TIME_CHUNK
    full = lambda shape: pl.BlockSpec(shape, lambda t: tuple(0 for _ in shape))
    return pl.pallas_call(
        _s5_kernel,
        grid=(seq // tc,),
        in_specs=[pl.BlockSpec((tc, batch, width), lambda t: (t, 0, 0)),
                  full(bmat.shape), full(cmat.shape), full(a_re.shape), full(a_im.shape),
                  full(dskip.shape), full(w_glu.shape), full(b_glu.shape)],
        out_specs=pl.BlockSpec((tc, batch, width), lambda t: (t, 0, 0)),
        out_shape=jax.ShapeDtypeStruct((seq, batch, width), F32),
        scratch_shapes=[pltpu.VMEM((tc * batch, 2 * SSM_LANES), F32),
                        pltpu.VMEM((tc * batch, 2 * SSM_LANES), BF16),
                        pltpu.VMEM((batch, 2 * SSM_LANES), F32)],
        compiler_params=pltpu.CompilerParams(dimension_semantics=("arbitrary",),
                                             vmem_limit_bytes=VMEM_LIMIT_BYTES),
        name="s5",
    )(u_tm, bmat, cmat, a_re, a_im, dskip, w_glu, b_glu)


def _s5_params(lam_re, lam_im, log_dt, b_re, b_im, c_re, c_im):
    dt = jnp.exp(log_dt.astype(F32))[:, None]
    mag = jnp.exp(lam_re.astype(F32) * dt)
    ang = lam_im.astype(F32) * dt
    lb_re, lb_im = mag * jnp.cos(ang), mag * jnp.sin(ang)
    den = lam_re * lam_re + lam_im * lam_im
    k_re = ((lb_re - 1.0) * lam_re + lb_im * lam_im) / den
    k_im = (lb_im * lam_re - (lb_re - 1.0) * lam_im) / den
    bb_re = k_re[:, :, None] * b_re - k_im[:, :, None] * b_im
    bb_im = k_re[:, :, None] * b_im + k_im[:, :, None] * b_re
    eye = jnp.eye(SSM_GROUPS, dtype=F32)
    drive = lambda bb: jnp.einsum('gpn,gh->gnhp', bb, eye).reshape(SSM_WIDTH, SSM_LANES)
    read = lambda cc: jnp.einsum('gnp,gh->gphn', cc, eye).reshape(SSM_LANES, SSM_WIDTH)
    bmat = jnp.concatenate([drive(bb_re), drive(bb_im)], axis=1)
    cmat = jnp.concatenate([read(c_re.astype(F32)), read(-c_im.astype(F32))], axis=0)
    return (bmat.astype(BF16), cmat.astype(BF16),
            lb_re.reshape(1, SSM_LANES), lb_im.reshape(1, SSM_LANES))


def _combine_kernel(x_ref, ca_ref, os_ref, gs_ref, qm_ref, gm_ref, mk_ref, mv_ref,
                    wo_ref, lg_ref, lb_ref, out_ref):
    tile = x_ref.shape[1]
    nt_dims = (((1,), (1,)), ((), ()))
    lane = lax.broadcasted_iota(I32, (tile, LANES), 1)
    low = lane < HEAD_DIM

    qm = qm_ref[0].astype(F32)
    mem_tiles = []
    for t in range(MEM_WIDTH // LANES):
        qt = qm[:, t * LANES:(t + 1) * LANES]
        mk = mk_ref[0, :, t * LANES:(t + 1) * LANES]
        mv = mv_ref[0, :, t * LANES:(t + 1) * LANES]
        halves = []
        for first in (True, False):
            qh = jnp.where(low if first else ~low, qt, 0.0).astype(BF16)
            s = lax.dot_general(qh, mk, nt_dims, preferred_element_type=F32)
            p = jnp.exp(s - jnp.max(s, axis=1, keepdims=True))
            o = jnp.dot(p.astype(BF16), mv, preferred_element_type=F32)
            halves.append(o / jnp.sum(p, axis=1, keepdims=True))
        mem_tiles.append(jnp.where(low, halves[0], halves[1]))
    o_mem = jnp.concatenate(mem_tiles, axis=1)

    c_att = ca_ref[0]
    c_ssm = (os_ref[...] * gs_ref[0]).astype(BF16)
    c_mem = (o_mem * gm_ref[0]).astype(BF16)
    a1, a2 = ATT_WIDTH, ATT_WIDTH + SSM_WIDTH
    sub = (jnp.dot(c_att, wo_ref[:a1, :], preferred_element_type=F32)
           + jnp.dot(c_ssm, wo_ref[a1:a2, :], preferred_element_type=F32)
           + jnp.dot(c_mem, wo_ref[a2:, :], preferred_element_type=F32))
    h = DN_ALPHA * x_ref[0] + sub
    mu = jnp.mean(h, axis=1, keepdims=True)
    d = h - mu
    var = jnp.mean(d * d, axis=1, keepdims=True)
    out_ref[0] = d * lax.rsqrt(var + LN_EPS) * lg_ref[...] + lb_ref[...]


def _combine(x, c_att, o_ssm_tm, g_ssm, qm, g_mem, mk, mv, wo_bf16, ln_g, ln_b):
    batch, seq, _ = x.shape
    tile = TOKEN_TILE
    tok = lambda w: pl.BlockSpec((1, tile, w), lambda b, j: (b, j, 0))
    time_major = pl.BlockSpec((tile, SSM_WIDTH), lambda b, j: (j, b))
    per_b = pl.BlockSpec((1, N_MEM, MEM_WIDTH), lambda b, j: (b, 0, 0))
    const = lambda shape: pl.BlockSpec(shape, lambda b, j: tuple(0 for _ in shape))
    return pl.pallas_call(
        _combine_kernel,
        grid=(batch, seq // tile),
        in_specs=[tok(D_MODEL), tok(ATT_WIDTH), time_major, tok(SSM_WIDTH),
                  tok(MEM_WIDTH), tok(MEM_WIDTH), per_b, per_b,
                  const((D_MODEL, D_MODEL)), const((1, D_MODEL)), const((1, D_MODEL))],
        out_specs=tok(D_MODEL),
        out_shape=jax.ShapeDtypeStruct((batch, seq, D_MODEL), x.dtype),
        compiler_params=pltpu.CompilerParams(dimension_semantics=("arbitrary", "arbitrary"),
                                             vmem_limit_bytes=VMEM_LIMIT_BYTES),
        name="combine",
    )(x, c_att, o_ssm_tm, g_ssm, qm, g_mem, mk, mv, wo_bf16, ln_g, ln_b)


def _reordered_w_in(w_in):
    offs = np.cumsum([0, ATT_WIDTH, KV_WIDTH, KV_WIDTH, IDX_WIDTH, IDX_DIM, IDX_HEADS,
                      ATT_WIDTH, SSM_WIDTH, SSM_WIDTH, MEM_WIDTH, MEM_WIDTH])
    part = lambda n: w_in[:, int(offs[n]):int(offs[n + 1])]
    perm = _att_perm()
    k_idx = part(4)
    w_idx = part(5)
    zeros = jnp.zeros((D_MODEL, LANES - IDX_HEADS), w_in.dtype)
    cols = [part(0)[:, perm], part(1), part(2), part(3),
            jnp.tile(k_idx, (1, LANES // IDX_DIM)),
            jnp.concatenate([w_idx, zeros], axis=1),
            part(6)[:, perm], part(7), part(8), part(9), part(10)]
    return jnp.concatenate(cols, axis=1).astype(BF16)


def kernel(x, mem, w_in, w_mem_kv, lam_re, lam_im, log_dt, b_re, b_im, c_re, c_im, d_skip, w_glu, b_glu,
           w_out, ln_g, ln_b):
    batch, seq, _ = x.shape
    assert seq % TOKEN_TILE == 0 and seq % Q_TILE == 0 and seq % TIME_CHUNK == 0

    w_all = _reordered_w_in(w_in)
    perm = _att_perm()
    wo = jnp.concatenate([w_out[:ATT_WIDTH][perm], w_out[ATT_WIDTH:]], axis=0).astype(BF16)
    bmat, cmat, a_re, a_im = _s5_params(lam_re, lam_im, log_dt, b_re, b_im, c_re, c_im)
    tabs = _rope_tables(seq, HEAD_DIM) + _rope_tables(seq, IDX_DIM)

    mk, mv = _memkv(mem.reshape(batch * N_MEM, D_MODEL), w_mem_kv.astype(BF16))
    mk = mk.reshape(batch, N_MEM, MEM_WIDTH)
    mv = mv.reshape(batch, N_MEM, MEM_WIDTH)

    q_t, k, vt, qi_t, ki, wi_t, g_att, u_tm, g_ssm, qm, g_mem = _inproj(x, w_all, tabs)
    c_att = _dsa(q_t, qi_t, wi_t, k, vt, ki, g_att)

    o_ssm_tm = _s5(u_tm.reshape(seq, batch, SSM_WIDTH), bmat, cmat, a_re, a_im,
                   d_skip.reshape(1, SSM_WIDTH).astype(F32), w_glu.astype(BF16),
                   b_glu.reshape(1, SSM_WIDTH).astype(F32))
    o_ssm = o_ssm_tm.reshape(seq, batch * SSM_WIDTH)

    return _combine(x, c_att, o_ssm, g_ssm, qm, g_mem, mk, mv, wo,
                    ln_g.reshape(1, D_MODEL).astype(F32), ln_b.reshape(1, D_MODEL).astype(F32))
```

```python
import functools
import math

import jax
import jax.numpy as jnp
import numpy as np
from jax import lax
from jax.experimental import pallas as pl
from jax.experimental.pallas import tpu as pltpu

F32 = jnp.float32
BF16 = jnp.bfloat16
I32 = jnp.int32

D_MODEL = 1024
N_MEM = 256
HEAD_DIM = 64
ATT_HEADS = 8
ATT_KV_HEADS = 2
ATT_REP = ATT_HEADS // ATT_KV_HEADS
ATT_WIDTH = ATT_HEADS * HEAD_DIM
KV_WIDTH = ATT_KV_HEADS * HEAD_DIM
IDX_HEADS = 8
IDX_DIM = 32
IDX_WIDTH = IDX_HEADS * IDX_DIM
TOPK_MAX = 256
SSM_WIDTH = D_MODEL // 4
SSM_GROUP = 16
SSM_GROUPS = SSM_WIDTH // SSM_GROUP
SSM_STATE = 64
SSM_LANES = SSM_GROUPS * SSM_STATE
MEM_HEADS = 4
MEM_WIDTH = MEM_HEADS * HEAD_DIM
ROPE_THETA = 500000.0
ROPE_FRAC = 4
LN_EPS = 1e-5
DEPTH = 1
DN_ALPHA = (2.0 * DEPTH) ** 0.25
ATT_SCALE = HEAD_DIM ** -0.5
IDX_SCALE = IDX_HEADS ** -0.5 * IDX_DIM ** -0.5
LOG2_E = math.log2(math.e)

LANES = 128
SUBLANES = 8
BF16_ROWS = 16
VMEM_LIMIT_BYTES = 48 * 1024 * 1024
TOKEN_TILE = 512
Q_TILE = 256
KEY_CHUNK = 256
COUNT_ROWS = 64
VT_ROWS = KV_WIDTH + BF16_ROWS
TIME_CHUNK = 128
SCAN_SPLIT = 2
SCAN_LANES = 256
NEG_BIG = -1e30
DENOM_FLOOR = 2.0 ** -100
SEARCH_MAX_PASSES = 24
SEARCH_FIRST_PASSES = 12
SEARCH_GROUP = 2
SEARCH_BISECT_PASSES = 2
SEARCH_CLIP = 0.1

INT_MIN = -2147483648
KEY_POS_INF = 0x7F800000
KEY_NEG_INF = INT_MIN + 0x7FFFFF

_G_Q, _G_K, _G_V, _G_QI, _G_KI, _G_W, _G_GA, _G_U, _G_GS, _G_QM, _G_GM = range(11)
_GROUP_WIDTHS = [ATT_WIDTH, KV_WIDTH, KV_WIDTH, IDX_WIDTH, LANES, LANES, ATT_WIDTH,
                 SSM_WIDTH, SSM_WIDTH, MEM_WIDTH, MEM_WIDTH]
_GROUP_OFFS = [int(v) for v in np.cumsum([0] + _GROUP_WIDTHS)]
IN_COLS = _GROUP_OFFS[-1]


def _att_perm():
    cols = []
    for j in range(ATT_REP):
        for g in range(ATT_KV_HEADS):
            h = g * ATT_REP + j
            cols.extend(range(h * HEAD_DIM, (h + 1) * HEAD_DIM))
    return np.asarray(cols, dtype=np.int32)


def _rope_tables(seq, period, dtype=F32):
    r = period // ROPE_FRAC
    half = r // 2
    inv = ROPE_THETA ** (-jnp.arange(0, half, dtype=F32) * 2.0 / r)
    ang = jnp.arange(seq).astype(F32)[:, None] * inv[None, :]
    cos, sin = jnp.cos(ang), jnp.sin(ang)
    ones = jnp.ones((seq, period - r), F32)
    zeros = jnp.zeros((seq, period - r), F32)
    c = jnp.concatenate([cos, cos, ones], axis=1)
    s = jnp.concatenate([-sin, sin, zeros], axis=1)
    reps = LANES // period
    return jnp.tile(c, (1, reps)).astype(dtype), jnp.tile(s, (1, reps)).astype(dtype)


def _rope(z, cos_t, sin_t, period):
    width = z.shape[1]
    half = period // ROPE_FRAC // 2
    reps = width // LANES
    if reps > 1:
        cos_t = jnp.concatenate([cos_t] * reps, axis=1)
        sin_t = jnp.concatenate([sin_t] * reps, axis=1)
    lane = lax.broadcasted_iota(I32, z.shape, 1)
    first = (lane & (period - 1)) < half
    up = pltpu.roll(z, half, 1)
    down = pltpu.roll(z, width - half, 1)
    return z * cos_t + jnp.where(first, down, up) * sin_t


def _memkv_kernel(mem_ref, w_ref, mk_ref, mv_ref):
    z = jnp.dot(mem_ref[...].astype(BF16), w_ref[...], preferred_element_type=F32)
    mk_ref[...] = z[:, :MEM_WIDTH].astype(BF16)
    mv_ref[...] = z[:, MEM_WIDTH:].astype(BF16)


def _memkv(mem2d, w_bf16):
    rows = mem2d.shape[0]
    tile = TOKEN_TILE
    return pl.pallas_call(
        _memkv_kernel,
        grid=(rows // tile,),
        in_specs=[pl.BlockSpec((tile, D_MODEL), lambda i: (i, 0)),
                  pl.BlockSpec((D_MODEL, 2 * MEM_WIDTH), lambda i: (0, 0))],
        out_specs=[pl.BlockSpec((tile, MEM_WIDTH), lambda i: (i, 0)),
                   pl.BlockSpec((tile, MEM_WIDTH), lambda i: (i, 0))],
        out_shape=[jax.ShapeDtypeStruct((rows, MEM_WIDTH), BF16),
                   jax.ShapeDtypeStruct((rows, MEM_WIDTH), BF16)],
        compiler_params=pltpu.CompilerParams(dimension_semantics=("arbitrary",),
                                             vmem_limit_bytes=VMEM_LIMIT_BYTES),
        name="memkv",
    )(mem2d, w_bf16)


def _inproj_kernel(x_ref, w_ref, ca_ref, sa_ref, ci_ref, si_ref,
                   q_ref, k_ref, vt_ref, qi_ref, ki_ref, wi_ref, ga_ref, u_ref, gs_ref, qm_ref, gm_ref):
    tile = x_ref.shape[1]
    pos0 = pl.multiple_of(pl.program_id(1) * tile, tile)
    xb = x_ref[0].astype(BF16)

    def proj(group):
        lo, hi = _GROUP_OFFS[group], _GROUP_OFFS[group + 1]
        return jnp.dot(xb, w_ref[:, lo:hi], preferred_element_type=F32)

    ca = ca_ref[pl.ds(pos0, tile), :]
    sa = sa_ref[pl.ds(pos0, tile), :]
    ci = ci_ref[pl.ds(pos0, tile), :]
    si = si_ref[pl.ds(pos0, tile), :]

    def store_transposed(ref, z):
        for qb in range(tile // Q_TILE):
            for t in range(z.shape[1] // LANES):
                zt = z[qb * Q_TILE:(qb + 1) * Q_TILE, t * LANES:(t + 1) * LANES].T
                ref[0, qb, :, t * Q_TILE:(t + 1) * Q_TILE] = zt.astype(ref.dtype)

    store_transposed(q_ref, _rope(proj(_G_Q), ca, sa, HEAD_DIM) * (ATT_SCALE * LOG2_E))
    k_ref[0] = _rope(proj(_G_K), ca, sa, HEAD_DIM).astype(BF16)
    v = proj(_G_V)
    for c in range(tile // KEY_CHUNK):
        vt_ref[0, c, :KV_WIDTH, :] = v[c * KEY_CHUNK:(c + 1) * KEY_CHUNK, :].T.astype(BF16)
        vt_ref[0, c, KV_WIDTH:, :] = jnp.ones((BF16_ROWS, KEY_CHUNK), BF16)
    store_transposed(qi_ref, _rope(proj(_G_QI), ci, si, IDX_DIM))
    ki_ref[0] = _rope(proj(_G_KI), ci, si, IDX_DIM).astype(BF16)
    wi = proj(_G_W) * IDX_SCALE
    for qb in range(tile // Q_TILE):
        wi_ref[0, qb] = wi[qb * Q_TILE:(qb + 1) * Q_TILE, :].T[:IDX_HEADS, :]
    ga_ref[0] = jax.nn.silu(proj(_G_GA))
    u_ref[...] = proj(_G_U)
    gs_ref[0] = jax.nn.silu(proj(_G_GS))
    qm_ref[0] = (proj(_G_QM) * ATT_SCALE).astype(BF16)
    gm_ref[0] = jax.nn.silu(proj(_G_GM))


def _inproj(x, w_bf16, tabs):
    batch, seq, _ = x.shape
    tile = TOKEN_TILE
    widths = _GROUP_WIDTHS
    dtypes = [BF16, BF16, BF16, BF16, BF16, F32, F32, F32, F32, BF16, F32]
    tab_spec = pl.BlockSpec((seq, LANES), lambda b, j: (0, 0))
    out_specs = [pl.BlockSpec((1, tile, w), lambda b, j: (b, j, 0)) for w in widths]
    out_shape = [jax.ShapeDtypeStruct((batch, seq, w), d) for w, d in zip(widths, dtypes)]
    out_specs[_G_V] = pl.BlockSpec((1, tile // KEY_CHUNK, VT_ROWS, KEY_CHUNK), lambda b, j: (b, j, 0, 0))
    out_shape[_G_V] = jax.ShapeDtypeStruct((batch, seq // KEY_CHUNK, VT_ROWS, KEY_CHUNK), BF16)
    for group, rows, dtype in ((_G_Q, LANES, BF16), (_G_QI, LANES, BF16), (_G_W, IDX_HEADS, F32)):
        cols = Q_TILE * max(widths[group] // LANES, 1)
        out_specs[group] = pl.BlockSpec((1, tile // Q_TILE, rows, cols), lambda b, j: (b, j, 0, 0))
        out_shape[group] = jax.ShapeDtypeStruct((batch, seq // Q_TILE, rows, cols), dtype)
    out_specs[_G_U] = pl.BlockSpec((tile, SSM_WIDTH), lambda b, j: (j, b))
    out_shape[_G_U] = jax.ShapeDtypeStruct((seq, batch * SSM_WIDTH), F32)
    return pl.pallas_call(
        _inproj_kernel,
        grid=(batch, seq // tile),
        in_specs=[pl.BlockSpec((1, tile, D_MODEL), lambda b, j: (b, j, 0)),
                  pl.BlockSpec((D_MODEL, IN_COLS), lambda b, j: (0, 0)),
                  tab_spec, tab_spec, tab_spec, tab_spec],
        out_specs=out_specs,
        out_shape=out_shape,
        compiler_params=pltpu.CompilerParams(dimension_semantics=("arbitrary", "arbitrary"),
                                             vmem_limit_bytes=VMEM_LIMIT_BYTES),
        name="inproj",
    )(x, w_bf16, *tabs)


def _key_to_float(key):
    bits = jnp.where(key >= 0, key, key ^ 0x7FFFFFFF)
    val = lax.bitcast_convert_type(bits, F32)
    val = jnp.where(key <= KEY_NEG_INF, -jnp.inf, val)
    return jnp.where(key >= KEY_POS_INF, jnp.inf, val)


def _dsa_kernel(q_ref, qi_ref, wi_ref, k_ref, vt_ref, ki_ref, ga_ref, o_ref,
                score_scr, bias_scr, row_scr, s_scr, acc_scr, og_scr, *, topk):
    i = pl.program_id(1)
    n_chunks = ((i + 1) * Q_TILE + KEY_CHUNK - 1) // KEY_CHUNK
    q0 = i * Q_TILE
    kf = float(topk)
    att_cols = ATT_REP * Q_TILE
    groups = KEY_CHUNK // SUBLANES

    def chunk_start(c):
        return pl.multiple_of(c * KEY_CHUNK, KEY_CHUNK)

    def over_chunk_pairs(body, init):
        def pair(c2, carry):
            return body(2 * c2 + 1, body(2 * c2, carry))
        carry = lax.fori_loop(0, n_chunks // 2, pair, init)
        return lax.cond(n_chunks % 2 == 1, lambda cr: body(n_chunks - 1, cr), lambda cr: cr, carry)

    def fold(v, op):
        return op(v.reshape(groups, SUBLANES, v.shape[-1]), axis=0)

    def only_rows(x, lo, n):
        parts = [jnp.zeros((lo, x.shape[1]), x.dtype)] if lo else []
        parts.append(x[lo:lo + n])
        if lo + n < x.shape[0]:
            parts.append(jnp.zeros((x.shape[0] - lo - n, x.shape[1]), x.dtype))
        return jnp.concatenate(parts, axis=0)

    qi_t = qi_ref[0, 0]
    per_tile = LANES // IDX_DIM

    def head_cols(h):
        t = h // per_tile
        return only_rows(qi_t[:, t * Q_TILE:(t + 1) * Q_TILE], (h % per_tile) * IDX_DIM, IDX_DIM)

    rhs_pairs = [jnp.concatenate([head_cols(2 * p), head_cols(2 * p + 1)], axis=1)
                 for p in range(IDX_HEADS // 2)]
    w_rows = [wi_ref[0, 0, h:h + 1, :] for h in range(IDX_HEADS)]
    q_t = q_ref[0, 0]
    qg_t = [only_rows(q_t, g * HEAD_DIM, HEAD_DIM) for g in range(ATT_KV_HEADS)]
    krow = lax.broadcasted_iota(I32, (KEY_CHUNK, Q_TILE), 0)
    kq = q0 + lax.broadcasted_iota(I32, (KEY_CHUNK, Q_TILE), 1)

    def matmul_chunk(c, stats):
        mx_a, mn_a, ge_a, gt_a, top_a = stats
        ks = chunk_start(c)
        kc = ki_ref[0, pl.ds(ks, KEY_CHUNK), :]
        acc = jnp.zeros((KEY_CHUNK, Q_TILE), F32)
        for p in range(IDX_HEADS // 2):
            s2 = jnp.dot(kc, rhs_pairs[p], preferred_element_type=F32)
            acc = acc + w_rows[2 * p] * jnp.maximum(s2[:, :Q_TILE], 0.0)
            acc = acc + w_rows[2 * p + 1] * jnp.maximum(s2[:, Q_TILE:], 0.0)
        causal = ks + krow <= kq
        sc = jnp.where(causal, acc, -jnp.inf)
        score_scr[pl.ds(ks, KEY_CHUNK), :] = sc
        kk = k_ref[0, pl.ds(ks, KEY_CHUNK), :]
        tops = []
        for g in range(ATT_KV_HEADS):
            s = jnp.dot(kk, qg_t[g], preferred_element_type=F32)
            s_scr[g, pl.ds(ks, KEY_CHUNK), :] = s
            tops.append(jnp.maximum(top_a[g], fold(s, jnp.max)))
        return (jnp.maximum(mx_a, fold(sc, jnp.max)),
                jnp.minimum(mn_a, fold(jnp.where(causal, acc, jnp.inf), jnp.min)),
                ge_a + fold(jnp.where(sc >= 0.0, 1.0, 0.0), jnp.sum),
                gt_a + fold(jnp.where(sc > 0.0, 1.0, 0.0), jnp.sum),
                tuple(tops))

    stat0 = lambda v: jnp.full((SUBLANES, Q_TILE), v, F32)
    top0 = tuple(jnp.full((SUBLANES, att_cols), NEG_BIG, F32) for _ in range(ATT_KV_HEADS))
    mx_a, mn_a, ge_a, gt_a, top_a = over_chunk_pairs(
        matmul_chunk, (stat0(-jnp.inf), stat0(jnp.inf), stat0(0.0), stat0(0.0), top0))
    m_top = [jnp.max(top_a[g], axis=0, keepdims=True) for g in range(ATT_KV_HEADS)]
    s_max = jnp.max(mx_a, axis=0, keepdims=True)
    s_min = jnp.min(mn_a, axis=0, keepdims=True)
    n_ge0 = jnp.sum(ge_a, axis=0, keepdims=True)
    n_gt0 = jnp.sum(gt_a, axis=0, keepdims=True)
    n_causal = (q0 + 1 + lax.broadcasted_iota(I32, (1, Q_TILE), 1)).astype(F32)

    def count(pred):
        def body(c, acc):
            ks = chunk_start(c)
            for part in range(KEY_CHUNK // LANES):
                hit = jnp.where(pred(score_scr[pl.ds(ks + part * LANES, LANES), :]), 1.0, 0.0)
                acc = acc + (hit[:COUNT_ROWS] + hit[COUNT_ROWS:])
            return acc
        acc = lax.fori_loop(0, n_chunks, body, jnp.zeros((COUNT_ROWS, Q_TILE), F32))
        rows = COUNT_ROWS
        while rows > SUBLANES:
            rows //= 2
            acc = acc[:rows] + acc[rows:]
        return jnp.sum(acc, axis=0, keepdims=True)

    def count_ge(value):
        vb = jnp.broadcast_to(value, (LANES, Q_TILE))
        return count(lambda sc: sc >= vb)

    zero_thr = (n_gt0 < kf) & (n_ge0 >= kf)
    keep_all = n_causal <= kf
    positive = n_gt0 >= kf
    settled = zero_thr | keep_all
    lo0 = jnp.where(settled, 0.0, jnp.where(positive, 0.0, s_min))
    hi0 = jnp.where(settled, 0.0, jnp.where(positive, s_max, 0.0))
    c_lo0 = jnp.where(positive, n_gt0, n_causal)
    c_hi0 = jnp.where(positive, 0.0, n_ge0)
    thr0 = jnp.where(keep_all, -jnp.inf, 0.0)
    done0 = jnp.where(settled, 1.0, 0.0)

    def search_pass(bisect, rows):
        lo, hi, c_lo, c_hi, thr, done = rows
        if bisect:
            frac = 0.5
        else:
            frac = jnp.clip((kf - c_hi + 0.5) / (c_lo - c_hi + 1.0), SEARCH_CLIP, 1.0 - SEARCH_CLIP)
        x = hi - (hi - lo) * frac
        c = count_ge(x)
        live = done == 0.0
        above = live & (c >= kf)
        below = live & (c < kf)
        hit = live & (c == kf)
        return (jnp.where(above, x, lo), jnp.where(below, x, hi),
                jnp.where(above, c, c_lo), jnp.where(below, c, c_hi),
                jnp.where(hit, x, thr), jnp.where(hit, 1.0, done))

    def search_cond(state):
        return (state[0] < SEARCH_MAX_PASSES) & (state[1] > 0.0)

    def search_body(state):
        it, _, rows = state
        for _ in range(SEARCH_GROUP):
            rows = search_pass(False, rows)
        return it + SEARCH_GROUP, jnp.sum(1.0 - rows[5]), rows

    def first_passes(rows):
        for p in range(SEARCH_FIRST_PASSES):
            rows = search_pass(p < SEARCH_BISECT_PASSES, rows)
        return rows

    rows0 = (lo0, hi0, c_lo0, c_hi0, thr0, done0)
    rows1 = lax.cond(jnp.sum(1.0 - done0) > 0.0, first_passes, lambda rows: rows, rows0)
    state = lax.while_loop(search_cond, search_body,
                           (jnp.int32(SEARCH_FIRST_PASSES), jnp.sum(1.0 - rows1[5]), rows1))
    pending, thr_s, done_s = state[1], state[2][4], state[2][5]
    row_scr[0:1, :] = thr_s
    row_scr[1:2, :] = jnp.where(zero_thr, n_gt0, 0.0)
    row_scr[2:3, :] = jnp.where(zero_thr, n_ge0, kf)

    @pl.when(pending > 0.0)
    def _():
        def count_ge_key(key):
            return count_ge(_key_to_float(key))

        key0 = jnp.where(count_ge_key(jnp.zeros((1, Q_TILE), I32)) >= kf, 0, INT_MIN).astype(I32)

        def bit_step(b, key):
            cand = key | jnp.left_shift(jnp.int32(1), 30 - b)
            return jnp.where(count_ge_key(cand) >= kf, cand, key)

        thr_f = _key_to_float(lax.fori_loop(0, 31, bit_step, key0))
        thr_fb = jnp.broadcast_to(thr_f, (LANES, Q_TILE))
        open_row = done_s == 0.0
        row_scr[0:1, :] = jnp.where(open_row, thr_f, thr_s)
        row_scr[1:2, :] = jnp.where(open_row, count(lambda sc: sc > thr_fb), row_scr[1:2, :])
        row_scr[2:3, :] = jnp.where(open_row, count(lambda sc: sc >= thr_fb), row_scr[2:3, :])

    thr = row_scr[0:1, :]
    thr_c = jnp.broadcast_to(thr, (KEY_CHUNK, Q_TILE))
    has_tie = (row_scr[2:3, :] > kf) & (thr > -jnp.inf)
    any_tie = jnp.max(jnp.where(has_tie, 1.0, 0.0)) > 0.0

    def plain_bias(c, carry):
        ks = chunk_start(c)
        keep = (score_scr[pl.ds(ks, KEY_CHUNK), :] >= thr_c) & (ks + krow <= kq)
        return jnp.where(keep, 0.0, NEG_BIG), carry

    def make_tie_bias():
        need = jnp.broadcast_to(kf - row_scr[1:2, :], (KEY_CHUNK, Q_TILE))
        lower = (lax.broadcasted_iota(I32, (KEY_CHUNK, KEY_CHUNK), 1)
                 <= lax.broadcasted_iota(I32, (KEY_CHUNK, KEY_CHUNK), 0))
        prefix_mat = jnp.where(lower, 1.0, 0.0).astype(BF16)

        def tie_bias(c, seen):
            ks = chunk_start(c)
            sc = score_scr[pl.ds(ks, KEY_CHUNK), :]
            tied = sc == thr_c
            rank = seen + jnp.dot(prefix_mat, jnp.where(tied, 1.0, 0.0).astype(BF16),
                                  preferred_element_type=F32)
            chosen = (sc > thr_c) | (tied & (rank <= need))
            return jnp.where(chosen & (ks + krow <= kq), 0.0, NEG_BIG), rank[KEY_CHUNK - 1:KEY_CHUNK, :]

        return tie_bias

    no_ties_seen = jnp.zeros((1, Q_TILE), F32)

    def pv_pass(bias_of, m, carry0):
        acc_scr[...] = jnp.zeros(acc_scr.shape, F32)

        def pv_chunk(c, carry):
            bias, carry = bias_of(c, carry)
            bias4 = jnp.concatenate([bias] * ATT_REP, axis=1)
            for g in range(ATT_KV_HEADS):
                p = jnp.exp2(s_scr[g, pl.ds(chunk_start(c), KEY_CHUNK), :] + bias4 - m[g])
                acc_scr[g] += jnp.dot(vt_ref[0, c], p.astype(BF16), preferred_element_type=F32)
            return carry

        over_chunk_pairs(pv_chunk, carry0)

    @pl.when(jnp.logical_not(any_tie))
    def _():
        pv_pass(plain_bias, m_top, jnp.int32(0))

    @pl.when(any_tie)
    def _():
        pv_pass(make_tie_bias(), m_top, no_ties_seen)

    d_min = jnp.min(jnp.minimum(acc_scr[0, KV_WIDTH:KV_WIDTH + 1, :], acc_scr[1, KV_WIDTH:KV_WIDTH + 1, :]))

    @pl.when(jnp.logical_not(d_min >= DENOM_FLOOR))
    def _():
        tie_bias = make_tie_bias()

        def mask_chunk(c, carry):
            seen, m_acc = carry
            ks = chunk_start(c)
            bias, seen = tie_bias(c, seen)
            bias_scr[pl.ds(ks, KEY_CHUNK), :] = bias
            bias4 = jnp.concatenate([bias] * ATT_REP, axis=1)
            return seen, tuple(jnp.maximum(m_acc[g], fold(s_scr[g, pl.ds(ks, KEY_CHUNK), :] + bias4, jnp.max))
                               for g in range(ATT_KV_HEADS))

        m_init = tuple(jnp.full((SUBLANES, att_cols), NEG_BIG, F32) for _ in range(ATT_KV_HEADS))
        _, m_acc = lax.fori_loop(0, n_chunks, mask_chunk, (no_ties_seen, m_init))
        m_sel = [jnp.max(m_acc[g], axis=0, keepdims=True) for g in range(ATT_KV_HEADS)]
        pv_pass(lambda c, carry: (bias_scr[pl.ds(chunk_start(c), KEY_CHUNK), :], carry), m_sel, jnp.int32(0))

    for g in range(ATT_KV_HEADS):
        denom = acc_scr[g, KV_WIDTH:KV_WIDTH + 1, :]
        og_scr[g * HEAD_DIM:(g + 1) * HEAD_DIM, :] = acc_scr[g, g * HEAD_DIM:(g + 1) * HEAD_DIM, :] / denom

    for j in range(ATT_REP):
        gated = og_scr[:, j * Q_TILE:(j + 1) * Q_TILE].T * ga_ref[0, :, j * LANES:(j + 1) * LANES]
        o_ref[0, :, j * LANES:(j + 1) * LANES] = gated.astype(o_ref.dtype)


def _dsa(q_t, qi_t, wi_t, k, vt, ki, gate):
    batch, seq, _ = k.shape
    topk = min(TOPK_MAX, seq // 4)
    att_cols = ATT_REP * Q_TILE
    per_q = lambda a: pl.BlockSpec((1, 1) + a.shape[2:], lambda b, i: (b, i, 0, 0))
    per_b = lambda w: pl.BlockSpec((1, seq, w), lambda b, i: (b, 0, 0))
    vt_spec = pl.BlockSpec((1, seq // KEY_CHUNK, VT_ROWS, KEY_CHUNK), lambda b, i: (b, 0, 0, 0))
    tok_spec = pl.BlockSpec((1, Q_TILE, ATT_WIDTH), lambda b, i: (b, i, 0))
    return pl.pallas_call(
        functools.partial(_dsa_kernel, topk=topk),
        grid=(batch, seq // Q_TILE),
        in_specs=[per_q(q_t), per_q(qi_t), per_q(wi_t), per_b(KV_WIDTH), vt_spec, per_b(LANES), tok_spec],
        out_specs=tok_spec,
        out_shape=jax.ShapeDtypeStruct((batch, seq, ATT_WIDTH), BF16),
        scratch_shapes=[pltpu.VMEM((seq, Q_TILE), F32),
                        pltpu.VMEM((seq, Q_TILE), F32),
                        pltpu.VMEM((SUBLANES, Q_TILE), F32),
                        pltpu.VMEM((ATT_KV_HEADS, seq, att_cols), F32),
                        pltpu.VMEM((ATT_KV_HEADS, VT_ROWS, att_cols), F32),
                        pltpu.VMEM((KV_WIDTH, att_cols), F32)],
        compiler_params=pltpu.CompilerParams(dimension_semantics=("arbitrary", "arbitrary"),
                                             vmem_limit_bytes=VMEM_LIMIT_BYTES),
        name="dsa",
    )(q_t, qi_t, wi_t, k, vt, ki, gate)


def _s5_kernel(u_ref, bm_ref, cm_ref, are_ref, aim_ref, dsk_ref, wg_ref, bg_ref, o_ref,
               st_scr, xb_scr, carry_scr):
    tc, batch, width = u_ref.shape
    steps = tc // SCAN_SPLIT
    rows = steps * batch

    @pl.when(pl.program_id(0) == 0)
    def _():
        carry_scr[...] = jnp.zeros(carry_scr.shape, F32)

    u = [u_ref[h * steps:(h + 1) * steps].reshape(rows, width) for h in range(SCAN_SPLIT)]
    for h in range(SCAN_SPLIT):
        st_scr[h] = jnp.dot(u[h].astype(BF16), bm_ref[...], preferred_element_type=F32)

    parts = []
    for part in range(SSM_LANES // SCAN_LANES):
        re_lo = part * SCAN_LANES
        im_lo = SSM_LANES + re_lo
        parts.append((slice(re_lo, re_lo + SCAN_LANES), slice(im_lo, im_lo + SCAN_LANES)))
    state = [(carry_scr[:, re], carry_scr[:, im]) for re, im in parts]

    for h in range(SCAN_SPLIT):
        for p, (re, im) in enumerate(parts):
            a_re = jnp.broadcast_to(are_ref[:, re], (batch, SCAN_LANES))
            a_im = jnp.broadcast_to(aim_ref[:, re], (batch, SCAN_LANES))
            x_re, x_im = state[p]
            for t in range(steps):
                r = slice(t * batch, (t + 1) * batch)
                x_re, x_im = (a_re * x_re - a_im * x_im + st_scr[h, r, re],
                              a_re * x_im + a_im * x_re + st_scr[h, r, im])
                xb_scr[h, r, re] = x_re.astype(BF16)
                xb_scr[h, r, im] = x_im.astype(BF16)
            state[p] = (x_re, x_im)
        y = jnp.dot(xb_scr[h], cm_ref[...], preferred_element_type=F32) + dsk_ref[...] * u[h]
        y = jax.nn.gelu(y)
        gate = jax.nn.sigmoid(jnp.dot(y.astype(BF16), wg_ref[...], preferred_element_type=F32) + bg_ref[...])
        o_ref[h * steps:(h + 1) * steps] = (y * gate).reshape(steps, batch, width)

    for (re, im), (x_re, x_im) in zip(parts, state):
        carry_scr[:, re] = x_re
        carry_scr[:, im] = x_im


def _s5(u_tm, bmat, cmat, a_re, a_im, dskip, w_glu, b_glu):
    seq, batch, width = u_tm.shape
    tc = TIME_CHUNK
    full = lambda shape: pl.BlockSpec(shape, lambda t: tuple(0 for _ in shape))
    return pl.pallas_call(
        _s5_kernel,
        grid=(seq // tc,),
        in_specs=[pl.BlockSpec((tc, batch, width), lambda t: (t, 0, 0)),
                  full(bmat.shape), full(cmat.shape), full(a_re.shape), full(a_im.shape),
                  full(dskip.shape), full(w_glu.shape), full(b_glu.shape)],
        out_specs=pl.BlockSpec((tc, batch, width), lambda t: (t, 0, 0)),
        out_shape=jax.ShapeDtypeStruct((seq, batch, width), F32),
        scratch_shapes=[pltpu.VMEM((SCAN_SPLIT, tc // SCAN_SPLIT * batch, 2 * SSM_LANES), F32),
                        pltpu.VMEM((SCAN_SPLIT, tc // SCAN_SPLIT * batch, 2 * SSM_LANES), BF16),
                        pltpu.VMEM((batch, 2 * SSM_LANES), F32)],
        compiler_params=pltpu.CompilerParams(dimension_semantics=("arbitrary",),
                                             vmem_limit_bytes=VMEM_LIMIT_BYTES),
        name="s5",
    )(u_tm, bmat, cmat, a_re, a_im, dskip, w_glu, b_glu)


def _s5_params(lam_re, lam_im, log_dt, b_re, b_im, c_re, c_im):
    dt = jnp.exp(log_dt.astype(F32))[:, None]
    mag = jnp.exp(lam_re.astype(F32) * dt)
    ang = lam_im.astype(F32) * dt
    lb_re, lb_im = mag * jnp.cos(ang), mag * jnp.sin(ang)
    den = lam_re * lam_re + lam_im * lam_im
    k_re = ((lb_re - 1.0) * lam_re + lb_im * lam_im) / den
    k_im = (lb_im * lam_re - (lb_re - 1.0) * lam_im) / den
    bb_re = k_re[:, :, None] * b_re - k_im[:, :, None] * b_im
    bb_im = k_re[:, :, None] * b_im + k_im[:, :, None] * b_re
    eye = jnp.eye(SSM_GROUPS, dtype=F32)
    drive = lambda bb: jnp.einsum('gpn,gh->gnhp', bb, eye).reshape(SSM_WIDTH, SSM_LANES)
    read = lambda cc: jnp.einsum('gnp,gh->gphn', cc, eye).reshape(SSM_LANES, SSM_WIDTH)
    bmat = jnp.concatenate([drive(bb_re), drive(bb_im)], axis=1)
    cmat = jnp.concatenate([read(c_re.astype(F32)), read(-c_im.astype(F32))], axis=0)
    return (bmat.astype(BF16), cmat.astype(BF16),
            lb_re.reshape(1, SSM_LANES), lb_im.reshape(1, SSM_LANES))


def _combine_kernel(x_ref, ca_ref, os_ref, gs_ref, qm_ref, gm_ref, mk_ref, mv_ref,
                    wo_ref, lg_ref, lb_ref, out_ref):
    tile = x_ref.shape[1]
    nt_dims = (((1,), (1,)), ((), ()))
    lane = lax.broadcasted_iota(I32, (tile, LANES), 1)
    low = lane < HEAD_DIM

    qm = qm_ref[0].astype(F32)
    mem_tiles = []
    for t in range(MEM_WIDTH // LANES):
        qt = qm[:, t * LANES:(t + 1) * LANES]
        mk = mk_ref[0, :, t * LANES:(t + 1) * LANES]
        mv = mv_ref[0, :, t * LANES:(t + 1) * LANES]
        halves = []
        for first in (True, False):
            qh = jnp.where(low if first else ~low, qt, 0.0).astype(BF16)
            s = lax.dot_general(qh, mk, nt_dims, preferred_element_type=F32)
            p = jnp.exp(s - jnp.max(s, axis=1, keepdims=True))
            o = jnp.dot(p.astype(BF16), mv, preferred_element_type=F32)
            halves.append(o / jnp.sum(p, axis=1, keepdims=True))
        mem_tiles.append(jnp.where(low, halves[0], halves[1]))
    o_mem = jnp.concatenate(mem_tiles, axis=1)

    c_att = ca_ref[0]
    c_ssm = (os_ref[...] * gs_ref[0]).astype(BF16)
    c_mem = (o_mem * gm_ref[0]).astype(BF16)
    a1, a2 = ATT_WIDTH, ATT_WIDTH + SSM_WIDTH
    sub = (jnp.dot(c_att, wo_ref[:a1, :], preferred_element_type=F32)
           + jnp.dot(c_ssm, wo_ref[a1:a2, :], preferred_element_type=F32)
           + jnp.dot(c_mem, wo_ref[a2:, :], preferred_element_type=F32))
    h = DN_ALPHA * x_ref[0] + sub
    mu = jnp.mean(h, axis=1, keepdims=True)
    d = h - mu
    var = jnp.mean(d * d, axis=1, keepdims=True)
    out_ref[0] = d * lax.rsqrt(var + LN_EPS) * lg_ref[...] + lb_ref[...]


def _combine(x, c_att, o_ssm_tm, g_ssm, qm, g_mem, mk, mv, wo_bf16, ln_g, ln_b):
    batch, seq, _ = x.shape
    tile = TOKEN_TILE
    tok = lambda w: pl.BlockSpec((1, tile, w), lambda b, j: (b, j, 0))
    time_major = pl.BlockSpec((tile, SSM_WIDTH), lambda b, j: (j, b))
    per_b = pl.BlockSpec((1, N_MEM, MEM_WIDTH), lambda b, j: (b, 0, 0))
    const = lambda shape: pl.BlockSpec(shape, lambda b, j: tuple(0 for _ in shape))
    return pl.pallas_call(
        _combine_kernel,
        grid=(batch, seq // tile),
        in_specs=[tok(D_MODEL), tok(ATT_WIDTH), time_major, tok(SSM_WIDTH),
                  tok(MEM_WIDTH), tok(MEM_WIDTH), per_b, per_b,
                  const((D_MODEL, D_MODEL)), const((1, D_MODEL)), const((1, D_MODEL))],
        out_specs=tok(D_MODEL),
        out_shape=jax.ShapeDtypeStruct((batch, seq, D_MODEL), x.dtype),
        compiler_params=pltpu.CompilerParams(dimension_semantics=("arbitrary", "arbitrary"),
                                             vmem_limit_bytes=VMEM_LIMIT_BYTES),
        name="combine",
    )(x, c_att, o_ssm_tm, g_ssm, qm, g_mem, mk, mv, wo_bf16, ln_g, ln_b)


def _reordered_w_in(w_in):
    offs = np.cumsum([0, ATT_WIDTH, KV_WIDTH, KV_WIDTH, IDX_WIDTH, IDX_DIM, IDX_HEADS,
                      ATT_WIDTH, SSM_WIDTH, SSM_WIDTH, MEM_WIDTH, MEM_WIDTH])
    part = lambda n: w_in[:, int(offs[n]):int(offs[n + 1])]
    perm = _att_perm()
    k_idx = part(4)
    w_idx = part(5)
    zeros = jnp.zeros((D_MODEL, LANES - IDX_HEADS), w_in.dtype)
    cols = [part(0)[:, perm], part(1), part(2), part(3),
            jnp.tile(k_idx, (1, LANES // IDX_DIM)),
            jnp.concatenate([w_idx, zeros], axis=1),
            part(6)[:, perm], part(7), part(8), part(9), part(10)]
    return jnp.concatenate(cols, axis=1).astype(BF16)


def kernel(x, mem, w_in, w_mem_kv, lam_re, lam_im, log_dt, b_re, b_im, c_re, c_im, d_skip, w_glu, b_glu,
           w_out, ln_g, ln_b):
    batch, seq, _ = x.shape
    assert seq % TOKEN_TILE == 0 and seq % Q_TILE == 0 and seq % TIME_CHUNK == 0

    w_all = _reordered_w_in(w_in)
    perm = _att_perm()
    wo = jnp.concatenate([w_out[:ATT_WIDTH][perm], w_out[ATT_WIDTH:]], axis=0).astype(BF16)
    bmat, cmat, a_re, a_im = _s5_params(lam_re, lam_im, log_dt, b_re, b_im, c_re, c_im)
    tabs = _rope_tables(seq, HEAD_DIM) + _rope_tables(seq, IDX_DIM)

    mk, mv = _memkv(mem.reshape(batch * N_MEM, D_MODEL), w_mem_kv.astype(BF16))
    mk = mk.reshape(batch, N_MEM, MEM_WIDTH)
    mv = mv.reshape(batch, N_MEM, MEM_WIDTH)

    q_t, k, vt, qi_t, ki, wi_t, g_att, u_tm, g_ssm, qm, g_mem = _inproj(x, w_all, tabs)
    c_att = _dsa(q_t, qi_t, wi_t, k, vt, ki, g_att)

    o_ssm_tm = _s5(u_tm.reshape(seq, batch, SSM_WIDTH), bmat, cmat, a_re, a_im,
                   d_skip.reshape(1, SSM_WIDTH).astype(F32), w_glu.astype(BF16),
                   b_glu.reshape(1, SSM_WIDTH).astype(F32))
    o_ssm = o_ssm_tm.reshape(seq, batch * SSM_WIDTH)

    return _combine(x, c_att, o_ssm, g_ssm, qm, g_mem, mk, mv, wo,
                    ln_g.reshape(1, D_MODEL).astype(F32), ln_b.reshape(1, D_MODEL).astype(F32))
```

```python
import functools
import math

import jax
import jax.numpy as jnp
import numpy as np
from jax import lax
from jax.experimental import pallas as pl
from jax.experimental.pallas import tpu as pltpu

F32 = jnp.float32
BF16 = jnp.bfloat16
I32 = jnp.int32

D_MODEL = 1024
N_MEM = 256
HEAD_DIM = 64
ATT_HEADS = 8
ATT_KV_HEADS = 2
ATT_REP = ATT_HEADS // ATT_KV_HEADS
ATT_WIDTH = ATT_HEADS * HEAD_DIM
KV_WIDTH = ATT_KV_HEADS * HEAD_DIM
IDX_HEADS = 8
IDX_DIM = 32
IDX_WIDTH = IDX_HEADS * IDX_DIM
TOPK_MAX = 256
SSM_WIDTH = D_MODEL // 4
SSM_GROUP = 16
SSM_GROUPS = SSM_WIDTH // SSM_GROUP
SSM_STATE = 64
SSM_LANES = SSM_GROUPS * SSM_STATE
MEM_HEADS = 4
MEM_WIDTH = MEM_HEADS * HEAD_DIM
ROPE_THETA = 500000.0
ROPE_FRAC = 4
LN_EPS = 1e-5
DEPTH = 1
DN_ALPHA = (2.0 * DEPTH) ** 0.25
ATT_SCALE = HEAD_DIM ** -0.5
IDX_SCALE = IDX_HEADS ** -0.5 * IDX_DIM ** -0.5
LOG2_E = math.log2(math.e)

LANES = 128
SUBLANES = 8
BF16_ROWS = 16
VMEM_LIMIT_BYTES = 48 * 1024 * 1024
TOKEN_TILE = 512
Q_TILE = 256
KEY_CHUNK = 256
COUNT_ROWS = 64
VT_ROWS = KV_WIDTH + BF16_ROWS
TIME_CHUNK = 128
SCAN_SPLIT = 2
SCAN_LANES = 256
NEG_BIG = -1e30
DENOM_FLOOR = 2.0 ** -100
SEARCH_MAX_PASSES = 24
SEARCH_FIRST_PASSES = 13
SEARCH_GROUP = 2
SEARCH_BISECT_PASSES = 2
SEARCH_CLIP = 0.1

INT_MIN = -2147483648
KEY_POS_INF = 0x7F800000
KEY_NEG_INF = INT_MIN + 0x7FFFFF

_G_Q, _G_K, _G_V, _G_QI, _G_KI, _G_W, _G_GA, _G_U, _G_GS, _G_QM, _G_GM = range(11)
_GROUP_WIDTHS = [ATT_WIDTH, KV_WIDTH, KV_WIDTH, IDX_WIDTH, LANES, LANES, ATT_WIDTH,
                 SSM_WIDTH, SSM_WIDTH, MEM_WIDTH, MEM_WIDTH]
_GROUP_OFFS = [int(v) for v in np.cumsum([0] + _GROUP_WIDTHS)]
IN_COLS = _GROUP_OFFS[-1]


def _pair_heads(w, axis):
    shape = w.shape
    split = shape[:axis] + (ATT_KV_HEADS, ATT_REP, HEAD_DIM) + shape[axis + 1:]
    return jnp.swapaxes(w.reshape(split), axis, axis + 1).reshape(shape)


def _rope_tables(seq, period, dtype=F32):
    r = period // ROPE_FRAC
    half = r // 2
    inv = ROPE_THETA ** (-jnp.arange(0, half, dtype=F32) * 2.0 / r)
    ang = jnp.arange(seq).astype(F32)[:, None] * inv[None, :]
    cos, sin = jnp.cos(ang), jnp.sin(ang)
    ones = jnp.ones((seq, period - r), F32)
    zeros = jnp.zeros((seq, period - r), F32)
    c = jnp.concatenate([cos, cos, ones], axis=1)
    s = jnp.concatenate([-sin, sin, zeros], axis=1)
    reps = LANES // period
    return jnp.tile(c, (1, reps)).astype(dtype), jnp.tile(s, (1, reps)).astype(dtype)


def _rope(z, cos_t, sin_t, period):
    width = z.shape[1]
    half = period // ROPE_FRAC // 2
    reps = width // LANES
    if reps > 1:
        cos_t = jnp.concatenate([cos_t] * reps, axis=1)
        sin_t = jnp.concatenate([sin_t] * reps, axis=1)
    lane = lax.broadcasted_iota(I32, z.shape, 1)
    first = (lane & (period - 1)) < half
    up = pltpu.roll(z, half, 1)
    down = pltpu.roll(z, width - half, 1)
    return z * cos_t + jnp.where(first, down, up) * sin_t


def _memkv_kernel(mem_ref, w_ref, mk_ref, mv_ref):
    z = jnp.dot(mem_ref[...].astype(BF16), w_ref[...], preferred_element_type=F32)
    mk_ref[...] = z[:, :MEM_WIDTH].astype(BF16)
    mv_ref[...] = z[:, MEM_WIDTH:].astype(BF16)


def _memkv(mem2d, w_bf16):
    rows = mem2d.shape[0]
    tile = TOKEN_TILE
    return pl.pallas_call(
        _memkv_kernel,
        grid=(rows // tile,),
        in_specs=[pl.BlockSpec((tile, D_MODEL), lambda i: (i, 0)),
                  pl.BlockSpec((D_MODEL, 2 * MEM_WIDTH), lambda i: (0, 0))],
        out_specs=[pl.BlockSpec((tile, MEM_WIDTH), lambda i: (i, 0)),
                   pl.BlockSpec((tile, MEM_WIDTH), lambda i: (i, 0))],
        out_shape=[jax.ShapeDtypeStruct((rows, MEM_WIDTH), BF16),
                   jax.ShapeDtypeStruct((rows, MEM_WIDTH), BF16)],
        compiler_params=pltpu.CompilerParams(dimension_semantics=("arbitrary",),
                                             vmem_limit_bytes=VMEM_LIMIT_BYTES),
        name="memkv",
    )(mem2d, w_bf16)


def _inproj_kernel(x_ref, w_ref, ca_ref, sa_ref, ci_ref, si_ref,
                   q_ref, k_ref, vt_ref, qi_ref, ki_ref, wi_ref, ga_ref, u_ref, gs_ref, qm_ref, gm_ref):
    tile = x_ref.shape[1]
    pos0 = pl.multiple_of(pl.program_id(1) * tile, tile)
    xb = x_ref[0].astype(BF16)

    def proj(group):
        lo, hi = _GROUP_OFFS[group], _GROUP_OFFS[group + 1]
        return jnp.dot(xb, w_ref[:, lo:hi], preferred_element_type=F32)

    ca = ca_ref[pl.ds(pos0, tile), :]
    sa = sa_ref[pl.ds(pos0, tile), :]
    ci = ci_ref[pl.ds(pos0, tile), :]
    si = si_ref[pl.ds(pos0, tile), :]

    def store_transposed(ref, z):
        for qb in range(tile // Q_TILE):
            for t in range(z.shape[1] // LANES):
                zt = z[qb * Q_TILE:(qb + 1) * Q_TILE, t * LANES:(t + 1) * LANES].T
                ref[0, qb, :, t * Q_TILE:(t + 1) * Q_TILE] = zt.astype(ref.dtype)

    store_transposed(q_ref, _rope(proj(_G_Q), ca, sa, HEAD_DIM) * (ATT_SCALE * LOG2_E))
    k_ref[0] = _rope(proj(_G_K), ca, sa, HEAD_DIM).astype(BF16)
    v = proj(_G_V)
    for c in range(tile // KEY_CHUNK):
        vt_ref[0, c, :KV_WIDTH, :] = v[c * KEY_CHUNK:(c + 1) * KEY_CHUNK, :].T.astype(BF16)
        vt_ref[0, c, KV_WIDTH:, :] = jnp.ones((BF16_ROWS, KEY_CHUNK), BF16)
    store_transposed(qi_ref, _rope(proj(_G_QI), ci, si, IDX_DIM))
    ki_ref[0] = _rope(proj(_G_KI), ci, si, IDX_DIM).astype(BF16)
    wi = proj(_G_W) * IDX_SCALE
    for qb in range(tile // Q_TILE):
        wi_ref[0, qb] = wi[qb * Q_TILE:(qb + 1) * Q_TILE, :].T[:IDX_HEADS, :]
    ga_ref[0] = jax.nn.silu(proj(_G_GA))
    u_ref[...] = proj(_G_U)
    gs_ref[0] = jax.nn.silu(proj(_G_GS))
    qm_ref[0] = (proj(_G_QM) * ATT_SCALE).astype(BF16)
    gm_ref[0] = jax.nn.silu(proj(_G_GM))


def _inproj(x, w_bf16, tabs):
    batch, seq, _ = x.shape
    tile = TOKEN_TILE
    widths = _GROUP_WIDTHS
    dtypes = [BF16, BF16, BF16, BF16, BF16, F32, F32, F32, F32, BF16, F32]
    tab_spec = pl.BlockSpec((seq, LANES), lambda b, j: (0, 0))
    out_specs = [pl.BlockSpec((1, tile, w), lambda b, j: (b, j, 0)) for w in widths]
    out_shape = [jax.ShapeDtypeStruct((batch, seq, w), d) for w, d in zip(widths, dtypes)]
    out_specs[_G_V] = pl.BlockSpec((1, tile // KEY_CHUNK, VT_ROWS, KEY_CHUNK), lambda b, j: (b, j, 0, 0))
    out_shape[_G_V] = jax.ShapeDtypeStruct((batch, seq // KEY_CHUNK, VT_ROWS, KEY_CHUNK), BF16)
    for group, rows, dtype in ((_G_Q, LANES, BF16), (_G_QI, LANES, BF16), (_G_W, IDX_HEADS, F32)):
        cols = Q_TILE * max(widths[group] // LANES, 1)
        out_specs[group] = pl.BlockSpec((1, tile // Q_TILE, rows, cols), lambda b, j: (b, j, 0, 0))
        out_shape[group] = jax.ShapeDtypeStruct((batch, seq // Q_TILE, rows, cols), dtype)
    out_specs[_G_U] = pl.BlockSpec((tile, SSM_WIDTH), lambda b, j: (j, b))
    out_shape[_G_U] = jax.ShapeDtypeStruct((seq, batch * SSM_WIDTH), F32)
    return pl.pallas_call(
        _inproj_kernel,
        grid=(batch, seq // tile),
        in_specs=[pl.BlockSpec((1, tile, D_MODEL), lambda b, j: (b, j, 0)),
                  pl.BlockSpec((D_MODEL, IN_COLS), lambda b, j: (0, 0)),
                  tab_spec, tab_spec, tab_spec, tab_spec],
        out_specs=out_specs,
        out_shape=out_shape,
        compiler_params=pltpu.CompilerParams(dimension_semantics=("arbitrary", "arbitrary"),
                                             vmem_limit_bytes=VMEM_LIMIT_BYTES),
        name="inproj",
    )(x, w_bf16, *tabs)


def _key_to_float(key):
    bits = jnp.where(key >= 0, key, key ^ 0x7FFFFFFF)
    val = lax.bitcast_convert_type(bits, F32)
    val = jnp.where(key <= KEY_NEG_INF, -jnp.inf, val)
    return jnp.where(key >= KEY_POS_INF, jnp.inf, val)


def _dsa_kernel(q_ref, qi_ref, wi_ref, k_ref, vt_ref, ki_ref, ga_ref, o_ref,
                score_scr, bias_scr, row_scr, s_scr, acc_scr, og_scr, *, topk):
    i = pl.program_id(1)
    n_chunks = ((i + 1) * Q_TILE + KEY_CHUNK - 1) // KEY_CHUNK
    q0 = i * Q_TILE
    kf = float(topk)
    att_cols = ATT_REP * Q_TILE
    groups = KEY_CHUNK // SUBLANES

    def chunk_start(c):
        return pl.multiple_of(c * KEY_CHUNK, KEY_CHUNK)

    def over_chunk_pairs(body, init):
        def pair(c2, carry):
            return body(2 * c2 + 1, body(2 * c2, carry))
        carry = lax.fori_loop(0, n_chunks // 2, pair, init)
        return lax.cond(n_chunks % 2 == 1, lambda cr: body(n_chunks - 1, cr), lambda cr: cr, carry)

    def fold(v, op):
        return op(v.reshape(groups, SUBLANES, v.shape[-1]), axis=0)

    def only_rows(x, lo, n):
        parts = [jnp.zeros((lo, x.shape[1]), x.dtype)] if lo else []
        parts.append(x[lo:lo + n])
        if lo + n < x.shape[0]:
            parts.append(jnp.zeros((x.shape[0] - lo - n, x.shape[1]), x.dtype))
        return jnp.concatenate(parts, axis=0)

    qi_t = qi_ref[0, 0]
    per_tile = LANES // IDX_DIM

    def head_cols(h):
        t = h // per_tile
        return only_rows(qi_t[:, t * Q_TILE:(t + 1) * Q_TILE], (h % per_tile) * IDX_DIM, IDX_DIM)

    rhs_pairs = [jnp.concatenate([head_cols(2 * p), head_cols(2 * p + 1)], axis=1)
                 for p in range(IDX_HEADS // 2)]
    w_rows = [wi_ref[0, 0, h:h + 1, :] for h in range(IDX_HEADS)]
    q_t = q_ref[0, 0]
    qg_t = [only_rows(q_t, g * HEAD_DIM, HEAD_DIM) for g in range(ATT_KV_HEADS)]
    krow = lax.broadcasted_iota(I32, (KEY_CHUNK, Q_TILE), 0)
    kq = q0 + lax.broadcasted_iota(I32, (KEY_CHUNK, Q_TILE), 1)

    def matmul_chunk(c, stats):
        mx_a, mn_a, ge_a, gt_a, top_a = stats
        ks = chunk_start(c)
        kc = ki_ref[0, pl.ds(ks, KEY_CHUNK), :]
        acc = jnp.zeros((KEY_CHUNK, Q_TILE), F32)
        for p in range(IDX_HEADS // 2):
            s2 = jnp.dot(kc, rhs_pairs[p], preferred_element_type=F32)
            acc = acc + w_rows[2 * p] * jnp.maximum(s2[:, :Q_TILE], 0.0)
            acc = acc + w_rows[2 * p + 1] * jnp.maximum(s2[:, Q_TILE:], 0.0)
        causal = ks + krow <= kq
        sc = jnp.where(causal, acc, -jnp.inf)
        score_scr[pl.ds(ks, KEY_CHUNK), :] = sc
        kk = k_ref[0, pl.ds(ks, KEY_CHUNK), :]
        tops = []
        for g in range(ATT_KV_HEADS):
            s = jnp.dot(kk, qg_t[g], preferred_element_type=F32)
            s_scr[g, pl.ds(ks, KEY_CHUNK), :] = s
            tops.append(jnp.maximum(top_a[g], fold(s, jnp.max)))
        return (jnp.maximum(mx_a, fold(sc, jnp.max)),
                jnp.minimum(mn_a, fold(jnp.where(causal, acc, jnp.inf), jnp.min)),
                ge_a + fold(jnp.where(sc >= 0.0, 1.0, 0.0), jnp.sum),
                gt_a + fold(jnp.where(sc > 0.0, 1.0, 0.0), jnp.sum),
                tuple(tops))

    stat0 = lambda v: jnp.full((SUBLANES, Q_TILE), v, F32)
    top0 = tuple(jnp.full((SUBLANES, att_cols), NEG_BIG, F32) for _ in range(ATT_KV_HEADS))
    mx_a, mn_a, ge_a, gt_a, top_a = over_chunk_pairs(
        matmul_chunk, (stat0(-jnp.inf), stat0(jnp.inf), stat0(0.0), stat0(0.0), top0))
    m_top = [jnp.max(top_a[g], axis=0, keepdims=True) for g in range(ATT_KV_HEADS)]
    s_max = jnp.max(mx_a, axis=0, keepdims=True)
    s_min = jnp.min(mn_a, axis=0, keepdims=True)
    n_ge0 = jnp.sum(ge_a, axis=0, keepdims=True)
    n_gt0 = jnp.sum(gt_a, axis=0, keepdims=True)
    n_causal = (q0 + 1 + lax.broadcasted_iota(I32, (1, Q_TILE), 1)).astype(F32)

    def count(pred):
        def body(c, acc):
            ks = chunk_start(c)
            for part in range(KEY_CHUNK // LANES):
                hit = jnp.where(pred(score_scr[pl.ds(ks + part * LANES, LANES), :]), 1.0, 0.0)
                acc = acc + (hit[:COUNT_ROWS] + hit[COUNT_ROWS:])
            return acc
        acc = over_chunk_pairs(body, jnp.zeros((COUNT_ROWS, Q_TILE), F32))
        rows = COUNT_ROWS
        while rows > SUBLANES:
            rows //= 2
            acc = acc[:rows] + acc[rows:]
        return jnp.sum(acc, axis=0, keepdims=True)

    def count_ge(value):
        vb = jnp.broadcast_to(value, (LANES, Q_TILE))
        return count(lambda sc: sc >= vb)

    zero_thr = (n_gt0 < kf) & (n_ge0 >= kf)
    keep_all = n_causal <= kf
    positive = n_gt0 >= kf
    settled = zero_thr | keep_all
    lo0 = jnp.where(settled, 0.0, jnp.where(positive, 0.0, s_min))
    hi0 = jnp.where(settled, 0.0, jnp.where(positive, s_max, 0.0))
    c_lo0 = jnp.where(positive, n_gt0, n_causal)
    c_hi0 = jnp.where(positive, 0.0, n_ge0)
    thr0 = jnp.where(keep_all, -jnp.inf, 0.0)
    done0 = jnp.where(settled, 1.0, 0.0)

    def search_pass(bisect, rows):
        lo, hi, c_lo, c_hi, thr, done = rows
        if bisect:
            frac = 0.5
        else:
            frac = jnp.clip((kf - c_hi + 0.5) / (c_lo - c_hi + 1.0), SEARCH_CLIP, 1.0 - SEARCH_CLIP)
        x = hi - (hi - lo) * frac
        c = count_ge(x)
        live = done == 0.0
        above = live & (c >= kf)
        below = live & (c < kf)
        hit = live & (c == kf)
        return (jnp.where(above, x, lo), jnp.where(below, x, hi),
                jnp.where(above, c, c_lo), jnp.where(below, c, c_hi),
                jnp.where(hit, x, thr), jnp.where(hit, 1.0, done))

    def search_cond(state):
        return (state[0] < SEARCH_MAX_PASSES) & (state[1] > 0.0)

    def search_body(state):
        it, _, rows = state
        for _ in range(SEARCH_GROUP):
            rows = search_pass(False, rows)
        return it + SEARCH_GROUP, jnp.sum(1.0 - rows[5]), rows

    def first_passes(rows):
        for p in range(SEARCH_FIRST_PASSES):
            rows = search_pass(p < SEARCH_BISECT_PASSES, rows)
        return rows

    rows0 = (lo0, hi0, c_lo0, c_hi0, thr0, done0)
    rows1 = lax.cond(jnp.sum(1.0 - done0) > 0.0, first_passes, lambda rows: rows, rows0)
    state = lax.while_loop(search_cond, search_body,
                           (jnp.int32(SEARCH_FIRST_PASSES), jnp.sum(1.0 - rows1[5]), rows1))
    pending, thr_s, done_s = state[1], state[2][4], state[2][5]
    row_scr[0:1, :] = thr_s
    row_scr[1:2, :] = jnp.where(zero_thr, n_gt0, 0.0)
    row_scr[2:3, :] = jnp.where(zero_thr, n_ge0, kf)

    @pl.when(pending > 0.0)
    def _():
        def count_ge_key(key):
            return count_ge(_key_to_float(key))

        key0 = jnp.where(count_ge_key(jnp.zeros((1, Q_TILE), I32)) >= kf, 0, INT_MIN).astype(I32)

        def bit_step(b, key):
            cand = key | jnp.left_shift(jnp.int32(1), 30 - b)
            return jnp.where(count_ge_key(cand) >= kf, cand, key)

        thr_f = _key_to_float(lax.fori_loop(0, 31, bit_step, key0))
        thr_fb = jnp.broadcast_to(thr_f, (LANES, Q_TILE))
        open_row = done_s == 0.0
        row_scr[0:1, :] = jnp.where(open_row, thr_f, thr_s)
        row_scr[1:2, :] = jnp.where(open_row, count(lambda sc: sc > thr_fb), row_scr[1:2, :])
        row_scr[2:3, :] = jnp.where(open_row, count(lambda sc: sc >= thr_fb), row_scr[2:3, :])

    thr = row_scr[0:1, :]
    thr_c = jnp.broadcast_to(thr, (KEY_CHUNK, Q_TILE))
    has_tie = (row_scr[2:3, :] > kf) & (thr > -jnp.inf)
    any_tie = jnp.max(jnp.where(has_tie, 1.0, 0.0)) > 0.0

    def plain_bias(c, carry):
        ks = chunk_start(c)
        keep = (score_scr[pl.ds(ks, KEY_CHUNK), :] >= thr_c) & (ks + krow <= kq)
        return jnp.where(keep, 0.0, NEG_BIG), carry

    def make_tie_bias():
        need = jnp.broadcast_to(kf - row_scr[1:2, :], (KEY_CHUNK, Q_TILE))
        lower = (lax.broadcasted_iota(I32, (KEY_CHUNK, KEY_CHUNK), 1)
                 <= lax.broadcasted_iota(I32, (KEY_CHUNK, KEY_CHUNK), 0))
        prefix_mat = jnp.where(lower, 1.0, 0.0).astype(BF16)

        def tie_bias(c, seen):
            ks = chunk_start(c)
            sc = score_scr[pl.ds(ks, KEY_CHUNK), :]
            tied = sc == thr_c
            rank = seen + jnp.dot(prefix_mat, jnp.where(tied, 1.0, 0.0).astype(BF16),
                                  preferred_element_type=F32)
            chosen = (sc > thr_c) | (tied & (rank <= need))
            return jnp.where(chosen & (ks + krow <= kq), 0.0, NEG_BIG), rank[KEY_CHUNK - 1:KEY_CHUNK, :]

        return tie_bias

    no_ties_seen = jnp.zeros((1, Q_TILE), F32)

    def pv_pass(bias_of, m, carry0):
        acc_scr[...] = jnp.zeros(acc_scr.shape, F32)

        def pv_chunk(c, carry):
            bias, carry = bias_of(c, carry)
            bias4 = jnp.concatenate([bias] * ATT_REP, axis=1)
            for g in range(ATT_KV_HEADS):
                p = jnp.exp2(s_scr[g, pl.ds(chunk_start(c), KEY_CHUNK), :] + bias4 - m[g])
                acc_scr[g] += jnp.dot(vt_ref[0, c], p.astype(BF16), preferred_element_type=F32)
            return carry

        over_chunk_pairs(pv_chunk, carry0)

    @pl.when(jnp.logical_not(any_tie))
    def _():
        pv_pass(plain_bias, m_top, jnp.int32(0))

    @pl.when(any_tie)
    def _():
        pv_pass(make_tie_bias(), m_top, no_ties_seen)

    d_min = jnp.min(jnp.minimum(acc_scr[0, KV_WIDTH:KV_WIDTH + 1, :], acc_scr[1, KV_WIDTH:KV_WIDTH + 1, :]))

    @pl.when(jnp.logical_not(d_min >= DENOM_FLOOR))
    def _():
        tie_bias = make_tie_bias()

        def mask_chunk(c, carry):
            seen, m_acc = carry
            ks = chunk_start(c)
            bias, seen = tie_bias(c, seen)
            bias_scr[pl.ds(ks, KEY_CHUNK), :] = bias
            bias4 = jnp.concatenate([bias] * ATT_REP, axis=1)
            return seen, tuple(jnp.maximum(m_acc[g], fold(s_scr[g, pl.ds(ks, KEY_CHUNK), :] + bias4, jnp.max))
                               for g in range(ATT_KV_HEADS))

        m_init = tuple(jnp.full((SUBLANES, att_cols), NEG_BIG, F32) for _ in range(ATT_KV_HEADS))
        _, m_acc = lax.fori_loop(0, n_chunks, mask_chunk, (no_ties_seen, m_init))
        m_sel = [jnp.max(m_acc[g], axis=0, keepdims=True) for g in range(ATT_KV_HEADS)]
        pv_pass(lambda c, carry: (bias_scr[pl.ds(chunk_start(c), KEY_CHUNK), :], carry), m_sel, jnp.int32(0))

    for g in range(ATT_KV_HEADS):
        denom = acc_scr[g, KV_WIDTH:KV_WIDTH + 1, :]
        og_scr[g * HEAD_DIM:(g + 1) * HEAD_DIM, :] = acc_scr[g, g * HEAD_DIM:(g + 1) * HEAD_DIM, :] / denom

    for j in range(ATT_REP):
        gated = og_scr[:, j * Q_TILE:(j + 1) * Q_TILE].T * ga_ref[0, :, j * LANES:(j + 1) * LANES]
        o_ref[0, :, j * LANES:(j + 1) * LANES] = gated.astype(o_ref.dtype)


def _dsa(q_t, qi_t, wi_t, k, vt, ki, gate):
    batch, seq, _ = k.shape
    topk = min(TOPK_MAX, seq // 4)
    att_cols = ATT_REP * Q_TILE
    per_q = lambda a: pl.BlockSpec((1, 1) + a.shape[2:], lambda b, i: (b, i, 0, 0))
    per_b = lambda w: pl.BlockSpec((1, seq, w), lambda b, i: (b, 0, 0))
    vt_spec = pl.BlockSpec((1, seq // KEY_CHUNK, VT_ROWS, KEY_CHUNK), lambda b, i: (b, 0, 0, 0))
    tok_spec = pl.BlockSpec((1, Q_TILE, ATT_WIDTH), lambda b, i: (b, i, 0))
    return pl.pallas_call(
        functools.partial(_dsa_kernel, topk=topk),
        grid=(batch, seq // Q_TILE),
        in_specs=[per_q(q_t), per_q(qi_t), per_q(wi_t), per_b(KV_WIDTH), vt_spec, per_b(LANES), tok_spec],
        out_specs=tok_spec,
        out_shape=jax.ShapeDtypeStruct((batch, seq, ATT_WIDTH), BF16),
        scratch_shapes=[pltpu.VMEM((seq, Q_TILE), F32),
                        pltpu.VMEM((seq, Q_TILE), F32),
                        pltpu.VMEM((SUBLANES, Q_TILE), F32),
                        pltpu.VMEM((ATT_KV_HEADS, seq, att_cols), F32),
                        pltpu.VMEM((ATT_KV_HEADS, VT_ROWS, att_cols), F32),
                        pltpu.VMEM((KV_WIDTH, att_cols), F32)],
        compiler_params=pltpu.CompilerParams(dimension_semantics=("arbitrary", "arbitrary"),
                                             vmem_limit_bytes=VMEM_LIMIT_BYTES),
        name="dsa",
    )(q_t, qi_t, wi_t, k, vt, ki, gate)


def _s5_kernel(u_ref, bm_ref, cm_ref, are_ref, aim_ref, dsk_ref, wg_ref, bg_ref, o_ref,
               st_scr, xb_scr, carry_scr):
    tc, batch, width = u_ref.shape
    steps = tc // SCAN_SPLIT
    rows = steps * batch

    @pl.when(pl.program_id(0) == 0)
    def _():
        carry_scr[...] = jnp.zeros(carry_scr.shape, F32)

    u = [u_ref[h * steps:(h + 1) * steps].reshape(rows, width) for h in range(SCAN_SPLIT)]
    for h in range(SCAN_SPLIT):
        st_scr[h] = jnp.dot(u[h].astype(BF16), bm_ref[...], preferred_element_type=F32)

    parts = []
    for part in range(SSM_LANES // SCAN_LANES):
        re_lo = part * SCAN_LANES
        im_lo = SSM_LANES + re_lo
        parts.append((slice(re_lo, re_lo + SCAN_LANES), slice(im_lo, im_lo + SCAN_LANES)))
    state = [(carry_scr[:, re], carry_scr[:, im]) for re, im in parts]

    for h in range(SCAN_SPLIT):
        for p, (re, im) in enumerate(parts):
            a_re = jnp.broadcast_to(are_ref[:, re], (batch, SCAN_LANES))
            a_im = jnp.broadcast_to(aim_ref[:, re], (batch, SCAN_LANES))
            x_re, x_im = state[p]
            for t in range(steps):
                r = slice(t * batch, (t + 1) * batch)
                x_re, x_im = (a_re * x_re - a_im * x_im + st_scr[h, r, re],
                              a_re * x_im + a_im * x_re + st_scr[h, r, im])
                xb_scr[h, r, re] = x_re.astype(BF16)
                xb_scr[h, r, im] = x_im.astype(BF16)
            state[p] = (x_re, x_im)
        y = jnp.dot(xb_scr[h], cm_ref[...], preferred_element_type=F32) + dsk_ref[...] * u[h]
        y = jax.nn.gelu(y)
        gate = jax.nn.sigmoid(jnp.dot(y.astype(BF16), wg_ref[...], preferred_element_type=F32) + bg_ref[...])
        o_ref[h * steps:(h + 1) * steps] = (y * gate).reshape(steps, batch, width)

    for (re, im), (x_re, x_im) in zip(parts, state):
        carry_scr[:, re] = x_re
        carry_scr[:, im] = x_im


def _s5(u_tm, bmat, cmat, a_re, a_im, dskip, w_glu, b_glu):
    seq, batch, width = u_tm.shape
    tc = TIME_CHUNK
    full = lambda shape: pl.BlockSpec(shape, lambda t: tuple(0 for _ in shape))
    return pl.pallas_call(
        _s5_kernel,
        grid=(seq // tc,),
        in_specs=[pl.BlockSpec((tc, batch, width), lambda t: (t, 0, 0)),
                  full(bmat.shape), full(cmat.shape), full(a_re.shape), full(a_im.shape),
                  full(dskip.shape), full(w_glu.shape), full(b_glu.shape)],
        out_specs=pl.BlockSpec((tc, batch, width), lambda t: (t, 0, 0)),
        out_shape=jax.ShapeDtypeStruct((seq, batch, width), F32),
        scratch_shapes=[pltpu.VMEM((SCAN_SPLIT, tc // SCAN_SPLIT * batch, 2 * SSM_LANES), F32),
                        pltpu.VMEM((SCAN_SPLIT, tc // SCAN_SPLIT * batch, 2 * SSM_LANES), BF16),
                        pltpu.VMEM((batch, 2 * SSM_LANES), F32)],
        compiler_params=pltpu.CompilerParams(dimension_semantics=("arbitrary",),
                                             vmem_limit_bytes=VMEM_LIMIT_BYTES),
        name="s5",
    )(u_tm, bmat, cmat, a_re, a_im, dskip, w_glu, b_glu)


def _s5_params(lam_re, lam_im, log_dt, b_re, b_im, c_re, c_im):
    dt = jnp.exp(log_dt.astype(F32))[:, None]
    mag = jnp.exp(lam_re.astype(F32) * dt)
    ang = lam_im.astype(F32) * dt
    lb_re, lb_im = mag * jnp.cos(ang), mag * jnp.sin(ang)
    den = lam_re * lam_re + lam_im * lam_im
    k_re = ((lb_re - 1.0) * lam_re + lb_im * lam_im) / den
    k_im = (lb_im * lam_re - (lb_re - 1.0) * lam_im) / den
    bb_re = k_re[:, :, None] * b_re - k_im[:, :, None] * b_im
    bb_im = k_re[:, :, None] * b_im + k_im[:, :, None] * b_re
    same_group = (np.arange(SSM_WIDTH)[:, None] // SSM_GROUP) == (np.arange(SSM_LANES)[None, :] // SSM_STATE)

    def drive(bb):
        per_group = jnp.swapaxes(bb, 1, 2).reshape(SSM_WIDTH, SSM_STATE)
        return jnp.where(same_group, jnp.tile(per_group, (1, SSM_GROUPS)), 0.0)

    def read(cc):
        per_group = jnp.swapaxes(cc, 1, 2).reshape(SSM_LANES, SSM_GROUP)
        return jnp.where(same_group.T, jnp.tile(per_group, (1, SSM_GROUPS)), 0.0)

    bmat = jnp.concatenate([drive(bb_re), drive(bb_im)], axis=1)
    cmat = jnp.concatenate([read(c_re.astype(F32)), read(-c_im.astype(F32))], axis=0)
    return (bmat.astype(BF16), cmat.astype(BF16),
            lb_re.reshape(1, SSM_LANES), lb_im.reshape(1, SSM_LANES))


def _combine_kernel(x_ref, ca_ref, os_ref, gs_ref, qm_ref, gm_ref, mk_ref, mv_ref,
                    wo_ref, lg_ref, lb_ref, out_ref):
    tile = x_ref.shape[1]
    nt_dims = (((1,), (1,)), ((), ()))
    lane = lax.broadcasted_iota(I32, (tile, LANES), 1)
    low = lane < HEAD_DIM

    qm = qm_ref[0].astype(F32)
    mem_tiles = []
    for t in range(MEM_WIDTH // LANES):
        qt = qm[:, t * LANES:(t + 1) * LANES]
        mk = mk_ref[0, :, t * LANES:(t + 1) * LANES]
        mv = mv_ref[0, :, t * LANES:(t + 1) * LANES]
        halves = []
        for first in (True, False):
            qh = jnp.where(low if first else ~low, qt, 0.0).astype(BF16)
            s = lax.dot_general(qh, mk, nt_dims, preferred_element_type=F32)
            p = jnp.exp(s - jnp.max(s, axis=1, keepdims=True))
            o = jnp.dot(p.astype(BF16), mv, preferred_element_type=F32)
            halves.append(o / jnp.sum(p, axis=1, keepdims=True))
        mem_tiles.append(jnp.where(low, halves[0], halves[1]))
    o_mem = jnp.concatenate(mem_tiles, axis=1)

    c_att = ca_ref[0]
    c_ssm = (os_ref[...] * gs_ref[0]).astype(BF16)
    c_mem = (o_mem * gm_ref[0]).astype(BF16)
    a1, a2 = ATT_WIDTH, ATT_WIDTH + SSM_WIDTH
    sub = (jnp.dot(c_att, wo_ref[:a1, :], preferred_element_type=F32)
           + jnp.dot(c_ssm, wo_ref[a1:a2, :], preferred_element_type=F32)
           + jnp.dot(c_mem, wo_ref[a2:, :], preferred_element_type=F32))
    h = DN_ALPHA * x_ref[0] + sub
    mu = jnp.mean(h, axis=1, keepdims=True)
    d = h - mu
    var = jnp.mean(d * d, axis=1, keepdims=True)
    out_ref[0] = d * lax.rsqrt(var + LN_EPS) * lg_ref[...] + lb_ref[...]


def _combine(x, c_att, o_ssm_tm, g_ssm, qm, g_mem, mk, mv, wo_bf16, ln_g, ln_b):
    batch, seq, _ = x.shape
    tile = TOKEN_TILE
    tok = lambda w: pl.BlockSpec((1, tile, w), lambda b, j: (b, j, 0))
    time_major = pl.BlockSpec((tile, SSM_WIDTH), lambda b, j: (j, b))
    per_b = pl.BlockSpec((1, N_MEM, MEM_WIDTH), lambda b, j: (b, 0, 0))
    const = lambda shape: pl.BlockSpec(shape, lambda b, j: tuple(0 for _ in shape))
    return pl.pallas_call(
        _combine_kernel,
        grid=(batch, seq // tile),
        in_specs=[tok(D_MODEL), tok(ATT_WIDTH), time_major, tok(SSM_WIDTH),
                  tok(MEM_WIDTH), tok(MEM_WIDTH), per_b, per_b,
                  const((D_MODEL, D_MODEL)), const((1, D_MODEL)), const((1, D_MODEL))],
        out_specs=tok(D_MODEL),
        out_shape=jax.ShapeDtypeStruct((batch, seq, D_MODEL), x.dtype),
        compiler_params=pltpu.CompilerParams(dimension_semantics=("arbitrary", "arbitrary"),
                                             vmem_limit_bytes=VMEM_LIMIT_BYTES),
        name="combine",
    )(x, c_att, o_ssm_tm, g_ssm, qm, g_mem, mk, mv, wo_bf16, ln_g, ln_b)


def _reordered_w_in(w_in):
    offs = np.cumsum([0, ATT_WIDTH, KV_WIDTH, KV_WIDTH, IDX_WIDTH, IDX_DIM, IDX_HEADS,
                      ATT_WIDTH, SSM_WIDTH, SSM_WIDTH, MEM_WIDTH, MEM_WIDTH])
    part = lambda n: w_in[:, int(offs[n]):int(offs[n + 1])]
    k_idx = part(4)
    w_idx = part(5)
    zeros = jnp.zeros((D_MODEL, LANES - IDX_HEADS), w_in.dtype)
    cols = [_pair_heads(part(0), 1), part(1), part(2), part(3),
            jnp.tile(k_idx, (1, LANES // IDX_DIM)),
            jnp.concatenate([w_idx, zeros], axis=1),
            _pair_heads(part(6), 1), part(7), part(8), part(9), part(10)]
    return jnp.concatenate(cols, axis=1).astype(BF16)


def kernel(x, mem, w_in, w_mem_kv, lam_re, lam_im, log_dt, b_re, b_im, c_re, c_im, d_skip, w_glu, b_glu,
           w_out, ln_g, ln_b):
    batch, seq, _ = x.shape
    assert seq % TOKEN_TILE == 0 and seq % Q_TILE == 0 and seq % TIME_CHUNK == 0

    w_all = _reordered_w_in(w_in)
    wo = jnp.concatenate([_pair_heads(w_out[:ATT_WIDTH], 0), w_out[ATT_WIDTH:]], axis=0).astype(BF16)
    bmat, cmat, a_re, a_im = _s5_params(lam_re, lam_im, log_dt, b_re, b_im, c_re, c_im)
    tabs = _rope_tables(seq, HEAD_DIM) + _rope_tables(seq, IDX_DIM)

    mk, mv = _memkv(mem.reshape(batch * N_MEM, D_MODEL), w_mem_kv.astype(BF16))
    mk = mk.reshape(batch, N_MEM, MEM_WIDTH)
    mv = mv.reshape(batch, N_MEM, MEM_WIDTH)

    q_t, k, vt, qi_t, ki, wi_t, g_att, u_tm, g_ssm, qm, g_mem = _inproj(x, w_all, tabs)
    c_att = _dsa(q_t, qi_t, wi_t, k, vt, ki, g_att)

    o_ssm_tm = _s5(u_tm.reshape(seq, batch, SSM_WIDTH), bmat, cmat, a_re, a_im,
                   d_skip.reshape(1, SSM_WIDTH).astype(F32), w_glu.astype(BF16),
                   b_glu.reshape(1, SSM_WIDTH).astype(F32))
    o_ssm = o_ssm_tm.reshape(seq, batch * SSM_WIDTH)

    return _combine(x, c_att, o_ssm, g_ssm, qm, g_mem, mk, mv, wo,
                    ln_g.reshape(1, D_MODEL).astype(F32), ln_b.reshape(1, D_MODEL).astype(F32))
```

```python
import functools
import math

import jax
import jax.numpy as jnp
import numpy as np
from jax import lax
from jax.experimental import pallas as pl
from jax.experimental.pallas import tpu as pltpu

F32 = jnp.float32
BF16 = jnp.bfloat16
I32 = jnp.int32

D_MODEL = 1024
N_MEM = 256
HEAD_DIM = 64
ATT_HEADS = 8
ATT_KV_HEADS = 2
ATT_REP = ATT_HEADS // ATT_KV_HEADS
ATT_WIDTH = ATT_HEADS * HEAD_DIM
KV_WIDTH = ATT_KV_HEADS * HEAD_DIM
IDX_HEADS = 8
IDX_DIM = 32
IDX_WIDTH = IDX_HEADS * IDX_DIM
TOPK_MAX = 256
SSM_WIDTH = D_MODEL // 4
SSM_GROUP = 16
SSM_GROUPS = SSM_WIDTH // SSM_GROUP
SSM_STATE = 64
SSM_LANES = SSM_GROUPS * SSM_STATE
MEM_HEADS = 4
MEM_WIDTH = MEM_HEADS * HEAD_DIM
ROPE_THETA = 500000.0
ROPE_FRAC = 4
LN_EPS = 1e-5
DEPTH = 1
DN_ALPHA = (2.0 * DEPTH) ** 0.25
ATT_SCALE = HEAD_DIM ** -0.5
IDX_SCALE = IDX_HEADS ** -0.5 * IDX_DIM ** -0.5
LOG2_E = math.log2(math.e)

LANES = 128
SUBLANES = 8
BF16_ROWS = 16
VMEM_LIMIT_BYTES = 48 * 1024 * 1024
TOKEN_TILE = 512
Q_TILE = 256
KEY_CHUNK = 256
COUNT_ROWS = 64
VT_ROWS = KV_WIDTH + BF16_ROWS
TIME_CHUNK = 128
SCAN_SPLIT = 2
SCAN_LANES = 256
NEG_BIG = -1e30
DENOM_FLOOR = 2.0 ** -100
SEARCH_MAX_PASSES = 24
SEARCH_FIRST_PASSES = 13
SEARCH_GROUP = 2
SEARCH_BISECT_PASSES = 2
SEARCH_CLIP = 0.1

INT_MIN = -2147483648
KEY_POS_INF = 0x7F800000
KEY_NEG_INF = INT_MIN + 0x7FFFFF

_G_Q, _G_K, _G_V, _G_QI, _G_KI, _G_W, _G_GA, _G_U, _G_GS, _G_QM, _G_GM = range(11)
_GROUP_WIDTHS = [ATT_WIDTH, KV_WIDTH, KV_WIDTH, IDX_WIDTH, LANES, LANES, ATT_WIDTH,
                 SSM_WIDTH, SSM_WIDTH, MEM_WIDTH, MEM_WIDTH]
_GROUP_OFFS = [int(v) for v in np.cumsum([0] + _GROUP_WIDTHS)]
IN_COLS = _GROUP_OFFS[-1]


def _pair_heads(w, axis):
    shape = w.shape
    split = shape[:axis] + (ATT_KV_HEADS, ATT_REP, HEAD_DIM) + shape[axis + 1:]
    return jnp.swapaxes(w.reshape(split), axis, axis + 1).reshape(shape)


def _rope_tables(seq, period, dtype=F32):
    r = period // ROPE_FRAC
    half = r // 2
    inv = ROPE_THETA ** (-jnp.arange(0, half, dtype=F32) * 2.0 / r)
    ang = jnp.arange(seq).astype(F32)[:, None] * inv[None, :]
    cos, sin = jnp.cos(ang), jnp.sin(ang)
    ones = jnp.ones((seq, period - r), F32)
    zeros = jnp.zeros((seq, period - r), F32)
    c = jnp.concatenate([cos, cos, ones], axis=1)
    s = jnp.concatenate([-sin, sin, zeros], axis=1)
    reps = LANES // period
    return jnp.tile(c, (1, reps)).astype(dtype), jnp.tile(s, (1, reps)).astype(dtype)


def _rope(z, cos_t, sin_t, period):
    width = z.shape[1]
    half = period // ROPE_FRAC // 2
    reps = width // LANES
    if reps > 1:
        cos_t = jnp.concatenate([cos_t] * reps, axis=1)
        sin_t = jnp.concatenate([sin_t] * reps, axis=1)
    lane = lax.broadcasted_iota(I32, z.shape, 1)
    first = (lane & (period - 1)) < half
    up = pltpu.roll(z, half, 1)
    down = pltpu.roll(z, width - half, 1)
    return z * cos_t + jnp.where(first, down, up) * sin_t


def _memkv_kernel(mem_ref, w_ref, mk_ref, mv_ref):
    z = jnp.dot(mem_ref[...].astype(BF16), w_ref[...], preferred_element_type=F32)
    mk_ref[...] = z[:, :MEM_WIDTH].astype(BF16)
    mv_ref[...] = z[:, MEM_WIDTH:].astype(BF16)


def _memkv(mem2d, w_bf16):
    rows = mem2d.shape[0]
    tile = TOKEN_TILE
    return pl.pallas_call(
        _memkv_kernel,
        grid=(rows // tile,),
        in_specs=[pl.BlockSpec((tile, D_MODEL), lambda i: (i, 0)),
                  pl.BlockSpec((D_MODEL, 2 * MEM_WIDTH), lambda i: (0, 0))],
        out_specs=[pl.BlockSpec((tile, MEM_WIDTH), lambda i: (i, 0)),
                   pl.BlockSpec((tile, MEM_WIDTH), lambda i: (i, 0))],
        out_shape=[jax.ShapeDtypeStruct((rows, MEM_WIDTH), BF16),
                   jax.ShapeDtypeStruct((rows, MEM_WIDTH), BF16)],
        compiler_params=pltpu.CompilerParams(dimension_semantics=("arbitrary",),
                                             vmem_limit_bytes=VMEM_LIMIT_BYTES),
        name="memkv",
    )(mem2d, w_bf16)


def _inproj_kernel(x_ref, w_ref, ca_ref, sa_ref, ci_ref, si_ref,
                   q_ref, k_ref, vt_ref, qi_ref, ki_ref, wi_ref, ga_ref, u_ref, gs_ref, qm_ref, gm_ref):
    tile = x_ref.shape[1]
    pos0 = pl.multiple_of(pl.program_id(1) * tile, tile)
    xb = x_ref[0].astype(BF16)

    def proj(group):
        lo, hi = _GROUP_OFFS[group], _GROUP_OFFS[group + 1]
        return jnp.dot(xb, w_ref[:, lo:hi], preferred_element_type=F32)

    ca = ca_ref[pl.ds(pos0, tile), :]
    sa = sa_ref[pl.ds(pos0, tile), :]
    ci = ci_ref[pl.ds(pos0, tile), :]
    si = si_ref[pl.ds(pos0, tile), :]

    def store_transposed(ref, z):
        for qb in range(tile // Q_TILE):
            for t in range(z.shape[1] // LANES):
                zt = z[qb * Q_TILE:(qb + 1) * Q_TILE, t * LANES:(t + 1) * LANES].T
                ref[0, qb, :, t * Q_TILE:(t + 1) * Q_TILE] = zt.astype(ref.dtype)

    store_transposed(q_ref, _rope(proj(_G_Q), ca, sa, HEAD_DIM) * (ATT_SCALE * LOG2_E))
    k_ref[0] = _rope(proj(_G_K), ca, sa, HEAD_DIM).astype(BF16)
    v = proj(_G_V)
    for c in range(tile // KEY_CHUNK):
        vt_ref[0, c, :KV_WIDTH, :] = v[c * KEY_CHUNK:(c + 1) * KEY_CHUNK, :].T.astype(BF16)
        vt_ref[0, c, KV_WIDTH:, :] = jnp.ones((BF16_ROWS, KEY_CHUNK), BF16)
    store_transposed(qi_ref, _rope(proj(_G_QI), ci, si, IDX_DIM))
    ki_ref[0] = _rope(proj(_G_KI), ci, si, IDX_DIM).astype(BF16)
    wi = proj(_G_W) * IDX_SCALE
    for qb in range(tile // Q_TILE):
        wi_ref[0, qb] = wi[qb * Q_TILE:(qb + 1) * Q_TILE, :].T[:IDX_HEADS, :]
    ga_ref[0] = jax.nn.silu(proj(_G_GA))
    u_ref[...] = proj(_G_U)
    gs_ref[0] = jax.nn.silu(proj(_G_GS))
    qm_ref[0] = (proj(_G_QM) * ATT_SCALE).astype(BF16)
    gm_ref[0] = jax.nn.silu(proj(_G_GM))


def _inproj(x, w_bf16, tabs):
    batch, seq, _ = x.shape
    tile = TOKEN_TILE
    widths = _GROUP_WIDTHS
    dtypes = [BF16, BF16, BF16, BF16, BF16, F32, F32, F32, F32, BF16, F32]
    tab_spec = pl.BlockSpec((seq, LANES), lambda b, j: (0, 0))
    out_specs = [pl.BlockSpec((1, tile, w), lambda b, j: (b, j, 0)) for w in widths]
    out_shape = [jax.ShapeDtypeStruct((batch, seq, w), d) for w, d in zip(widths, dtypes)]
    out_specs[_G_V] = pl.BlockSpec((1, tile // KEY_CHUNK, VT_ROWS, KEY_CHUNK), lambda b, j: (b, j, 0, 0))
    out_shape[_G_V] = jax.ShapeDtypeStruct((batch, seq // KEY_CHUNK, VT_ROWS, KEY_CHUNK), BF16)
    for group, rows, dtype in ((_G_Q, LANES, BF16), (_G_QI, LANES, BF16), (_G_W, IDX_HEADS, F32)):
        cols = Q_TILE * max(widths[group] // LANES, 1)
        out_specs[group] = pl.BlockSpec((1, tile // Q_TILE, rows, cols), lambda b, j: (b, j, 0, 0))
        out_shape[group] = jax.ShapeDtypeStruct((batch, seq // Q_TILE, rows, cols), dtype)
    out_specs[_G_U] = pl.BlockSpec((tile, SSM_WIDTH), lambda b, j: (j, b))
    out_shape[_G_U] = jax.ShapeDtypeStruct((seq, batch * SSM_WIDTH), F32)
    return pl.pallas_call(
        _inproj_kernel,
        grid=(batch, seq // tile),
        in_specs=[pl.BlockSpec((1, tile, D_MODEL), lambda b, j: (b, j, 0)),
                  pl.BlockSpec((D_MODEL, IN_COLS), lambda b, j: (0, 0)),
                  tab_spec, tab_spec, tab_spec, tab_spec],
        out_specs=out_specs,
        out_shape=out_shape,
        compiler_params=pltpu.CompilerParams(dimension_semantics=("arbitrary", "arbitrary"),
                                             vmem_limit_bytes=VMEM_LIMIT_BYTES),
        name="inproj",
    )(x, w_bf16, *tabs)


def _key_to_float(key):
    bits = jnp.where(key >= 0, key, key ^ 0x7FFFFFFF)
    val = lax.bitcast_convert_type(bits, F32)
    val = jnp.where(key <= KEY_NEG_INF, -jnp.inf, val)
    return jnp.where(key >= KEY_POS_INF, jnp.inf, val)


def _dsa_kernel(q_ref, qi_ref, wi_ref, k_ref, vt_ref, ki_ref, ga_ref, o_ref,
                score_scr, bias_scr, row_scr, s_scr, acc_scr, og_scr, *, topk):
    i = pl.program_id(1)
    n_chunks = ((i + 1) * Q_TILE + KEY_CHUNK - 1) // KEY_CHUNK
    q0 = i * Q_TILE
    kf = float(topk)
    att_cols = ATT_REP * Q_TILE
    groups = KEY_CHUNK // SUBLANES

    def chunk_start(c):
        return pl.multiple_of(c * KEY_CHUNK, KEY_CHUNK)

    def over_chunk_pairs(body, init):
        def pair(c2, carry):
            return body(2 * c2 + 1, body(2 * c2, carry))
        carry = lax.fori_loop(0, n_chunks // 2, pair, init)
        return lax.cond(n_chunks % 2 == 1, lambda cr: body(n_chunks - 1, cr), lambda cr: cr, carry)

    def fold(v, op):
        return op(v.reshape(groups, SUBLANES, v.shape[-1]), axis=0)

    def only_rows(x, lo, n):
        parts = [jnp.zeros((lo, x.shape[1]), x.dtype)] if lo else []
        parts.append(x[lo:lo + n])
        if lo + n < x.shape[0]:
            parts.append(jnp.zeros((x.shape[0] - lo - n, x.shape[1]), x.dtype))
        return jnp.concatenate(parts, axis=0)

    qi_t = qi_ref[0, 0]
    per_tile = LANES // IDX_DIM

    def head_cols(h):
        t = h // per_tile
        return only_rows(qi_t[:, t * Q_TILE:(t + 1) * Q_TILE], (h % per_tile) * IDX_DIM, IDX_DIM)

    rhs_pairs = [jnp.concatenate([head_cols(2 * p), head_cols(2 * p + 1)], axis=1)
                 for p in range(IDX_HEADS // 2)]
    w_rows = [wi_ref[0, 0, h:h + 1, :] for h in range(IDX_HEADS)]
    q_t = q_ref[0, 0]
    qg_t = [only_rows(q_t, g * HEAD_DIM, HEAD_DIM) for g in range(ATT_KV_HEADS)]
    krow = lax.broadcasted_iota(I32, (KEY_CHUNK, Q_TILE), 0)
    kq = q0 + lax.broadcasted_iota(I32, (KEY_CHUNK, Q_TILE), 1)

    def matmul_chunk(c, stats):
        mx_a, mn_a, ge_a, gt_a, top_a = stats
        ks = chunk_start(c)
        kc = ki_ref[0, pl.ds(ks, KEY_CHUNK), :]
        acc = jnp.zeros((KEY_CHUNK, Q_TILE), F32)
        for p in range(IDX_HEADS // 2):
            s2 = jnp.dot(kc, rhs_pairs[p], preferred_element_type=F32)
            acc = acc + w_rows[2 * p] * jnp.maximum(s2[:, :Q_TILE], 0.0)
            acc = acc + w_rows[2 * p + 1] * jnp.maximum(s2[:, Q_TILE:], 0.0)
        causal = ks + krow <= kq
        sc = jnp.where(causal, acc, -jnp.inf)
        score_scr[pl.ds(ks, KEY_CHUNK), :] = sc
        kk = k_ref[0, pl.ds(ks, KEY_CHUNK), :]
        tops = []
        for g in range(ATT_KV_HEADS):
            s = jnp.dot(kk, qg_t[g], preferred_element_type=F32)
            s_scr[g, pl.ds(ks, KEY_CHUNK), :] = s
            tops.append(jnp.maximum(top_a[g], fold(s, jnp.max)))
        return (jnp.maximum(mx_a, fold(sc, jnp.max)),
                jnp.minimum(mn_a, fold(jnp.where(causal, acc, jnp.inf), jnp.min)),
                ge_a + fold(jnp.where(sc >= 0.0, 1.0, 0.0), jnp.sum),
                gt_a + fold(jnp.where(sc > 0.0, 1.0, 0.0), jnp.sum),
                tuple(tops))

    stat0 = lambda v: jnp.full((SUBLANES, Q_TILE), v, F32)
    top0 = tuple(jnp.full((SUBLANES, att_cols), NEG_BIG, F32) for _ in range(ATT_KV_HEADS))
    mx_a, mn_a, ge_a, gt_a, top_a = over_chunk_pairs(
        matmul_chunk, (stat0(-jnp.inf), stat0(jnp.inf), stat0(0.0), stat0(0.0), top0))
    m_top = [jnp.max(top_a[g], axis=0, keepdims=True) for g in range(ATT_KV_HEADS)]
    s_max = jnp.max(mx_a, axis=0, keepdims=True)
    s_min = jnp.min(mn_a, axis=0, keepdims=True)
    n_ge0 = jnp.sum(ge_a, axis=0, keepdims=True)
    n_gt0 = jnp.sum(gt_a, axis=0, keepdims=True)
    n_causal = (q0 + 1 + lax.broadcasted_iota(I32, (1, Q_TILE), 1)).astype(F32)

    def count(pred):
        def body(c, acc):
            ks = chunk_start(c)
            for part in range(KEY_CHUNK // LANES):
                hit = jnp.where(pred(score_scr[pl.ds(ks + part * LANES, LANES), :]), 1.0, 0.0)
                acc = acc + (hit[:COUNT_ROWS] + hit[COUNT_ROWS:])
            return acc
        acc = over_chunk_pairs(body, jnp.zeros((COUNT_ROWS, Q_TILE), F32))
        rows = COUNT_ROWS
        while rows > SUBLANES:
            rows //= 2
            acc = acc[:rows] + acc[rows:]
        return jnp.sum(acc, axis=0, keepdims=True)

    def count_ge(value):
        vb = jnp.broadcast_to(value, (LANES, Q_TILE))
        return count(lambda sc: sc >= vb)

    zero_thr = (n_gt0 < kf) & (n_ge0 >= kf)
    keep_all = n_causal <= kf
    positive = n_gt0 >= kf
    settled = zero_thr | keep_all
    lo0 = jnp.where(settled, 0.0, jnp.where(positive, 0.0, s_min))
    hi0 = jnp.where(settled, 0.0, jnp.where(positive, s_max, 0.0))
    c_lo0 = jnp.where(positive, n_gt0, n_causal)
    c_hi0 = jnp.where(positive, 0.0, n_ge0)
    thr0 = jnp.where(keep_all, -jnp.inf, 0.0)
    done0 = jnp.where(settled, 1.0, 0.0)

    def search_pass(bisect, rows):
        lo, hi, c_lo, c_hi, thr, done = rows
        if bisect:
            frac = 0.5
        else:
            frac = jnp.clip((kf - c_hi + 0.5) / (c_lo - c_hi + 1.0), SEARCH_CLIP, 1.0 - SEARCH_CLIP)
        x = hi - (hi - lo) * frac
        c = count_ge(x)
        live = done == 0.0
        above = live & (c >= kf)
        below = live & (c < kf)
        hit = live & (c == kf)
        return (jnp.where(above, x, lo), jnp.where(below, x, hi),
                jnp.where(above, c, c_lo), jnp.where(below, c, c_hi),
                jnp.where(hit, x, thr), jnp.where(hit, 1.0, done))

    def search_cond(state):
        return (state[0] < SEARCH_MAX_PASSES) & (state[1] > 0.0)

    def search_body(state):
        it, _, rows = state
        for _ in range(SEARCH_GROUP):
            rows = search_pass(False, rows)
        return it + SEARCH_GROUP, jnp.sum(1.0 - rows[5]), rows

    def first_passes(rows):
        for p in range(SEARCH_FIRST_PASSES):
            rows = search_pass(p < SEARCH_BISECT_PASSES, rows)
        return rows

    rows0 = (lo0, hi0, c_lo0, c_hi0, thr0, done0)
    rows1 = lax.cond(jnp.sum(1.0 - done0) > 0.0, first_passes, lambda rows: rows, rows0)
    state = lax.while_loop(search_cond, search_body,
                           (jnp.int32(SEARCH_FIRST_PASSES), jnp.sum(1.0 - rows1[5]), rows1))
    pending, thr_s, done_s = state[1], state[2][4], state[2][5]
    row_scr[0:1, :] = thr_s
    row_scr[1:2, :] = jnp.where(zero_thr, n_gt0, 0.0)
    row_scr[2:3, :] = jnp.where(zero_thr, n_ge0, kf)

    @pl.when(pending > 0.0)
    def _():
        def count_ge_key(key):
            return count_ge(_key_to_float(key))

        key0 = jnp.where(count_ge_key(jnp.zeros((1, Q_TILE), I32)) >= kf, 0, INT_MIN).astype(I32)

        def bit_step(b, key):
            cand = key | jnp.left_shift(jnp.int32(1), 30 - b)
            return jnp.where(count_ge_key(cand) >= kf, cand, key)

        thr_f = _key_to_float(lax.fori_loop(0, 31, bit_step, key0))
        thr_fb = jnp.broadcast_to(thr_f, (LANES, Q_TILE))
        open_row = done_s == 0.0
        row_scr[0:1, :] = jnp.where(open_row, thr_f, thr_s)
        row_scr[1:2, :] = jnp.where(open_row, count(lambda sc: sc > thr_fb), row_scr[1:2, :])
        row_scr[2:3, :] = jnp.where(open_row, count(lambda sc: sc >= thr_fb), row_scr[2:3, :])

    thr = row_scr[0:1, :]
    thr_c = jnp.broadcast_to(thr, (KEY_CHUNK, Q_TILE))
    has_tie = (row_scr[2:3, :] > kf) & (thr > -jnp.inf)
    any_tie = jnp.max(jnp.where(has_tie, 1.0, 0.0)) > 0.0

    def plain_bias(c, carry):
        ks = chunk_start(c)
        keep = (score_scr[pl.ds(ks, KEY_CHUNK), :] >= thr_c) & (ks + krow <= kq)
        return jnp.where(keep, 0.0, NEG_BIG), carry

    def make_tie_bias():
        need = jnp.broadcast_to(kf - row_scr[1:2, :], (KEY_CHUNK, Q_TILE))
        lower = (lax.broadcasted_iota(I32, (KEY_CHUNK, KEY_CHUNK), 1)
                 <= lax.broadcasted_iota(I32, (KEY_CHUNK, KEY_CHUNK), 0))
        prefix_mat = jnp.where(lower, 1.0, 0.0).astype(BF16)

        def tie_bias(c, seen):
            ks = chunk_start(c)
            sc = score_scr[pl.ds(ks, KEY_CHUNK), :]
            tied = sc == thr_c
            rank = seen + jnp.dot(prefix_mat, jnp.where(tied, 1.0, 0.0).astype(BF16),
                                  preferred_element_type=F32)
            chosen = (sc > thr_c) | (tied & (rank <= need))
            return jnp.where(chosen & (ks + krow <= kq), 0.0, NEG_BIG), rank[KEY_CHUNK - 1:KEY_CHUNK, :]

        return tie_bias

    no_ties_seen = jnp.zeros((1, Q_TILE), F32)

    def pv_pass(bias_of, m, carry0):
        acc_scr[...] = jnp.zeros(acc_scr.shape, F32)

        def pv_chunk(c, state):
            bias, carry = state
            state = bias_of(jnp.minimum(c + 1, n_chunks - 1), carry)
            bias4 = jnp.concatenate([bias] * ATT_REP, axis=1)
            for g in range(ATT_KV_HEADS):
                p = jnp.exp2(s_scr[g, pl.ds(chunk_start(c), KEY_CHUNK), :] + bias4 - m[g])
                acc_scr[g] += jnp.dot(vt_ref[0, c], p.astype(BF16), preferred_element_type=F32)
            return state

        over_chunk_pairs(pv_chunk, bias_of(jnp.int32(0), carry0))

    @pl.when(jnp.logical_not(any_tie))
    def _():
        pv_pass(plain_bias, m_top, jnp.int32(0))

    @pl.when(any_tie)
    def _():
        pv_pass(make_tie_bias(), m_top, no_ties_seen)

    d_min = jnp.min(jnp.minimum(acc_scr[0, KV_WIDTH:KV_WIDTH + 1, :], acc_scr[1, KV_WIDTH:KV_WIDTH + 1, :]))

    @pl.when(jnp.logical_not(d_min >= DENOM_FLOOR))
    def _():
        tie_bias = make_tie_bias()

        def mask_chunk(c, carry):
            seen, m_acc = carry
            ks = chunk_start(c)
            bias, seen = tie_bias(c, seen)
            bias_scr[pl.ds(ks, KEY_CHUNK), :] = bias
            bias4 = jnp.concatenate([bias] * ATT_REP, axis=1)
            return seen, tuple(jnp.maximum(m_acc[g], fold(s_scr[g, pl.ds(ks, KEY_CHUNK), :] + bias4, jnp.max))
                               for g in range(ATT_KV_HEADS))

        m_init = tuple(jnp.full((SUBLANES, att_cols), NEG_BIG, F32) for _ in range(ATT_KV_HEADS))
        _, m_acc = lax.fori_loop(0, n_chunks, mask_chunk, (no_ties_seen, m_init))
        m_sel = [jnp.max(m_acc[g], axis=0, keepdims=True) for g in range(ATT_KV_HEADS)]
        pv_pass(lambda c, carry: (bias_scr[pl.ds(chunk_start(c), KEY_CHUNK), :], carry), m_sel, jnp.int32(0))

    for g in range(ATT_KV_HEADS):
        denom = acc_scr[g, KV_WIDTH:KV_WIDTH + 1, :]
        og_scr[g * HEAD_DIM:(g + 1) * HEAD_DIM, :] = acc_scr[g, g * HEAD_DIM:(g + 1) * HEAD_DIM, :] / denom

    for j in range(ATT_REP):
        gated = og_scr[:, j * Q_TILE:(j + 1) * Q_TILE].T * ga_ref[0, :, j * LANES:(j + 1) * LANES]
        o_ref[0, :, j * LANES:(j + 1) * LANES] = gated.astype(o_ref.dtype)


def _dsa(q_t, qi_t, wi_t, k, vt, ki, gate):
    batch, seq, _ = k.shape
    topk = min(TOPK_MAX, seq // 4)
    att_cols = ATT_REP * Q_TILE
    per_q = lambda a: pl.BlockSpec((1, 1) + a.shape[2:], lambda b, i: (b, i, 0, 0))
    per_b = lambda w: pl.BlockSpec((1, seq, w), lambda b, i: (b, 0, 0))
    vt_spec = pl.BlockSpec((1, seq // KEY_CHUNK, VT_ROWS, KEY_CHUNK), lambda b, i: (b, 0, 0, 0))
    tok_spec = pl.BlockSpec((1, Q_TILE, ATT_WIDTH), lambda b, i: (b, i, 0))
    return pl.pallas_call(
        functools.partial(_dsa_kernel, topk=topk),
        grid=(batch, seq // Q_TILE),
        in_specs=[per_q(q_t), per_q(qi_t), per_q(wi_t), per_b(KV_WIDTH), vt_spec, per_b(LANES), tok_spec],
        out_specs=tok_spec,
        out_shape=jax.ShapeDtypeStruct((batch, seq, ATT_WIDTH), BF16),
        scratch_shapes=[pltpu.VMEM((seq, Q_TILE), F32),
                        pltpu.VMEM((seq, Q_TILE), F32),
                        pltpu.VMEM((SUBLANES, Q_TILE), F32),
                        pltpu.VMEM((ATT_KV_HEADS, seq, att_cols), F32),
                        pltpu.VMEM((ATT_KV_HEADS, VT_ROWS, att_cols), F32),
                        pltpu.VMEM((KV_WIDTH, att_cols), F32)],
        compiler_params=pltpu.CompilerParams(dimension_semantics=("arbitrary", "arbitrary"),
                                             vmem_limit_bytes=VMEM_LIMIT_BYTES),
        name="dsa",
    )(q_t, qi_t, wi_t, k, vt, ki, gate)


def _s5_kernel(u_ref, bm_ref, cm_ref, are_ref, aim_ref, dsk_ref, wg_ref, bg_ref, o_ref,
               st_scr, xb_scr, carry_scr):
    tc, batch, width = u_ref.shape
    steps = tc // SCAN_SPLIT
    rows = steps * batch

    @pl.when(pl.program_id(0) == 0)
    def _():
        carry_scr[...] = jnp.zeros(carry_scr.shape, F32)

    u = [u_ref[h * steps:(h + 1) * steps].reshape(rows, width) for h in range(SCAN_SPLIT)]
    for h in range(SCAN_SPLIT):
        st_scr[h] = jnp.dot(u[h].astype(BF16), bm_ref[...], preferred_element_type=F32)

    parts = []
    for part in range(SSM_LANES // SCAN_LANES):
        re_lo = part * SCAN_LANES
        im_lo = SSM_LANES + re_lo
        parts.append((slice(re_lo, re_lo + SCAN_LANES), slice(im_lo, im_lo + SCAN_LANES)))
    state = [(carry_scr[:, re], carry_scr[:, im]) for re, im in parts]

    for h in range(SCAN_SPLIT):
        for p, (re, im) in enumerate(parts):
            a_re = jnp.broadcast_to(are_ref[:, re], (batch, SCAN_LANES))
            a_im = jnp.broadcast_to(aim_ref[:, re], (batch, SCAN_LANES))
            x_re, x_im = state[p]
            for t in range(steps):
                r = slice(t * batch, (t + 1) * batch)
                x_re, x_im = (a_re * x_re - a_im * x_im + st_scr[h, r, re],
                              a_re * x_im + a_im * x_re + st_scr[h, r, im])
                xb_scr[h, r, re] = x_re.astype(BF16)
                xb_scr[h, r, im] = x_im.astype(BF16)
            state[p] = (x_re, x_im)
        y = jnp.dot(xb_scr[h], cm_ref[...], preferred_element_type=F32) + dsk_ref[...] * u[h]
        y = jax.nn.gelu(y)
        gate = jax.nn.sigmoid(jnp.dot(y.astype(BF16), wg_ref[...], preferred_element_type=F32) + bg_ref[...])
        o_ref[h * steps:(h + 1) * steps] = (y * gate).reshape(steps, batch, width)

    for (re, im), (x_re, x_im) in zip(parts, state):
        carry_scr[:, re] = x_re
        carry_scr[:, im] = x_im


def _s5(u_tm, bmat, cmat, a_re, a_im, dskip, w_glu, b_glu):
    seq, batch, width = u_tm.shape
    tc = TIME_CHUNK
    full = lambda shape: pl.BlockSpec(shape, lambda t: tuple(0 for _ in shape))
    return pl.pallas_call(
        _s5_kernel,
        grid=(seq // tc,),
        in_specs=[pl.BlockSpec((tc, batch, width), lambda t: (t, 0, 0)),
                  full(bmat.shape), full(cmat.shape), full(a_re.shape), full(a_im.shape),
                  full(dskip.shape), full(w_glu.shape), full(b_glu.shape)],
        out_specs=pl.BlockSpec((tc, batch, width), lambda t: (t, 0, 0)),
        out_shape=jax.ShapeDtypeStruct((seq, batch, width), F32),
        scratch_shapes=[pltpu.VMEM((SCAN_SPLIT, tc // SCAN_SPLIT * batch, 2 * SSM_LANES), F32),
                        pltpu.VMEM((SCAN_SPLIT, tc // SCAN_SPLIT * batch, 2 * SSM_LANES), BF16),
                        pltpu.VMEM((batch, 2 * SSM_LANES), F32)],
        compiler_params=pltpu.CompilerParams(dimension_semantics=("arbitrary",),
                                             vmem_limit_bytes=VMEM_LIMIT_BYTES),
        name="s5",
    )(u_tm, bmat, cmat, a_re, a_im, dskip, w_glu, b_glu)


def _s5_params(lam_re, lam_im, log_dt, b_re, b_im, c_re, c_im):
    dt = jnp.exp(log_dt.astype(F32))[:, None]
    mag = jnp.exp(lam_re.astype(F32) * dt)
    ang = lam_im.astype(F32) * dt
    lb_re, lb_im = mag * jnp.cos(ang), mag * jnp.sin(ang)
    den = lam_re * lam_re + lam_im * lam_im
    k_re = ((lb_re - 1.0) * lam_re + lb_im * lam_im) / den
    k_im = (lb_im * lam_re - (lb_re - 1.0) * lam_im) / den
    bb_re = k_re[:, :, None] * b_re - k_im[:, :, None] * b_im
    bb_im = k_re[:, :, None] * b_im + k_im[:, :, None] * b_re
    same_group = (np.arange(SSM_WIDTH)[:, None] // SSM_GROUP) == (np.arange(SSM_LANES)[None, :] // SSM_STATE)

    def drive(bb):
        per_group = jnp.swapaxes(bb, 1, 2).reshape(SSM_WIDTH, SSM_STATE)
        return jnp.where(same_group, jnp.tile(per_group, (1, SSM_GROUPS)), 0.0)

    def read(cc):
        per_group = jnp.swapaxes(cc, 1, 2).reshape(SSM_LANES, SSM_GROUP)
        return jnp.where(same_group.T, jnp.tile(per_group, (1, SSM_GROUPS)), 0.0)

    bmat = jnp.concatenate([drive(bb_re), drive(bb_im)], axis=1)
    cmat = jnp.concatenate([read(c_re.astype(F32)), read(-c_im.astype(F32))], axis=0)
    return (bmat.astype(BF16), cmat.astype(BF16),
            lb_re.reshape(1, SSM_LANES), lb_im.reshape(1, SSM_LANES))


def _combine_kernel(x_ref, ca_ref, os_ref, gs_ref, qm_ref, gm_ref, mk_ref, mv_ref,
                    wo_ref, lg_ref, lb_ref, out_ref):
    tile = x_ref.shape[1]
    nt_dims = (((1,), (1,)), ((), ()))
    lane = lax.broadcasted_iota(I32, (tile, LANES), 1)
    low = lane < HEAD_DIM

    qm = qm_ref[0].astype(F32)
    mem_tiles = []
    for t in range(MEM_WIDTH // LANES):
        qt = qm[:, t * LANES:(t + 1) * LANES]
        mk = mk_ref[0, :, t * LANES:(t + 1) * LANES]
        mv = mv_ref[0, :, t * LANES:(t + 1) * LANES]
        halves = []
        for first in (True, False):
            qh = jnp.where(low if first else ~low, qt, 0.0).astype(BF16)
            s = lax.dot_general(qh, mk, nt_dims, preferred_element_type=F32)
            p = jnp.exp(s - jnp.max(s, axis=1, keepdims=True))
            o = jnp.dot(p.astype(BF16), mv, preferred_element_type=F32)
            halves.append(o / jnp.sum(p, axis=1, keepdims=True))
        mem_tiles.append(jnp.where(low, halves[0], halves[1]))
    o_mem = jnp.concatenate(mem_tiles, axis=1)

    c_att = ca_ref[0]
    c_ssm = (os_ref[...] * gs_ref[0]).astype(BF16)
    c_mem = (o_mem * gm_ref[0]).astype(BF16)
    a1, a2 = ATT_WIDTH, ATT_WIDTH + SSM_WIDTH
    sub = (jnp.dot(c_att, wo_ref[:a1, :], preferred_element_type=F32)
           + jnp.dot(c_ssm, wo_ref[a1:a2, :], preferred_element_type=F32)
           + jnp.dot(c_mem, wo_ref[a2:, :], preferred_element_type=F32))
    h = DN_ALPHA * x_ref[0] + sub
    mu = jnp.mean(h, axis=1, keepdims=True)
    d = h - mu
    var = jnp.mean(d * d, axis=1, keepdims=True)
    out_ref[0] = d * lax.rsqrt(var + LN_EPS) * lg_ref[...] + lb_ref[...]


def _combine(x, c_att, o_ssm_tm, g_ssm, qm, g_mem, mk, mv, wo_bf16, ln_g, ln_b):
    batch, seq, _ = x.shape
    tile = TOKEN_TILE
    tok = lambda w: pl.BlockSpec((1, tile, w), lambda b, j: (b, j, 0))
    time_major = pl.BlockSpec((tile, SSM_WIDTH), lambda b, j: (j, b))
    per_b = pl.BlockSpec((1, N_MEM, MEM_WIDTH), lambda b, j: (b, 0, 0))
    const = lambda shape: pl.BlockSpec(shape, lambda b, j: tuple(0 for _ in shape))
    return pl.pallas_call(
        _combine_kernel,
        grid=(batch, seq // tile),
        in_specs=[tok(D_MODEL), tok(ATT_WIDTH), time_major, tok(SSM_WIDTH),
                  tok(MEM_WIDTH), tok(MEM_WIDTH), per_b, per_b,
                  const((D_MODEL, D_MODEL)), const((1, D_MODEL)), const((1, D_MODEL))],
        out_specs=tok(D_MODEL),
        out_shape=jax.ShapeDtypeStruct((batch, seq, D_MODEL), x.dtype),
        compiler_params=pltpu.CompilerParams(dimension_semantics=("arbitrary", "arbitrary"),
                                             vmem_limit_bytes=VMEM_LIMIT_BYTES),
        name="combine",
    )(x, c_att, o_ssm_tm, g_ssm, qm, g_mem, mk, mv, wo_bf16, ln_g, ln_b)


def _reordered_w_in(w_in):
    offs = np.cumsum([0, ATT_WIDTH, KV_WIDTH, KV_WIDTH, IDX_WIDTH, IDX_DIM, IDX_HEADS,
                      ATT_WIDTH, SSM_WIDTH, SSM_WIDTH, MEM_WIDTH, MEM_WIDTH])
    part = lambda n: w_in[:, int(offs[n]):int(offs[n + 1])]
    k_idx = part(4)
    w_idx = part(5)
    zeros = jnp.zeros((D_MODEL, LANES - IDX_HEADS), w_in.dtype)
    cols = [_pair_heads(part(0), 1), part(1), part(2), part(3),
            jnp.tile(k_idx, (1, LANES // IDX_DIM)),
            jnp.concatenate([w_idx, zeros], axis=1),
            _pair_heads(part(6), 1), part(7), part(8), part(9), part(10)]
    return jnp.concatenate(cols, axis=1).astype(BF16)


def kernel(x, mem, w_in, w_mem_kv, lam_re, lam_im, log_dt, b_re, b_im, c_re, c_im, d_skip, w_glu, b_glu,
           w_out, ln_g, ln_b):
    batch, seq, _ = x.shape
    assert seq % TOKEN_TILE == 0 and seq % Q_TILE == 0 and seq % TIME_CHUNK == 0

    w_all = _reordered_w_in(w_in)
    wo = jnp.concatenate([_pair_heads(w_out[:ATT_WIDTH], 0), w_out[ATT_WIDTH:]], axis=0).astype(BF16)
    bmat, cmat, a_re, a_im = _s5_params(lam_re, lam_im, log_dt, b_re, b_im, c_re, c_im)
    tabs = _rope_tables(seq, HEAD_DIM) + _rope_tables(seq, IDX_DIM)

    mk, mv = _memkv(mem.reshape(batch * N_MEM, D_MODEL), w_mem_kv.astype(BF16))
    mk = mk.reshape(batch, N_MEM, MEM_WIDTH)
    mv = mv.reshape(batch, N_MEM, MEM_WIDTH)

    q_t, k, vt, qi_t, ki, wi_t, g_att, u_tm, g_ssm, qm, g_mem = _inproj(x, w_all, tabs)
    c_att = _dsa(q_t, qi_t, wi_t, k, vt, ki, g_att)

    o_ssm_tm = _s5(u_tm.reshape(seq, batch, SSM_WIDTH), bmat, cmat, a_re, a_im,
                   d_skip.reshape(1, SSM_WIDTH).astype(F32), w_glu.astype(BF16),
                   b_glu.reshape(1, SSM_WIDTH).astype(F32))
    o_ssm = o_ssm_tm.reshape(seq, batch * SSM_WIDTH)

    return _combine(x, c_att, o_ssm, g_ssm, qm, g_mem, mk, mv, wo,
                    ln_g.reshape(1, D_MODEL).astype(F32), ln_b.reshape(1, D_MODEL).astype(F32))
```

```python
import functools
import math

import jax
import jax.numpy as jnp
import numpy as np
from jax import lax
from jax.experimental import pallas as pl
from jax.experimental.pallas import tpu as pltpu

F32 = jnp.float32
BF16 = jnp.bfloat16
I32 = jnp.int32

D_MODEL = 1024
N_MEM = 256
HEAD_DIM = 64
ATT_HEADS = 8
ATT_KV_HEADS = 2
ATT_REP = ATT_HEADS // ATT_KV_HEADS
ATT_WIDTH = ATT_HEADS * HEAD_DIM
KV_WIDTH = ATT_KV_HEADS * HEAD_DIM
IDX_HEADS = 8
IDX_DIM = 32
IDX_WIDTH = IDX_HEADS * IDX_DIM
TOPK_MAX = 256
SSM_WIDTH = D_MODEL // 4
SSM_GROUP = 16
SSM_GROUPS = SSM_WIDTH // SSM_GROUP
SSM_STATE = 64
SSM_LANES = SSM_GROUPS * SSM_STATE
MEM_HEADS = 4
MEM_WIDTH = MEM_HEADS * HEAD_DIM
ROPE_THETA = 500000.0
ROPE_FRAC = 4
LN_EPS = 1e-5
DEPTH = 1
DN_ALPHA = (2.0 * DEPTH) ** 0.25
ATT_SCALE = HEAD_DIM ** -0.5
IDX_SCALE = IDX_HEADS ** -0.5 * IDX_DIM ** -0.5
LOG2_E = math.log2(math.e)

LANES = 128
SUBLANES = 8
BF16_ROWS = 16
VMEM_LIMIT_BYTES = 48 * 1024 * 1024
TOKEN_TILE = 512
Q_TILE = 256
KEY_CHUNK = 256
COUNT_ROWS = 64
VT_ROWS = KV_WIDTH + BF16_ROWS
TIME_CHUNK = 128
SCAN_SPLIT = 2
SCAN_LANES = 256
NEG_BIG = -1e30
DENOM_FLOOR = 2.0 ** -100
SEARCH_MAX_PASSES = 24
SEARCH_FIRST_PASSES = 13
SEARCH_GROUP = 2
SEARCH_BISECT_PASSES = 2
SEARCH_CLIP = 0.1

INT_MIN = -2147483648
KEY_POS_INF = 0x7F800000
KEY_NEG_INF = INT_MIN + 0x7FFFFF

_G_Q, _G_K, _G_V, _G_QI, _G_KI, _G_W, _G_GA, _G_U, _G_GS, _G_QM, _G_GM = range(11)
_GROUP_WIDTHS = [ATT_WIDTH, KV_WIDTH, KV_WIDTH, IDX_WIDTH, LANES, LANES, ATT_WIDTH,
                 SSM_WIDTH, SSM_WIDTH, MEM_WIDTH, MEM_WIDTH]
_GROUP_OFFS = [int(v) for v in np.cumsum([0] + _GROUP_WIDTHS)]
IN_COLS = _GROUP_OFFS[-1]


def _pair_heads(w, axis):
    shape = w.shape
    split = shape[:axis] + (ATT_KV_HEADS, ATT_REP, HEAD_DIM) + shape[axis + 1:]
    return jnp.swapaxes(w.reshape(split), axis, axis + 1).reshape(shape)


def _rope_tables(seq, period, dtype=F32):
    r = period // ROPE_FRAC
    half = r // 2
    inv = ROPE_THETA ** (-jnp.arange(0, half, dtype=F32) * 2.0 / r)
    ang = jnp.arange(seq).astype(F32)[:, None] * inv[None, :]
    cos, sin = jnp.cos(ang), jnp.sin(ang)
    ones = jnp.ones((seq, period - r), F32)
    zeros = jnp.zeros((seq, period - r), F32)
    c = jnp.concatenate([cos, cos, ones], axis=1)
    s = jnp.concatenate([-sin, sin, zeros], axis=1)
    reps = LANES // period
    return jnp.tile(c, (1, reps)).astype(dtype), jnp.tile(s, (1, reps)).astype(dtype)


def _rope(z, cos_t, sin_t, period):
    width = z.shape[1]
    half = period // ROPE_FRAC // 2
    reps = width // LANES
    if reps > 1:
        cos_t = jnp.concatenate([cos_t] * reps, axis=1)
        sin_t = jnp.concatenate([sin_t] * reps, axis=1)
    lane = lax.broadcasted_iota(I32, z.shape, 1)
    first = (lane & (period - 1)) < half
    up = pltpu.roll(z, half, 1)
    down = pltpu.roll(z, width - half, 1)
    return z * cos_t + jnp.where(first, down, up) * sin_t


def _memkv_kernel(mem_ref, w_ref, mk_ref, mv_ref):
    z = jnp.dot(mem_ref[...].astype(BF16), w_ref[...], preferred_element_type=F32)
    mk_ref[...] = z[:, :MEM_WIDTH].astype(BF16)
    mv_ref[...] = z[:, MEM_WIDTH:].astype(BF16)


def _memkv(mem2d, w_bf16):
    rows = mem2d.shape[0]
    tile = TOKEN_TILE
    return pl.pallas_call(
        _memkv_kernel,
        grid=(rows // tile,),
        in_specs=[pl.BlockSpec((tile, D_MODEL), lambda i: (i, 0)),
                  pl.BlockSpec((D_MODEL, 2 * MEM_WIDTH), lambda i: (0, 0))],
        out_specs=[pl.BlockSpec((tile, MEM_WIDTH), lambda i: (i, 0)),
                   pl.BlockSpec((tile, MEM_WIDTH), lambda i: (i, 0))],
        out_shape=[jax.ShapeDtypeStruct((rows, MEM_WIDTH), BF16),
                   jax.ShapeDtypeStruct((rows, MEM_WIDTH), BF16)],
        compiler_params=pltpu.CompilerParams(dimension_semantics=("arbitrary",),
                                             vmem_limit_bytes=VMEM_LIMIT_BYTES),
        name="memkv",
    )(mem2d, w_bf16)


def _inproj_kernel(x_ref, w_ref, ca_ref, sa_ref, ci_ref, si_ref,
                   q_ref, k_ref, vt_ref, qi_ref, ki_ref, wi_ref, ga_ref, u_ref, gs_ref, qm_ref, gm_ref):
    tile = x_ref.shape[1]
    pos0 = pl.multiple_of(pl.program_id(1) * tile, tile)
    xb = x_ref[0].astype(BF16)

    def proj(group, last=None):
        lo, hi = _GROUP_OFFS[group], _GROUP_OFFS[(group if last is None else last) + 1]
        return jnp.dot(xb, w_ref[:, lo:hi], preferred_element_type=F32)

    ca = ca_ref[pl.ds(pos0, tile), :]
    sa = sa_ref[pl.ds(pos0, tile), :]
    ci = ci_ref[pl.ds(pos0, tile), :]
    si = si_ref[pl.ds(pos0, tile), :]

    def store_transposed(ref, z):
        for qb in range(tile // Q_TILE):
            for t in range(z.shape[1] // LANES):
                zt = z[qb * Q_TILE:(qb + 1) * Q_TILE, t * LANES:(t + 1) * LANES].T
                ref[0, qb, :, t * Q_TILE:(t + 1) * Q_TILE] = zt.astype(ref.dtype)

    store_transposed(q_ref, _rope(proj(_G_Q), ca, sa, HEAD_DIM) * (ATT_SCALE * LOG2_E))
    kv = proj(_G_K, _G_V)
    k_ref[0] = _rope(kv[:, :KV_WIDTH], ca, sa, HEAD_DIM).astype(BF16)
    v = kv[:, KV_WIDTH:]
    for c in range(tile // KEY_CHUNK):
        vt_ref[0, c, :KV_WIDTH, :] = v[c * KEY_CHUNK:(c + 1) * KEY_CHUNK, :].T.astype(BF16)
        vt_ref[0, c, KV_WIDTH:, :] = jnp.ones((BF16_ROWS, KEY_CHUNK), BF16)
    store_transposed(qi_ref, _rope(proj(_G_QI), ci, si, IDX_DIM))
    kiw = proj(_G_KI, _G_W)
    ki_ref[0] = _rope(kiw[:, :LANES], ci, si, IDX_DIM).astype(BF16)
    wi = kiw[:, LANES:] * IDX_SCALE
    for qb in range(tile // Q_TILE):
        wi_ref[0, qb] = wi[qb * Q_TILE:(qb + 1) * Q_TILE, :].T[:IDX_HEADS, :]
    ga_ref[0] = jax.nn.silu(proj(_G_GA))
    u_ref[...] = proj(_G_U)
    gs_ref[0] = jax.nn.silu(proj(_G_GS))
    qm_ref[0] = (proj(_G_QM) * ATT_SCALE).astype(BF16)
    gm_ref[0] = jax.nn.silu(proj(_G_GM))


def _inproj(x, w_bf16, tabs):
    batch, seq, _ = x.shape
    tile = TOKEN_TILE
    widths = _GROUP_WIDTHS
    dtypes = [BF16, BF16, BF16, BF16, BF16, F32, F32, F32, F32, BF16, F32]
    tab_spec = pl.BlockSpec((seq, LANES), lambda b, j: (0, 0))
    out_specs = [pl.BlockSpec((1, tile, w), lambda b, j: (b, j, 0)) for w in widths]
    out_shape = [jax.ShapeDtypeStruct((batch, seq, w), d) for w, d in zip(widths, dtypes)]
    out_specs[_G_V] = pl.BlockSpec((1, tile // KEY_CHUNK, VT_ROWS, KEY_CHUNK), lambda b, j: (b, j, 0, 0))
    out_shape[_G_V] = jax.ShapeDtypeStruct((batch, seq // KEY_CHUNK, VT_ROWS, KEY_CHUNK), BF16)
    for group, rows, dtype in ((_G_Q, LANES, BF16), (_G_QI, LANES, BF16), (_G_W, IDX_HEADS, F32)):
        cols = Q_TILE * max(widths[group] // LANES, 1)
        out_specs[group] = pl.BlockSpec((1, tile // Q_TILE, rows, cols), lambda b, j: (b, j, 0, 0))
        out_shape[group] = jax.ShapeDtypeStruct((batch, seq // Q_TILE, rows, cols), dtype)
    out_specs[_G_U] = pl.BlockSpec((tile, SSM_WIDTH), lambda b, j: (j, b))
    out_shape[_G_U] = jax.ShapeDtypeStruct((seq, batch * SSM_WIDTH), F32)
    return pl.pallas_call(
        _inproj_kernel,
        grid=(batch, seq // tile),
        in_specs=[pl.BlockSpec((1, tile, D_MODEL), lambda b, j: (b, j, 0)),
                  pl.BlockSpec((D_MODEL, IN_COLS), lambda b, j: (0, 0)),
                  tab_spec, tab_spec, tab_spec, tab_spec],
        out_specs=out_specs,
        out_shape=out_shape,
        compiler_params=pltpu.CompilerParams(dimension_semantics=("arbitrary", "arbitrary"),
                                             vmem_limit_bytes=VMEM_LIMIT_BYTES),
        name="inproj",
    )(x, w_bf16, *tabs)


def _key_to_float(key):
    bits = jnp.where(key >= 0, key, key ^ 0x7FFFFFFF)
    val = lax.bitcast_convert_type(bits, F32)
    val = jnp.where(key <= KEY_NEG_INF, -jnp.inf, val)
    return jnp.where(key >= KEY_POS_INF, jnp.inf, val)


def _dsa_kernel(q_ref, qi_ref, wi_ref, k_ref, vt_ref, ki_ref, ga_ref, o_ref,
                score_scr, bias_scr, row_scr, s_scr, acc_scr, og_scr, *, topk):
    i = pl.program_id(1)
    n_chunks = ((i + 1) * Q_TILE + KEY_CHUNK - 1) // KEY_CHUNK
    q0 = i * Q_TILE
    kf = float(topk)
    att_cols = ATT_REP * Q_TILE
    groups = KEY_CHUNK // SUBLANES

    def chunk_start(c):
        return pl.multiple_of(c * KEY_CHUNK, KEY_CHUNK)

    def over_chunk_pairs(body, init):
        def pair(c2, carry):
            return body(2 * c2 + 1, body(2 * c2, carry))
        carry = lax.fori_loop(0, n_chunks // 2, pair, init)
        return lax.cond(n_chunks % 2 == 1, lambda cr: body(n_chunks - 1, cr), lambda cr: cr, carry)

    def fold(v, op):
        return op(v.reshape(groups, SUBLANES, v.shape[-1]), axis=0)

    def only_rows(x, lo, n):
        parts = [jnp.zeros((lo, x.shape[1]), x.dtype)] if lo else []
        parts.append(x[lo:lo + n])
        if lo + n < x.shape[0]:
            parts.append(jnp.zeros((x.shape[0] - lo - n, x.shape[1]), x.dtype))
        return jnp.concatenate(parts, axis=0)

    qi_t = qi_ref[0, 0]
    per_tile = LANES // IDX_DIM

    def head_cols(h):
        t = h // per_tile
        return only_rows(qi_t[:, t * Q_TILE:(t + 1) * Q_TILE], (h % per_tile) * IDX_DIM, IDX_DIM)

    rhs_pairs = [jnp.concatenate([head_cols(2 * p), head_cols(2 * p + 1)], axis=1)
                 for p in range(IDX_HEADS // 2)]
    w_rows = [wi_ref[0, 0, h:h + 1, :] for h in range(IDX_HEADS)]
    q_t = q_ref[0, 0]
    qg_t = [only_rows(q_t, g * HEAD_DIM, HEAD_DIM) for g in range(ATT_KV_HEADS)]
    krow = lax.broadcasted_iota(I32, (KEY_CHUNK, Q_TILE), 0)
    kq = q0 + lax.broadcasted_iota(I32, (KEY_CHUNK, Q_TILE), 1)

    def matmul_chunk(c, stats):
        mx_a, mn_a, ge_a, gt_a, top_a = stats
        ks = chunk_start(c)
        kc = ki_ref[0, pl.ds(ks, KEY_CHUNK), :]
        acc = jnp.zeros((KEY_CHUNK, Q_TILE), F32)
        for p in range(IDX_HEADS // 2):
            s2 = jnp.dot(kc, rhs_pairs[p], preferred_element_type=F32)
            acc = acc + w_rows[2 * p] * jnp.maximum(s2[:, :Q_TILE], 0.0)
            acc = acc + w_rows[2 * p + 1] * jnp.maximum(s2[:, Q_TILE:], 0.0)
        causal = ks + krow <= kq
        sc = jnp.where(causal, acc, -jnp.inf)
        score_scr[pl.ds(ks, KEY_CHUNK), :] = sc
        kk = k_ref[0, pl.ds(ks, KEY_CHUNK), :]
        tops = []
        for g in range(ATT_KV_HEADS):
            s = jnp.dot(kk, qg_t[g], preferred_element_type=F32)
            s_scr[g, pl.ds(ks, KEY_CHUNK), :] = s
            tops.append(jnp.maximum(top_a[g], fold(s, jnp.max)))
        return (jnp.maximum(mx_a, fold(sc, jnp.max)),
                jnp.minimum(mn_a, fold(jnp.where(causal, acc, jnp.inf), jnp.min)),
                ge_a + fold(jnp.where(sc >= 0.0, 1.0, 0.0), jnp.sum),
                gt_a + fold(jnp.where(sc > 0.0, 1.0, 0.0), jnp.sum),
                tuple(tops))

    stat0 = lambda v: jnp.full((SUBLANES, Q_TILE), v, F32)
    top0 = tuple(jnp.full((SUBLANES, att_cols), NEG_BIG, F32) for _ in range(ATT_KV_HEADS))
    mx_a, mn_a, ge_a, gt_a, top_a = over_chunk_pairs(
        matmul_chunk, (stat0(-jnp.inf), stat0(jnp.inf), stat0(0.0), stat0(0.0), top0))
    m_top = [jnp.max(top_a[g], axis=0, keepdims=True) for g in range(ATT_KV_HEADS)]
    s_max = jnp.max(mx_a, axis=0, keepdims=True)
    s_min = jnp.min(mn_a, axis=0, keepdims=True)
    n_ge0 = jnp.sum(ge_a, axis=0, keepdims=True)
    n_gt0 = jnp.sum(gt_a, axis=0, keepdims=True)
    n_causal = (q0 + 1 + lax.broadcasted_iota(I32, (1, Q_TILE), 1)).astype(F32)

    def count(pred):
        def body(c, acc):
            ks = chunk_start(c)
            for part in range(KEY_CHUNK // LANES):
                hit = jnp.where(pred(score_scr[pl.ds(ks + part * LANES, LANES), :]), 1.0, 0.0)
                acc = acc + (hit[:COUNT_ROWS] + hit[COUNT_ROWS:])
            return acc
        acc = over_chunk_pairs(body, jnp.zeros((COUNT_ROWS, Q_TILE), F32))
        rows = COUNT_ROWS
        while rows > SUBLANES:
            rows //= 2
            acc = acc[:rows] + acc[rows:]
        return jnp.sum(acc, axis=0, keepdims=True)

    def count_ge(value):
        vb = jnp.broadcast_to(value, (LANES, Q_TILE))
        return count(lambda sc: sc >= vb)

    zero_thr = (n_gt0 < kf) & (n_ge0 >= kf)
    keep_all = n_causal <= kf
    positive = n_gt0 >= kf
    settled = zero_thr | keep_all
    lo0 = jnp.where(settled, 0.0, jnp.where(positive, 0.0, s_min))
    hi0 = jnp.where(settled, 0.0, jnp.where(positive, s_max, 0.0))
    c_lo0 = jnp.where(positive, n_gt0, n_causal)
    c_hi0 = jnp.where(positive, 0.0, n_ge0)
    thr0 = jnp.where(keep_all, -jnp.inf, 0.0)
    done0 = jnp.where(settled, 1.0, 0.0)

    def search_pass(bisect, rows):
        lo, hi, c_lo, c_hi, thr, done = rows
        if bisect:
            frac = 0.5
        else:
            frac = jnp.clip((kf - c_hi + 0.5) / (c_lo - c_hi + 1.0), SEARCH_CLIP, 1.0 - SEARCH_CLIP)
        x = hi - (hi - lo) * frac
        c = count_ge(x)
        live = done == 0.0
        above = live & (c >= kf)
        below = live & (c < kf)
        hit = live & (c == kf)
        return (jnp.where(above, x, lo), jnp.where(below, x, hi),
                jnp.where(above, c, c_lo), jnp.where(below, c, c_hi),
                jnp.where(hit, x, thr), jnp.where(hit, 1.0, done))

    def search_cond(state):
        return (state[0] < SEARCH_MAX_PASSES) & (state[1] > 0.0)

    def search_body(state):
        it, _, rows = state
        for _ in range(SEARCH_GROUP):
            rows = search_pass(False, rows)
        return it + SEARCH_GROUP, jnp.sum(1.0 - rows[5]), rows

    def first_passes(rows):
        for p in range(SEARCH_FIRST_PASSES):
            rows = search_pass(p < SEARCH_BISECT_PASSES, rows)
        return rows

    rows0 = (lo0, hi0, c_lo0, c_hi0, thr0, done0)
    rows1 = lax.cond(jnp.sum(1.0 - done0) > 0.0, first_passes, lambda rows: rows, rows0)
    state = lax.while_loop(search_cond, search_body,
                           (jnp.int32(SEARCH_FIRST_PASSES), jnp.sum(1.0 - rows1[5]), rows1))
    pending, thr_s, done_s = state[1], state[2][4], state[2][5]
    row_scr[0:1, :] = thr_s
    row_scr[1:2, :] = jnp.where(zero_thr, n_gt0, 0.0)
    row_scr[2:3, :] = jnp.where(zero_thr, n_ge0, kf)

    @pl.when(pending > 0.0)
    def _():
        def count_ge_key(key):
            return count_ge(_key_to_float(key))

        key0 = jnp.where(count_ge_key(jnp.zeros((1, Q_TILE), I32)) >= kf, 0, INT_MIN).astype(I32)

        def bit_step(b, key):
            cand = key | jnp.left_shift(jnp.int32(1), 30 - b)
            return jnp.where(count_ge_key(cand) >= kf, cand, key)

        thr_f = _key_to_float(lax.fori_loop(0, 31, bit_step, key0))
        thr_fb = jnp.broadcast_to(thr_f, (LANES, Q_TILE))
        open_row = done_s == 0.0
        row_scr[0:1, :] = jnp.where(open_row, thr_f, thr_s)
        row_scr[1:2, :] = jnp.where(open_row, count(lambda sc: sc > thr_fb), row_scr[1:2, :])
        row_scr[2:3, :] = jnp.where(open_row, count(lambda sc: sc >= thr_fb), row_scr[2:3, :])

    thr = row_scr[0:1, :]
    thr_c = jnp.broadcast_to(thr, (KEY_CHUNK, Q_TILE))
    has_tie = (row_scr[2:3, :] > kf) & (thr > -jnp.inf)
    any_tie = jnp.max(jnp.where(has_tie, 1.0, 0.0)) > 0.0

    def plain_bias(c, carry):
        ks = chunk_start(c)
        keep = (score_scr[pl.ds(ks, KEY_CHUNK), :] >= thr_c) & (ks + krow <= kq)
        return jnp.where(keep, 0.0, NEG_BIG), carry

    def make_tie_bias():
        need = jnp.broadcast_to(kf - row_scr[1:2, :], (KEY_CHUNK, Q_TILE))
        lower = (lax.broadcasted_iota(I32, (KEY_CHUNK, KEY_CHUNK), 1)
                 <= lax.broadcasted_iota(I32, (KEY_CHUNK, KEY_CHUNK), 0))
        prefix_mat = jnp.where(lower, 1.0, 0.0).astype(BF16)

        def tie_bias(c, seen):
            ks = chunk_start(c)
            sc = score_scr[pl.ds(ks, KEY_CHUNK), :]
            tied = sc == thr_c
            rank = seen + jnp.dot(prefix_mat, jnp.where(tied, 1.0, 0.0).astype(BF16),
                                  preferred_element_type=F32)
            chosen = (sc > thr_c) | (tied & (rank <= need))
            return jnp.where(chosen & (ks + krow <= kq), 0.0, NEG_BIG), rank[KEY_CHUNK - 1:KEY_CHUNK, :]

        return tie_bias

    no_ties_seen = jnp.zeros((1, Q_TILE), F32)

    def pv_pass(bias_of, m, carry0):
        acc_scr[...] = jnp.zeros(acc_scr.shape, F32)

        def pv_chunk(c, carry):
            bias, carry = bias_of(c, carry)
            bias4 = jnp.concatenate([bias] * ATT_REP, axis=1)
            for g in range(ATT_KV_HEADS):
                p = jnp.exp2(s_scr[g, pl.ds(chunk_start(c), KEY_CHUNK), :] + bias4 - m[g])
                acc_scr[g] += jnp.dot(vt_ref[0, c], p.astype(BF16), preferred_element_type=F32)
            return carry

        over_chunk_pairs(pv_chunk, carry0)

    @pl.when(jnp.logical_not(any_tie))
    def _():
        pv_pass(plain_bias, m_top, jnp.int32(0))

    @pl.when(any_tie)
    def _():
        pv_pass(make_tie_bias(), m_top, no_ties_seen)

    d_min = jnp.min(jnp.minimum(acc_scr[0, KV_WIDTH:KV_WIDTH + 1, :], acc_scr[1, KV_WIDTH:KV_WIDTH + 1, :]))

    @pl.when(jnp.logical_not(d_min >= DENOM_FLOOR))
    def _():
        tie_bias = make_tie_bias()

        def mask_chunk(c, carry):
            seen, m_acc = carry
            ks = chunk_start(c)
            bias, seen = tie_bias(c, seen)
            bias_scr[pl.ds(ks, KEY_CHUNK), :] = bias
            bias4 = jnp.concatenate([bias] * ATT_REP, axis=1)
            return seen, tuple(jnp.maximum(m_acc[g], fold(s_scr[g, pl.ds(ks, KEY_CHUNK), :] + bias4, jnp.max))
                               for g in range(ATT_KV_HEADS))

        m_init = tuple(jnp.full((SUBLANES, att_cols), NEG_BIG, F32) for _ in range(ATT_KV_HEADS))
        _, m_acc = lax.fori_loop(0, n_chunks, mask_chunk, (no_ties_seen, m_init))
        m_sel = [jnp.max(m_acc[g], axis=0, keepdims=True) for g in range(ATT_KV_HEADS)]
        pv_pass(lambda c, carry: (bias_scr[pl.ds(chunk_start(c), KEY_CHUNK), :], carry), m_sel, jnp.int32(0))

    for g in range(ATT_KV_HEADS):
        denom = acc_scr[g, KV_WIDTH:KV_WIDTH + 1, :]
        og_scr[g * HEAD_DIM:(g + 1) * HEAD_DIM, :] = acc_scr[g, g * HEAD_DIM:(g + 1) * HEAD_DIM, :] / denom

    for j in range(ATT_REP):
        gated = og_scr[:, j * Q_TILE:(j + 1) * Q_TILE].T * ga_ref[0, :, j * LANES:(j + 1) * LANES]
        o_ref[0, :, j * LANES:(j + 1) * LANES] = gated.astype(o_ref.dtype)


def _dsa(q_t, qi_t, wi_t, k, vt, ki, gate):
    batch, seq, _ = k.shape
    topk = min(TOPK_MAX, seq // 4)
    att_cols = ATT_REP * Q_TILE
    per_q = lambda a: pl.BlockSpec((1, 1) + a.shape[2:], lambda b, i: (b, i, 0, 0))
    per_b = lambda w: pl.BlockSpec((1, seq, w), lambda b, i: (b, 0, 0))
    vt_spec = pl.BlockSpec((1, seq // KEY_CHUNK, VT_ROWS, KEY_CHUNK), lambda b, i: (b, 0, 0, 0))
    tok_spec = pl.BlockSpec((1, Q_TILE, ATT_WIDTH), lambda b, i: (b, i, 0))
    return pl.pallas_call(
        functools.partial(_dsa_kernel, topk=topk),
        grid=(batch, seq // Q_TILE),
        in_specs=[per_q(q_t), per_q(qi_t), per_q(wi_t), per_b(KV_WIDTH), vt_spec, per_b(LANES), tok_spec],
        out_specs=tok_spec,
        out_shape=jax.ShapeDtypeStruct((batch, seq, ATT_WIDTH), BF16),
        scratch_shapes=[pltpu.VMEM((seq, Q_TILE), F32),
                        pltpu.VMEM((seq, Q_TILE), F32),
                        pltpu.VMEM((SUBLANES, Q_TILE), F32),
                        pltpu.VMEM((ATT_KV_HEADS, seq, att_cols), F32),
                        pltpu.VMEM((ATT_KV_HEADS, VT_ROWS, att_cols), F32),
                        pltpu.VMEM((KV_WIDTH, att_cols), F32)],
        compiler_params=pltpu.CompilerParams(dimension_semantics=("arbitrary", "arbitrary"),
                                             vmem_limit_bytes=VMEM_LIMIT_BYTES),
        name="dsa",
    )(q_t, qi_t, wi_t, k, vt, ki, gate)


def _s5_kernel(u_ref, bm_ref, cm_ref, are_ref, aim_ref, dsk_ref, wg_ref, bg_ref, o_ref,
               st_scr, xb_scr, carry_scr):
    tc, batch, width = u_ref.shape
    steps = tc // SCAN_SPLIT
    rows = steps * batch

    @pl.when(pl.program_id(0) == 0)
    def _():
        carry_scr[...] = jnp.zeros(carry_scr.shape, F32)

    u = [u_ref[h * steps:(h + 1) * steps].reshape(rows, width) for h in range(SCAN_SPLIT)]
    for h in range(SCAN_SPLIT):
        st_scr[h] = jnp.dot(u[h].astype(BF16), bm_ref[...], preferred_element_type=F32)

    parts = []
    for part in range(SSM_LANES // SCAN_LANES):
        re_lo = part * SCAN_LANES
        im_lo = SSM_LANES + re_lo
        parts.append((slice(re_lo, re_lo + SCAN_LANES), slice(im_lo, im_lo + SCAN_LANES)))
    state = [(carry_scr[:, re], carry_scr[:, im]) for re, im in parts]

    for h in range(SCAN_SPLIT):
        for p, (re, im) in enumerate(parts):
            a_re = jnp.broadcast_to(are_ref[:, re], (batch, SCAN_LANES))
            a_im = jnp.broadcast_to(aim_ref[:, re], (batch, SCAN_LANES))
            x_re, x_im = state[p]
            for t in range(steps):
                r = slice(t * batch, (t + 1) * batch)
                x_re, x_im = (a_re * x_re - a_im * x_im + st_scr[h, r, re],
                              a_re * x_im + a_im * x_re + st_scr[h, r, im])
                xb_scr[h, r, re] = x_re.astype(BF16)
                xb_scr[h, r, im] = x_im.astype(BF16)
            state[p] = (x_re, x_im)
        y = jnp.dot(xb_scr[h], cm_ref[...], preferred_element_type=F32) + dsk_ref[...] * u[h]
        y = jax.nn.gelu(y)
        gate = jax.nn.sigmoid(jnp.dot(y.astype(BF16), wg_ref[...], preferred_element_type=F32) + bg_ref[...])
        o_ref[h * steps:(h + 1) * steps] = (y * gate).reshape(steps, batch, width)

    for (re, im), (x_re, x_im) in zip(parts, state):
        carry_scr[:, re] = x_re
        carry_scr[:, im] = x_im


def _s5(u_tm, bmat, cmat, a_re, a_im, dskip, w_glu, b_glu):
    seq, batch, width = u_tm.shape
    tc = TIME_CHUNK
    full = lambda shape: pl.BlockSpec(shape, lambda t: tuple(0 for _ in shape))
    return pl.pallas_call(
        _s5_kernel,
        grid=(seq // tc,),
        in_specs=[pl.BlockSpec((tc, batch, width), lambda t: (t, 0, 0)),
                  full(bmat.shape), full(cmat.shape), full(a_re.shape), full(a_im.shape),
                  full(dskip.shape), full(w_glu.shape), full(b_glu.shape)],
        out_specs=pl.BlockSpec((tc, batch, width), lambda t: (t, 0, 0)),
        out_shape=jax.ShapeDtypeStruct((seq, batch, width), F32),
        scratch_shapes=[pltpu.VMEM((SCAN_SPLIT, tc // SCAN_SPLIT * batch, 2 * SSM_LANES), F32),
                        pltpu.VMEM((SCAN_SPLIT, tc // SCAN_SPLIT * batch, 2 * SSM_LANES), BF16),
                        pltpu.VMEM((batch, 2 * SSM_LANES), F32)],
        compiler_params=pltpu.CompilerParams(dimension_semantics=("arbitrary",),
                                             vmem_limit_bytes=VMEM_LIMIT_BYTES),
        name="s5",
    )(u_tm, bmat, cmat, a_re, a_im, dskip, w_glu, b_glu)


def _s5_params(lam_re, lam_im, log_dt, b_re, b_im, c_re, c_im):
    dt = jnp.exp(log_dt.astype(F32))[:, None]
    mag = jnp.exp(lam_re.astype(F32) * dt)
    ang = lam_im.astype(F32) * dt
    lb_re, lb_im = mag * jnp.cos(ang), mag * jnp.sin(ang)
    den = lam_re * lam_re + lam_im * lam_im
    k_re = ((lb_re - 1.0) * lam_re + lb_im * lam_im) / den
    k_im = (lb_im * lam_re - (lb_re - 1.0) * lam_im) / den
    bb_re = k_re[:, :, None] * b_re - k_im[:, :, None] * b_im
    bb_im = k_re[:, :, None] * b_im + k_im[:, :, None] * b_re
    same_group = (np.arange(SSM_WIDTH)[:, None] // SSM_GROUP) == (np.arange(SSM_LANES)[None, :] // SSM_STATE)

    def drive(bb):
        per_group = jnp.swapaxes(bb, 1, 2).reshape(SSM_WIDTH, SSM_STATE)
        return jnp.where(same_group, jnp.tile(per_group, (1, SSM_GROUPS)), 0.0)

    def read(cc):
        per_group = jnp.swapaxes(cc, 1, 2).reshape(SSM_LANES, SSM_GROUP)
        return jnp.where(same_group.T, jnp.tile(per_group, (1, SSM_GROUPS)), 0.0)

    bmat = jnp.concatenate([drive(bb_re), drive(bb_im)], axis=1)
    cmat = jnp.concatenate([read(c_re.astype(F32)), read(-c_im.astype(F32))], axis=0)
    return (bmat.astype(BF16), cmat.astype(BF16),
            lb_re.reshape(1, SSM_LANES), lb_im.reshape(1, SSM_LANES))


def _combine_kernel(x_ref, ca_ref, os_ref, gs_ref, qm_ref, gm_ref, mk_ref, mv_ref,
                    wo_ref, lg_ref, lb_ref, out_ref):
    tile = x_ref.shape[1]
    nt_dims = (((1,), (1,)), ((), ()))
    lane = lax.broadcasted_iota(I32, (tile, LANES), 1)
    low = lane < HEAD_DIM

    qm = qm_ref[0].astype(F32)
    mem_tiles = []
    for t in range(MEM_WIDTH // LANES):
        qt = qm[:, t * LANES:(t + 1) * LANES]
        mk = mk_ref[0, :, t * LANES:(t + 1) * LANES]
        mv = mv_ref[0, :, t * LANES:(t + 1) * LANES]
        halves = []
        for first in (True, False):
            qh = jnp.where(low if first else ~low, qt, 0.0).astype(BF16)
            s = lax.dot_general(qh, mk, nt_dims, preferred_element_type=F32)
            p = jnp.exp(s - jnp.max(s, axis=1, keepdims=True))
            o = jnp.dot(p.astype(BF16), mv, preferred_element_type=F32)
            halves.append(o / jnp.sum(p, axis=1, keepdims=True))
        mem_tiles.append(jnp.where(low, halves[0], halves[1]))
    o_mem = jnp.concatenate(mem_tiles, axis=1)

    c_att = ca_ref[0]
    c_ssm = (os_ref[...] * gs_ref[0]).astype(BF16)
    c_mem = (o_mem * gm_ref[0]).astype(BF16)
    a1, a2 = ATT_WIDTH, ATT_WIDTH + SSM_WIDTH
    sub = (jnp.dot(c_att, wo_ref[:a1, :], preferred_element_type=F32)
           + jnp.dot(c_ssm, wo_ref[a1:a2, :], preferred_element_type=F32)
           + jnp.dot(c_mem, wo_ref[a2:, :], preferred_element_type=F32))
    h = DN_ALPHA * x_ref[0] + sub
    mu = jnp.mean(h, axis=1, keepdims=True)
    d = h - mu
    var = jnp.mean(d * d, axis=1, keepdims=True)
    out_ref[0] = d * lax.rsqrt(var + LN_EPS) * lg_ref[...] + lb_ref[...]


def _combine(x, c_att, o_ssm_tm, g_ssm, qm, g_mem, mk, mv, wo_bf16, ln_g, ln_b):
    batch, seq, _ = x.shape
    tile = TOKEN_TILE
    tok = lambda w: pl.BlockSpec((1, tile, w), lambda b, j: (b, j, 0))
    time_major = pl.BlockSpec((tile, SSM_WIDTH), lambda b, j: (j, b))
    per_b = pl.BlockSpec((1, N_MEM, MEM_WIDTH), lambda b, j: (b, 0, 0))
    const = lambda shape: pl.BlockSpec(shape, lambda b, j: tuple(0 for _ in shape))
    return pl.pallas_call(
        _combine_kernel,
        grid=(batch, seq // tile),
        in_specs=[tok(D_MODEL), tok(ATT_WIDTH), time_major, tok(SSM_WIDTH),
                  tok(MEM_WIDTH), tok(MEM_WIDTH), per_b, per_b,
                  const((D_MODEL, D_MODEL)), const((1, D_MODEL)), const((1, D_MODEL))],
        out_specs=tok(D_MODEL),
        out_shape=jax.ShapeDtypeStruct((batch, seq, D_MODEL), x.dtype),
        compiler_params=pltpu.CompilerParams(dimension_semantics=("arbitrary", "arbitrary"),
                                             vmem_limit_bytes=VMEM_LIMIT_BYTES),
        name="combine",
    )(x, c_att, o_ssm_tm, g_ssm, qm, g_mem, mk, mv, wo_bf16, ln_g, ln_b)


def _reordered_w_in(w_in):
    offs = np.cumsum([0, ATT_WIDTH, KV_WIDTH, KV_WIDTH, IDX_WIDTH, IDX_DIM, IDX_HEADS,
                      ATT_WIDTH, SSM_WIDTH, SSM_WIDTH, MEM_WIDTH, MEM_WIDTH])
    part = lambda n: w_in[:, int(offs[n]):int(offs[n + 1])]
    k_idx = part(4)
    w_idx = part(5)
    zeros = jnp.zeros((D_MODEL, LANES - IDX_HEADS), w_in.dtype)
    cols = [_pair_heads(part(0), 1), part(1), part(2), part(3),
            jnp.tile(k_idx, (1, LANES // IDX_DIM)),
            jnp.concatenate([w_idx, zeros], axis=1),
            _pair_heads(part(6), 1), part(7), part(8), part(9), part(10)]
    return jnp.concatenate(cols, axis=1).astype(BF16)


def kernel(x, mem, w_in, w_mem_kv, lam_re, lam_im, log_dt, b_re, b_im, c_re, c_im, d_skip, w_glu, b_glu,
           w_out, ln_g, ln_b):
    batch, seq, _ = x.shape
    assert seq % TOKEN_TILE == 0 and seq % Q_TILE == 0 and seq % TIME_CHUNK == 0

    w_all = _reordered_w_in(w_in)
    wo = jnp.concatenate([_pair_heads(w_out[:ATT_WIDTH], 0), w_out[ATT_WIDTH:]], axis=0).astype(BF16)
    bmat, cmat, a_re, a_im = _s5_params(lam_re, lam_im, log_dt, b_re, b_im, c_re, c_im)
    tabs = _rope_tables(seq, HEAD_DIM) + _rope_tables(seq, IDX_DIM)

    mk, mv = _memkv(mem.reshape(batch * N_MEM, D_MODEL), w_mem_kv.astype(BF16))
    mk = mk.reshape(batch, N_MEM, MEM_WIDTH)
    mv = mv.reshape(batch, N_MEM, MEM_WIDTH)

    q_t, k, vt, qi_t, ki, wi_t, g_att, u_tm, g_ssm, qm, g_mem = _inproj(x, w_all, tabs)
    c_att = _dsa(q_t, qi_t, wi_t, k, vt, ki, g_att)

    o_ssm_tm = _s5(u_tm.reshape(seq, batch, SSM_WIDTH), bmat, cmat, a_re, a_im,
                   d_skip.reshape(1, SSM_WIDTH).astype(F32), w_glu.astype(BF16),
                   b_glu.reshape(1, SSM_WIDTH).astype(F32))
    o_ssm = o_ssm_tm.reshape(seq, batch * SSM_WIDTH)

    return _combine(x, c_att, o_ssm, g_ssm, qm, g_mem, mk, mv, wo,
                    ln_g.reshape(1, D_MODEL).astype(F32), ln_b.reshape(1, D_MODEL).astype(F32))
```

```python
import functools
import math

import jax
import jax.numpy as jnp
import numpy as np
from jax import lax
from jax.experimental import pallas as pl
from jax.experimental.pallas import tpu as pltpu

F32 = jnp.float32
BF16 = jnp.bfloat16
I32 = jnp.int32

D_MODEL = 1024
N_MEM = 256
HEAD_DIM = 64
ATT_HEADS = 8
ATT_KV_HEADS = 2
ATT_REP = ATT_HEADS // ATT_KV_HEADS
ATT_WIDTH = ATT_HEADS * HEAD_DIM
KV_WIDTH = ATT_KV_HEADS * HEAD_DIM
IDX_HEADS = 8
IDX_DIM = 32
IDX_WIDTH = IDX_HEADS * IDX_DIM
TOPK_MAX = 256
SSM_WIDTH = D_MODEL // 4
SSM_GROUP = 16
SSM_GROUPS = SSM_WIDTH // SSM_GROUP
SSM_STATE = 64
SSM_LANES = SSM_GROUPS * SSM_STATE
MEM_HEADS = 4
MEM_WIDTH = MEM_HEADS * HEAD_DIM
ROPE_THETA = 500000.0
ROPE_FRAC = 4
LN_EPS = 1e-5
DEPTH = 1
DN_ALPHA = (2.0 * DEPTH) ** 0.25
ATT_SCALE = HEAD_DIM ** -0.5
IDX_SCALE = IDX_HEADS ** -0.5 * IDX_DIM ** -0.5
LOG2_E = math.log2(math.e)

LANES = 128
SUBLANES = 8
BF16_ROWS = 16
VMEM_LIMIT_BYTES = 48 * 1024 * 1024
TOKEN_TILE = 512
Q_TILE = 256
KEY_CHUNK = 256
COUNT_ROWS = 64
VT_ROWS = KV_WIDTH + BF16_ROWS
TIME_CHUNK = 128
SCAN_SPLIT = 2
SCAN_LANES = 256
NEG_BIG = -1e30
DENOM_FLOOR = 2.0 ** -100
SEARCH_MAX_PASSES = 24
SEARCH_FIRST_PASSES = 13
SEARCH_GROUP = 2
SEARCH_BISECT_PASSES = 2
SEARCH_CLIP = 0.1

INT_MIN = -2147483648
KEY_POS_INF = 0x7F800000
KEY_NEG_INF = INT_MIN + 0x7FFFFF

_G_Q, _G_K, _G_V, _G_QI, _G_KI, _G_W, _G_GA, _G_U, _G_GS, _G_QM, _G_GM = range(11)
_GROUP_WIDTHS = [ATT_WIDTH, KV_WIDTH, KV_WIDTH, IDX_WIDTH, LANES, LANES, ATT_WIDTH,
                 SSM_WIDTH, SSM_WIDTH, MEM_WIDTH, MEM_WIDTH]
_GROUP_OFFS = [int(v) for v in np.cumsum([0] + _GROUP_WIDTHS)]
IN_COLS = _GROUP_OFFS[-1]


def _pair_heads(w, axis):
    shape = w.shape
    split = shape[:axis] + (ATT_KV_HEADS, ATT_REP, HEAD_DIM) + shape[axis + 1:]
    return jnp.swapaxes(w.reshape(split), axis, axis + 1).reshape(shape)


def _rope_tables(seq, period, dtype=F32):
    r = period // ROPE_FRAC
    half = r // 2
    inv = ROPE_THETA ** (-jnp.arange(0, half, dtype=F32) * 2.0 / r)
    ang = jnp.arange(seq).astype(F32)[:, None] * inv[None, :]
    cos, sin = jnp.cos(ang), jnp.sin(ang)
    ones = jnp.ones((seq, period - r), F32)
    zeros = jnp.zeros((seq, period - r), F32)
    c = jnp.concatenate([cos, cos, ones], axis=1)
    s = jnp.concatenate([-sin, sin, zeros], axis=1)
    reps = LANES // period
    return jnp.tile(c, (1, reps)).astype(dtype), jnp.tile(s, (1, reps)).astype(dtype)


def _rope(z, cos_t, sin_t, period):
    width = z.shape[1]
    half = period // ROPE_FRAC // 2
    reps = width // LANES
    if reps > 1:
        cos_t = jnp.concatenate([cos_t] * reps, axis=1)
        sin_t = jnp.concatenate([sin_t] * reps, axis=1)
    lane = lax.broadcasted_iota(I32, z.shape, 1)
    first = (lane & (period - 1)) < half
    up = pltpu.roll(z, half, 1)
    down = pltpu.roll(z, width - half, 1)
    return z * cos_t + jnp.where(first, down, up) * sin_t


def _inproj_kernel(x_ref, w_ref, ca_ref, sa_ref, ci_ref, si_ref,
                   q_ref, k_ref, vt_ref, qi_ref, ki_ref, wi_ref, ga_ref, u_ref, gs_ref, qm_ref, gm_ref):
    tile = x_ref.shape[1]
    pos0 = pl.multiple_of(pl.program_id(1) * tile, tile)
    xb = x_ref[0].astype(BF16)

    def proj(group, last=None):
        lo, hi = _GROUP_OFFS[group], _GROUP_OFFS[(group if last is None else last) + 1]
        return jnp.dot(xb, w_ref[:, lo:hi], preferred_element_type=F32)

    ca = ca_ref[pl.ds(pos0, tile), :]
    sa = sa_ref[pl.ds(pos0, tile), :]
    ci = ci_ref[pl.ds(pos0, tile), :]
    si = si_ref[pl.ds(pos0, tile), :]

    def store_transposed(ref, z):
        for qb in range(tile // Q_TILE):
            for t in range(z.shape[1] // LANES):
                zt = z[qb * Q_TILE:(qb + 1) * Q_TILE, t * LANES:(t + 1) * LANES].T
                ref[0, qb, :, t * Q_TILE:(t + 1) * Q_TILE] = zt.astype(ref.dtype)

    store_transposed(q_ref, _rope(proj(_G_Q), ca, sa, HEAD_DIM) * (ATT_SCALE * LOG2_E))
    kv = proj(_G_K, _G_V)
    k_ref[0] = _rope(kv[:, :KV_WIDTH], ca, sa, HEAD_DIM).astype(BF16)
    v = kv[:, KV_WIDTH:]
    for c in range(tile // KEY_CHUNK):
        vt_ref[0, c, :KV_WIDTH, :] = v[c * KEY_CHUNK:(c + 1) * KEY_CHUNK, :].T.astype(BF16)
        vt_ref[0, c, KV_WIDTH:, :] = jnp.ones((BF16_ROWS, KEY_CHUNK), BF16)
    store_transposed(qi_ref, _rope(proj(_G_QI), ci, si, IDX_DIM))
    kiw = proj(_G_KI, _G_W)
    ki_ref[0] = _rope(kiw[:, :LANES], ci, si, IDX_DIM).astype(BF16)
    wi = kiw[:, LANES:] * IDX_SCALE
    for qb in range(tile // Q_TILE):
        wi_ref[0, qb] = wi[qb * Q_TILE:(qb + 1) * Q_TILE, :].T[:IDX_HEADS, :]
    ga_ref[0] = jax.nn.silu(proj(_G_GA))
    u_ref[...] = proj(_G_U)
    gs_ref[0] = jax.nn.silu(proj(_G_GS))
    qm_ref[0] = (proj(_G_QM) * ATT_SCALE).astype(BF16)
    gm_ref[0] = jax.nn.silu(proj(_G_GM))


def _inproj(x, w_bf16, tabs):
    batch, seq, _ = x.shape
    tile = TOKEN_TILE
    widths = _GROUP_WIDTHS
    dtypes = [BF16, BF16, BF16, BF16, BF16, F32, F32, F32, F32, BF16, F32]
    tab_spec = pl.BlockSpec((seq, LANES), lambda b, j: (0, 0))
    out_specs = [pl.BlockSpec((1, tile, w), lambda b, j: (b, j, 0)) for w in widths]
    out_shape = [jax.ShapeDtypeStruct((batch, seq, w), d) for w, d in zip(widths, dtypes)]
    out_specs[_G_V] = pl.BlockSpec((1, tile // KEY_CHUNK, VT_ROWS, KEY_CHUNK), lambda b, j: (b, j, 0, 0))
    out_shape[_G_V] = jax.ShapeDtypeStruct((batch, seq // KEY_CHUNK, VT_ROWS, KEY_CHUNK), BF16)
    for group, rows, dtype in ((_G_Q, LANES, BF16), (_G_QI, LANES, BF16), (_G_W, IDX_HEADS, F32)):
        cols = Q_TILE * max(widths[group] // LANES, 1)
        out_specs[group] = pl.BlockSpec((1, tile // Q_TILE, rows, cols), lambda b, j: (b, j, 0, 0))
        out_shape[group] = jax.ShapeDtypeStruct((batch, seq // Q_TILE, rows, cols), dtype)
    out_specs[_G_U] = pl.BlockSpec((tile, SSM_WIDTH), lambda b, j: (j, b))
    out_shape[_G_U] = jax.ShapeDtypeStruct((seq, batch * SSM_WIDTH), F32)
    return pl.pallas_call(
        _inproj_kernel,
        grid=(batch, seq // tile),
        in_specs=[pl.BlockSpec((1, tile, D_MODEL), lambda b, j: (b, j, 0)),
                  pl.BlockSpec((D_MODEL, IN_COLS), lambda b, j: (0, 0)),
                  tab_spec, tab_spec, tab_spec, tab_spec],
        out_specs=out_specs,
        out_shape=out_shape,
        compiler_params=pltpu.CompilerParams(dimension_semantics=("arbitrary", "arbitrary"),
                                             vmem_limit_bytes=VMEM_LIMIT_BYTES),
        name="inproj",
    )(x, w_bf16, *tabs)


def _key_to_float(key):
    bits = jnp.where(key >= 0, key, key ^ 0x7FFFFFFF)
    val = lax.bitcast_convert_type(bits, F32)
    val = jnp.where(key <= KEY_NEG_INF, -jnp.inf, val)
    return jnp.where(key >= KEY_POS_INF, jnp.inf, val)


def _dsa_kernel(q_ref, qi_ref, wi_ref, k_ref, vt_ref, ki_ref, ga_ref, o_ref,
                score_scr, bias_scr, row_scr, s_scr, acc_scr, og_scr, *, topk):
    i = pl.program_id(1)
    n_chunks = ((i + 1) * Q_TILE + KEY_CHUNK - 1) // KEY_CHUNK
    q0 = i * Q_TILE
    kf = float(topk)
    att_cols = ATT_REP * Q_TILE
    groups = KEY_CHUNK // SUBLANES

    def chunk_start(c):
        return pl.multiple_of(c * KEY_CHUNK, KEY_CHUNK)

    def over_chunk_pairs(body, init):
        def pair(c2, carry):
            return body(2 * c2 + 1, body(2 * c2, carry))
        carry = lax.fori_loop(0, n_chunks // 2, pair, init)
        return lax.cond(n_chunks % 2 == 1, lambda cr: body(n_chunks - 1, cr), lambda cr: cr, carry)

    def fold(v, op):
        return op(v.reshape(groups, SUBLANES, v.shape[-1]), axis=0)

    def only_rows(x, lo, n):
        parts = [jnp.zeros((lo, x.shape[1]), x.dtype)] if lo else []
        parts.append(x[lo:lo + n])
        if lo + n < x.shape[0]:
            parts.append(jnp.zeros((x.shape[0] - lo - n, x.shape[1]), x.dtype))
        return jnp.concatenate(parts, axis=0)

    qi_t = qi_ref[0, 0]
    per_tile = LANES // IDX_DIM

    def head_cols(h):
        t = h // per_tile
        return only_rows(qi_t[:, t * Q_TILE:(t + 1) * Q_TILE], (h % per_tile) * IDX_DIM, IDX_DIM)

    rhs_pairs = [jnp.concatenate([head_cols(2 * p), head_cols(2 * p + 1)], axis=1)
                 for p in range(IDX_HEADS // 2)]
    w_rows = [wi_ref[0, 0, h:h + 1, :] for h in range(IDX_HEADS)]
    q_t = q_ref[0, 0]
    qg_t = [only_rows(q_t, g * HEAD_DIM, HEAD_DIM) for g in range(ATT_KV_HEADS)]
    krow = lax.broadcasted_iota(I32, (KEY_CHUNK, Q_TILE), 0)
    kq = q0 + lax.broadcasted_iota(I32, (KEY_CHUNK, Q_TILE), 1)

    def matmul_chunk(c, stats):
        mx_a, mn_a, ge_a, gt_a, top_a = stats
        ks = chunk_start(c)
        kc = ki_ref[0, pl.ds(ks, KEY_CHUNK), :]
        acc = jnp.zeros((KEY_CHUNK, Q_TILE), F32)
        for p in range(IDX_HEADS // 2):
            s2 = jnp.dot(kc, rhs_pairs[p], preferred_element_type=F32)
            acc = acc + w_rows[2 * p] * jnp.maximum(s2[:, :Q_TILE], 0.0)
            acc = acc + w_rows[2 * p + 1] * jnp.maximum(s2[:, Q_TILE:], 0.0)
        causal = ks + krow <= kq
        sc = jnp.where(causal, acc, -jnp.inf)
        score_scr[pl.ds(ks, KEY_CHUNK), :] = sc
        kk = k_ref[0, pl.ds(ks, KEY_CHUNK), :]
        tops = []
        for g in range(ATT_KV_HEADS):
            s = jnp.dot(kk, qg_t[g], preferred_element_type=F32)
            s_scr[g, pl.ds(ks, KEY_CHUNK), :] = s
            tops.append(jnp.maximum(top_a[g], fold(s, jnp.max)))
        return (jnp.maximum(mx_a, fold(sc, jnp.max)),
                jnp.minimum(mn_a, fold(jnp.where(causal, acc, jnp.inf), jnp.min)),
                ge_a + fold(jnp.where(sc >= 0.0, 1.0, 0.0), jnp.sum),
                gt_a + fold(jnp.where(sc > 0.0, 1.0, 0.0), jnp.sum),
                tuple(tops))

    stat0 = lambda v: jnp.full((SUBLANES, Q_TILE), v, F32)
    top0 = tuple(jnp.full((SUBLANES, att_cols), NEG_BIG, F32) for _ in range(ATT_KV_HEADS))
    mx_a, mn_a, ge_a, gt_a, top_a = over_chunk_pairs(
        matmul_chunk, (stat0(-jnp.inf), stat0(jnp.inf), stat0(0.0), stat0(0.0), top0))
    m_top = [jnp.max(top_a[g], axis=0, keepdims=True) for g in range(ATT_KV_HEADS)]
    s_max = jnp.max(mx_a, axis=0, keepdims=True)
    s_min = jnp.min(mn_a, axis=0, keepdims=True)
    n_ge0 = jnp.sum(ge_a, axis=0, keepdims=True)
    n_gt0 = jnp.sum(gt_a, axis=0, keepdims=True)
    n_causal = (q0 + 1 + lax.broadcasted_iota(I32, (1, Q_TILE), 1)).astype(F32)

    def count(pred):
        def body(c, acc):
            ks = chunk_start(c)
            for part in range(KEY_CHUNK // LANES):
                hit = jnp.where(pred(score_scr[pl.ds(ks + part * LANES, LANES), :]), 1.0, 0.0)
                acc = acc + (hit[:COUNT_ROWS] + hit[COUNT_ROWS:])
            return acc
        acc = over_chunk_pairs(body, jnp.zeros((COUNT_ROWS, Q_TILE), F32))
        rows = COUNT_ROWS
        while rows > SUBLANES:
            rows //= 2
            acc = acc[:rows] + acc[rows:]
        return jnp.sum(acc, axis=0, keepdims=True)

    def count_ge(value):
        vb = jnp.broadcast_to(value, (LANES, Q_TILE))
        return count(lambda sc: sc >= vb)

    zero_thr = (n_gt0 < kf) & (n_ge0 >= kf)
    keep_all = n_causal <= kf
    positive = n_gt0 >= kf
    settled = zero_thr | keep_all
    lo0 = jnp.where(settled, 0.0, jnp.where(positive, 0.0, s_min))
    hi0 = jnp.where(settled, 0.0, jnp.where(positive, s_max, 0.0))
    c_lo0 = jnp.where(positive, n_gt0, n_causal)
    c_hi0 = jnp.where(positive, 0.0, n_ge0)
    thr0 = jnp.where(keep_all, -jnp.inf, 0.0)
    done0 = jnp.where(settled, 1.0, 0.0)

    def search_pass(bisect, rows):
        lo, hi, c_lo, c_hi, thr, done = rows
        if bisect:
            frac = 0.5
        else:
            frac = jnp.clip((kf - c_hi + 0.5) / (c_lo - c_hi + 1.0), SEARCH_CLIP, 1.0 - SEARCH_CLIP)
        x = hi - (hi - lo) * frac
        c = count_ge(x)
        live = done == 0.0
        above = live & (c >= kf)
        below = live & (c < kf)
        hit = live & (c == kf)
        return (jnp.where(above, x, lo), jnp.where(below, x, hi),
                jnp.where(above, c, c_lo), jnp.where(below, c, c_hi),
                jnp.where(hit, x, thr), jnp.where(hit, 1.0, done))

    def search_cond(state):
        return (state[0] < SEARCH_MAX_PASSES) & (state[1] > 0.0)

    def search_body(state):
        it, _, rows = state
        for _ in range(SEARCH_GROUP):
            rows = search_pass(False, rows)
        return it + SEARCH_GROUP, jnp.sum(1.0 - rows[5]), rows

    def first_passes(rows):
        for p in range(SEARCH_FIRST_PASSES):
            rows = search_pass(p < SEARCH_BISECT_PASSES, rows)
        return rows

    rows0 = (lo0, hi0, c_lo0, c_hi0, thr0, done0)
    rows1 = lax.cond(jnp.sum(1.0 - done0) > 0.0, first_passes, lambda rows: rows, rows0)
    state = lax.while_loop(search_cond, search_body,
                           (jnp.int32(SEARCH_FIRST_PASSES), jnp.sum(1.0 - rows1[5]), rows1))
    pending, thr_s, done_s = state[1], state[2][4], state[2][5]
    row_scr[0:1, :] = thr_s
    row_scr[1:2, :] = jnp.where(zero_thr, n_gt0, 0.0)
    row_scr[2:3, :] = jnp.where(zero_thr, n_ge0, kf)

    @pl.when(pending > 0.0)
    def _():
        def count_ge_key(key):
            return count_ge(_key_to_float(key))

        key0 = jnp.where(count_ge_key(jnp.zeros((1, Q_TILE), I32)) >= kf, 0, INT_MIN).astype(I32)

        def bit_step(b, key):
            cand = key | jnp.left_shift(jnp.int32(1), 30 - b)
            return jnp.where(count_ge_key(cand) >= kf, cand, key)

        thr_f = _key_to_float(lax.fori_loop(0, 31, bit_step, key0))
        thr_fb = jnp.broadcast_to(thr_f, (LANES, Q_TILE))
        open_row = done_s == 0.0
        row_scr[0:1, :] = jnp.where(open_row, thr_f, thr_s)
        row_scr[1:2, :] = jnp.where(open_row, count(lambda sc: sc > thr_fb), row_scr[1:2, :])
        row_scr[2:3, :] = jnp.where(open_row, count(lambda sc: sc >= thr_fb), row_scr[2:3, :])

    thr = row_scr[0:1, :]
    thr_c = jnp.broadcast_to(thr, (KEY_CHUNK, Q_TILE))
    has_tie = (row_scr[2:3, :] > kf) & (thr > -jnp.inf)
    any_tie = jnp.max(jnp.where(has_tie, 1.0, 0.0)) > 0.0

    def plain_bias(c, carry):
        ks = chunk_start(c)
        keep = (score_scr[pl.ds(ks, KEY_CHUNK), :] >= thr_c) & (ks + krow <= kq)
        return jnp.where(keep, 0.0, NEG_BIG), carry

    def make_tie_bias():
        need = jnp.broadcast_to(kf - row_scr[1:2, :], (KEY_CHUNK, Q_TILE))
        lower = (lax.broadcasted_iota(I32, (KEY_CHUNK, KEY_CHUNK), 1)
                 <= lax.broadcasted_iota(I32, (KEY_CHUNK, KEY_CHUNK), 0))
        prefix_mat = jnp.where(lower, 1.0, 0.0).astype(BF16)

        def tie_bias(c, seen):
            ks = chunk_start(c)
            sc = score_scr[pl.ds(ks, KEY_CHUNK), :]
            tied = sc == thr_c
            rank = seen + jnp.dot(prefix_mat, jnp.where(tied, 1.0, 0.0).astype(BF16),
                                  preferred_element_type=F32)
            chosen = (sc > thr_c) | (tied & (rank <= need))
            return jnp.where(chosen & (ks + krow <= kq), 0.0, NEG_BIG), rank[KEY_CHUNK - 1:KEY_CHUNK, :]

        return tie_bias

    no_ties_seen = jnp.zeros((1, Q_TILE), F32)

    def pv_pass(bias_of, m, carry0):
        acc_scr[...] = jnp.zeros(acc_scr.shape, F32)

        def pv_chunk(c, carry):
            bias, carry = bias_of(c, carry)
            bias4 = jnp.concatenate([bias] * ATT_REP, axis=1)
            for g in range(ATT_KV_HEADS):
                p = jnp.exp2(s_scr[g, pl.ds(chunk_start(c), KEY_CHUNK), :] + bias4 - m[g])
                acc_scr[g] += jnp.dot(vt_ref[0, c], p.astype(BF16), preferred_element_type=F32)
            return carry

        over_chunk_pairs(pv_chunk, carry0)

    @pl.when(jnp.logical_not(any_tie))
    def _():
        pv_pass(plain_bias, m_top, jnp.int32(0))

    @pl.when(any_tie)
    def _():
        pv_pass(make_tie_bias(), m_top, no_ties_seen)

    d_min = jnp.min(jnp.minimum(acc_scr[0, KV_WIDTH:KV_WIDTH + 1, :], acc_scr[1, KV_WIDTH:KV_WIDTH + 1, :]))

    @pl.when(jnp.logical_not(d_min >= DENOM_FLOOR))
    def _():
        tie_bias = make_tie_bias()

        def mask_chunk(c, carry):
            seen, m_acc = carry
            ks = chunk_start(c)
            bias, seen = tie_bias(c, seen)
            bias_scr[pl.ds(ks, KEY_CHUNK), :] = bias
            bias4 = jnp.concatenate([bias] * ATT_REP, axis=1)
            return seen, tuple(jnp.maximum(m_acc[g], fold(s_scr[g, pl.ds(ks, KEY_CHUNK), :] + bias4, jnp.max))
                               for g in range(ATT_KV_HEADS))

        m_init = tuple(jnp.full((SUBLANES, att_cols), NEG_BIG, F32) for _ in range(ATT_KV_HEADS))
        _, m_acc = lax.fori_loop(0, n_chunks, mask_chunk, (no_ties_seen, m_init))
        m_sel = [jnp.max(m_acc[g], axis=0, keepdims=True) for g in range(ATT_KV_HEADS)]
        pv_pass(lambda c, carry: (bias_scr[pl.ds(chunk_start(c), KEY_CHUNK), :], carry), m_sel, jnp.int32(0))

    for g in range(ATT_KV_HEADS):
        denom = acc_scr[g, KV_WIDTH:KV_WIDTH + 1, :]
        og_scr[g * HEAD_DIM:(g + 1) * HEAD_DIM, :] = acc_scr[g, g * HEAD_DIM:(g + 1) * HEAD_DIM, :] / denom

    for j in range(ATT_REP):
        gated = og_scr[:, j * Q_TILE:(j + 1) * Q_TILE].T * ga_ref[0, :, j * LANES:(j + 1) * LANES]
        o_ref[0, :, j * LANES:(j + 1) * LANES] = gated.astype(o_ref.dtype)


def _dsa(q_t, qi_t, wi_t, k, vt, ki, gate):
    batch, seq, _ = k.shape
    topk = min(TOPK_MAX, seq // 4)
    att_cols = ATT_REP * Q_TILE
    per_q = lambda a: pl.BlockSpec((1, 1) + a.shape[2:], lambda b, i: (b, i, 0, 0))
    per_b = lambda w: pl.BlockSpec((1, seq, w), lambda b, i: (b, 0, 0))
    vt_spec = pl.BlockSpec((1, seq // KEY_CHUNK, VT_ROWS, KEY_CHUNK), lambda b, i: (b, 0, 0, 0))
    tok_spec = pl.BlockSpec((1, Q_TILE, ATT_WIDTH), lambda b, i: (b, i, 0))
    return pl.pallas_call(
        functools.partial(_dsa_kernel, topk=topk),
        grid=(batch, seq // Q_TILE),
        in_specs=[per_q(q_t), per_q(qi_t), per_q(wi_t), per_b(KV_WIDTH), vt_spec, per_b(LANES), tok_spec],
        out_specs=tok_spec,
        out_shape=jax.ShapeDtypeStruct((batch, seq, ATT_WIDTH), BF16),
        scratch_shapes=[pltpu.VMEM((seq, Q_TILE), F32),
                        pltpu.VMEM((seq, Q_TILE), F32),
                        pltpu.VMEM((SUBLANES, Q_TILE), F32),
                        pltpu.VMEM((ATT_KV_HEADS, seq, att_cols), F32),
                        pltpu.VMEM((ATT_KV_HEADS, VT_ROWS, att_cols), F32),
                        pltpu.VMEM((KV_WIDTH, att_cols), F32)],
        compiler_params=pltpu.CompilerParams(dimension_semantics=("arbitrary", "arbitrary"),
                                             vmem_limit_bytes=VMEM_LIMIT_BYTES),
        name="dsa",
    )(q_t, qi_t, wi_t, k, vt, ki, gate)


def _s5_kernel(u_ref, bm_ref, cm_ref, are_ref, aim_ref, dsk_ref, wg_ref, bg_ref, o_ref,
               st_scr, xb_scr, carry_scr):
    tc, batch, width = u_ref.shape
    steps = tc // SCAN_SPLIT
    rows = steps * batch

    @pl.when(pl.program_id(0) == 0)
    def _():
        carry_scr[...] = jnp.zeros(carry_scr.shape, F32)

    u = [u_ref[h * steps:(h + 1) * steps].reshape(rows, width) for h in range(SCAN_SPLIT)]
    for h in range(SCAN_SPLIT):
        st_scr[h] = jnp.dot(u[h].astype(BF16), bm_ref[...], preferred_element_type=F32)

    parts = []
    for part in range(SSM_LANES // SCAN_LANES):
        re_lo = part * SCAN_LANES
        im_lo = SSM_LANES + re_lo
        parts.append((slice(re_lo, re_lo + SCAN_LANES), slice(im_lo, im_lo + SCAN_LANES)))
    state = [(carry_scr[:, re], carry_scr[:, im]) for re, im in parts]

    for h in range(SCAN_SPLIT):
        for p, (re, im) in enumerate(parts):
            a_re = jnp.broadcast_to(are_ref[:, re], (batch, SCAN_LANES))
            a_im = jnp.broadcast_to(aim_ref[:, re], (batch, SCAN_LANES))
            x_re, x_im = state[p]
            for t in range(steps):
                r = slice(t * batch, (t + 1) * batch)
                x_re, x_im = (a_re * x_re - a_im * x_im + st_scr[h, r, re],
                              a_re * x_im + a_im * x_re + st_scr[h, r, im])
                xb_scr[h, r, re] = x_re.astype(BF16)
                xb_scr[h, r, im] = x_im.astype(BF16)
            state[p] = (x_re, x_im)
        y = jnp.dot(xb_scr[h], cm_ref[...], preferred_element_type=F32) + dsk_ref[...] * u[h]
        y = jax.nn.gelu(y)
        gate = jax.nn.sigmoid(jnp.dot(y.astype(BF16), wg_ref[...], preferred_element_type=F32) + bg_ref[...])
        o_ref[h * steps:(h + 1) * steps] = (y * gate).reshape(steps, batch, width)

    for (re, im), (x_re, x_im) in zip(parts, state):
        carry_scr[:, re] = x_re
        carry_scr[:, im] = x_im


def _s5(u_tm, bmat, cmat, a_re, a_im, dskip, w_glu, b_glu):
    seq, batch, width = u_tm.shape
    tc = TIME_CHUNK
    full = lambda shape: pl.BlockSpec(shape, lambda t: tuple(0 for _ in shape))
    return pl.pallas_call(
        _s5_kernel,
        grid=(seq // tc,),
        in_specs=[pl.BlockSpec((tc, batch, width), lambda t: (t, 0, 0)),
                  full(bmat.shape), full(cmat.shape), full(a_re.shape), full(a_im.shape),
                  full(dskip.shape), full(w_glu.shape), full(b_glu.shape)],
        out_specs=pl.BlockSpec((tc, batch, width), lambda t: (t, 0, 0)),
        out_shape=jax.ShapeDtypeStruct((seq, batch, width), F32),
        scratch_shapes=[pltpu.VMEM((SCAN_SPLIT, tc // SCAN_SPLIT * batch, 2 * SSM_LANES), F32),
                        pltpu.VMEM((SCAN_SPLIT, tc // SCAN_SPLIT * batch, 2 * SSM_LANES), BF16),
                        pltpu.VMEM((batch, 2 * SSM_LANES), F32)],
        compiler_params=pltpu.CompilerParams(dimension_semantics=("arbitrary",),
                                             vmem_limit_bytes=VMEM_LIMIT_BYTES),
        name="s5",
    )(u_tm, bmat, cmat, a_re, a_im, dskip, w_glu, b_glu)


def _s5_params(lam_re, lam_im, log_dt, b_re, b_im, c_re, c_im):
    dt = jnp.exp(log_dt.astype(F32))[:, None]
    mag = jnp.exp(lam_re.astype(F32) * dt)
    ang = lam_im.astype(F32) * dt
    lb_re, lb_im = mag * jnp.cos(ang), mag * jnp.sin(ang)
    den = lam_re * lam_re + lam_im * lam_im
    k_re = ((lb_re - 1.0) * lam_re + lb_im * lam_im) / den
    k_im = (lb_im * lam_re - (lb_re - 1.0) * lam_im) / den
    bb_re = k_re[:, :, None] * b_re - k_im[:, :, None] * b_im
    bb_im = k_re[:, :, None] * b_im + k_im[:, :, None] * b_re
    same_group = (np.arange(SSM_WIDTH)[:, None] // SSM_GROUP) == (np.arange(SSM_LANES)[None, :] // SSM_STATE)

    def drive(bb):
        per_group = jnp.swapaxes(bb, 1, 2).reshape(SSM_WIDTH, SSM_STATE)
        return jnp.where(same_group, jnp.tile(per_group, (1, SSM_GROUPS)), 0.0)

    def read(cc):
        per_group = jnp.swapaxes(cc, 1, 2).reshape(SSM_LANES, SSM_GROUP)
        return jnp.where(same_group.T, jnp.tile(per_group, (1, SSM_GROUPS)), 0.0)

    bmat = jnp.concatenate([drive(bb_re), drive(bb_im)], axis=1)
    cmat = jnp.concatenate([read(c_re.astype(F32)), read(-c_im.astype(F32))], axis=0)
    return (bmat.astype(BF16), cmat.astype(BF16),
            lb_re.reshape(1, SSM_LANES), lb_im.reshape(1, SSM_LANES))


def _combine_kernel(x_ref, ca_ref, os_ref, gs_ref, qm_ref, gm_ref, mem_ref, wkv_ref,
                    wo_ref, lg_ref, lb_ref, out_ref, mk_scr, mv_scr):
    tile = x_ref.shape[1]
    nt_dims = (((1,), (1,)), ((), ()))
    lane = lax.broadcasted_iota(I32, (tile, LANES), 1)
    low = lane < HEAD_DIM

    @pl.when(pl.program_id(1) == 0)
    def _():
        mkv = jnp.dot(mem_ref[0].astype(BF16), wkv_ref[...], preferred_element_type=F32)
        mk_scr[...] = mkv[:, :MEM_WIDTH].astype(BF16)
        mv_scr[...] = mkv[:, MEM_WIDTH:].astype(BF16)

    qm = qm_ref[0].astype(F32)
    mem_tiles = []
    for t in range(MEM_WIDTH // LANES):
        qt = qm[:, t * LANES:(t + 1) * LANES]
        mk = mk_scr[:, t * LANES:(t + 1) * LANES]
        mv = mv_scr[:, t * LANES:(t + 1) * LANES]
        halves = []
        for first in (True, False):
            qh = jnp.where(low if first else ~low, qt, 0.0).astype(BF16)
            s = lax.dot_general(qh, mk, nt_dims, preferred_element_type=F32)
            p = jnp.exp(s - jnp.max(s, axis=1, keepdims=True))
            o = jnp.dot(p.astype(BF16), mv, preferred_element_type=F32)
            halves.append(o / jnp.sum(p, axis=1, keepdims=True))
        mem_tiles.append(jnp.where(low, halves[0], halves[1]))
    o_mem = jnp.concatenate(mem_tiles, axis=1)

    c_att = ca_ref[0]
    c_ssm = (os_ref[...] * gs_ref[0]).astype(BF16)
    c_mem = (o_mem * gm_ref[0]).astype(BF16)
    a1, a2 = ATT_WIDTH, ATT_WIDTH + SSM_WIDTH
    sub = (jnp.dot(c_att, wo_ref[:a1, :], preferred_element_type=F32)
           + jnp.dot(c_ssm, wo_ref[a1:a2, :], preferred_element_type=F32)
           + jnp.dot(c_mem, wo_ref[a2:, :], preferred_element_type=F32))
    h = DN_ALPHA * x_ref[0] + sub
    mu = jnp.mean(h, axis=1, keepdims=True)
    d = h - mu
    var = jnp.mean(d * d, axis=1, keepdims=True)
    out_ref[0] = d * lax.rsqrt(var + LN_EPS) * lg_ref[...] + lb_ref[...]


def _combine(x, c_att, o_ssm_tm, g_ssm, qm, g_mem, mem, w_mem_kv_bf16, wo_bf16, ln_g, ln_b):
    batch, seq, _ = x.shape
    tile = TOKEN_TILE
    tok = lambda w: pl.BlockSpec((1, tile, w), lambda b, j: (b, j, 0))
    time_major = pl.BlockSpec((tile, SSM_WIDTH), lambda b, j: (j, b))
    per_b = pl.BlockSpec((1, N_MEM, D_MODEL), lambda b, j: (b, 0, 0))
    const = lambda shape: pl.BlockSpec(shape, lambda b, j: tuple(0 for _ in shape))
    return pl.pallas_call(
        _combine_kernel,
        grid=(batch, seq // tile),
        in_specs=[tok(D_MODEL), tok(ATT_WIDTH), time_major, tok(SSM_WIDTH),
                  tok(MEM_WIDTH), tok(MEM_WIDTH), per_b, const((D_MODEL, 2 * MEM_WIDTH)),
                  const((D_MODEL, D_MODEL)), const((1, D_MODEL)), const((1, D_MODEL))],
        out_specs=tok(D_MODEL),
        out_shape=jax.ShapeDtypeStruct((batch, seq, D_MODEL), x.dtype),
        scratch_shapes=[pltpu.VMEM((N_MEM, MEM_WIDTH), BF16),
                        pltpu.VMEM((N_MEM, MEM_WIDTH), BF16)],
        compiler_params=pltpu.CompilerParams(dimension_semantics=("arbitrary", "arbitrary"),
                                             vmem_limit_bytes=VMEM_LIMIT_BYTES),
        name="combine",
    )(x, c_att, o_ssm_tm, g_ssm, qm, g_mem, mem, w_mem_kv_bf16, wo_bf16, ln_g, ln_b)


def _reordered_w_in(w_in):
    offs = np.cumsum([0, ATT_WIDTH, KV_WIDTH, KV_WIDTH, IDX_WIDTH, IDX_DIM, IDX_HEADS,
                      ATT_WIDTH, SSM_WIDTH, SSM_WIDTH, MEM_WIDTH, MEM_WIDTH])
    part = lambda n: w_in[:, int(offs[n]):int(offs[n + 1])]
    k_idx = part(4)
    w_idx = part(5)
    zeros = jnp.zeros((D_MODEL, LANES - IDX_HEADS), w_in.dtype)
    run = lambda first, last: w_in[:, int(offs[first]):int(offs[last + 1])]
    cols = [_pair_heads(part(0), 1), run(1, 3),
            jnp.tile(k_idx, (1, LANES // IDX_DIM)),
            w_idx, zeros,
            _pair_heads(part(6), 1), run(7, 10)]
    return jnp.concatenate(cols, axis=1).astype(BF16)


def kernel(x, mem, w_in, w_mem_kv, lam_re, lam_im, log_dt, b_re, b_im, c_re, c_im, d_skip, w_glu, b_glu,
           w_out, ln_g, ln_b):
    batch, seq, _ = x.shape
    assert seq % TOKEN_TILE == 0 and seq % Q_TILE == 0 and seq % TIME_CHUNK == 0

    w_all = _reordered_w_in(w_in)
    wo = jnp.concatenate([_pair_heads(w_out[:ATT_WIDTH], 0), w_out[ATT_WIDTH:]], axis=0).astype(BF16)
    bmat, cmat, a_re, a_im = _s5_params(lam_re, lam_im, log_dt, b_re, b_im, c_re, c_im)
    tabs = _rope_tables(seq, HEAD_DIM) + _rope_tables(seq, IDX_DIM)

    q_t, k, vt, qi_t, ki, wi_t, g_att, u_tm, g_ssm, qm, g_mem = _inproj(x, w_all, tabs)
    c_att = _dsa(q_t, qi_t, wi_t, k, vt, ki, g_att)

    o_ssm_tm = _s5(u_tm.reshape(seq, batch, SSM_WIDTH), bmat, cmat, a_re, a_im,
                   d_skip.reshape(1, SSM_WIDTH).astype(F32), w_glu.astype(BF16),
                   b_glu.reshape(1, SSM_WIDTH).astype(F32))
    o_ssm = o_ssm_tm.reshape(seq, batch * SSM_WIDTH)

    return _combine(x, c_att, o_ssm, g_ssm, qm, g_mem, mem, w_mem_kv.astype(BF16), wo,
                    ln_g.reshape(1, D_MODEL).astype(F32), ln_b.reshape(1, D_MODEL).astype(F32))
```

```python
import functools
import math

import jax
import jax.numpy as jnp
import numpy as np
from jax import lax
from jax.experimental import pallas as pl
from jax.experimental.pallas import tpu as pltpu

F32 = jnp.float32
BF16 = jnp.bfloat16
I32 = jnp.int32

D_MODEL = 1024
N_MEM = 256
HEAD_DIM = 64
ATT_HEADS = 8
ATT_KV_HEADS = 2
ATT_REP = ATT_HEADS // ATT_KV_HEADS
ATT_WIDTH = ATT_HEADS * HEAD_DIM
KV_WIDTH = ATT_KV_HEADS * HEAD_DIM
IDX_HEADS = 8
IDX_DIM = 32
IDX_WIDTH = IDX_HEADS * IDX_DIM
TOPK_MAX = 256
SSM_WIDTH = D_MODEL // 4
SSM_GROUP = 16
SSM_GROUPS = SSM_WIDTH // SSM_GROUP
SSM_STATE = 64
SSM_LANES = SSM_GROUPS * SSM_STATE
MEM_HEADS = 4
MEM_WIDTH = MEM_HEADS * HEAD_DIM
ROPE_THETA = 500000.0
ROPE_FRAC = 4
LN_EPS = 1e-5
DEPTH = 1
DN_ALPHA = (2.0 * DEPTH) ** 0.25
ATT_SCALE = HEAD_DIM ** -0.5
IDX_SCALE = IDX_HEADS ** -0.5 * IDX_DIM ** -0.5
LOG2_E = math.log2(math.e)

LANES = 128
SUBLANES = 8
BF16_ROWS = 16
VMEM_LIMIT_BYTES = 48 * 1024 * 1024
TOKEN_TILE = 512
COMBINE_TILE = 1024
Q_TILE = 256
KEY_CHUNK = 256
COUNT_ROWS = 64
VT_ROWS = KV_WIDTH + BF16_ROWS
TIME_CHUNK = 128
SCAN_SPLIT = 2
SCAN_LANES = 256
NEG_BIG = -1e30
DENOM_FLOOR = 2.0 ** -100
SEARCH_MAX_PASSES = 24
SEARCH_FIRST_PASSES = 13
SEARCH_GROUP = 2
SEARCH_BISECT_PASSES = 2
SEARCH_CLIP = 0.1

INT_MIN = -2147483648
KEY_POS_INF = 0x7F800000
KEY_NEG_INF = INT_MIN + 0x7FFFFF

_G_Q, _G_K, _G_V, _G_QI, _G_KI, _G_W, _G_GA, _G_U, _G_GS, _G_QM, _G_GM = range(11)
_GROUP_WIDTHS = [ATT_WIDTH, KV_WIDTH, KV_WIDTH, IDX_WIDTH, LANES, LANES, ATT_WIDTH,
                 SSM_WIDTH, SSM_WIDTH, MEM_WIDTH, MEM_WIDTH]
_GROUP_OFFS = [int(v) for v in np.cumsum([0] + _GROUP_WIDTHS)]
IN_COLS = _GROUP_OFFS[-1]


def _pair_heads(w, axis):
    shape = w.shape
    split = shape[:axis] + (ATT_KV_HEADS, ATT_REP, HEAD_DIM) + shape[axis + 1:]
    return jnp.swapaxes(w.reshape(split), axis, axis + 1).reshape(shape)


def _rope_tables(seq, period, dtype=F32):
    r = period // ROPE_FRAC
    half = r // 2
    inv = ROPE_THETA ** (-jnp.arange(0, half, dtype=F32) * 2.0 / r)
    ang = jnp.arange(seq).astype(F32)[:, None] * inv[None, :]
    cos, sin = jnp.cos(ang), jnp.sin(ang)
    ones = jnp.ones((seq, period - r), F32)
    zeros = jnp.zeros((seq, period - r), F32)
    c = jnp.concatenate([cos, cos, ones], axis=1)
    s = jnp.concatenate([-sin, sin, zeros], axis=1)
    reps = LANES // period
    return jnp.tile(c, (1, reps)).astype(dtype), jnp.tile(s, (1, reps)).astype(dtype)


def _rope(z, cos_t, sin_t, period):
    width = z.shape[1]
    half = period // ROPE_FRAC // 2
    reps = width // LANES
    if reps > 1:
        cos_t = jnp.concatenate([cos_t] * reps, axis=1)
        sin_t = jnp.concatenate([sin_t] * reps, axis=1)
    lane = lax.broadcasted_iota(I32, z.shape, 1)
    first = (lane & (period - 1)) < half
    up = pltpu.roll(z, half, 1)
    down = pltpu.roll(z, width - half, 1)
    return z * cos_t + jnp.where(first, down, up) * sin_t


def _inproj_kernel(x_ref, w_ref, ca_ref, sa_ref, ci_ref, si_ref,
                   q_ref, k_ref, vt_ref, qi_ref, ki_ref, wi_ref, ga_ref, u_ref, gs_ref, qm_ref, gm_ref):
    tile = x_ref.shape[1]
    pos0 = pl.multiple_of(pl.program_id(1) * tile, tile)
    xb = x_ref[0].astype(BF16)

    def proj(group, last=None):
        lo, hi = _GROUP_OFFS[group], _GROUP_OFFS[(group if last is None else last) + 1]
        return jnp.dot(xb, w_ref[:, lo:hi], preferred_element_type=F32)

    ca = ca_ref[pl.ds(pos0, tile), :]
    sa = sa_ref[pl.ds(pos0, tile), :]
    ci = ci_ref[pl.ds(pos0, tile), :]
    si = si_ref[pl.ds(pos0, tile), :]

    def store_transposed(ref, z):
        for qb in range(tile // Q_TILE):
            for t in range(z.shape[1] // LANES):
                zt = z[qb * Q_TILE:(qb + 1) * Q_TILE, t * LANES:(t + 1) * LANES].T
                ref[0, qb, :, t * Q_TILE:(t + 1) * Q_TILE] = zt.astype(ref.dtype)

    store_transposed(q_ref, _rope(proj(_G_Q), ca, sa, HEAD_DIM) * (ATT_SCALE * LOG2_E))
    kv = proj(_G_K, _G_V)
    k_ref[0] = _rope(kv[:, :KV_WIDTH], ca, sa, HEAD_DIM).astype(BF16)
    v = kv[:, KV_WIDTH:]
    for c in range(tile // KEY_CHUNK):
        vt_ref[0, c, :KV_WIDTH, :] = v[c * KEY_CHUNK:(c + 1) * KEY_CHUNK, :].T.astype(BF16)
        vt_ref[0, c, KV_WIDTH:, :] = jnp.ones((BF16_ROWS, KEY_CHUNK), BF16)
    store_transposed(qi_ref, _rope(proj(_G_QI), ci, si, IDX_DIM))
    kiw = proj(_G_KI, _G_W)
    ki_ref[0] = _rope(kiw[:, :LANES], ci, si, IDX_DIM).astype(BF16)
    wi = kiw[:, LANES:] * IDX_SCALE
    for qb in range(tile // Q_TILE):
        wi_ref[0, qb] = wi[qb * Q_TILE:(qb + 1) * Q_TILE, :].T[:IDX_HEADS, :]
    ga_ref[0] = jax.nn.silu(proj(_G_GA))
    u_ref[...] = proj(_G_U)
    gs_ref[0] = jax.nn.silu(proj(_G_GS))
    qm_ref[0] = (proj(_G_QM) * ATT_SCALE).astype(BF16)
    gm_ref[0] = jax.nn.silu(proj(_G_GM))


def _inproj(x, w_bf16, tabs):
    batch, seq, _ = x.shape
    tile = TOKEN_TILE
    widths = _GROUP_WIDTHS
    dtypes = [BF16, BF16, BF16, BF16, BF16, F32, F32, F32, F32, BF16, F32]
    tab_spec = pl.BlockSpec((seq, LANES), lambda b, j: (0, 0))
    out_specs = [pl.BlockSpec((1, tile, w), lambda b, j: (b, j, 0)) for w in widths]
    out_shape = [jax.ShapeDtypeStruct((batch, seq, w), d) for w, d in zip(widths, dtypes)]
    out_specs[_G_V] = pl.BlockSpec((1, tile // KEY_CHUNK, VT_ROWS, KEY_CHUNK), lambda b, j: (b, j, 0, 0))
    out_shape[_G_V] = jax.ShapeDtypeStruct((batch, seq // KEY_CHUNK, VT_ROWS, KEY_CHUNK), BF16)
    for group, rows, dtype in ((_G_Q, LANES, BF16), (_G_QI, LANES, BF16), (_G_W, IDX_HEADS, F32)):
        cols = Q_TILE * max(widths[group] // LANES, 1)
        out_specs[group] = pl.BlockSpec((1, tile // Q_TILE, rows, cols), lambda b, j: (b, j, 0, 0))
        out_shape[group] = jax.ShapeDtypeStruct((batch, seq // Q_TILE, rows, cols), dtype)
    out_specs[_G_U] = pl.BlockSpec((tile, SSM_WIDTH), lambda b, j: (j, b))
    out_shape[_G_U] = jax.ShapeDtypeStruct((seq, batch * SSM_WIDTH), F32)
    return pl.pallas_call(
        _inproj_kernel,
        grid=(batch, seq // tile),
        in_specs=[pl.BlockSpec((1, tile, D_MODEL), lambda b, j: (b, j, 0)),
                  pl.BlockSpec((D_MODEL, IN_COLS), lambda b, j: (0, 0)),
                  tab_spec, tab_spec, tab_spec, tab_spec],
        out_specs=out_specs,
        out_shape=out_shape,
        compiler_params=pltpu.CompilerParams(dimension_semantics=("arbitrary", "arbitrary"),
                                             vmem_limit_bytes=VMEM_LIMIT_BYTES),
        name="inproj",
    )(x, w_bf16, *tabs)


def _key_to_float(key):
    bits = jnp.where(key >= 0, key, key ^ 0x7FFFFFFF)
    val = lax.bitcast_convert_type(bits, F32)
    val = jnp.where(key <= KEY_NEG_INF, -jnp.inf, val)
    return jnp.where(key >= KEY_POS_INF, jnp.inf, val)


def _dsa_kernel(q_ref, qi_ref, wi_ref, k_ref, vt_ref, ki_ref, ga_ref, o_ref,
                score_scr, bias_scr, row_scr, s_scr, acc_scr, og_scr, *, topk):
    i = pl.program_id(1)
    n_chunks = ((i + 1) * Q_TILE + KEY_CHUNK - 1) // KEY_CHUNK
    q0 = i * Q_TILE
    kf = float(topk)
    att_cols = ATT_REP * Q_TILE
    groups = KEY_CHUNK // SUBLANES

    def chunk_start(c):
        return pl.multiple_of(c * KEY_CHUNK, KEY_CHUNK)

    def over_chunk_pairs(body, init):
        def pair(c2, carry):
            return body(2 * c2 + 1, body(2 * c2, carry))
        carry = lax.fori_loop(0, n_chunks // 2, pair, init)
        return lax.cond(n_chunks % 2 == 1, lambda cr: body(n_chunks - 1, cr), lambda cr: cr, carry)

    def fold(v, op):
        return op(v.reshape(groups, SUBLANES, v.shape[-1]), axis=0)

    def only_rows(x, lo, n):
        parts = [jnp.zeros((lo, x.shape[1]), x.dtype)] if lo else []
        parts.append(x[lo:lo + n])
        if lo + n < x.shape[0]:
            parts.append(jnp.zeros((x.shape[0] - lo - n, x.shape[1]), x.dtype))
        return jnp.concatenate(parts, axis=0)

    qi_t = qi_ref[0, 0]
    per_tile = LANES // IDX_DIM

    def head_cols(h):
        t = h // per_tile
        return only_rows(qi_t[:, t * Q_TILE:(t + 1) * Q_TILE], (h % per_tile) * IDX_DIM, IDX_DIM)

    rhs_pairs = [jnp.concatenate([head_cols(2 * p), head_cols(2 * p + 1)], axis=1)
                 for p in range(IDX_HEADS // 2)]
    w_rows = [wi_ref[0, 0, h:h + 1, :] for h in range(IDX_HEADS)]
    q_t = q_ref[0, 0]
    qg_t = [only_rows(q_t, g * HEAD_DIM, HEAD_DIM) for g in range(ATT_KV_HEADS)]
    krow = lax.broadcasted_iota(I32, (KEY_CHUNK, Q_TILE), 0)
    kq = q0 + lax.broadcasted_iota(I32, (KEY_CHUNK, Q_TILE), 1)

    def matmul_chunk(c, stats):
        mx_a, mn_a, ge_a, gt_a, top_a = stats
        ks = chunk_start(c)
        kc = ki_ref[0, pl.ds(ks, KEY_CHUNK), :]
        acc = jnp.zeros((KEY_CHUNK, Q_TILE), F32)
        for p in range(IDX_HEADS // 2):
            s2 = jnp.dot(kc, rhs_pairs[p], preferred_element_type=F32)
            acc = acc + w_rows[2 * p] * jnp.maximum(s2[:, :Q_TILE], 0.0)
            acc = acc + w_rows[2 * p + 1] * jnp.maximum(s2[:, Q_TILE:], 0.0)
        causal = ks + krow <= kq
        sc = jnp.where(causal, acc, -jnp.inf)
        score_scr[pl.ds(ks, KEY_CHUNK), :] = sc
        kk = k_ref[0, pl.ds(ks, KEY_CHUNK), :]
        tops = []
        for g in range(ATT_KV_HEADS):
            s = jnp.dot(kk, qg_t[g], preferred_element_type=F32)
            s_scr[g, pl.ds(ks, KEY_CHUNK), :] = s
            tops.append(jnp.maximum(top_a[g], fold(s, jnp.max)))
        return (jnp.maximum(mx_a, fold(sc, jnp.max)),
                jnp.minimum(mn_a, fold(jnp.where(causal, acc, jnp.inf), jnp.min)),
                ge_a + fold(jnp.where(sc >= 0.0, 1.0, 0.0), jnp.sum),
                gt_a + fold(jnp.where(sc > 0.0, 1.0, 0.0), jnp.sum),
                tuple(tops))

    stat0 = lambda v: jnp.full((SUBLANES, Q_TILE), v, F32)
    top0 = tuple(jnp.full((SUBLANES, att_cols), NEG_BIG, F32) for _ in range(ATT_KV_HEADS))
    mx_a, mn_a, ge_a, gt_a, top_a = over_chunk_pairs(
        matmul_chunk, (stat0(-jnp.inf), stat0(jnp.inf), stat0(0.0), stat0(0.0), top0))
    m_top = [jnp.max(top_a[g], axis=0, keepdims=True) for g in range(ATT_KV_HEADS)]
    s_max = jnp.max(mx_a, axis=0, keepdims=True)
    s_min = jnp.min(mn_a, axis=0, keepdims=True)
    n_ge0 = jnp.sum(ge_a, axis=0, keepdims=True)
    n_gt0 = jnp.sum(gt_a, axis=0, keepdims=True)
    n_causal = (q0 + 1 + lax.broadcasted_iota(I32, (1, Q_TILE), 1)).astype(F32)

    def count(pred):
        def body(c, acc):
            ks = chunk_start(c)
            for part in range(KEY_CHUNK // LANES):
                hit = jnp.where(pred(score_scr[pl.ds(ks + part * LANES, LANES), :]), 1.0, 0.0)
                acc = acc + (hit[:COUNT_ROWS] + hit[COUNT_ROWS:])
            return acc
        acc = over_chunk_pairs(body, jnp.zeros((COUNT_ROWS, Q_TILE), F32))
        rows = COUNT_ROWS
        while rows > SUBLANES:
            rows //= 2
            acc = acc[:rows] + acc[rows:]
        return jnp.sum(acc, axis=0, keepdims=True)

    def count_ge(value):
        vb = jnp.broadcast_to(value, (LANES, Q_TILE))
        return count(lambda sc: sc >= vb)

    zero_thr = (n_gt0 < kf) & (n_ge0 >= kf)
    keep_all = n_causal <= kf
    positive = n_gt0 >= kf
    settled = zero_thr | keep_all
    lo0 = jnp.where(settled, 0.0, jnp.where(positive, 0.0, s_min))
    hi0 = jnp.where(settled, 0.0, jnp.where(positive, s_max, 0.0))
    c_lo0 = jnp.where(positive, n_gt0, n_causal)
    c_hi0 = jnp.where(positive, 0.0, n_ge0)
    thr0 = jnp.where(keep_all, -jnp.inf, 0.0)
    done0 = jnp.where(settled, 1.0, 0.0)
    zero_ties = jnp.max(jnp.where(zero_thr & (n_ge0 > kf) & jnp.logical_not(keep_all), 1.0, 0.0)) > 0.0

    def search_pass(bisect, rows):
        lo, hi, c_lo, c_hi, thr, done = rows
        if bisect:
            frac = 0.5
        else:
            frac = jnp.clip((kf - c_hi + 0.5) / (c_lo - c_hi + 1.0), SEARCH_CLIP, 1.0 - SEARCH_CLIP)
        x = hi - (hi - lo) * frac
        c = count_ge(x)
        live = done == 0.0
        above = live & (c >= kf)
        below = live & (c < kf)
        hit = live & (c == kf)
        return (jnp.where(above, x, lo), jnp.where(below, x, hi),
                jnp.where(above, c, c_lo), jnp.where(below, c, c_hi),
                jnp.where(hit, x, thr), jnp.where(hit, 1.0, done))

    def search_cond(state):
        return (state[0] < SEARCH_MAX_PASSES) & (state[1] > 0.0)

    def search_body(state):
        it, _, rows = state
        for _ in range(SEARCH_GROUP):
            rows = search_pass(False, rows)
        return it + SEARCH_GROUP, jnp.sum(1.0 - rows[5]), rows

    def first_passes(rows):
        for p in range(SEARCH_FIRST_PASSES):
            rows = search_pass(p < SEARCH_BISECT_PASSES, rows)
        return rows

    rows0 = (lo0, hi0, c_lo0, c_hi0, thr0, done0)
    rows1 = lax.cond((i + 1) * Q_TILE > topk, first_passes, lambda rows: rows, rows0)
    state = lax.while_loop(search_cond, search_body,
                           (jnp.int32(SEARCH_FIRST_PASSES), jnp.sum(1.0 - rows1[5]), rows1))
    pending, thr_s, done_s = state[1], state[2][4], state[2][5]
    row_scr[0:1, :] = thr_s
    row_scr[1:2, :] = jnp.where(zero_thr, n_gt0, 0.0)
    row_scr[2:3, :] = jnp.where(zero_thr, n_ge0, kf)

    @pl.when(pending > 0.0)
    def _():
        def count_ge_key(key):
            return count_ge(_key_to_float(key))

        key0 = jnp.where(count_ge_key(jnp.zeros((1, Q_TILE), I32)) >= kf, 0, INT_MIN).astype(I32)

        def bit_step(b, key):
            cand = key | jnp.left_shift(jnp.int32(1), 30 - b)
            return jnp.where(count_ge_key(cand) >= kf, cand, key)

        thr_f = _key_to_float(lax.fori_loop(0, 31, bit_step, key0))
        thr_fb = jnp.broadcast_to(thr_f, (LANES, Q_TILE))
        open_row = done_s == 0.0
        row_scr[0:1, :] = jnp.where(open_row, thr_f, thr_s)
        row_scr[1:2, :] = jnp.where(open_row, count(lambda sc: sc > thr_fb), row_scr[1:2, :])
        row_scr[2:3, :] = jnp.where(open_row, count(lambda sc: sc >= thr_fb), row_scr[2:3, :])

    thr = row_scr[0:1, :]
    thr_c = jnp.broadcast_to(thr, (KEY_CHUNK, Q_TILE))
    any_tie = zero_ties | (pending > 0.0)

    def plain_bias(c, carry):
        ks = chunk_start(c)
        keep = (score_scr[pl.ds(ks, KEY_CHUNK), :] >= thr_c) & (ks + krow <= kq)
        return jnp.where(keep, 0.0, NEG_BIG), carry

    def make_tie_bias():
        need = jnp.broadcast_to(kf - row_scr[1:2, :], (KEY_CHUNK, Q_TILE))
        lower = (lax.broadcasted_iota(I32, (KEY_CHUNK, KEY_CHUNK), 1)
                 <= lax.broadcasted_iota(I32, (KEY_CHUNK, KEY_CHUNK), 0))
        prefix_mat = jnp.where(lower, 1.0, 0.0).astype(BF16)

        def tie_bias(c, seen):
            ks = chunk_start(c)
            sc = score_scr[pl.ds(ks, KEY_CHUNK), :]
            tied = sc == thr_c
            rank = seen + jnp.dot(prefix_mat, jnp.where(tied, 1.0, 0.0).astype(BF16),
                                  preferred_element_type=F32)
            chosen = (sc > thr_c) | (tied & (rank <= need))
            return jnp.where(chosen & (ks + krow <= kq), 0.0, NEG_BIG), rank[KEY_CHUNK - 1:KEY_CHUNK, :]

        return tie_bias

    no_ties_seen = jnp.zeros((1, Q_TILE), F32)

    def pv_pass(bias_of, m, carry0):
        acc_scr[...] = jnp.zeros(acc_scr.shape, F32)

        def pv_chunk(c, carry):
            bias, carry = bias_of(c, carry)
            bias4 = jnp.concatenate([bias] * ATT_REP, axis=1)
            for g in range(ATT_KV_HEADS):
                p = jnp.exp2(s_scr[g, pl.ds(chunk_start(c), KEY_CHUNK), :] + bias4 - m[g])
                acc_scr[g] += jnp.dot(vt_ref[0, c], p.astype(BF16), preferred_element_type=F32)
            return carry

        over_chunk_pairs(pv_chunk, carry0)

    @pl.when(jnp.logical_not(any_tie))
    def _():
        pv_pass(plain_bias, m_top, jnp.int32(0))

    @pl.when(any_tie)
    def _():
        pv_pass(make_tie_bias(), m_top, no_ties_seen)

    d_min = jnp.min(jnp.minimum(acc_scr[0, KV_WIDTH:KV_WIDTH + 1, :], acc_scr[1, KV_WIDTH:KV_WIDTH + 1, :]))

    @pl.when(jnp.logical_not(d_min >= DENOM_FLOOR))
    def _():
        tie_bias = make_tie_bias()

        def mask_chunk(c, carry):
            seen, m_acc = carry
            ks = chunk_start(c)
            bias, seen = tie_bias(c, seen)
            bias_scr[pl.ds(ks, KEY_CHUNK), :] = bias
            bias4 = jnp.concatenate([bias] * ATT_REP, axis=1)
            return seen, tuple(jnp.maximum(m_acc[g], fold(s_scr[g, pl.ds(ks, KEY_CHUNK), :] + bias4, jnp.max))
                               for g in range(ATT_KV_HEADS))

        m_init = tuple(jnp.full((SUBLANES, att_cols), NEG_BIG, F32) for _ in range(ATT_KV_HEADS))
        _, m_acc = lax.fori_loop(0, n_chunks, mask_chunk, (no_ties_seen, m_init))
        m_sel = [jnp.max(m_acc[g], axis=0, keepdims=True) for g in range(ATT_KV_HEADS)]
        pv_pass(lambda c, carry: (bias_scr[pl.ds(chunk_start(c), KEY_CHUNK), :], carry), m_sel, jnp.int32(0))

    for g in range(ATT_KV_HEADS):
        denom = acc_scr[g, KV_WIDTH:KV_WIDTH + 1, :]
        og_scr[g * HEAD_DIM:(g + 1) * HEAD_DIM, :] = acc_scr[g, g * HEAD_DIM:(g + 1) * HEAD_DIM, :] / denom

    for j in range(ATT_REP):
        gated = og_scr[:, j * Q_TILE:(j + 1) * Q_TILE].T * ga_ref[0, :, j * LANES:(j + 1) * LANES]
        o_ref[0, :, j * LANES:(j + 1) * LANES] = gated.astype(o_ref.dtype)


def _dsa(q_t, qi_t, wi_t, k, vt, ki, gate):
    batch, seq, _ = k.shape
    topk = min(TOPK_MAX, seq // 4)
    att_cols = ATT_REP * Q_TILE
    per_q = lambda a: pl.BlockSpec((1, 1) + a.shape[2:], lambda b, i: (b, i, 0, 0))
    per_b = lambda w: pl.BlockSpec((1, seq, w), lambda b, i: (b, 0, 0))
    vt_spec = pl.BlockSpec((1, seq // KEY_CHUNK, VT_ROWS, KEY_CHUNK), lambda b, i: (b, 0, 0, 0))
    tok_spec = pl.BlockSpec((1, Q_TILE, ATT_WIDTH), lambda b, i: (b, i, 0))
    return pl.pallas_call(
        functools.partial(_dsa_kernel, topk=topk),
        grid=(batch, seq // Q_TILE),
        in_specs=[per_q(q_t), per_q(qi_t), per_q(wi_t), per_b(KV_WIDTH), vt_spec, per_b(LANES), tok_spec],
        out_specs=tok_spec,
        out_shape=jax.ShapeDtypeStruct((batch, seq, ATT_WIDTH), BF16),
        scratch_shapes=[pltpu.VMEM((seq, Q_TILE), F32),
                        pltpu.VMEM((seq, Q_TILE), F32),
                        pltpu.VMEM((SUBLANES, Q_TILE), F32),
                        pltpu.VMEM((ATT_KV_HEADS, seq, att_cols), F32),
                        pltpu.VMEM((ATT_KV_HEADS, VT_ROWS, att_cols), F32),
                        pltpu.VMEM((KV_WIDTH, att_cols), F32)],
        compiler_params=pltpu.CompilerParams(dimension_semantics=("arbitrary", "arbitrary"),
                                             vmem_limit_bytes=VMEM_LIMIT_BYTES),
        name="dsa",
    )(q_t, qi_t, wi_t, k, vt, ki, gate)


def _s5_kernel(u_ref, bm_ref, cm_ref, are_ref, aim_ref, dsk_ref, wg_ref, bg_ref, o_ref,
               st_scr, xb_scr, carry_scr):
    tc, batch, width = u_ref.shape
    steps = tc // SCAN_SPLIT
    rows = steps * batch

    @pl.when(pl.program_id(0) == 0)
    def _():
        carry_scr[...] = jnp.zeros(carry_scr.shape, F32)

    u = [u_ref[h * steps:(h + 1) * steps].reshape(rows, width) for h in range(SCAN_SPLIT)]
    for h in range(SCAN_SPLIT):
        st_scr[h] = jnp.dot(u[h].astype(BF16), bm_ref[...], preferred_element_type=F32)

    parts = []
    for part in range(SSM_LANES // SCAN_LANES):
        re_lo = part * SCAN_LANES
        im_lo = SSM_LANES + re_lo
        parts.append((slice(re_lo, re_lo + SCAN_LANES), slice(im_lo, im_lo + SCAN_LANES)))
    state = [(carry_scr[:, re], carry_scr[:, im]) for re, im in parts]

    for h in range(SCAN_SPLIT):
        for p, (re, im) in enumerate(parts):
            a_re = jnp.broadcast_to(are_ref[:, re], (batch, SCAN_LANES))
            a_im = jnp.broadcast_to(aim_ref[:, re], (batch, SCAN_LANES))
            x_re, x_im = state[p]
            for t in range(steps):
                r = slice(t * batch, (t + 1) * batch)
                x_re, x_im = (a_re * x_re - a_im * x_im + st_scr[h, r, re],
                              a_re * x_im + a_im * x_re + st_scr[h, r, im])
                xb_scr[h, r, re] = x_re.astype(BF16)
                xb_scr[h, r, im] = x_im.astype(BF16)
            state[p] = (x_re, x_im)
        y = jnp.dot(xb_scr[h], cm_ref[...], preferred_element_type=F32) + dsk_ref[...] * u[h]
        y = jax.nn.gelu(y)
        gate = jax.nn.sigmoid(jnp.dot(y.astype(BF16), wg_ref[...], preferred_element_type=F32) + bg_ref[...])
        o_ref[h * steps:(h + 1) * steps] = (y * gate).reshape(steps, batch, width)

    for (re, im), (x_re, x_im) in zip(parts, state):
        carry_scr[:, re] = x_re
        carry_scr[:, im] = x_im


def _s5(u_tm, bmat, cmat, a_re, a_im, dskip, w_glu, b_glu):
    seq, batch, width = u_tm.shape
    tc = TIME_CHUNK
    full = lambda shape: pl.BlockSpec(shape, lambda t: tuple(0 for _ in shape))
    return pl.pallas_call(
        _s5_kernel,
        grid=(seq // tc,),
        in_specs=[pl.BlockSpec((tc, batch, width), lambda t: (t, 0, 0)),
                  full(bmat.shape), full(cmat.shape), full(a_re.shape), full(a_im.shape),
                  full(dskip.shape), full(w_glu.shape), full(b_glu.shape)],
        out_specs=pl.BlockSpec((tc, batch, width), lambda t: (t, 0, 0)),
        out_shape=jax.ShapeDtypeStruct((seq, batch, width), F32),
        scratch_shapes=[pltpu.VMEM((SCAN_SPLIT, tc // SCAN_SPLIT * batch, 2 * SSM_LANES), F32),
                        pltpu.VMEM((SCAN_SPLIT, tc // SCAN_SPLIT * batch, 2 * SSM_LANES), BF16),
                        pltpu.VMEM((batch, 2 * SSM_LANES), F32)],
        compiler_params=pltpu.CompilerParams(dimension_semantics=("arbitrary",),
                                             vmem_limit_bytes=VMEM_LIMIT_BYTES),
        name="s5",
    )(u_tm, bmat, cmat, a_re, a_im, dskip, w_glu, b_glu)


def _s5_params(lam_re, lam_im, log_dt, b_re, b_im, c_re, c_im):
    dt = jnp.exp(log_dt.astype(F32))[:, None]
    mag = jnp.exp(lam_re.astype(F32) * dt)
    ang = lam_im.astype(F32) * dt
    lb_re, lb_im = mag * jnp.cos(ang), mag * jnp.sin(ang)
    den = lam_re * lam_re + lam_im * lam_im
    k_re = ((lb_re - 1.0) * lam_re + lb_im * lam_im) / den
    k_im = (lb_im * lam_re - (lb_re - 1.0) * lam_im) / den
    bb_re = k_re[:, :, None] * b_re - k_im[:, :, None] * b_im
    bb_im = k_re[:, :, None] * b_im + k_im[:, :, None] * b_re
    same_group = (np.arange(SSM_WIDTH)[:, None] // SSM_GROUP) == (np.arange(SSM_LANES)[None, :] // SSM_STATE)

    def drive(bb):
        per_group = jnp.swapaxes(bb, 1, 2).reshape(SSM_WIDTH, SSM_STATE)
        return jnp.where(same_group, jnp.tile(per_group, (1, SSM_GROUPS)), 0.0)

    def read(cc):
        per_group = jnp.swapaxes(cc, 1, 2).reshape(SSM_LANES, SSM_GROUP)
        return jnp.where(same_group.T, jnp.tile(per_group, (1, SSM_GROUPS)), 0.0)

    bmat = jnp.concatenate([drive(bb_re), drive(bb_im)], axis=1)
    cmat = jnp.concatenate([read(c_re.astype(F32)), read(-c_im.astype(F32))], axis=0)
    return (bmat.astype(BF16), cmat.astype(BF16),
            lb_re.reshape(1, SSM_LANES), lb_im.reshape(1, SSM_LANES))


def _combine_kernel(x_ref, ca_ref, os_ref, gs_ref, qm_ref, gm_ref, mem_ref, wkv_ref,
                    wo_ref, lg_ref, lb_ref, out_ref, mk_scr, mv_scr):
    tile = x_ref.shape[1]
    nt_dims = (((1,), (1,)), ((), ()))
    lane = lax.broadcasted_iota(I32, (tile, LANES), 1)
    low = lane < HEAD_DIM

    @pl.when(pl.program_id(1) == 0)
    def _():
        mkv = jnp.dot(mem_ref[0].astype(BF16), wkv_ref[...], preferred_element_type=F32)
        mk_scr[...] = mkv[:, :MEM_WIDTH].astype(BF16)
        mv_scr[...] = mkv[:, MEM_WIDTH:].astype(BF16)

    qm = qm_ref[0].astype(F32)
    mem_tiles = []
    for t in range(MEM_WIDTH // LANES):
        qt = qm[:, t * LANES:(t + 1) * LANES]
        mk = mk_scr[:, t * LANES:(t + 1) * LANES]
        mv = mv_scr[:, t * LANES:(t + 1) * LANES]
        halves = []
        for first in (True, False):
            qh = jnp.where(low if first else ~low, qt, 0.0).astype(BF16)
            s = lax.dot_general(qh, mk, nt_dims, preferred_element_type=F32)
            p = jnp.exp(s - jnp.max(s, axis=1, keepdims=True))
            o = jnp.dot(p.astype(BF16), mv, preferred_element_type=F32)
            halves.append(o / jnp.sum(p, axis=1, keepdims=True))
        mem_tiles.append(jnp.where(low, halves[0], halves[1]))
    o_mem = jnp.concatenate(mem_tiles, axis=1)

    c_att = ca_ref[0]
    c_ssm = (os_ref[...] * gs_ref[0]).astype(BF16)
    c_mem = (o_mem * gm_ref[0]).astype(BF16)
    a1, a2 = ATT_WIDTH, ATT_WIDTH + SSM_WIDTH
    sub = (jnp.dot(c_att, wo_ref[:a1, :], preferred_element_type=F32)
           + jnp.dot(c_ssm, wo_ref[a1:a2, :], preferred_element_type=F32)
           + jnp.dot(c_mem, wo_ref[a2:, :], preferred_element_type=F32))
    h = DN_ALPHA * x_ref[0] + sub
    mu = jnp.mean(h, axis=1, keepdims=True)
    d = h - mu
    var = jnp.mean(d * d, axis=1, keepdims=True)
    out_ref[0] = d * lax.rsqrt(var + LN_EPS) * lg_ref[...] + lb_ref[...]


def _combine(x, c_att, o_ssm_tm, g_ssm, qm, g_mem, mem, w_mem_kv_bf16, wo_bf16, ln_g, ln_b):
    batch, seq, _ = x.shape
    tile = COMBINE_TILE
    tok = lambda w: pl.BlockSpec((1, tile, w), lambda b, j: (b, j, 0))
    time_major = pl.BlockSpec((tile, SSM_WIDTH), lambda b, j: (j, b))
    per_b = pl.BlockSpec((1, N_MEM, D_MODEL), lambda b, j: (b, 0, 0))
    const = lambda shape: pl.BlockSpec(shape, lambda b, j: tuple(0 for _ in shape))
    return pl.pallas_call(
        _combine_kernel,
        grid=(batch, seq // tile),
        in_specs=[tok(D_MODEL), tok(ATT_WIDTH), time_major, tok(SSM_WIDTH),
                  tok(MEM_WIDTH), tok(MEM_WIDTH), per_b, const((D_MODEL, 2 * MEM_WIDTH)),
                  const((D_MODEL, D_MODEL)), const((1, D_MODEL)), const((1, D_MODEL))],
        out_specs=tok(D_MODEL),
        out_shape=jax.ShapeDtypeStruct((batch, seq, D_MODEL), x.dtype),
        scratch_shapes=[pltpu.VMEM((N_MEM, MEM_WIDTH), BF16),
                        pltpu.VMEM((N_MEM, MEM_WIDTH), BF16)],
        compiler_params=pltpu.CompilerParams(dimension_semantics=("arbitrary", "arbitrary"),
                                             vmem_limit_bytes=VMEM_LIMIT_BYTES),
        name="combine",
    )(x, c_att, o_ssm_tm, g_ssm, qm, g_mem, mem, w_mem_kv_bf16, wo_bf16, ln_g, ln_b)


def _reordered_w_in(w_in):
    offs = np.cumsum([0, ATT_WIDTH, KV_WIDTH, KV_WIDTH, IDX_WIDTH, IDX_DIM, IDX_HEADS,
                      ATT_WIDTH, SSM_WIDTH, SSM_WIDTH, MEM_WIDTH, MEM_WIDTH])
    part = lambda n: w_in[:, int(offs[n]):int(offs[n + 1])]
    k_idx = part(4)
    w_idx = part(5)
    zeros = jnp.zeros((D_MODEL, LANES - IDX_HEADS), w_in.dtype)
    run = lambda first, last: w_in[:, int(offs[first]):int(offs[last + 1])]
    cols = [_pair_heads(part(0), 1), run(1, 3),
            jnp.tile(k_idx, (1, LANES // IDX_DIM)),
            w_idx, zeros,
            _pair_heads(part(6), 1), run(7, 10)]
    return jnp.concatenate(cols, axis=1).astype(BF16)


def kernel(x, mem, w_in, w_mem_kv, lam_re, lam_im, log_dt, b_re, b_im, c_re, c_im, d_skip, w_glu, b_glu,
           w_out, ln_g, ln_b):
    batch, seq, _ = x.shape
    assert seq % TOKEN_TILE == 0 and seq % COMBINE_TILE == 0 and seq % Q_TILE == 0 and seq % TIME_CHUNK == 0

    w_all = _reordered_w_in(w_in)
    wo = jnp.concatenate([_pair_heads(w_out[:ATT_WIDTH], 0), w_out[ATT_WIDTH:]], axis=0).astype(BF16)
    bmat, cmat, a_re, a_im = _s5_params(lam_re, lam_im, log_dt, b_re, b_im, c_re, c_im)
    tabs = _rope_tables(seq, HEAD_DIM) + _rope_tables(seq, IDX_DIM)

    q_t, k, vt, qi_t, ki, wi_t, g_att, u_tm, g_ssm, qm, g_mem = _inproj(x, w_all, tabs)
    c_att = _dsa(q_t, qi_t, wi_t, k, vt, ki, g_att)

    o_ssm_tm = _s5(u_tm.reshape(seq, batch, SSM_WIDTH), bmat, cmat, a_re, a_im,
                   d_skip.reshape(1, SSM_WIDTH).astype(F32), w_glu.astype(BF16),
                   b_glu.reshape(1, SSM_WIDTH).astype(F32))
    o_ssm = o_ssm_tm.reshape(seq, batch * SSM_WIDTH)

    return _combine(x, c_att, o_ssm, g_ssm, qm, g_mem, mem, w_mem_kv.astype(BF16), wo,
                    ln_g.reshape(1, D_MODEL).astype(F32), ln_b.reshape(1, D_MODEL).astype(F32))
```

```python
import functools
import math

import jax
import jax.numpy as jnp
import numpy as np
from jax import lax
from jax.experimental import pallas as pl
from jax.experimental.pallas import tpu as pltpu

F32 = jnp.float32
BF16 = jnp.bfloat16
I32 = jnp.int32

D_MODEL = 1024
N_MEM = 256
HEAD_DIM = 64
ATT_HEADS = 8
ATT_KV_HEADS = 2
ATT_REP = ATT_HEADS // ATT_KV_HEADS
ATT_WIDTH = ATT_HEADS * HEAD_DIM
KV_WIDTH = ATT_KV_HEADS * HEAD_DIM
IDX_HEADS = 8
IDX_DIM = 32
IDX_WIDTH = IDX_HEADS * IDX_DIM
TOPK_MAX = 256
SSM_WIDTH = D_MODEL // 4
SSM_GROUP = 16
SSM_GROUPS = SSM_WIDTH // SSM_GROUP
SSM_STATE = 64
SSM_LANES = SSM_GROUPS * SSM_STATE
MEM_HEADS = 4
MEM_WIDTH = MEM_HEADS * HEAD_DIM
ROPE_THETA = 500000.0
ROPE_FRAC = 4
LN_EPS = 1e-5
DEPTH = 1
DN_ALPHA = (2.0 * DEPTH) ** 0.25
ATT_SCALE = HEAD_DIM ** -0.5
IDX_SCALE = IDX_HEADS ** -0.5 * IDX_DIM ** -0.5
LOG2_E = math.log2(math.e)

LANES = 128
SUBLANES = 8
BF16_ROWS = 16
VMEM_LIMIT_BYTES = 48 * 1024 * 1024
TOKEN_TILE = 512
COMBINE_TILE = 1024
Q_TILE = 256
KEY_CHUNK = 256
COUNT_ROWS = 64
VT_ROWS = KV_WIDTH + BF16_ROWS
TIME_CHUNK = 128
SCAN_SPLIT = 2
SCAN_LANES = 256
NEG_BIG = -1e30
DENOM_FLOOR = 2.0 ** -100
SEARCH_MAX_PASSES = 24
SEARCH_FIRST_PASSES = 13
SEARCH_GROUP = 2
SEARCH_BISECT_PASSES = 2
SEARCH_CLIP = 0.1

INT_MIN = -2147483648
KEY_POS_INF = 0x7F800000
KEY_NEG_INF = INT_MIN + 0x7FFFFF

_G_Q, _G_K, _G_V, _G_QI, _G_KI, _G_W, _G_GA, _G_U, _G_GS, _G_QM, _G_GM = range(11)
_GROUP_WIDTHS = [ATT_WIDTH, KV_WIDTH, KV_WIDTH, IDX_WIDTH, LANES, LANES, ATT_WIDTH,
                 SSM_WIDTH, SSM_WIDTH, MEM_WIDTH, MEM_WIDTH]
_GROUP_OFFS = [int(v) for v in np.cumsum([0] + _GROUP_WIDTHS)]
IN_COLS = _GROUP_OFFS[-1]


def _pair_heads(w, axis):
    shape = w.shape
    split = shape[:axis] + (ATT_KV_HEADS, ATT_REP, HEAD_DIM) + shape[axis + 1:]
    return jnp.swapaxes(w.reshape(split), axis, axis + 1).reshape(shape)


def _rope_tables(seq, period, dtype=F32):
    r = period // ROPE_FRAC
    half = r // 2
    inv = ROPE_THETA ** (-jnp.arange(0, half, dtype=F32) * 2.0 / r)
    ang = jnp.arange(seq).astype(F32)[:, None] * inv[None, :]
    cos, sin = jnp.cos(ang), jnp.sin(ang)
    ones = jnp.ones((seq, period - r), F32)
    zeros = jnp.zeros((seq, period - r), F32)
    c = jnp.concatenate([cos, cos, ones], axis=1)
    s = jnp.concatenate([-sin, sin, zeros], axis=1)
    reps = LANES // period
    return jnp.tile(c, (1, reps)).astype(dtype), jnp.tile(s, (1, reps)).astype(dtype)


def _rope(z, cos_t, sin_t, period):
    width = z.shape[1]
    half = period // ROPE_FRAC // 2
    reps = width // LANES
    if reps > 1:
        cos_t = jnp.concatenate([cos_t] * reps, axis=1)
        sin_t = jnp.concatenate([sin_t] * reps, axis=1)
    lane = lax.broadcasted_iota(I32, z.shape, 1)
    first = (lane & (period - 1)) < half
    up = pltpu.roll(z, half, 1)
    down = pltpu.roll(z, width - half, 1)
    return z * cos_t + jnp.where(first, down, up) * sin_t


def _inproj_kernel(x_ref, w_ref, ca_ref, sa_ref, ci_ref, si_ref,
                   q_ref, k_ref, vt_ref, qi_ref, ki_ref, wi_ref, ga_ref, u_ref, gs_ref, qm_ref, gm_ref):
    tile = x_ref.shape[1]
    pos0 = pl.multiple_of(pl.program_id(1) * tile, tile)
    xb = x_ref[0].astype(BF16)

    def proj(group, last=None):
        lo, hi = _GROUP_OFFS[group], _GROUP_OFFS[(group if last is None else last) + 1]
        return jnp.dot(xb, w_ref[:, lo:hi], preferred_element_type=F32)

    ca = ca_ref[pl.ds(pos0, tile), :]
    sa = sa_ref[pl.ds(pos0, tile), :]
    ci = ci_ref[pl.ds(pos0, tile), :]
    si = si_ref[pl.ds(pos0, tile), :]

    def store_transposed(ref, z):
        for qb in range(tile // Q_TILE):
            for t in range(z.shape[1] // LANES):
                zt = z[qb * Q_TILE:(qb + 1) * Q_TILE, t * LANES:(t + 1) * LANES].T
                ref[0, qb, :, t * Q_TILE:(t + 1) * Q_TILE] = zt.astype(ref.dtype)

    store_transposed(q_ref, _rope(proj(_G_Q), ca, sa, HEAD_DIM) * (ATT_SCALE * LOG2_E))
    kv = proj(_G_K, _G_V)
    k_ref[0] = _rope(kv[:, :KV_WIDTH], ca, sa, HEAD_DIM).astype(BF16)
    v = kv[:, KV_WIDTH:]
    for c in range(tile // KEY_CHUNK):
        vt_ref[0, c, :KV_WIDTH, :] = v[c * KEY_CHUNK:(c + 1) * KEY_CHUNK, :].T.astype(BF16)
        vt_ref[0, c, KV_WIDTH:, :] = jnp.ones((BF16_ROWS, KEY_CHUNK), BF16)
    store_transposed(qi_ref, _rope(proj(_G_QI), ci, si, IDX_DIM))
    kiw = proj(_G_KI, _G_W)
    ki_ref[0] = _rope(kiw[:, :LANES], ci, si, IDX_DIM).astype(BF16)
    wi = kiw[:, LANES:] * IDX_SCALE
    for qb in range(tile // Q_TILE):
        wi_ref[0, qb] = wi[qb * Q_TILE:(qb + 1) * Q_TILE, :].T[:IDX_HEADS, :]
    ga_ref[0] = jax.nn.silu(proj(_G_GA))
    u_ref[...] = proj(_G_U)
    gs_ref[0] = jax.nn.silu(proj(_G_GS))
    qm_ref[0] = (proj(_G_QM) * ATT_SCALE).astype(BF16)
    gm_ref[0] = jax.nn.silu(proj(_G_GM))


def _inproj(x, w_bf16, tabs):
    batch, seq, _ = x.shape
    tile = TOKEN_TILE
    widths = _GROUP_WIDTHS
    dtypes = [BF16, BF16, BF16, BF16, BF16, F32, F32, F32, F32, BF16, F32]
    tab_spec = pl.BlockSpec((seq, LANES), lambda b, j: (0, 0))
    out_specs = [pl.BlockSpec((1, tile, w), lambda b, j: (b, j, 0)) for w in widths]
    out_shape = [jax.ShapeDtypeStruct((batch, seq, w), d) for w, d in zip(widths, dtypes)]
    out_specs[_G_V] = pl.BlockSpec((1, tile // KEY_CHUNK, VT_ROWS, KEY_CHUNK), lambda b, j: (b, j, 0, 0))
    out_shape[_G_V] = jax.ShapeDtypeStruct((batch, seq // KEY_CHUNK, VT_ROWS, KEY_CHUNK), BF16)
    for group, rows, dtype in ((_G_Q, LANES, BF16), (_G_QI, LANES, BF16), (_G_W, IDX_HEADS, F32)):
        cols = Q_TILE * max(widths[group] // LANES, 1)
        out_specs[group] = pl.BlockSpec((1, tile // Q_TILE, rows, cols), lambda b, j: (b, j, 0, 0))
        out_shape[group] = jax.ShapeDtypeStruct((batch, seq // Q_TILE, rows, cols), dtype)
    out_specs[_G_U] = pl.BlockSpec((tile, SSM_WIDTH), lambda b, j: (j, b))
    out_shape[_G_U] = jax.ShapeDtypeStruct((seq, batch * SSM_WIDTH), F32)
    return pl.pallas_call(
        _inproj_kernel,
        grid=(batch, seq // tile),
        in_specs=[pl.BlockSpec((1, tile, D_MODEL), lambda b, j: (b, j, 0)),
                  pl.BlockSpec((D_MODEL, IN_COLS), lambda b, j: (0, 0)),
                  tab_spec, tab_spec, tab_spec, tab_spec],
        out_specs=out_specs,
        out_shape=out_shape,
        compiler_params=pltpu.CompilerParams(dimension_semantics=("arbitrary", "arbitrary"),
                                             vmem_limit_bytes=VMEM_LIMIT_BYTES),
        name="inproj",
    )(x, w_bf16, *tabs)


def _key_to_float(key):
    bits = jnp.where(key >= 0, key, key ^ 0x7FFFFFFF)
    val = lax.bitcast_convert_type(bits, F32)
    val = jnp.where(key <= KEY_NEG_INF, -jnp.inf, val)
    return jnp.where(key >= KEY_POS_INF, jnp.inf, val)


def _dsa_kernel(q_ref, qi_ref, wi_ref, k_ref, vt_ref, ki_ref, ga_ref, o_ref,
                score_scr, bias_scr, row_scr, s_scr, acc_scr, og_scr, *, topk):
    i = pl.program_id(1)
    n_chunks = ((i + 1) * Q_TILE + KEY_CHUNK - 1) // KEY_CHUNK
    q0 = i * Q_TILE
    kf = float(topk)
    att_cols = ATT_REP * Q_TILE
    groups = KEY_CHUNK // SUBLANES

    def chunk_start(c):
        return pl.multiple_of(c * KEY_CHUNK, KEY_CHUNK)

    def over_chunk_pairs(body, init):
        def pair(c2, carry):
            return body(2 * c2 + 1, body(2 * c2, carry))
        carry = lax.fori_loop(0, n_chunks // 2, pair, init)
        return lax.cond(n_chunks % 2 == 1, lambda cr: body(n_chunks - 1, cr), lambda cr: cr, carry)

    def fold(v, op):
        return op(v.reshape(groups, SUBLANES, v.shape[-1]), axis=0)

    def only_rows(x, lo, n):
        parts = [jnp.zeros((lo, x.shape[1]), x.dtype)] if lo else []
        parts.append(x[lo:lo + n])
        if lo + n < x.shape[0]:
            parts.append(jnp.zeros((x.shape[0] - lo - n, x.shape[1]), x.dtype))
        return jnp.concatenate(parts, axis=0)

    qi_t = qi_ref[0, 0]
    per_tile = LANES // IDX_DIM

    def head_cols(h):
        t = h // per_tile
        return only_rows(qi_t[:, t * Q_TILE:(t + 1) * Q_TILE], (h % per_tile) * IDX_DIM, IDX_DIM)

    rhs_pairs = [jnp.concatenate([head_cols(2 * p), head_cols(2 * p + 1)], axis=1)
                 for p in range(IDX_HEADS // 2)]
    w_rows = [wi_ref[0, 0, h:h + 1, :] for h in range(IDX_HEADS)]
    q_t = q_ref[0, 0]
    qg_t = [only_rows(q_t, g * HEAD_DIM, HEAD_DIM) for g in range(ATT_KV_HEADS)]
    krow = lax.broadcasted_iota(I32, (KEY_CHUNK, Q_TILE), 0)
    kq = q0 + lax.broadcasted_iota(I32, (KEY_CHUNK, Q_TILE), 1)

    def matmul_chunk(c, stats):
        mx_a, mn_a, ge_a, gt_a, top_a = stats
        ks = chunk_start(c)
        kc = ki_ref[0, pl.ds(ks, KEY_CHUNK), :]
        acc = jnp.zeros((KEY_CHUNK, Q_TILE), F32)
        for p in range(IDX_HEADS // 2):
            s2 = jnp.dot(kc, rhs_pairs[p], preferred_element_type=F32)
            acc = acc + w_rows[2 * p] * jnp.maximum(s2[:, :Q_TILE], 0.0)
            acc = acc + w_rows[2 * p + 1] * jnp.maximum(s2[:, Q_TILE:], 0.0)
        causal = ks + krow <= kq
        sc = jnp.where(causal, acc, -jnp.inf)
        score_scr[pl.ds(ks, KEY_CHUNK), :] = sc
        kk = k_ref[0, pl.ds(ks, KEY_CHUNK), :]
        tops = []
        for g in range(ATT_KV_HEADS):
            s = jnp.dot(kk, qg_t[g], preferred_element_type=F32)
            s_scr[g, pl.ds(ks, KEY_CHUNK), :] = s
            tops.append(jnp.maximum(top_a[g], fold(s, jnp.max)))
        return (jnp.maximum(mx_a, fold(sc, jnp.max)),
                jnp.minimum(mn_a, fold(jnp.where(causal, acc, jnp.inf), jnp.min)),
                ge_a + fold(jnp.where(sc >= 0.0, 1.0, 0.0), jnp.sum),
                gt_a + fold(jnp.where(sc > 0.0, 1.0, 0.0), jnp.sum),
                tuple(tops))

    stat0 = lambda v: jnp.full((SUBLANES, Q_TILE), v, F32)
    top0 = tuple(jnp.full((SUBLANES, att_cols), NEG_BIG, F32) for _ in range(ATT_KV_HEADS))
    mx_a, mn_a, ge_a, gt_a, top_a = over_chunk_pairs(
        matmul_chunk, (stat0(-jnp.inf), stat0(jnp.inf), stat0(0.0), stat0(0.0), top0))
    m_top = [jnp.max(top_a[g], axis=0, keepdims=True) for g in range(ATT_KV_HEADS)]
    s_max = jnp.max(mx_a, axis=0, keepdims=True)
    s_min = jnp.min(mn_a, axis=0, keepdims=True)
    n_ge0 = jnp.sum(ge_a, axis=0, keepdims=True)
    n_gt0 = jnp.sum(gt_a, axis=0, keepdims=True)
    n_causal = (q0 + 1 + lax.broadcasted_iota(I32, (1, Q_TILE), 1)).astype(F32)

    def count(pred):
        def body(c, acc):
            ks = chunk_start(c)
            for part in range(KEY_CHUNK // LANES):
                hit = jnp.where(pred(score_scr[pl.ds(ks + part * LANES, LANES), :]), 1.0, 0.0)
                acc = acc + (hit[:COUNT_ROWS] + hit[COUNT_ROWS:])
            return acc
        acc = over_chunk_pairs(body, jnp.zeros((COUNT_ROWS, Q_TILE), F32))
        rows = COUNT_ROWS
        while rows > SUBLANES:
            rows //= 2
            acc = acc[:rows] + acc[rows:]
        return jnp.sum(acc, axis=0, keepdims=True)

    def count_ge(value):
        vb = jnp.broadcast_to(value, (LANES, Q_TILE))
        return count(lambda sc: sc >= vb)

    zero_thr = (n_gt0 < kf) & (n_ge0 >= kf)
    keep_all = n_causal <= kf
    positive = n_gt0 >= kf
    settled = zero_thr | keep_all
    lo0 = jnp.where(settled, 0.0, jnp.where(positive, 0.0, s_min))
    hi0 = jnp.where(settled, 0.0, jnp.where(positive, s_max, 0.0))
    c_lo0 = jnp.where(positive, n_gt0, n_causal)
    c_hi0 = jnp.where(positive, 0.0, n_ge0)
    thr0 = jnp.where(keep_all, -jnp.inf, 0.0)
    done0 = jnp.where(settled, 1.0, 0.0)
    zero_ties = jnp.max(jnp.where(zero_thr & (n_ge0 > kf) & jnp.logical_not(keep_all), 1.0, 0.0)) > 0.0

    def search_pass(bisect, rows):
        lo, hi, c_lo, c_hi, thr, done = rows
        if bisect:
            frac = 0.5
        else:
            frac = jnp.clip((kf - c_hi + 0.5) / (c_lo - c_hi + 1.0), SEARCH_CLIP, 1.0 - SEARCH_CLIP)
        x = hi - (hi - lo) * frac
        c = count_ge(x)
        live = done == 0.0
        above = live & (c >= kf)
        below = live & (c < kf)
        hit = live & (c == kf)
        return (jnp.where(above, x, lo), jnp.where(below, x, hi),
                jnp.where(above, c, c_lo), jnp.where(below, c, c_hi),
                jnp.where(hit, x, thr), jnp.where(hit, 1.0, done))

    def search_cond(state):
        return (state[0] < SEARCH_MAX_PASSES) & (state[1] > 0.0)

    def search_body(state):
        it, _, rows = state
        for _ in range(SEARCH_GROUP):
            rows = search_pass(False, rows)
        return it + SEARCH_GROUP, jnp.sum(1.0 - rows[5]), rows

    def first_passes(rows):
        for p in range(SEARCH_FIRST_PASSES):
            rows = search_pass(p < SEARCH_BISECT_PASSES, rows)
        return rows

    rows0 = (lo0, hi0, c_lo0, c_hi0, thr0, done0)
    rows1 = lax.cond((i + 1) * Q_TILE > topk, first_passes, lambda rows: rows, rows0)
    state = lax.while_loop(search_cond, search_body,
                           (jnp.int32(SEARCH_FIRST_PASSES), jnp.sum(1.0 - rows1[5]), rows1))
    pending, thr_s, done_s = state[1], state[2][4], state[2][5]
    row_scr[0:1, :] = thr_s
    row_scr[1:2, :] = jnp.where(zero_thr, n_gt0, 0.0)
    row_scr[2:3, :] = jnp.where(zero_thr, n_ge0, kf)

    @pl.when(pending > 0.0)
    def _():
        def count_ge_key(key):
            return count_ge(_key_to_float(key))

        key0 = jnp.where(count_ge_key(jnp.zeros((1, Q_TILE), I32)) >= kf, 0, INT_MIN).astype(I32)

        def bit_step(b, key):
            cand = key | jnp.left_shift(jnp.int32(1), 30 - b)
            return jnp.where(count_ge_key(cand) >= kf, cand, key)

        thr_f = _key_to_float(lax.fori_loop(0, 31, bit_step, key0))
        thr_fb = jnp.broadcast_to(thr_f, (LANES, Q_TILE))
        open_row = done_s == 0.0
        row_scr[0:1, :] = jnp.where(open_row, thr_f, thr_s)
        row_scr[1:2, :] = jnp.where(open_row, count(lambda sc: sc > thr_fb), row_scr[1:2, :])
        row_scr[2:3, :] = jnp.where(open_row, count(lambda sc: sc >= thr_fb), row_scr[2:3, :])

    thr = row_scr[0:1, :]
    thr_c = jnp.broadcast_to(thr, (KEY_CHUNK, Q_TILE))
    any_tie = zero_ties | (pending > 0.0)

    def plain_bias(c, carry):
        ks = chunk_start(c)
        keep = (score_scr[pl.ds(ks, KEY_CHUNK), :] >= thr_c) & (ks + krow <= kq)
        return jnp.where(keep, 0.0, NEG_BIG), carry

    def make_tie_bias():
        need = jnp.broadcast_to(kf - row_scr[1:2, :], (KEY_CHUNK, Q_TILE))
        lower = (lax.broadcasted_iota(I32, (KEY_CHUNK, KEY_CHUNK), 1)
                 <= lax.broadcasted_iota(I32, (KEY_CHUNK, KEY_CHUNK), 0))
        prefix_mat = jnp.where(lower, 1.0, 0.0).astype(BF16)

        def tie_bias(c, seen):
            ks = chunk_start(c)
            sc = score_scr[pl.ds(ks, KEY_CHUNK), :]
            tied = sc == thr_c
            rank = seen + jnp.dot(prefix_mat, jnp.where(tied, 1.0, 0.0).astype(BF16),
                                  preferred_element_type=F32)
            chosen = (sc > thr_c) | (tied & (rank <= need))
            return jnp.where(chosen & (ks + krow <= kq), 0.0, NEG_BIG), rank[KEY_CHUNK - 1:KEY_CHUNK, :]

        return tie_bias

    no_ties_seen = jnp.zeros((1, Q_TILE), F32)

    def pv_pass(bias_of, m, carry0):
        acc_scr[...] = jnp.zeros(acc_scr.shape, F32)

        def pv_chunk(c, carry):
            bias, carry = bias_of(c, carry)
            bias4 = jnp.concatenate([bias] * ATT_REP, axis=1)
            for g in range(ATT_KV_HEADS):
                p = jnp.exp2(s_scr[g, pl.ds(chunk_start(c), KEY_CHUNK), :] + bias4 - m[g])
                acc_scr[g] += jnp.dot(vt_ref[0, c], p.astype(BF16), preferred_element_type=F32)
            return carry

        over_chunk_pairs(pv_chunk, carry0)

    @pl.when(jnp.logical_not(any_tie))
    def _():
        pv_pass(plain_bias, m_top, jnp.int32(0))

    @pl.when(any_tie)
    def _():
        pv_pass(make_tie_bias(), m_top, no_ties_seen)

    def write_output():
        for g in range(ATT_KV_HEADS):
            denom = acc_scr[g, KV_WIDTH:KV_WIDTH + 1, :]
            og_scr[g * HEAD_DIM:(g + 1) * HEAD_DIM, :] = acc_scr[g, g * HEAD_DIM:(g + 1) * HEAD_DIM, :] / denom
        for j in range(ATT_REP):
            gated = og_scr[:, j * Q_TILE:(j + 1) * Q_TILE].T * ga_ref[0, :, j * LANES:(j + 1) * LANES]
            o_ref[0, :, j * LANES:(j + 1) * LANES] = gated.astype(o_ref.dtype)

    d_min = jnp.min(jnp.minimum(acc_scr[0, KV_WIDTH:KV_WIDTH + 1, :], acc_scr[1, KV_WIDTH:KV_WIDTH + 1, :]))
    write_output()

    @pl.when(jnp.logical_not(d_min >= DENOM_FLOOR))
    def _():
        tie_bias = make_tie_bias()

        def mask_chunk(c, carry):
            seen, m_acc = carry
            ks = chunk_start(c)
            bias, seen = tie_bias(c, seen)
            bias_scr[pl.ds(ks, KEY_CHUNK), :] = bias
            bias4 = jnp.concatenate([bias] * ATT_REP, axis=1)
            return seen, tuple(jnp.maximum(m_acc[g], fold(s_scr[g, pl.ds(ks, KEY_CHUNK), :] + bias4, jnp.max))
                               for g in range(ATT_KV_HEADS))

        m_init = tuple(jnp.full((SUBLANES, att_cols), NEG_BIG, F32) for _ in range(ATT_KV_HEADS))
        _, m_acc = lax.fori_loop(0, n_chunks, mask_chunk, (no_ties_seen, m_init))
        m_sel = [jnp.max(m_acc[g], axis=0, keepdims=True) for g in range(ATT_KV_HEADS)]
        pv_pass(lambda c, carry: (bias_scr[pl.ds(chunk_start(c), KEY_CHUNK), :], carry), m_sel, jnp.int32(0))
        write_output()


def _dsa(q_t, qi_t, wi_t, k, vt, ki, gate):
    batch, seq, _ = k.shape
    topk = min(TOPK_MAX, seq // 4)
    att_cols = ATT_REP * Q_TILE
    per_q = lambda a: pl.BlockSpec((1, 1) + a.shape[2:], lambda b, i: (b, i, 0, 0))
    per_b = lambda w: pl.BlockSpec((1, seq, w), lambda b, i: (b, 0, 0))
    vt_spec = pl.BlockSpec((1, seq // KEY_CHUNK, VT_ROWS, KEY_CHUNK), lambda b, i: (b, 0, 0, 0))
    tok_spec = pl.BlockSpec((1, Q_TILE, ATT_WIDTH), lambda b, i: (b, i, 0))
    return pl.pallas_call(
        functools.partial(_dsa_kernel, topk=topk),
        grid=(batch, seq // Q_TILE),
        in_specs=[per_q(q_t), per_q(qi_t), per_q(wi_t), per_b(KV_WIDTH), vt_spec, per_b(LANES), tok_spec],
        out_specs=tok_spec,
        out_shape=jax.ShapeDtypeStruct((batch, seq, ATT_WIDTH), BF16),
        scratch_shapes=[pltpu.VMEM((seq, Q_TILE), F32),
                        pltpu.VMEM((seq, Q_TILE), F32),
                        pltpu.VMEM((SUBLANES, Q_TILE), F32),
                        pltpu.VMEM((ATT_KV_HEADS, seq, att_cols), F32),
                        pltpu.VMEM((ATT_KV_HEADS, VT_ROWS, att_cols), F32),
                        pltpu.VMEM((KV_WIDTH, att_cols), F32)],
        compiler_params=pltpu.CompilerParams(dimension_semantics=("arbitrary", "arbitrary"),
                                             vmem_limit_bytes=VMEM_LIMIT_BYTES),
        name="dsa",
    )(q_t, qi_t, wi_t, k, vt, ki, gate)


def _s5_kernel(u_ref, bm_ref, cm_ref, are_ref, aim_ref, dsk_ref, wg_ref, bg_ref, o_ref,
               st_scr, xb_scr, carry_scr):
    tc, batch, width = u_ref.shape
    steps = tc // SCAN_SPLIT
    rows = steps * batch

    @pl.when(pl.program_id(0) == 0)
    def _():
        carry_scr[...] = jnp.zeros(carry_scr.shape, F32)

    u = [u_ref[h * steps:(h + 1) * steps].reshape(rows, width) for h in range(SCAN_SPLIT)]
    for h in range(SCAN_SPLIT):
        st_scr[h] = jnp.dot(u[h].astype(BF16), bm_ref[...], preferred_element_type=F32)

    parts = []
    for part in range(SSM_LANES // SCAN_LANES):
        re_lo = part * SCAN_LANES
        im_lo = SSM_LANES + re_lo
        parts.append((slice(re_lo, re_lo + SCAN_LANES), slice(im_lo, im_lo + SCAN_LANES)))
    state = [(carry_scr[:, re], carry_scr[:, im]) for re, im in parts]

    for h in range(SCAN_SPLIT):
        for p, (re, im) in enumerate(parts):
            a_re = jnp.broadcast_to(are_ref[:, re], (batch, SCAN_LANES))
            a_im = jnp.broadcast_to(aim_ref[:, re], (batch, SCAN_LANES))
            x_re, x_im = state[p]
            for t in range(steps):
                r = slice(t * batch, (t + 1) * batch)
                x_re, x_im = (a_re * x_re - a_im * x_im + st_scr[h, r, re],
                              a_re * x_im + a_im * x_re + st_scr[h, r, im])
                xb_scr[h, r, re] = x_re.astype(BF16)
                xb_scr[h, r, im] = x_im.astype(BF16)
            state[p] = (x_re, x_im)
        y = jnp.dot(xb_scr[h], cm_ref[...], preferred_element_type=F32) + dsk_ref[...] * u[h]
        y = jax.nn.gelu(y)
        gate = jax.nn.sigmoid(jnp.dot(y.astype(BF16), wg_ref[...], preferred_element_type=F32) + bg_ref[...])
        o_ref[h * steps:(h + 1) * steps] = (y * gate).reshape(steps, batch, width)

    for (re, im), (x_re, x_im) in zip(parts, state):
        carry_scr[:, re] = x_re
        carry_scr[:, im] = x_im


def _s5(u_tm, bmat, cmat, a_re, a_im, dskip, w_glu, b_glu):
    seq, batch, width = u_tm.shape
    tc = TIME_CHUNK
    full = lambda shape: pl.BlockSpec(shape, lambda t: tuple(0 for _ in shape))
    return pl.pallas_call(
        _s5_kernel,
        grid=(seq // tc,),
        in_specs=[pl.BlockSpec((tc, batch, width), lambda t: (t, 0, 0)),
                  full(bmat.shape), full(cmat.shape), full(a_re.shape), full(a_im.shape),
                  full(dskip.shape), full(w_glu.shape), full(b_glu.shape)],
        out_specs=pl.BlockSpec((tc, batch, width), lambda t: (t, 0, 0)),
        out_shape=jax.ShapeDtypeStruct((seq, batch, width), F32),
        scratch_shapes=[pltpu.VMEM((SCAN_SPLIT, tc // SCAN_SPLIT * batch, 2 * SSM_LANES), F32),
                        pltpu.VMEM((SCAN_SPLIT, tc // SCAN_SPLIT * batch, 2 * SSM_LANES), BF16),
                        pltpu.VMEM((batch, 2 * SSM_LANES), F32)],
        compiler_params=pltpu.CompilerParams(dimension_semantics=("arbitrary",),
                                             vmem_limit_bytes=VMEM_LIMIT_BYTES),
        name="s5",
    )(u_tm, bmat, cmat, a_re, a_im, dskip, w_glu, b_glu)


def _s5_params(lam_re, lam_im, log_dt, b_re, b_im, c_re, c_im):
    dt = jnp.exp(log_dt.astype(F32))[:, None]
    mag = jnp.exp(lam_re.astype(F32) * dt)
    ang = lam_im.astype(F32) * dt
    lb_re, lb_im = mag * jnp.cos(ang), mag * jnp.sin(ang)
    den = lam_re * lam_re + lam_im * lam_im
    k_re = ((lb_re - 1.0) * lam_re + lb_im * lam_im) / den
    k_im = (lb_im * lam_re - (lb_re - 1.0) * lam_im) / den
    bb_re = k_re[:, :, None] * b_re - k_im[:, :, None] * b_im
    bb_im = k_re[:, :, None] * b_im + k_im[:, :, None] * b_re
    same_group = (np.arange(SSM_WIDTH)[:, None] // SSM_GROUP) == (np.arange(SSM_LANES)[None, :] // SSM_STATE)

    def drive(bb):
        per_group = jnp.swapaxes(bb, 1, 2).reshape(SSM_WIDTH, SSM_STATE)
        return jnp.where(same_group, jnp.tile(per_group, (1, SSM_GROUPS)), 0.0)

    def read(cc):
        per_group = jnp.swapaxes(cc, 1, 2).reshape(SSM_LANES, SSM_GROUP)
        return jnp.where(same_group.T, jnp.tile(per_group, (1, SSM_GROUPS)), 0.0)

    bmat = jnp.concatenate([drive(bb_re), drive(bb_im)], axis=1)
    cmat = jnp.concatenate([read(c_re.astype(F32)), read(-c_im.astype(F32))], axis=0)
    return (bmat.astype(BF16), cmat.astype(BF16),
            lb_re.reshape(1, SSM_LANES), lb_im.reshape(1, SSM_LANES))


def _combine_kernel(x_ref, ca_ref, os_ref, gs_ref, qm_ref, gm_ref, mem_ref, wkv_ref,
                    wo_ref, lg_ref, lb_ref, out_ref, mk_scr, mv_scr):
    tile = x_ref.shape[1]
    nt_dims = (((1,), (1,)), ((), ()))
    lane = lax.broadcasted_iota(I32, (tile, LANES), 1)
    low = lane < HEAD_DIM

    @pl.when(pl.program_id(1) == 0)
    def _():
        mkv = jnp.dot(mem_ref[0].astype(BF16), wkv_ref[...], preferred_element_type=F32)
        mk_scr[...] = mkv[:, :MEM_WIDTH].astype(BF16)
        mv_scr[...] = mkv[:, MEM_WIDTH:].astype(BF16)

    qm = qm_ref[0].astype(F32)
    mem_tiles = []
    for t in range(MEM_WIDTH // LANES):
        qt = qm[:, t * LANES:(t + 1) * LANES]
        mk = mk_scr[:, t * LANES:(t + 1) * LANES]
        mv = mv_scr[:, t * LANES:(t + 1) * LANES]
        halves = []
        for first in (True, False):
            qh = jnp.where(low if first else ~low, qt, 0.0).astype(BF16)
            s = lax.dot_general(qh, mk, nt_dims, preferred_element_type=F32)
            p = jnp.exp(s - jnp.max(s, axis=1, keepdims=True))
            o = jnp.dot(p.astype(BF16), mv, preferred_element_type=F32)
            halves.append(o / jnp.sum(p, axis=1, keepdims=True))
        mem_tiles.append(jnp.where(low, halves[0], halves[1]))
    o_mem = jnp.concatenate(mem_tiles, axis=1)

    c_att = ca_ref[0]
    c_ssm = (os_ref[...] * gs_ref[0]).astype(BF16)
    c_mem = (o_mem * gm_ref[0]).astype(BF16)
    a1, a2 = ATT_WIDTH, ATT_WIDTH + SSM_WIDTH
    sub = (jnp.dot(c_att, wo_ref[:a1, :], preferred_element_type=F32)
           + jnp.dot(c_ssm, wo_ref[a1:a2, :], preferred_element_type=F32)
           + jnp.dot(c_mem, wo_ref[a2:, :], preferred_element_type=F32))
    h = DN_ALPHA * x_ref[0] + sub
    mu = jnp.mean(h, axis=1, keepdims=True)
    d = h - mu
    var = jnp.mean(d * d, axis=1, keepdims=True)
    out_ref[0] = d * lax.rsqrt(var + LN_EPS) * lg_ref[...] + lb_ref[...]


def _combine(x, c_att, o_ssm_tm, g_ssm, qm, g_mem, mem, w_mem_kv_bf16, wo_bf16, ln_g, ln_b):
    batch, seq, _ = x.shape
    tile = COMBINE_TILE
    tok = lambda w: pl.BlockSpec((1, tile, w), lambda b, j: (b, j, 0))
    time_major = pl.BlockSpec((tile, SSM_WIDTH), lambda b, j: (j, b))
    per_b = pl.BlockSpec((1, N_MEM, D_MODEL), lambda b, j: (b, 0, 0))
    const = lambda shape: pl.BlockSpec(shape, lambda b, j: tuple(0 for _ in shape))
    return pl.pallas_call(
        _combine_kernel,
        grid=(batch, seq // tile),
        in_specs=[tok(D_MODEL), tok(ATT_WIDTH), time_major, tok(SSM_WIDTH),
                  tok(MEM_WIDTH), tok(MEM_WIDTH), per_b, const((D_MODEL, 2 * MEM_WIDTH)),
                  const((D_MODEL, D_MODEL)), const((1, D_MODEL)), const((1, D_MODEL))],
        out_specs=tok(D_MODEL),
        out_shape=jax.ShapeDtypeStruct((batch, seq, D_MODEL), x.dtype),
        scratch_shapes=[pltpu.VMEM((N_MEM, MEM_WIDTH), BF16),
                        pltpu.VMEM((N_MEM, MEM_WIDTH), BF16)],
        compiler_params=pltpu.CompilerParams(dimension_semantics=("arbitrary", "arbitrary"),
                                             vmem_limit_bytes=VMEM_LIMIT_BYTES),
        name="combine",
    )(x, c_att, o_ssm_tm, g_ssm, qm, g_mem, mem, w_mem_kv_bf16, wo_bf16, ln_g, ln_b)


def _reordered_w_in(w_in):
    offs = np.cumsum([0, ATT_WIDTH, KV_WIDTH, KV_WIDTH, IDX_WIDTH, IDX_DIM, IDX_HEADS,
                      ATT_WIDTH, SSM_WIDTH, SSM_WIDTH, MEM_WIDTH, MEM_WIDTH])
    part = lambda n: w_in[:, int(offs[n]):int(offs[n + 1])]
    k_idx = part(4)
    w_idx = part(5)
    zeros = jnp.zeros((D_MODEL, LANES - IDX_HEADS), w_in.dtype)
    run = lambda first, last: w_in[:, int(offs[first]):int(offs[last + 1])]
    cols = [_pair_heads(part(0), 1), run(1, 3),
            jnp.tile(k_idx, (1, LANES // IDX_DIM)),
            w_idx, zeros,
            _pair_heads(part(6), 1), run(7, 10)]
    return jnp.concatenate(cols, axis=1).astype(BF16)


def kernel(x, mem, w_in, w_mem_kv, lam_re, lam_im, log_dt, b_re, b_im, c_re, c_im, d_skip, w_glu, b_glu,
           w_out, ln_g, ln_b):
    batch, seq, _ = x.shape
    assert seq % TOKEN_TILE == 0 and seq % COMBINE_TILE == 0 and seq % Q_TILE == 0 and seq % TIME_CHUNK == 0

    w_all = _reordered_w_in(w_in)
    wo = jnp.concatenate([_pair_heads(w_out[:ATT_WIDTH], 0), w_out[ATT_WIDTH:]], axis=0).astype(BF16)
    bmat, cmat, a_re, a_im = _s5_params(lam_re, lam_im, log_dt, b_re, b_im, c_re, c_im)
    tabs = _rope_tables(seq, HEAD_DIM) + _rope_tables(seq, IDX_DIM)

    q_t, k, vt, qi_t, ki, wi_t, g_att, u_tm, g_ssm, qm, g_mem = _inproj(x, w_all, tabs)
    c_att = _dsa(q_t, qi_t, wi_t, k, vt, ki, g_att)

    o_ssm_tm = _s5(u_tm.reshape(seq, batch, SSM_WIDTH), bmat, cmat, a_re, a_im,
                   d_skip.reshape(1, SSM_WIDTH).astype(F32), w_glu.astype(BF16),
                   b_glu.reshape(1, SSM_WIDTH).astype(F32))
    o_ssm = o_ssm_tm.reshape(seq, batch * SSM_WIDTH)

    return _combine(x, c_att, o_ssm, g_ssm, qm, g_mem, mem, w_mem_kv.astype(BF16), wo,
                    ln_g.reshape(1, D_MODEL).astype(F32), ln_b.reshape(1, D_MODEL).astype(F32))
```

```python
import functools
import math

import jax
import jax.numpy as jnp
import numpy as np
from jax import lax
from jax.experimental import pallas as pl
from jax.experimental.pallas import tpu as pltpu

F32 = jnp.float32
BF16 = jnp.bfloat16
I32 = jnp.int32

D_MODEL = 1024
N_MEM = 256
HEAD_DIM = 64
ATT_HEADS = 8
ATT_KV_HEADS = 2
ATT_REP = ATT_HEADS // ATT_KV_HEADS
ATT_WIDTH = ATT_HEADS * HEAD_DIM
KV_WIDTH = ATT_KV_HEADS * HEAD_DIM
IDX_HEADS = 8
IDX_DIM = 32
IDX_WIDTH = IDX_HEADS * IDX_DIM
TOPK_MAX = 256
SSM_WIDTH = D_MODEL // 4
SSM_GROUP = 16
SSM_GROUPS = SSM_WIDTH // SSM_GROUP
SSM_STATE = 64
SSM_LANES = SSM_GROUPS * SSM_STATE
MEM_HEADS = 4
MEM_WIDTH = MEM_HEADS * HEAD_DIM
ROPE_THETA = 500000.0
ROPE_FRAC = 4
LN_EPS = 1e-5
DEPTH = 1
DN_ALPHA = (2.0 * DEPTH) ** 0.25
ATT_SCALE = HEAD_DIM ** -0.5
IDX_SCALE = IDX_HEADS ** -0.5 * IDX_DIM ** -0.5
LOG2_E = math.log2(math.e)

LANES = 128
SUBLANES = 8
BF16_ROWS = 16
VMEM_LIMIT_BYTES = 48 * 1024 * 1024
TOKEN_TILE = 512
COMBINE_TILE = 1024
Q_TILE = 256
KEY_CHUNK = 256
COUNT_ROWS = 64
VT_ROWS = KV_WIDTH + BF16_ROWS
TIME_CHUNK = 128
SCAN_SPLIT = 2
SCAN_LANES = 256
NEG_BIG = -1e30
DENOM_FLOOR = 2.0 ** -100
SEARCH_MAX_PASSES = 24
SEARCH_FIRST_PASSES = 13
SEARCH_GROUP = 2
SEARCH_BISECT_PASSES = 2
SEARCH_CLIP = 0.1

INT_MIN = -2147483648
KEY_POS_INF = 0x7F800000
KEY_NEG_INF = INT_MIN + 0x7FFFFF

_G_Q, _G_K, _G_V, _G_QI, _G_KI, _G_W, _G_GA, _G_U, _G_GS, _G_QM, _G_GM = range(11)
_GROUP_WIDTHS = [ATT_WIDTH, KV_WIDTH, KV_WIDTH, IDX_WIDTH, LANES, LANES, ATT_WIDTH,
                 SSM_WIDTH, SSM_WIDTH, MEM_WIDTH, MEM_WIDTH]
_GROUP_OFFS = [int(v) for v in np.cumsum([0] + _GROUP_WIDTHS)]
IN_COLS = _GROUP_OFFS[-1]


def _pair_heads(w, axis):
    shape = w.shape
    split = shape[:axis] + (ATT_KV_HEADS, ATT_REP, HEAD_DIM) + shape[axis + 1:]
    return jnp.swapaxes(w.reshape(split), axis, axis + 1).reshape(shape)


def _rope_tables(seq, period, dtype=F32):
    r = period // ROPE_FRAC
    half = r // 2
    inv = ROPE_THETA ** (-jnp.arange(0, half, dtype=F32) * 2.0 / r)
    ang = jnp.arange(seq).astype(F32)[:, None] * inv[None, :]
    cos, sin = jnp.cos(ang), jnp.sin(ang)
    ones = jnp.ones((seq, period - r), F32)
    zeros = jnp.zeros((seq, period - r), F32)
    c = jnp.concatenate([cos, cos, ones], axis=1)
    s = jnp.concatenate([-sin, sin, zeros], axis=1)
    reps = LANES // period
    return jnp.tile(c, (1, reps)).astype(dtype), jnp.tile(s, (1, reps)).astype(dtype)


def _rope(z, cos_t, sin_t, period):
    width = z.shape[1]
    half = period // ROPE_FRAC // 2
    reps = width // LANES
    if reps > 1:
        cos_t = jnp.concatenate([cos_t] * reps, axis=1)
        sin_t = jnp.concatenate([sin_t] * reps, axis=1)
    lane = lax.broadcasted_iota(I32, z.shape, 1)
    first = (lane & (period - 1)) < half
    up = pltpu.roll(z, half, 1)
    down = pltpu.roll(z, width - half, 1)
    return z * cos_t + jnp.where(first, down, up) * sin_t


def _inproj_kernel(x_ref, w_ref, ca_ref, sa_ref, ci_ref, si_ref,
                   q_ref, k_ref, vt_ref, qi_ref, ki_ref, wi_ref, ga_ref, u_ref, gs_ref, qm_ref, gm_ref):
    tile = x_ref.shape[1]
    pos0 = pl.multiple_of(pl.program_id(1) * tile, tile)
    xb = x_ref[0].astype(BF16)

    def proj(group, last=None):
        lo, hi = _GROUP_OFFS[group], _GROUP_OFFS[(group if last is None else last) + 1]
        return jnp.dot(xb, w_ref[:, lo:hi], preferred_element_type=F32)

    ca = ca_ref[pl.ds(pos0, tile), :]
    sa = sa_ref[pl.ds(pos0, tile), :]
    ci = ci_ref[pl.ds(pos0, tile), :]
    si = si_ref[pl.ds(pos0, tile), :]

    def store_transposed(ref, z):
        for qb in range(tile // Q_TILE):
            for t in range(z.shape[1] // LANES):
                zt = z[qb * Q_TILE:(qb + 1) * Q_TILE, t * LANES:(t + 1) * LANES].T
                ref[0, qb, :, t * Q_TILE:(t + 1) * Q_TILE] = zt.astype(ref.dtype)

    store_transposed(q_ref, _rope(proj(_G_Q), ca, sa, HEAD_DIM) * (ATT_SCALE * LOG2_E))
    kv = proj(_G_K, _G_V)
    k_ref[0] = _rope(kv[:, :KV_WIDTH], ca, sa, HEAD_DIM).astype(BF16)
    v = kv[:, KV_WIDTH:]
    for c in range(tile // KEY_CHUNK):
        vt_ref[0, c, :KV_WIDTH, :] = v[c * KEY_CHUNK:(c + 1) * KEY_CHUNK, :].T.astype(BF16)
        vt_ref[0, c, KV_WIDTH:, :] = jnp.ones((BF16_ROWS, KEY_CHUNK), BF16)
    store_transposed(qi_ref, _rope(proj(_G_QI), ci, si, IDX_DIM))
    kiw = proj(_G_KI, _G_W)
    ki_ref[0] = _rope(kiw[:, :LANES], ci, si, IDX_DIM).astype(BF16)
    wi = kiw[:, LANES:] * IDX_SCALE
    for qb in range(tile // Q_TILE):
        wi_ref[0, qb] = wi[qb * Q_TILE:(qb + 1) * Q_TILE, :].T[:IDX_HEADS, :]
    ga_ref[0] = jax.nn.silu(proj(_G_GA))
    u_ref[...] = proj(_G_U)
    gs_ref[0] = jax.nn.silu(proj(_G_GS))
    qm_ref[0] = (proj(_G_QM) * ATT_SCALE).astype(BF16)
    gm_ref[0] = jax.nn.silu(proj(_G_GM))


def _inproj(x, w_bf16, tabs):
    batch, seq, _ = x.shape
    tile = TOKEN_TILE
    widths = _GROUP_WIDTHS
    dtypes = [BF16, BF16, BF16, BF16, BF16, F32, F32, F32, F32, BF16, F32]
    tab_spec = pl.BlockSpec((seq, LANES), lambda b, j: (0, 0))
    out_specs = [pl.BlockSpec((1, tile, w), lambda b, j: (b, j, 0)) for w in widths]
    out_shape = [jax.ShapeDtypeStruct((batch, seq, w), d) for w, d in zip(widths, dtypes)]
    out_specs[_G_V] = pl.BlockSpec((1, tile // KEY_CHUNK, VT_ROWS, KEY_CHUNK), lambda b, j: (b, j, 0, 0))
    out_shape[_G_V] = jax.ShapeDtypeStruct((batch, seq // KEY_CHUNK, VT_ROWS, KEY_CHUNK), BF16)
    for group, rows, dtype in ((_G_Q, LANES, BF16), (_G_QI, LANES, BF16), (_G_W, IDX_HEADS, F32)):
        cols = Q_TILE * max(widths[group] // LANES, 1)
        out_specs[group] = pl.BlockSpec((1, tile // Q_TILE, rows, cols), lambda b, j: (b, j, 0, 0))
        out_shape[group] = jax.ShapeDtypeStruct((batch, seq // Q_TILE, rows, cols), dtype)
    out_specs[_G_U] = pl.BlockSpec((tile, SSM_WIDTH), lambda b, j: (j, b))
    out_shape[_G_U] = jax.ShapeDtypeStruct((seq, batch * SSM_WIDTH), F32)
    return pl.pallas_call(
        _inproj_kernel,
        grid=(batch, seq // tile),
        in_specs=[pl.BlockSpec((1, tile, D_MODEL), lambda b, j: (b, j, 0)),
                  pl.BlockSpec((D_MODEL, IN_COLS), lambda b, j: (0, 0)),
                  tab_spec, tab_spec, tab_spec, tab_spec],
        out_specs=out_specs,
        out_shape=out_shape,
        compiler_params=pltpu.CompilerParams(dimension_semantics=("arbitrary", "arbitrary"),
                                             vmem_limit_bytes=VMEM_LIMIT_BYTES),
        name="inproj",
    )(x, w_bf16, *tabs)


def _key_to_float(key):
    bits = jnp.where(key >= 0, key, key ^ 0x7FFFFFFF)
    val = lax.bitcast_convert_type(bits, F32)
    val = jnp.where(key <= KEY_NEG_INF, -jnp.inf, val)
    return jnp.where(key >= KEY_POS_INF, jnp.inf, val)


def _dsa_kernel(q_ref, qi_ref, wi_ref, k_ref, vt_ref, ki_ref, ga_ref, o_ref,
                score_scr, bias_scr, row_scr, s_scr, acc_scr, og_scr, *, topk):
    i = pl.program_id(1)
    n_chunks = ((i + 1) * Q_TILE + KEY_CHUNK - 1) // KEY_CHUNK
    q0 = i * Q_TILE
    kf = float(topk)
    att_cols = ATT_REP * Q_TILE
    groups = KEY_CHUNK // SUBLANES

    def chunk_start(c):
        return pl.multiple_of(c * KEY_CHUNK, KEY_CHUNK)

    def over_chunk_pairs(body, init):
        def pair(c2, carry):
            return body(2 * c2 + 1, body(2 * c2, carry))
        carry = lax.fori_loop(0, n_chunks // 2, pair, init)
        return lax.cond(n_chunks % 2 == 1, lambda cr: body(n_chunks - 1, cr), lambda cr: cr, carry)

    def fold(v, op):
        return op(v.reshape(groups, SUBLANES, v.shape[-1]), axis=0)

    def only_rows(x, lo, n):
        parts = [jnp.zeros((lo, x.shape[1]), x.dtype)] if lo else []
        parts.append(x[lo:lo + n])
        if lo + n < x.shape[0]:
            parts.append(jnp.zeros((x.shape[0] - lo - n, x.shape[1]), x.dtype))
        return jnp.concatenate(parts, axis=0)

    qi_t = qi_ref[0, 0]
    per_tile = LANES // IDX_DIM

    def head_cols(h):
        t = h // per_tile
        return only_rows(qi_t[:, t * Q_TILE:(t + 1) * Q_TILE], (h % per_tile) * IDX_DIM, IDX_DIM)

    rhs_pairs = [jnp.concatenate([head_cols(2 * p), head_cols(2 * p + 1)], axis=1)
                 for p in range(IDX_HEADS // 2)]
    w_rows = [wi_ref[0, 0, h:h + 1, :] for h in range(IDX_HEADS)]
    q_t = q_ref[0, 0]
    qg_t = [only_rows(q_t, g * HEAD_DIM, HEAD_DIM) for g in range(ATT_KV_HEADS)]
    krow = lax.broadcasted_iota(I32, (KEY_CHUNK, Q_TILE), 0)
    kq = q0 + lax.broadcasted_iota(I32, (KEY_CHUNK, Q_TILE), 1)

    def matmul_chunk(c, stats):
        mx_a, mn_a, ge_a, gt_a, top_a = stats
        ks = chunk_start(c)
        kc = ki_ref[0, pl.ds(ks, KEY_CHUNK), :]
        acc = jnp.zeros((KEY_CHUNK, Q_TILE), F32)
        for p in range(IDX_HEADS // 2):
            s2 = jnp.dot(kc, rhs_pairs[p], preferred_element_type=F32)
            acc = acc + w_rows[2 * p] * jnp.maximum(s2[:, :Q_TILE], 0.0)
            acc = acc + w_rows[2 * p + 1] * jnp.maximum(s2[:, Q_TILE:], 0.0)
        causal = ks + krow <= kq
        sc = jnp.where(causal, acc, -jnp.inf)
        score_scr[pl.ds(ks, KEY_CHUNK), :] = sc
        kk = k_ref[0, pl.ds(ks, KEY_CHUNK), :]
        tops = []
        for g in range(ATT_KV_HEADS):
            s = jnp.dot(kk, qg_t[g], preferred_element_type=F32)
            s_scr[g, pl.ds(ks, KEY_CHUNK), :] = s
            tops.append(jnp.maximum(top_a[g], fold(s, jnp.max)))
        return (jnp.maximum(mx_a, fold(sc, jnp.max)),
                jnp.minimum(mn_a, fold(jnp.where(causal, acc, jnp.inf), jnp.min)),
                ge_a + fold(jnp.where(sc >= 0.0, 1.0, 0.0), jnp.sum),
                gt_a + fold(jnp.where(sc > 0.0, 1.0, 0.0), jnp.sum),
                tuple(tops))

    stat0 = lambda v: jnp.full((SUBLANES, Q_TILE), v, F32)
    top0 = tuple(jnp.full((SUBLANES, att_cols), NEG_BIG, F32) for _ in range(ATT_KV_HEADS))
    mx_a, mn_a, ge_a, gt_a, top_a = over_chunk_pairs(
        matmul_chunk, (stat0(-jnp.inf), stat0(jnp.inf), stat0(0.0), stat0(0.0), top0))
    m_top = [jnp.max(top_a[g], axis=0, keepdims=True) for g in range(ATT_KV_HEADS)]
    s_max = jnp.max(mx_a, axis=0, keepdims=True)
    s_min = jnp.min(mn_a, axis=0, keepdims=True)
    n_ge0 = jnp.sum(ge_a, axis=0, keepdims=True)
    n_gt0 = jnp.sum(gt_a, axis=0, keepdims=True)
    n_causal = (q0 + 1 + lax.broadcasted_iota(I32, (1, Q_TILE), 1)).astype(F32)

    def count(pred):
        def body(c, acc):
            ks = chunk_start(c)
            for part in range(KEY_CHUNK // LANES):
                hit = jnp.where(pred(score_scr[pl.ds(ks + part * LANES, LANES), :]), 1.0, 0.0)
                acc = acc + (hit[:COUNT_ROWS] + hit[COUNT_ROWS:])
            return acc
        acc = over_chunk_pairs(body, jnp.zeros((COUNT_ROWS, Q_TILE), F32))
        rows = COUNT_ROWS
        while rows > SUBLANES:
            rows //= 2
            acc = acc[:rows] + acc[rows:]
        return jnp.sum(acc, axis=0, keepdims=True)

    def count_ge(value):
        vb = jnp.broadcast_to(value, (LANES, Q_TILE))
        return count(lambda sc: sc >= vb)

    zero_thr = (n_gt0 < kf) & (n_ge0 >= kf)
    keep_all = n_causal <= kf
    positive = n_gt0 >= kf
    settled = zero_thr | keep_all
    lo0 = jnp.where(settled, 0.0, jnp.where(positive, 0.0, s_min))
    hi0 = jnp.where(settled, 0.0, jnp.where(positive, s_max, 0.0))
    c_lo0 = jnp.where(positive, n_gt0, n_causal)
    c_hi0 = jnp.where(positive, 0.0, n_ge0)
    thr0 = jnp.where(keep_all, -jnp.inf, 0.0)
    done0 = jnp.where(settled, 1.0, 0.0)
    zero_ties = jnp.max(jnp.where(zero_thr & (n_ge0 > kf) & jnp.logical_not(keep_all), 1.0, 0.0)) > 0.0

    def search_pass(bisect, rows):
        lo, hi, c_lo, c_hi, thr, done = rows
        if bisect:
            frac = 0.5
        else:
            frac = jnp.clip((kf - c_hi + 0.5) / (c_lo - c_hi + 1.0), SEARCH_CLIP, 1.0 - SEARCH_CLIP)
        x = hi - (hi - lo) * frac
        c = count_ge(x)
        live = done == 0.0
        above = live & (c >= kf)
        below = live & (c < kf)
        hit = live & (c == kf)
        return (jnp.where(above, x, lo), jnp.where(below, x, hi),
                jnp.where(above, c, c_lo), jnp.where(below, c, c_hi),
                jnp.where(hit, x, thr), jnp.where(hit, 1.0, done))

    def search_cond(state):
        return (state[0] < SEARCH_MAX_PASSES) & (state[1] > 0.0)

    def search_body(state):
        it, _, rows = state
        for _ in range(SEARCH_GROUP):
            rows = search_pass(False, rows)
        return it + SEARCH_GROUP, jnp.sum(1.0 - rows[5]), rows

    def first_passes(rows):
        for p in range(SEARCH_FIRST_PASSES):
            rows = search_pass(p < SEARCH_BISECT_PASSES, rows)
        return rows

    rows0 = (lo0, hi0, c_lo0, c_hi0, thr0, done0)
    rows1 = lax.cond((i + 1) * Q_TILE > topk, first_passes, lambda rows: rows, rows0)
    state = lax.while_loop(search_cond, search_body,
                           (jnp.int32(SEARCH_FIRST_PASSES), jnp.sum(1.0 - rows1[5]), rows1))
    pending, thr_s, done_s = state[1], state[2][4], state[2][5]
    row_scr[0:1, :] = thr_s
    row_scr[1:2, :] = jnp.where(zero_thr, n_gt0, 0.0)
    row_scr[2:3, :] = jnp.where(zero_thr, n_ge0, kf)

    @pl.when(pending > 0.0)
    def _():
        def count_ge_key(key):
            return count_ge(_key_to_float(key))

        key0 = jnp.where(count_ge_key(jnp.zeros((1, Q_TILE), I32)) >= kf, 0, INT_MIN).astype(I32)

        def bit_step(b, key):
            cand = key | jnp.left_shift(jnp.int32(1), 30 - b)
            return jnp.where(count_ge_key(cand) >= kf, cand, key)

        thr_f = _key_to_float(lax.fori_loop(0, 31, bit_step, key0))
        thr_fb = jnp.broadcast_to(thr_f, (LANES, Q_TILE))
        open_row = done_s == 0.0
        row_scr[0:1, :] = jnp.where(open_row, thr_f, thr_s)
        row_scr[1:2, :] = jnp.where(open_row, count(lambda sc: sc > thr_fb), row_scr[1:2, :])
        row_scr[2:3, :] = jnp.where(open_row, count(lambda sc: sc >= thr_fb), row_scr[2:3, :])

    thr = row_scr[0:1, :]
    thr_c = jnp.broadcast_to(thr, (KEY_CHUNK, Q_TILE))
    any_tie = zero_ties | (pending > 0.0)

    def plain_bias(c, carry):
        ks = chunk_start(c)
        keep = (score_scr[pl.ds(ks, KEY_CHUNK), :] >= thr_c) & (ks + krow <= kq)
        return jnp.where(keep, 0.0, NEG_BIG), carry

    def make_tie_bias():
        need = jnp.broadcast_to(kf - row_scr[1:2, :], (KEY_CHUNK, Q_TILE))
        lower = (lax.broadcasted_iota(I32, (KEY_CHUNK, KEY_CHUNK), 1)
                 <= lax.broadcasted_iota(I32, (KEY_CHUNK, KEY_CHUNK), 0))
        prefix_mat = jnp.where(lower, 1.0, 0.0).astype(BF16)

        def tie_bias(c, seen):
            ks = chunk_start(c)
            sc = score_scr[pl.ds(ks, KEY_CHUNK), :]
            tied = sc == thr_c
            rank = seen + jnp.dot(prefix_mat, jnp.where(tied, 1.0, 0.0).astype(BF16),
                                  preferred_element_type=F32)
            chosen = (sc > thr_c) | (tied & (rank <= need))
            return jnp.where(chosen & (ks + krow <= kq), 0.0, NEG_BIG), rank[KEY_CHUNK - 1:KEY_CHUNK, :]

        return tie_bias

    no_ties_seen = jnp.zeros((1, Q_TILE), F32)

    def pv_pass(bias_of, m, carry0):
        acc_scr[...] = jnp.zeros(acc_scr.shape, F32)

        def pv_chunk(c, carry):
            bias, carry = bias_of(c, carry)
            bias4 = jnp.concatenate([bias] * ATT_REP, axis=1)
            for g in range(ATT_KV_HEADS):
                p = jnp.exp2(s_scr[g, pl.ds(chunk_start(c), KEY_CHUNK), :] + bias4 - m[g])
                acc_scr[g] += jnp.dot(vt_ref[0, c], p.astype(BF16), preferred_element_type=F32)
            return carry

        over_chunk_pairs(pv_chunk, carry0)

    @pl.when(jnp.logical_not(any_tie))
    def _():
        pv_pass(plain_bias, m_top, jnp.int32(0))

    @pl.when(any_tie)
    def _():
        pv_pass(make_tie_bias(), m_top, no_ties_seen)

    def write_output():
        for g in range(ATT_KV_HEADS):
            denom = acc_scr[g, KV_WIDTH:KV_WIDTH + 1, :]
            og_scr[g * HEAD_DIM:(g + 1) * HEAD_DIM, :] = acc_scr[g, g * HEAD_DIM:(g + 1) * HEAD_DIM, :] / denom
        for j in range(ATT_REP):
            gated = og_scr[:, j * Q_TILE:(j + 1) * Q_TILE].T * ga_ref[0, :, j * LANES:(j + 1) * LANES]
            o_ref[0, :, j * LANES:(j + 1) * LANES] = gated.astype(o_ref.dtype)

    d_min = jnp.min(jnp.minimum(acc_scr[0, KV_WIDTH:KV_WIDTH + 1, :], acc_scr[1, KV_WIDTH:KV_WIDTH + 1, :]))
    write_output()

    @pl.when(jnp.logical_not(d_min >= DENOM_FLOOR))
    def _():
        tie_bias = make_tie_bias()

        def mask_chunk(c, carry):
            seen, m_acc = carry
            ks = chunk_start(c)
            bias, seen = tie_bias(c, seen)
            bias_scr[pl.ds(ks, KEY_CHUNK), :] = bias
            bias4 = jnp.concatenate([bias] * ATT_REP, axis=1)
            return seen, tuple(jnp.maximum(m_acc[g], fold(s_scr[g, pl.ds(ks, KEY_CHUNK), :] + bias4, jnp.max))
                               for g in range(ATT_KV_HEADS))

        m_init = tuple(jnp.full((SUBLANES, att_cols), NEG_BIG, F32) for _ in range(ATT_KV_HEADS))
        _, m_acc = lax.fori_loop(0, n_chunks, mask_chunk, (no_ties_seen, m_init))
        m_sel = [jnp.max(m_acc[g], axis=0, keepdims=True) for g in range(ATT_KV_HEADS)]
        pv_pass(lambda c, carry: (bias_scr[pl.ds(chunk_start(c), KEY_CHUNK), :], carry), m_sel, jnp.int32(0))
        write_output()


def _dsa(q_t, qi_t, wi_t, k, vt, ki, gate):
    batch, seq, _ = k.shape
    topk = min(TOPK_MAX, seq // 4)
    att_cols = ATT_REP * Q_TILE
    per_q = lambda a: pl.BlockSpec((1, 1) + a.shape[2:], lambda b, i: (b, i, 0, 0))
    per_b = lambda w: pl.BlockSpec((1, seq, w), lambda b, i: (b, 0, 0))
    vt_spec = pl.BlockSpec((1, seq // KEY_CHUNK, VT_ROWS, KEY_CHUNK), lambda b, i: (b, 0, 0, 0))
    tok_spec = pl.BlockSpec((1, Q_TILE, ATT_WIDTH), lambda b, i: (b, i, 0))
    return pl.pallas_call(
        functools.partial(_dsa_kernel, topk=topk),
        grid=(batch, seq // Q_TILE),
        in_specs=[per_q(q_t), per_q(qi_t), per_q(wi_t), per_b(KV_WIDTH), vt_spec, per_b(LANES), tok_spec],
        out_specs=tok_spec,
        out_shape=jax.ShapeDtypeStruct((batch, seq, ATT_WIDTH), BF16),
        scratch_shapes=[pltpu.VMEM((seq, Q_TILE), F32),
                        pltpu.VMEM((seq, Q_TILE), F32),
                        pltpu.VMEM((SUBLANES, Q_TILE), F32),
                        pltpu.VMEM((ATT_KV_HEADS, seq, att_cols), F32),
                        pltpu.VMEM((ATT_KV_HEADS, VT_ROWS, att_cols), F32),
                        pltpu.VMEM((KV_WIDTH, att_cols), F32)],
        compiler_params=pltpu.CompilerParams(dimension_semantics=("arbitrary", "arbitrary"),
                                             vmem_limit_bytes=VMEM_LIMIT_BYTES),
        name="dsa",
    )(q_t, qi_t, wi_t, k, vt, ki, gate)


def _s5_kernel(u_ref, bm_ref, cm_ref, are_ref, aim_ref, dsk_ref, wg_ref, bg_ref, o_ref,
               st_scr, xb_scr, carry_scr):
    tc = u_ref.shape[0]
    batch, width = carry_scr.shape[0], dsk_ref.shape[1]
    steps = tc // SCAN_SPLIT
    rows = steps * batch

    @pl.when(pl.program_id(0) == 0)
    def _():
        carry_scr[...] = jnp.zeros(carry_scr.shape, F32)

    u = [u_ref[h * steps:(h + 1) * steps].reshape(steps, batch, width).reshape(rows, width)
         for h in range(SCAN_SPLIT)]
    for h in range(SCAN_SPLIT):
        st_scr[h] = jnp.dot(u[h].astype(BF16), bm_ref[...], preferred_element_type=F32)

    parts = []
    for part in range(SSM_LANES // SCAN_LANES):
        re_lo = part * SCAN_LANES
        im_lo = SSM_LANES + re_lo
        parts.append((slice(re_lo, re_lo + SCAN_LANES), slice(im_lo, im_lo + SCAN_LANES)))
    state = [(carry_scr[:, re], carry_scr[:, im]) for re, im in parts]

    for h in range(SCAN_SPLIT):
        for p, (re, im) in enumerate(parts):
            a_re = jnp.broadcast_to(are_ref[:, re], (batch, SCAN_LANES))
            a_im = jnp.broadcast_to(aim_ref[:, re], (batch, SCAN_LANES))
            x_re, x_im = state[p]
            for t in range(steps):
                r = slice(t * batch, (t + 1) * batch)
                x_re, x_im = (a_re * x_re - a_im * x_im + st_scr[h, r, re],
                              a_re * x_im + a_im * x_re + st_scr[h, r, im])
                xb_scr[h, r, re] = x_re.astype(BF16)
                xb_scr[h, r, im] = x_im.astype(BF16)
            state[p] = (x_re, x_im)
        y = jnp.dot(xb_scr[h], cm_ref[...], preferred_element_type=F32) + dsk_ref[...] * u[h]
        y = jax.nn.gelu(y)
        gate = jax.nn.sigmoid(jnp.dot(y.astype(BF16), wg_ref[...], preferred_element_type=F32) + bg_ref[...])
        o_ref[h * steps:(h + 1) * steps] = (y * gate).reshape(steps, batch, width).reshape(steps, batch * width)

    for (re, im), (x_re, x_im) in zip(parts, state):
        carry_scr[:, re] = x_re
        carry_scr[:, im] = x_im


def _s5(u_tm, batch, bmat, cmat, a_re, a_im, dskip, w_glu, b_glu):
    seq = u_tm.shape[0]
    width = u_tm.shape[1] // batch
    tc = TIME_CHUNK
    full = lambda shape: pl.BlockSpec(shape, lambda t: tuple(0 for _ in shape))
    return pl.pallas_call(
        _s5_kernel,
        grid=(seq // tc,),
        in_specs=[pl.BlockSpec((tc, batch * width), lambda t: (t, 0)),
                  full(bmat.shape), full(cmat.shape), full(a_re.shape), full(a_im.shape),
                  full(dskip.shape), full(w_glu.shape), full(b_glu.shape)],
        out_specs=pl.BlockSpec((tc, batch * width), lambda t: (t, 0)),
        out_shape=jax.ShapeDtypeStruct((seq, batch * width), F32),
        scratch_shapes=[pltpu.VMEM((SCAN_SPLIT, tc // SCAN_SPLIT * batch, 2 * SSM_LANES), F32),
                        pltpu.VMEM((SCAN_SPLIT, tc // SCAN_SPLIT * batch, 2 * SSM_LANES), BF16),
                        pltpu.VMEM((batch, 2 * SSM_LANES), F32)],
        compiler_params=pltpu.CompilerParams(dimension_semantics=("arbitrary",),
                                             vmem_limit_bytes=VMEM_LIMIT_BYTES),
        name="s5",
    )(u_tm, bmat, cmat, a_re, a_im, dskip, w_glu, b_glu)


def _s5_params(lam_re, lam_im, log_dt, b_re, b_im, c_re, c_im):
    dt = jnp.exp(log_dt.astype(F32))[:, None]
    mag = jnp.exp(lam_re.astype(F32) * dt)
    ang = lam_im.astype(F32) * dt
    lb_re, lb_im = mag * jnp.cos(ang), mag * jnp.sin(ang)
    den = lam_re * lam_re + lam_im * lam_im
    k_re = ((lb_re - 1.0) * lam_re + lb_im * lam_im) / den
    k_im = (lb_im * lam_re - (lb_re - 1.0) * lam_im) / den
    bb_re = k_re[:, :, None] * b_re - k_im[:, :, None] * b_im
    bb_im = k_re[:, :, None] * b_im + k_im[:, :, None] * b_re
    same_group = (np.arange(SSM_WIDTH)[:, None] // SSM_GROUP) == (np.arange(SSM_LANES)[None, :] // SSM_STATE)

    def drive(bb):
        per_group = jnp.swapaxes(bb, 1, 2).reshape(SSM_WIDTH, SSM_STATE)
        return jnp.where(same_group, jnp.tile(per_group, (1, SSM_GROUPS)), 0.0)

    def read(cc):
        per_group = jnp.swapaxes(cc, 1, 2).reshape(SSM_LANES, SSM_GROUP)
        return jnp.where(same_group.T, jnp.tile(per_group, (1, SSM_GROUPS)), 0.0)

    bmat = jnp.concatenate([drive(bb_re), drive(bb_im)], axis=1)
    cmat = jnp.concatenate([read(c_re.astype(F32)), read(-c_im.astype(F32))], axis=0)
    return (bmat.astype(BF16), cmat.astype(BF16),
            lb_re.reshape(1, SSM_LANES), lb_im.reshape(1, SSM_LANES))


def _combine_kernel(x_ref, ca_ref, os_ref, gs_ref, qm_ref, gm_ref, mem_ref, wkv_ref,
                    wo_ref, lg_ref, lb_ref, out_ref, mk_scr, mv_scr):
    tile = x_ref.shape[1]
    nt_dims = (((1,), (1,)), ((), ()))
    lane = lax.broadcasted_iota(I32, (tile, LANES), 1)
    low = lane < HEAD_DIM

    @pl.when(pl.program_id(1) == 0)
    def _():
        mkv = jnp.dot(mem_ref[0].astype(BF16), wkv_ref[...], preferred_element_type=F32)
        mk_scr[...] = mkv[:, :MEM_WIDTH].astype(BF16)
        mv_scr[...] = mkv[:, MEM_WIDTH:].astype(BF16)

    qm = qm_ref[0].astype(F32)
    mem_tiles = []
    for t in range(MEM_WIDTH // LANES):
        qt = qm[:, t * LANES:(t + 1) * LANES]
        mk = mk_scr[:, t * LANES:(t + 1) * LANES]
        mv = mv_scr[:, t * LANES:(t + 1) * LANES]
        halves = []
        for first in (True, False):
            qh = jnp.where(low if first else ~low, qt, 0.0).astype(BF16)
            s = lax.dot_general(qh, mk, nt_dims, preferred_element_type=F32)
            p = jnp.exp(s - jnp.max(s, axis=1, keepdims=True))
            o = jnp.dot(p.astype(BF16), mv, preferred_element_type=F32)
            halves.append(o / jnp.sum(p, axis=1, keepdims=True))
        mem_tiles.append(jnp.where(low, halves[0], halves[1]))
    o_mem = jnp.concatenate(mem_tiles, axis=1)

    c_att = ca_ref[0]
    c_ssm = (os_ref[...] * gs_ref[0]).astype(BF16)
    c_mem = (o_mem * gm_ref[0]).astype(BF16)
    a1, a2 = ATT_WIDTH, ATT_WIDTH + SSM_WIDTH
    sub = (jnp.dot(c_att, wo_ref[:a1, :], preferred_element_type=F32)
           + jnp.dot(c_ssm, wo_ref[a1:a2, :], preferred_element_type=F32)
           + jnp.dot(c_mem, wo_ref[a2:, :], preferred_element_type=F32))
    h = DN_ALPHA * x_ref[0] + sub
    mu = jnp.mean(h, axis=1, keepdims=True)
    d = h - mu
    var = jnp.mean(d * d, axis=1, keepdims=True)
    out_ref[0] = d * lax.rsqrt(var + LN_EPS) * lg_ref[...] + lb_ref[...]


def _combine(x, c_att, o_ssm_tm, g_ssm, qm, g_mem, mem, w_mem_kv_bf16, wo_bf16, ln_g, ln_b):
    batch, seq, _ = x.shape
    tile = COMBINE_TILE
    tok = lambda w: pl.BlockSpec((1, tile, w), lambda b, j: (b, j, 0))
    time_major = pl.BlockSpec((tile, SSM_WIDTH), lambda b, j: (j, b))
    per_b = pl.BlockSpec((1, N_MEM, D_MODEL), lambda b, j: (b, 0, 0))
    const = lambda shape: pl.BlockSpec(shape, lambda b, j: tuple(0 for _ in shape))
    return pl.pallas_call(
        _combine_kernel,
        grid=(batch, seq // tile),
        in_specs=[tok(D_MODEL), tok(ATT_WIDTH), time_major, tok(SSM_WIDTH),
                  tok(MEM_WIDTH), tok(MEM_WIDTH), per_b, const((D_MODEL, 2 * MEM_WIDTH)),
                  const((D_MODEL, D_MODEL)), const((1, D_MODEL)), const((1, D_MODEL))],
        out_specs=tok(D_MODEL),
        out_shape=jax.ShapeDtypeStruct((batch, seq, D_MODEL), x.dtype),
        scratch_shapes=[pltpu.VMEM((N_MEM, MEM_WIDTH), BF16),
                        pltpu.VMEM((N_MEM, MEM_WIDTH), BF16)],
        compiler_params=pltpu.CompilerParams(dimension_semantics=("arbitrary", "arbitrary"),
                                             vmem_limit_bytes=VMEM_LIMIT_BYTES),
        name="combine",
    )(x, c_att, o_ssm_tm, g_ssm, qm, g_mem, mem, w_mem_kv_bf16, wo_bf16, ln_g, ln_b)


def _reordered_w_in(w_in):
    offs = np.cumsum([0, ATT_WIDTH, KV_WIDTH, KV_WIDTH, IDX_WIDTH, IDX_DIM, IDX_HEADS,
                      ATT_WIDTH, SSM_WIDTH, SSM_WIDTH, MEM_WIDTH, MEM_WIDTH])
    part = lambda n: w_in[:, int(offs[n]):int(offs[n + 1])]
    k_idx = part(4)
    w_idx = part(5)
    zeros = jnp.zeros((D_MODEL, LANES - IDX_HEADS), w_in.dtype)
    run = lambda first, last: w_in[:, int(offs[first]):int(offs[last + 1])]
    cols = [_pair_heads(part(0), 1), run(1, 3),
            jnp.tile(k_idx, (1, LANES // IDX_DIM)),
            w_idx, zeros,
            _pair_heads(part(6), 1), run(7, 10)]
    return jnp.concatenate(cols, axis=1).astype(BF16)


def kernel(x, mem, w_in, w_mem_kv, lam_re, lam_im, log_dt, b_re, b_im, c_re, c_im, d_skip, w_glu, b_glu,
           w_out, ln_g, ln_b):
    batch, seq, _ = x.shape
    assert seq % TOKEN_TILE == 0 and seq % COMBINE_TILE == 0 and seq % Q_TILE == 0 and seq % TIME_CHUNK == 0

    w_all = _reordered_w_in(w_in)
    wo = jnp.concatenate([_pair_heads(w_out[:ATT_WIDTH], 0), w_out[ATT_WIDTH:]], axis=0).astype(BF16)
    bmat, cmat, a_re, a_im = _s5_params(lam_re, lam_im, log_dt, b_re, b_im, c_re, c_im)
    tabs = _rope_tables(seq, HEAD_DIM) + _rope_tables(seq, IDX_DIM)

    q_t, k, vt, qi_t, ki, wi_t, g_att, u_tm, g_ssm, qm, g_mem = _inproj(x, w_all, tabs)
    c_att = _dsa(q_t, qi_t, wi_t, k, vt, ki, g_att)

    o_ssm_tm = _s5(u_tm, batch, bmat, cmat, a_re, a_im,
                   d_skip.reshape(1, SSM_WIDTH).astype(F32), w_glu.astype(BF16),
                   b_glu.reshape(1, SSM_WIDTH).astype(F32))

    return _combine(x, c_att, o_ssm_tm, g_ssm, qm, g_mem, mem, w_mem_kv.astype(BF16), wo,
                    ln_g.reshape(1, D_MODEL).astype(F32), ln_b.reshape(1, D_MODEL).astype(F32))
```

```python
import functools
import math

import jax
import jax.numpy as jnp
import numpy as np
from jax import lax
from jax.experimental import pallas as pl
from jax.experimental.pallas import tpu as pltpu

F32 = jnp.float32
BF16 = jnp.bfloat16
I32 = jnp.int32

D_MODEL = 1024
N_MEM = 256
HEAD_DIM = 64
ATT_HEADS = 8
ATT_KV_HEADS = 2
ATT_REP = ATT_HEADS // ATT_KV_HEADS
ATT_WIDTH = ATT_HEADS * HEAD_DIM
KV_WIDTH = ATT_KV_HEADS * HEAD_DIM
IDX_HEADS = 8
IDX_DIM = 32
IDX_WIDTH = IDX_HEADS * IDX_DIM
TOPK_MAX = 256
SSM_WIDTH = D_MODEL // 4
SSM_GROUP = 16
SSM_GROUPS = SSM_WIDTH // SSM_GROUP
SSM_STATE = 64
SSM_LANES = SSM_GROUPS * SSM_STATE
MEM_HEADS = 4
MEM_WIDTH = MEM_HEADS * HEAD_DIM
ROPE_THETA = 500000.0
ROPE_FRAC = 4
LN_EPS = 1e-5
DEPTH = 1
DN_ALPHA = (2.0 * DEPTH) ** 0.25
ATT_SCALE = HEAD_DIM ** -0.5
IDX_SCALE = IDX_HEADS ** -0.5 * IDX_DIM ** -0.5
LOG2_E = math.log2(math.e)

LANES = 128
SUBLANES = 8
BF16_ROWS = 16
VMEM_LIMIT_BYTES = 48 * 1024 * 1024
TOKEN_TILE = 512
COMBINE_TILE = 1024
COMBINE_ROWS = 512
Q_TILE = 256
KEY_CHUNK = 256
COUNT_ROWS = 64
VT_ROWS = KV_WIDTH + BF16_ROWS
TIME_CHUNK = 128
SCAN_SPLIT = 2
SCAN_LANES = 256
NEG_BIG = -1e30
DENOM_FLOOR = 2.0 ** -100
SEARCH_MAX_PASSES = 24
SEARCH_FIRST_PASSES = 13
SEARCH_GROUP = 2
SEARCH_BISECT_PASSES = 2
SEARCH_CLIP = 0.1

INT_MIN = -2147483648
KEY_POS_INF = 0x7F800000
KEY_NEG_INF = INT_MIN + 0x7FFFFF

_G_Q, _G_K, _G_V, _G_QI, _G_KI, _G_W, _G_GA, _G_U, _G_GS, _G_QM, _G_GM = range(11)
_GROUP_WIDTHS = [ATT_WIDTH, KV_WIDTH, KV_WIDTH, IDX_WIDTH, LANES, LANES, ATT_WIDTH,
                 SSM_WIDTH, SSM_WIDTH, MEM_WIDTH, MEM_WIDTH]
_GROUP_OFFS = [int(v) for v in np.cumsum([0] + _GROUP_WIDTHS)]
IN_COLS = _GROUP_OFFS[-1]


def _pair_heads(w, axis):
    shape = w.shape
    split = shape[:axis] + (ATT_KV_HEADS, ATT_REP, HEAD_DIM) + shape[axis + 1:]
    return jnp.swapaxes(w.reshape(split), axis, axis + 1).reshape(shape)


def _rope_tables(seq, period, dtype=F32):
    r = period // ROPE_FRAC
    half = r // 2
    inv = ROPE_THETA ** (-jnp.arange(0, half, dtype=F32) * 2.0 / r)
    ang = jnp.arange(seq).astype(F32)[:, None] * inv[None, :]
    cos, sin = jnp.cos(ang), jnp.sin(ang)
    ones = jnp.ones((seq, period - r), F32)
    zeros = jnp.zeros((seq, period - r), F32)
    c = jnp.concatenate([cos, cos, ones], axis=1)
    s = jnp.concatenate([-sin, sin, zeros], axis=1)
    reps = LANES // period
    return jnp.tile(c, (1, reps)).astype(dtype), jnp.tile(s, (1, reps)).astype(dtype)


def _rope(z, cos_t, sin_t, period):
    width = z.shape[1]
    half = period // ROPE_FRAC // 2
    reps = width // LANES
    if reps > 1:
        cos_t = jnp.concatenate([cos_t] * reps, axis=1)
        sin_t = jnp.concatenate([sin_t] * reps, axis=1)
    lane = lax.broadcasted_iota(I32, z.shape, 1)
    first = (lane & (period - 1)) < half
    up = pltpu.roll(z, half, 1)
    down = pltpu.roll(z, width - half, 1)
    return z * cos_t + jnp.where(first, down, up) * sin_t


def _inproj_kernel(x_ref, w_ref, ca_ref, sa_ref, ci_ref, si_ref,
                   q_ref, k_ref, vt_ref, qi_ref, ki_ref, wi_ref, ga_ref, u_ref, gs_ref, qm_ref, gm_ref):
    tile = x_ref.shape[1]
    pos0 = pl.multiple_of(pl.program_id(1) * tile, tile)
    xb = x_ref[0].astype(BF16)

    def proj(group, last=None):
        lo, hi = _GROUP_OFFS[group], _GROUP_OFFS[(group if last is None else last) + 1]
        return jnp.dot(xb, w_ref[:, lo:hi], preferred_element_type=F32)

    ca = ca_ref[pl.ds(pos0, tile), :]
    sa = sa_ref[pl.ds(pos0, tile), :]
    ci = ci_ref[pl.ds(pos0, tile), :]
    si = si_ref[pl.ds(pos0, tile), :]

    def store_transposed(ref, z):
        for qb in range(tile // Q_TILE):
            for t in range(z.shape[1] // LANES):
                zt = z[qb * Q_TILE:(qb + 1) * Q_TILE, t * LANES:(t + 1) * LANES].T
                ref[0, qb, :, t * Q_TILE:(t + 1) * Q_TILE] = zt.astype(ref.dtype)

    store_transposed(q_ref, _rope(proj(_G_Q), ca, sa, HEAD_DIM) * (ATT_SCALE * LOG2_E))
    kv = proj(_G_K, _G_V)
    k_ref[0] = _rope(kv[:, :KV_WIDTH], ca, sa, HEAD_DIM).astype(BF16)
    v = kv[:, KV_WIDTH:]
    for c in range(tile // KEY_CHUNK):
        vt_ref[0, c, :KV_WIDTH, :] = v[c * KEY_CHUNK:(c + 1) * KEY_CHUNK, :].T.astype(BF16)
        vt_ref[0, c, KV_WIDTH:, :] = jnp.ones((BF16_ROWS, KEY_CHUNK), BF16)
    store_transposed(qi_ref, _rope(proj(_G_QI), ci, si, IDX_DIM))
    kiw = proj(_G_KI, _G_W)
    ki_ref[0] = _rope(kiw[:, :LANES], ci, si, IDX_DIM).astype(BF16)
    wi = kiw[:, LANES:] * IDX_SCALE
    for qb in range(tile // Q_TILE):
        wi_ref[0, qb] = wi[qb * Q_TILE:(qb + 1) * Q_TILE, :].T[:IDX_HEADS, :]
    ga_ref[0] = jax.nn.silu(proj(_G_GA))
    u_ref[...] = proj(_G_U)
    gs_ref[0] = jax.nn.silu(proj(_G_GS))
    qm_ref[0] = (proj(_G_QM) * ATT_SCALE).astype(BF16)
    gm_ref[0] = jax.nn.silu(proj(_G_GM))


def _inproj(x, w_bf16, tabs):
    batch, seq, _ = x.shape
    tile = TOKEN_TILE
    widths = _GROUP_WIDTHS
    dtypes = [BF16, BF16, BF16, BF16, BF16, F32, F32, F32, F32, BF16, F32]
    tab_spec = pl.BlockSpec((seq, LANES), lambda b, j: (0, 0))
    out_specs = [pl.BlockSpec((1, tile, w), lambda b, j: (b, j, 0)) for w in widths]
    out_shape = [jax.ShapeDtypeStruct((batch, seq, w), d) for w, d in zip(widths, dtypes)]
    out_specs[_G_V] = pl.BlockSpec((1, tile // KEY_CHUNK, VT_ROWS, KEY_CHUNK), lambda b, j: (b, j, 0, 0))
    out_shape[_G_V] = jax.ShapeDtypeStruct((batch, seq // KEY_CHUNK, VT_ROWS, KEY_CHUNK), BF16)
    for group, rows, dtype in ((_G_Q, LANES, BF16), (_G_QI, LANES, BF16), (_G_W, IDX_HEADS, F32)):
        cols = Q_TILE * max(widths[group] // LANES, 1)
        out_specs[group] = pl.BlockSpec((1, tile // Q_TILE, rows, cols), lambda b, j: (b, j, 0, 0))
        out_shape[group] = jax.ShapeDtypeStruct((batch, seq // Q_TILE, rows, cols), dtype)
    out_specs[_G_U] = pl.BlockSpec((tile, SSM_WIDTH), lambda b, j: (j, b))
    out_shape[_G_U] = jax.ShapeDtypeStruct((seq, batch * SSM_WIDTH), F32)
    return pl.pallas_call(
        _inproj_kernel,
        grid=(batch, seq // tile),
        in_specs=[pl.BlockSpec((1, tile, D_MODEL), lambda b, j: (b, j, 0)),
                  pl.BlockSpec((D_MODEL, IN_COLS), lambda b, j: (0, 0)),
                  tab_spec, tab_spec, tab_spec, tab_spec],
        out_specs=out_specs,
        out_shape=out_shape,
        compiler_params=pltpu.CompilerParams(dimension_semantics=("arbitrary", "arbitrary"),
                                             vmem_limit_bytes=VMEM_LIMIT_BYTES),
        name="inproj",
    )(x, w_bf16, *tabs)


def _key_to_float(key):
    bits = jnp.where(key >= 0, key, key ^ 0x7FFFFFFF)
    val = lax.bitcast_convert_type(bits, F32)
    val = jnp.where(key <= KEY_NEG_INF, -jnp.inf, val)
    return jnp.where(key >= KEY_POS_INF, jnp.inf, val)


def _dsa_kernel(q_ref, qi_ref, wi_ref, k_ref, vt_ref, ki_ref, ga_ref, o_ref,
                score_scr, bias_scr, row_scr, s_scr, acc_scr, og_scr, *, topk):
    i = pl.program_id(1)
    n_chunks = ((i + 1) * Q_TILE + KEY_CHUNK - 1) // KEY_CHUNK
    q0 = i * Q_TILE
    kf = float(topk)
    att_cols = ATT_REP * Q_TILE
    groups = KEY_CHUNK // SUBLANES

    def chunk_start(c):
        return pl.multiple_of(c * KEY_CHUNK, KEY_CHUNK)

    def over_chunk_pairs(body, init):
        def pair(c2, carry):
            return body(2 * c2 + 1, body(2 * c2, carry))
        carry = lax.fori_loop(0, n_chunks // 2, pair, init)
        return lax.cond(n_chunks % 2 == 1, lambda cr: body(n_chunks - 1, cr), lambda cr: cr, carry)

    def fold(v, op):
        return op(v.reshape(groups, SUBLANES, v.shape[-1]), axis=0)

    def only_rows(x, lo, n):
        parts = [jnp.zeros((lo, x.shape[1]), x.dtype)] if lo else []
        parts.append(x[lo:lo + n])
        if lo + n < x.shape[0]:
            parts.append(jnp.zeros((x.shape[0] - lo - n, x.shape[1]), x.dtype))
        return jnp.concatenate(parts, axis=0)

    qi_t = qi_ref[0, 0]
    per_tile = LANES // IDX_DIM

    def head_cols(h):
        t = h // per_tile
        return only_rows(qi_t[:, t * Q_TILE:(t + 1) * Q_TILE], (h % per_tile) * IDX_DIM, IDX_DIM)

    rhs_pairs = [jnp.concatenate([head_cols(2 * p), head_cols(2 * p + 1)], axis=1)
                 for p in range(IDX_HEADS // 2)]
    w_rows = [wi_ref[0, 0, h:h + 1, :] for h in range(IDX_HEADS)]
    q_t = q_ref[0, 0]
    qg_t = [only_rows(q_t, g * HEAD_DIM, HEAD_DIM) for g in range(ATT_KV_HEADS)]
    krow = lax.broadcasted_iota(I32, (KEY_CHUNK, Q_TILE), 0)
    kq = q0 + lax.broadcasted_iota(I32, (KEY_CHUNK, Q_TILE), 1)

    def matmul_chunk(c, stats):
        mx_a, mn_a, ge_a, gt_a, top_a = stats
        ks = chunk_start(c)
        kc = ki_ref[0, pl.ds(ks, KEY_CHUNK), :]
        acc = jnp.zeros((KEY_CHUNK, Q_TILE), F32)
        for p in range(IDX_HEADS // 2):
            s2 = jnp.dot(kc, rhs_pairs[p], preferred_element_type=F32)
            acc = acc + w_rows[2 * p] * jnp.maximum(s2[:, :Q_TILE], 0.0)
            acc = acc + w_rows[2 * p + 1] * jnp.maximum(s2[:, Q_TILE:], 0.0)
        causal = ks + krow <= kq
        sc = jnp.where(causal, acc, -jnp.inf)
        score_scr[pl.ds(ks, KEY_CHUNK), :] = sc
        kk = k_ref[0, pl.ds(ks, KEY_CHUNK), :]
        tops = []
        for g in range(ATT_KV_HEADS):
            s = jnp.dot(kk, qg_t[g], preferred_element_type=F32)
            s_scr[g, pl.ds(ks, KEY_CHUNK), :] = s
            tops.append(jnp.maximum(top_a[g], fold(s, jnp.max)))
        return (jnp.maximum(mx_a, fold(sc, jnp.max)),
                jnp.minimum(mn_a, fold(jnp.where(causal, acc, jnp.inf), jnp.min)),
                ge_a + fold(jnp.where(sc >= 0.0, 1.0, 0.0), jnp.sum),
                gt_a + fold(jnp.where(sc > 0.0, 1.0, 0.0), jnp.sum),
                tuple(tops))

    stat0 = lambda v: jnp.full((SUBLANES, Q_TILE), v, F32)
    top0 = tuple(jnp.full((SUBLANES, att_cols), NEG_BIG, F32) for _ in range(ATT_KV_HEADS))
    mx_a, mn_a, ge_a, gt_a, top_a = over_chunk_pairs(
        matmul_chunk, (stat0(-jnp.inf), stat0(jnp.inf), stat0(0.0), stat0(0.0), top0))
    m_top = [jnp.max(top_a[g], axis=0, keepdims=True) for g in range(ATT_KV_HEADS)]
    s_max = jnp.max(mx_a, axis=0, keepdims=True)
    s_min = jnp.min(mn_a, axis=0, keepdims=True)
    n_ge0 = jnp.sum(ge_a, axis=0, keepdims=True)
    n_gt0 = jnp.sum(gt_a, axis=0, keepdims=True)
    n_causal = (q0 + 1 + lax.broadcasted_iota(I32, (1, Q_TILE), 1)).astype(F32)

    def count(pred):
        def body(c, acc):
            ks = chunk_start(c)
            for part in range(KEY_CHUNK // LANES):
                hit = jnp.where(pred(score_scr[pl.ds(ks + part * LANES, LANES), :]), 1.0, 0.0)
                acc = acc + (hit[:COUNT_ROWS] + hit[COUNT_ROWS:])
            return acc
        acc = over_chunk_pairs(body, jnp.zeros((COUNT_ROWS, Q_TILE), F32))
        rows = COUNT_ROWS
        while rows > SUBLANES:
            rows //= 2
            acc = acc[:rows] + acc[rows:]
        return jnp.sum(acc, axis=0, keepdims=True)

    def count_ge(value):
        vb = jnp.broadcast_to(value, (LANES, Q_TILE))
        return count(lambda sc: sc >= vb)

    zero_thr = (n_gt0 < kf) & (n_ge0 >= kf)
    keep_all = n_causal <= kf
    positive = n_gt0 >= kf
    settled = zero_thr | keep_all
    lo0 = jnp.where(settled, 0.0, jnp.where(positive, 0.0, s_min))
    hi0 = jnp.where(settled, 0.0, jnp.where(positive, s_max, 0.0))
    c_lo0 = jnp.where(positive, n_gt0, n_causal)
    c_hi0 = jnp.where(positive, 0.0, n_ge0)
    thr0 = jnp.where(keep_all, -jnp.inf, 0.0)
    done0 = jnp.where(settled, 1.0, 0.0)
    zero_ties = jnp.max(jnp.where(zero_thr & (n_ge0 > kf) & jnp.logical_not(keep_all), 1.0, 0.0)) > 0.0

    def search_pass(bisect, rows):
        lo, hi, c_lo, c_hi, thr, done = rows
        if bisect:
            frac = 0.5
        else:
            frac = jnp.clip((kf - c_hi + 0.5) / (c_lo - c_hi + 1.0), SEARCH_CLIP, 1.0 - SEARCH_CLIP)
        x = hi - (hi - lo) * frac
        c = count_ge(x)
        live = done == 0.0
        above = live & (c >= kf)
        below = live & (c < kf)
        hit = live & (c == kf)
        return (jnp.where(above, x, lo), jnp.where(below, x, hi),
                jnp.where(above, c, c_lo), jnp.where(below, c, c_hi),
                jnp.where(hit, x, thr), jnp.where(hit, 1.0, done))

    def search_cond(state):
        return (state[0] < SEARCH_MAX_PASSES) & (state[1] > 0.0)

    def search_body(state):
        it, _, rows = state
        for _ in range(SEARCH_GROUP):
            rows = search_pass(False, rows)
        return it + SEARCH_GROUP, jnp.sum(1.0 - rows[5]), rows

    def first_passes(rows):
        for p in range(SEARCH_FIRST_PASSES):
            rows = search_pass(p < SEARCH_BISECT_PASSES, rows)
        return rows

    rows0 = (lo0, hi0, c_lo0, c_hi0, thr0, done0)
    rows1 = lax.cond((i + 1) * Q_TILE > topk, first_passes, lambda rows: rows, rows0)
    state = lax.while_loop(search_cond, search_body,
                           (jnp.int32(SEARCH_FIRST_PASSES), jnp.sum(1.0 - rows1[5]), rows1))
    pending, thr_s, done_s = state[1], state[2][4], state[2][5]
    row_scr[0:1, :] = thr_s
    row_scr[1:2, :] = jnp.where(zero_thr, n_gt0, 0.0)
    row_scr[2:3, :] = jnp.where(zero_thr, n_ge0, kf)

    @pl.when(pending > 0.0)
    def _():
        def count_ge_key(key):
            return count_ge(_key_to_float(key))

        key0 = jnp.where(count_ge_key(jnp.zeros((1, Q_TILE), I32)) >= kf, 0, INT_MIN).astype(I32)

        def bit_step(b, key):
            cand = key | jnp.left_shift(jnp.int32(1), 30 - b)
            return jnp.where(count_ge_key(cand) >= kf, cand, key)

        thr_f = _key_to_float(lax.fori_loop(0, 31, bit_step, key0))
        thr_fb = jnp.broadcast_to(thr_f, (LANES, Q_TILE))
        open_row = done_s == 0.0
        row_scr[0:1, :] = jnp.where(open_row, thr_f, thr_s)
        row_scr[1:2, :] = jnp.where(open_row, count(lambda sc: sc > thr_fb), row_scr[1:2, :])
        row_scr[2:3, :] = jnp.where(open_row, count(lambda sc: sc >= thr_fb), row_scr[2:3, :])

    thr = row_scr[0:1, :]
    thr_c = jnp.broadcast_to(thr, (KEY_CHUNK, Q_TILE))
    any_tie = zero_ties | (pending > 0.0)

    def plain_bias(c, carry):
        ks = chunk_start(c)
        keep = (score_scr[pl.ds(ks, KEY_CHUNK), :] >= thr_c) & (ks + krow <= kq)
        return jnp.where(keep, 0.0, NEG_BIG), carry

    def make_tie_bias():
        need = jnp.broadcast_to(kf - row_scr[1:2, :], (KEY_CHUNK, Q_TILE))
        lower = (lax.broadcasted_iota(I32, (KEY_CHUNK, KEY_CHUNK), 1)
                 <= lax.broadcasted_iota(I32, (KEY_CHUNK, KEY_CHUNK), 0))
        prefix_mat = jnp.where(lower, 1.0, 0.0).astype(BF16)

        def tie_bias(c, seen):
            ks = chunk_start(c)
            sc = score_scr[pl.ds(ks, KEY_CHUNK), :]
            tied = sc == thr_c
            rank = seen + jnp.dot(prefix_mat, jnp.where(tied, 1.0, 0.0).astype(BF16),
                                  preferred_element_type=F32)
            chosen = (sc > thr_c) | (tied & (rank <= need))
            return jnp.where(chosen & (ks + krow <= kq), 0.0, NEG_BIG), rank[KEY_CHUNK - 1:KEY_CHUNK, :]

        return tie_bias

    no_ties_seen = jnp.zeros((1, Q_TILE), F32)

    def pv_pass(bias_of, m, carry0):
        acc_scr[...] = jnp.zeros(acc_scr.shape, F32)

        def pv_chunk(c, carry):
            bias, carry = bias_of(c, carry)
            bias4 = jnp.concatenate([bias] * ATT_REP, axis=1)
            for g in range(ATT_KV_HEADS):
                p = jnp.exp2(s_scr[g, pl.ds(chunk_start(c), KEY_CHUNK), :] + bias4 - m[g])
                acc_scr[g] += jnp.dot(vt_ref[0, c], p.astype(BF16), preferred_element_type=F32)
            return carry

        over_chunk_pairs(pv_chunk, carry0)

    @pl.when(jnp.logical_not(any_tie))
    def _():
        pv_pass(plain_bias, m_top, jnp.int32(0))

    @pl.when(any_tie)
    def _():
        pv_pass(make_tie_bias(), m_top, no_ties_seen)

    def write_output():
        for g in range(ATT_KV_HEADS):
            denom = acc_scr[g, KV_WIDTH:KV_WIDTH + 1, :]
            og_scr[g * HEAD_DIM:(g + 1) * HEAD_DIM, :] = acc_scr[g, g * HEAD_DIM:(g + 1) * HEAD_DIM, :] / denom
        for j in range(ATT_REP):
            gated = og_scr[:, j * Q_TILE:(j + 1) * Q_TILE].T * ga_ref[0, :, j * LANES:(j + 1) * LANES]
            o_ref[0, :, j * LANES:(j + 1) * LANES] = gated.astype(o_ref.dtype)

    d_min = jnp.min(jnp.minimum(acc_scr[0, KV_WIDTH:KV_WIDTH + 1, :], acc_scr[1, KV_WIDTH:KV_WIDTH + 1, :]))
    write_output()

    @pl.when(jnp.logical_not(d_min >= DENOM_FLOOR))
    def _():
        tie_bias = make_tie_bias()

        def mask_chunk(c, carry):
            seen, m_acc = carry
            ks = chunk_start(c)
            bias, seen = tie_bias(c, seen)
            bias_scr[pl.ds(ks, KEY_CHUNK), :] = bias
            bias4 = jnp.concatenate([bias] * ATT_REP, axis=1)
            return seen, tuple(jnp.maximum(m_acc[g], fold(s_scr[g, pl.ds(ks, KEY_CHUNK), :] + bias4, jnp.max))
                               for g in range(ATT_KV_HEADS))

        m_init = tuple(jnp.full((SUBLANES, att_cols), NEG_BIG, F32) for _ in range(ATT_KV_HEADS))
        _, m_acc = lax.fori_loop(0, n_chunks, mask_chunk, (no_ties_seen, m_init))
        m_sel = [jnp.max(m_acc[g], axis=0, keepdims=True) for g in range(ATT_KV_HEADS)]
        pv_pass(lambda c, carry: (bias_scr[pl.ds(chunk_start(c), KEY_CHUNK), :], carry), m_sel, jnp.int32(0))
        write_output()


def _dsa(q_t, qi_t, wi_t, k, vt, ki, gate):
    batch, seq, _ = k.shape
    topk = min(TOPK_MAX, seq // 4)
    att_cols = ATT_REP * Q_TILE
    per_q = lambda a: pl.BlockSpec((1, 1) + a.shape[2:], lambda b, i: (b, i, 0, 0))
    per_b = lambda w: pl.BlockSpec((1, seq, w), lambda b, i: (b, 0, 0))
    vt_spec = pl.BlockSpec((1, seq // KEY_CHUNK, VT_ROWS, KEY_CHUNK), lambda b, i: (b, 0, 0, 0))
    tok_spec = pl.BlockSpec((1, Q_TILE, ATT_WIDTH), lambda b, i: (b, i, 0))
    return pl.pallas_call(
        functools.partial(_dsa_kernel, topk=topk),
        grid=(batch, seq // Q_TILE),
        in_specs=[per_q(q_t), per_q(qi_t), per_q(wi_t), per_b(KV_WIDTH), vt_spec, per_b(LANES), tok_spec],
        out_specs=tok_spec,
        out_shape=jax.ShapeDtypeStruct((batch, seq, ATT_WIDTH), BF16),
        scratch_shapes=[pltpu.VMEM((seq, Q_TILE), F32),
                        pltpu.VMEM((seq, Q_TILE), F32),
                        pltpu.VMEM((SUBLANES, Q_TILE), F32),
                        pltpu.VMEM((ATT_KV_HEADS, seq, att_cols), F32),
                        pltpu.VMEM((ATT_KV_HEADS, VT_ROWS, att_cols), F32),
                        pltpu.VMEM((KV_WIDTH, att_cols), F32)],
        compiler_params=pltpu.CompilerParams(dimension_semantics=("arbitrary", "arbitrary"),
                                             vmem_limit_bytes=VMEM_LIMIT_BYTES),
        name="dsa",
    )(q_t, qi_t, wi_t, k, vt, ki, gate)


def _s5_kernel(u_ref, bm_ref, cm_ref, are_ref, aim_ref, dsk_ref, wg_ref, bg_ref, o_ref,
               st_scr, xb_scr, carry_scr):
    tc = u_ref.shape[0]
    batch, width = carry_scr.shape[0], dsk_ref.shape[1]
    steps = tc // SCAN_SPLIT
    rows = steps * batch

    @pl.when(pl.program_id(0) == 0)
    def _():
        carry_scr[...] = jnp.zeros(carry_scr.shape, F32)

    u = [u_ref[h * steps:(h + 1) * steps].reshape(steps, batch, width).reshape(rows, width)
         for h in range(SCAN_SPLIT)]
    for h in range(SCAN_SPLIT):
        st_scr[h] = jnp.dot(u[h].astype(BF16), bm_ref[...], preferred_element_type=F32)

    parts = []
    for part in range(SSM_LANES // SCAN_LANES):
        re_lo = part * SCAN_LANES
        im_lo = SSM_LANES + re_lo
        parts.append((slice(re_lo, re_lo + SCAN_LANES), slice(im_lo, im_lo + SCAN_LANES)))
    state = [(carry_scr[:, re], carry_scr[:, im]) for re, im in parts]

    for h in range(SCAN_SPLIT):
        for p, (re, im) in enumerate(parts):
            a_re = jnp.broadcast_to(are_ref[:, re], (batch, SCAN_LANES))
            a_im = jnp.broadcast_to(aim_ref[:, re], (batch, SCAN_LANES))
            x_re, x_im = state[p]
            for t in range(steps):
                r = slice(t * batch, (t + 1) * batch)
                x_re, x_im = (a_re * x_re - a_im * x_im + st_scr[h, r, re],
                              a_re * x_im + a_im * x_re + st_scr[h, r, im])
                xb_scr[h, r, re] = x_re.astype(BF16)
                xb_scr[h, r, im] = x_im.astype(BF16)
            state[p] = (x_re, x_im)
        y = jnp.dot(xb_scr[h], cm_ref[...], preferred_element_type=F32) + dsk_ref[...] * u[h]
        y = jax.nn.gelu(y)
        gate = jax.nn.sigmoid(jnp.dot(y.astype(BF16), wg_ref[...], preferred_element_type=F32) + bg_ref[...])
        o_ref[h * steps:(h + 1) * steps] = (y * gate).reshape(steps, batch, width).reshape(steps, batch * width)

    for (re, im), (x_re, x_im) in zip(parts, state):
        carry_scr[:, re] = x_re
        carry_scr[:, im] = x_im


def _s5(u_tm, batch, bmat, cmat, a_re, a_im, dskip, w_glu, b_glu):
    seq = u_tm.shape[0]
    width = u_tm.shape[1] // batch
    tc = TIME_CHUNK
    full = lambda shape: pl.BlockSpec(shape, lambda t: tuple(0 for _ in shape))
    return pl.pallas_call(
        _s5_kernel,
        grid=(seq // tc,),
        in_specs=[pl.BlockSpec((tc, batch * width), lambda t: (t, 0)),
                  full(bmat.shape), full(cmat.shape), full(a_re.shape), full(a_im.shape),
                  full(dskip.shape), full(w_glu.shape), full(b_glu.shape)],
        out_specs=pl.BlockSpec((tc, batch * width), lambda t: (t, 0)),
        out_shape=jax.ShapeDtypeStruct((seq, batch * width), F32),
        scratch_shapes=[pltpu.VMEM((SCAN_SPLIT, tc // SCAN_SPLIT * batch, 2 * SSM_LANES), F32),
                        pltpu.VMEM((SCAN_SPLIT, tc // SCAN_SPLIT * batch, 2 * SSM_LANES), BF16),
                        pltpu.VMEM((batch, 2 * SSM_LANES), F32)],
        compiler_params=pltpu.CompilerParams(dimension_semantics=("arbitrary",),
                                             vmem_limit_bytes=VMEM_LIMIT_BYTES),
        name="s5",
    )(u_tm, bmat, cmat, a_re, a_im, dskip, w_glu, b_glu)


def _s5_params(lam_re, lam_im, log_dt, b_re, b_im, c_re, c_im):
    dt = jnp.exp(log_dt.astype(F32))[:, None]
    mag = jnp.exp(lam_re.astype(F32) * dt)
    ang = lam_im.astype(F32) * dt
    lb_re, lb_im = mag * jnp.cos(ang), mag * jnp.sin(ang)
    den = lam_re * lam_re + lam_im * lam_im
    k_re = ((lb_re - 1.0) * lam_re + lb_im * lam_im) / den
    k_im = (lb_im * lam_re - (lb_re - 1.0) * lam_im) / den
    bb_re = k_re[:, :, None] * b_re - k_im[:, :, None] * b_im
    bb_im = k_re[:, :, None] * b_im + k_im[:, :, None] * b_re
    same_group = (np.arange(SSM_WIDTH)[:, None] // SSM_GROUP) == (np.arange(SSM_LANES)[None, :] // SSM_STATE)

    def drive(bb):
        per_group = jnp.swapaxes(bb, 1, 2).reshape(SSM_WIDTH, SSM_STATE)
        return jnp.where(same_group, jnp.tile(per_group, (1, SSM_GROUPS)), 0.0)

    def read(cc):
        per_group = jnp.swapaxes(cc, 1, 2).reshape(SSM_LANES, SSM_GROUP)
        return jnp.where(same_group.T, jnp.tile(per_group, (1, SSM_GROUPS)), 0.0)

    bmat = jnp.concatenate([drive(bb_re), drive(bb_im)], axis=1)
    cmat = jnp.concatenate([read(c_re.astype(F32)), read(-c_im.astype(F32))], axis=0)
    return (bmat.astype(BF16), cmat.astype(BF16),
            lb_re.reshape(1, SSM_LANES), lb_im.reshape(1, SSM_LANES))


def _combine_kernel(x_ref, ca_ref, os_ref, gs_ref, qm_ref, gm_ref, mem_ref, wkv_ref,
                    wo_ref, lg_ref, lb_ref, out_ref, mk_scr, mv_scr):
    tile = x_ref.shape[1]
    nt_dims = (((1,), (1,)), ((), ()))
    lane = lax.broadcasted_iota(I32, (COMBINE_ROWS, LANES), 1)
    low = lane < HEAD_DIM

    @pl.when(pl.program_id(1) == 0)
    def _():
        mkv = jnp.dot(mem_ref[0].astype(BF16), wkv_ref[...], preferred_element_type=F32)
        mk_scr[...] = mkv[:, :MEM_WIDTH].astype(BF16)
        mv_scr[...] = mkv[:, MEM_WIDTH:].astype(BF16)

    for r0 in range(0, tile, COMBINE_ROWS):
        rows = slice(r0, r0 + COMBINE_ROWS)
        qm = qm_ref[0, rows, :].astype(F32)
        mem_tiles = []
        for t in range(MEM_WIDTH // LANES):
            qt = qm[:, t * LANES:(t + 1) * LANES]
            mk = mk_scr[:, t * LANES:(t + 1) * LANES]
            mv = mv_scr[:, t * LANES:(t + 1) * LANES]
            halves = []
            for first in (True, False):
                qh = jnp.where(low if first else ~low, qt, 0.0).astype(BF16)
                s = lax.dot_general(qh, mk, nt_dims, preferred_element_type=F32)
                p = jnp.exp(s - jnp.max(s, axis=1, keepdims=True))
                o = jnp.dot(p.astype(BF16), mv, preferred_element_type=F32)
                halves.append(o / jnp.sum(p, axis=1, keepdims=True))
            mem_tiles.append(jnp.where(low, halves[0], halves[1]))
        o_mem = jnp.concatenate(mem_tiles, axis=1)

        c_att = ca_ref[0, rows, :]
        c_ssm = (os_ref[rows, :] * gs_ref[0, rows, :]).astype(BF16)
        c_mem = (o_mem * gm_ref[0, rows, :]).astype(BF16)
        a1, a2 = ATT_WIDTH, ATT_WIDTH + SSM_WIDTH
        sub = (jnp.dot(c_att, wo_ref[:a1, :], preferred_element_type=F32)
               + jnp.dot(c_ssm, wo_ref[a1:a2, :], preferred_element_type=F32)
               + jnp.dot(c_mem, wo_ref[a2:, :], preferred_element_type=F32))
        h = DN_ALPHA * x_ref[0, rows, :] + sub
        mu = jnp.mean(h, axis=1, keepdims=True)
        d = h - mu
        var = jnp.mean(d * d, axis=1, keepdims=True)
        out_ref[0, rows, :] = d * lax.rsqrt(var + LN_EPS) * lg_ref[...] + lb_ref[...]


def _combine(x, c_att, o_ssm_tm, g_ssm, qm, g_mem, mem, w_mem_kv_bf16, wo_bf16, ln_g, ln_b):
    batch, seq, _ = x.shape
    tile = COMBINE_TILE
    tok = lambda w: pl.BlockSpec((1, tile, w), lambda b, j: (b, j, 0))
    time_major = pl.BlockSpec((tile, SSM_WIDTH), lambda b, j: (j, b))
    per_b = pl.BlockSpec((1, N_MEM, D_MODEL), lambda b, j: (b, 0, 0))
    const = lambda shape: pl.BlockSpec(shape, lambda b, j: tuple(0 for _ in shape))
    return pl.pallas_call(
        _combine_kernel,
        grid=(batch, seq // tile),
        in_specs=[tok(D_MODEL), tok(ATT_WIDTH), time_major, tok(SSM_WIDTH),
                  tok(MEM_WIDTH), tok(MEM_WIDTH), per_b, const((D_MODEL, 2 * MEM_WIDTH)),
                  const((D_MODEL, D_MODEL)), const((1, D_MODEL)), const((1, D_MODEL))],
        out_specs=tok(D_MODEL),
        out_shape=jax.ShapeDtypeStruct((batch, seq, D_MODEL), x.dtype),
        scratch_shapes=[pltpu.VMEM((N_MEM, MEM_WIDTH), BF16),
                        pltpu.VMEM((N_MEM, MEM_WIDTH), BF16)],
        compiler_params=pltpu.CompilerParams(dimension_semantics=("arbitrary", "arbitrary"),
                                             vmem_limit_bytes=VMEM_LIMIT_BYTES),
        name="combine",
    )(x, c_att, o_ssm_tm, g_ssm, qm, g_mem, mem, w_mem_kv_bf16, wo_bf16, ln_g, ln_b)


def _reordered_w_in(w_in):
    offs = np.cumsum([0, ATT_WIDTH, KV_WIDTH, KV_WIDTH, IDX_WIDTH, IDX_DIM, IDX_HEADS,
                      ATT_WIDTH, SSM_WIDTH, SSM_WIDTH, MEM_WIDTH, MEM_WIDTH])
    part = lambda n: w_in[:, int(offs[n]):int(offs[n + 1])]
    k_idx = part(4)
    w_idx = part(5)
    zeros = jnp.zeros((D_MODEL, LANES - IDX_HEADS), w_in.dtype)
    run = lambda first, last: w_in[:, int(offs[first]):int(offs[last + 1])]
    cols = [_pair_heads(part(0), 1), run(1, 3),
            jnp.tile(k_idx, (1, LANES // IDX_DIM)),
            w_idx, zeros,
            _pair_heads(part(6), 1), run(7, 10)]
    return jnp.concatenate(cols, axis=1).astype(BF16)


def kernel(x, mem, w_in, w_mem_kv, lam_re, lam_im, log_dt, b_re, b_im, c_re, c_im, d_skip, w_glu, b_glu,
           w_out, ln_g, ln_b):
    batch, seq, _ = x.shape
    assert seq % TOKEN_TILE == 0 and seq % COMBINE_TILE == 0 and seq % Q_TILE == 0 and seq % TIME_CHUNK == 0

    w_all = _reordered_w_in(w_in)
    wo = jnp.concatenate([_pair_heads(w_out[:ATT_WIDTH], 0), w_out[ATT_WIDTH:]], axis=0).astype(BF16)
    bmat, cmat, a_re, a_im = _s5_params(lam_re, lam_im, log_dt, b_re, b_im, c_re, c_im)
    tabs = _rope_tables(seq, HEAD_DIM) + _rope_tables(seq, IDX_DIM)

    q_t, k, vt, qi_t, ki, wi_t, g_att, u_tm, g_ssm, qm, g_mem = _inproj(x, w_all, tabs)
    c_att = _dsa(q_t, qi_t, wi_t, k, vt, ki, g_att)

    o_ssm_tm = _s5(u_tm, batch, bmat, cmat, a_re, a_im,
                   d_skip.reshape(1, SSM_WIDTH).astype(F32), w_glu.astype(BF16),
                   b_glu.reshape(1, SSM_WIDTH).astype(F32))

    return _combine(x, c_att, o_ssm_tm, g_ssm, qm, g_mem, mem, w_mem_kv.astype(BF16), wo,
                    ln_g.reshape(1, D_MODEL).astype(F32), ln_b.reshape(1, D_MODEL).astype(F32))
```

```python
import functools
import math

import jax
import jax.numpy as jnp
import numpy as np
from jax import lax
from jax.experimental import pallas as pl
from jax.experimental.pallas import tpu as pltpu

F32 = jnp.float32
BF16 = jnp.bfloat16
I32 = jnp.int32

D_MODEL = 1024
N_MEM = 256
HEAD_DIM = 64
ATT_HEADS = 8
ATT_KV_HEADS = 2
ATT_REP = ATT_HEADS // ATT_KV_HEADS
ATT_WIDTH = ATT_HEADS * HEAD_DIM
KV_WIDTH = ATT_KV_HEADS * HEAD_DIM
IDX_HEADS = 8
IDX_DIM = 32
IDX_WIDTH = IDX_HEADS * IDX_DIM
TOPK_MAX = 256
SSM_WIDTH = D_MODEL // 4
SSM_GROUP = 16
SSM_GROUPS = SSM_WIDTH // SSM_GROUP
SSM_STATE = 64
SSM_LANES = SSM_GROUPS * SSM_STATE
MEM_HEADS = 4
MEM_WIDTH = MEM_HEADS * HEAD_DIM
ROPE_THETA = 500000.0
ROPE_FRAC = 4
LN_EPS = 1e-5
DEPTH = 1
DN_ALPHA = (2.0 * DEPTH) ** 0.25
ATT_SCALE = HEAD_DIM ** -0.5
IDX_SCALE = IDX_HEADS ** -0.5 * IDX_DIM ** -0.5
LOG2_E = math.log2(math.e)

LANES = 128
SUBLANES = 8
BF16_ROWS = 16
VMEM_LIMIT_BYTES = 48 * 1024 * 1024
TOKEN_TILE = 512
COMBINE_TILE = 1024
Q_TILE = 256
KEY_CHUNK = 256
COUNT_ROWS = 64
VT_ROWS = KV_WIDTH + BF16_ROWS
TIME_CHUNK = 128
SCAN_SPLIT = 2
SCAN_LANES = 256
NEG_BIG = -1e30
DENOM_FLOOR = 2.0 ** -100
SEARCH_MAX_PASSES = 24
SEARCH_FIRST_PASSES = 13
SEARCH_GROUP = 2
SEARCH_BISECT_PASSES = 2
SEARCH_CLIP = 0.1

INT_MIN = -2147483648
KEY_POS_INF = 0x7F800000
KEY_NEG_INF = INT_MIN + 0x7FFFFF

_G_Q, _G_K, _G_V, _G_QI, _G_KI, _G_W, _G_GA, _G_U, _G_GS, _G_QM, _G_GM = range(11)
_GROUP_WIDTHS = [ATT_WIDTH, KV_WIDTH, KV_WIDTH, IDX_WIDTH, LANES, LANES, ATT_WIDTH,
                 SSM_WIDTH, SSM_WIDTH, MEM_WIDTH, MEM_WIDTH]
_GROUP_OFFS = [int(v) for v in np.cumsum([0] + _GROUP_WIDTHS)]
IN_COLS = _GROUP_OFFS[-1]


def _pair_heads(w, axis):
    shape = w.shape
    split = shape[:axis] + (ATT_KV_HEADS, ATT_REP, HEAD_DIM) + shape[axis + 1:]
    return jnp.swapaxes(w.reshape(split), axis, axis + 1).reshape(shape)


def _rope_tables(seq, period, dtype=F32):
    r = period // ROPE_FRAC
    half = r // 2
    inv = ROPE_THETA ** (-jnp.arange(0, half, dtype=F32) * 2.0 / r)
    ang = jnp.arange(seq).astype(F32)[:, None] * inv[None, :]
    cos, sin = jnp.cos(ang), jnp.sin(ang)
    ones = jnp.ones((seq, period - r), F32)
    zeros = jnp.zeros((seq, period - r), F32)
    c = jnp.concatenate([cos, cos, ones], axis=1)
    s = jnp.concatenate([-sin, sin, zeros], axis=1)
    reps = LANES // period
    return jnp.tile(c, (1, reps)).astype(dtype), jnp.tile(s, (1, reps)).astype(dtype)


def _rope(z, cos_t, sin_t, period):
    width = z.shape[1]
    half = period // ROPE_FRAC // 2
    reps = width // LANES
    if reps > 1:
        cos_t = jnp.concatenate([cos_t] * reps, axis=1)
        sin_t = jnp.concatenate([sin_t] * reps, axis=1)
    lane = lax.broadcasted_iota(I32, z.shape, 1)
    first = (lane & (period - 1)) < half
    up = pltpu.roll(z, half, 1)
    down = pltpu.roll(z, width - half, 1)
    return z * cos_t + jnp.where(first, down, up) * sin_t


def _inproj_kernel(x_ref, w_ref, ca_ref, sa_ref, ci_ref, si_ref,
                   q_ref, k_ref, vt_ref, qi_ref, ki_ref, wi_ref, ga_ref, u_ref, gs_ref, qm_ref, gm_ref):
    tile = x_ref.shape[1]
    pos0 = pl.multiple_of(pl.program_id(1) * tile, tile)
    xb = x_ref[0].astype(BF16)

    def proj(group, last=None):
        lo, hi = _GROUP_OFFS[group], _GROUP_OFFS[(group if last is None else last) + 1]
        return jnp.dot(xb, w_ref[:, lo:hi], preferred_element_type=F32)

    ca = ca_ref[pl.ds(pos0, tile), :]
    sa = sa_ref[pl.ds(pos0, tile), :]
    ci = ci_ref[pl.ds(pos0, tile), :]
    si = si_ref[pl.ds(pos0, tile), :]

    def store_transposed(ref, z):
        for qb in range(tile // Q_TILE):
            for t in range(z.shape[1] // LANES):
                zt = z[qb * Q_TILE:(qb + 1) * Q_TILE, t * LANES:(t + 1) * LANES].T
                ref[0, qb, :, t * Q_TILE:(t + 1) * Q_TILE] = zt.astype(ref.dtype)

    store_transposed(q_ref, _rope(proj(_G_Q), ca, sa, HEAD_DIM) * (ATT_SCALE * LOG2_E))
    kv = proj(_G_K, _G_V)
    k_ref[0] = _rope(kv[:, :KV_WIDTH], ca, sa, HEAD_DIM).astype(BF16)
    v = kv[:, KV_WIDTH:]
    for c in range(tile // KEY_CHUNK):
        vt_ref[0, c, :KV_WIDTH, :] = v[c * KEY_CHUNK:(c + 1) * KEY_CHUNK, :].T.astype(BF16)
        vt_ref[0, c, KV_WIDTH:, :] = jnp.ones((BF16_ROWS, KEY_CHUNK), BF16)
    store_transposed(qi_ref, _rope(proj(_G_QI), ci, si, IDX_DIM))
    kiw = proj(_G_KI, _G_W)
    ki_ref[0] = _rope(kiw[:, :LANES], ci, si, IDX_DIM).astype(BF16)
    wi = kiw[:, LANES:] * IDX_SCALE
    for qb in range(tile // Q_TILE):
        wi_ref[0, qb] = wi[qb * Q_TILE:(qb + 1) * Q_TILE, :].T[:IDX_HEADS, :]
    ga_ref[0] = jax.nn.silu(proj(_G_GA))
    u_ref[...] = proj(_G_U)
    gs_ref[0] = jax.nn.silu(proj(_G_GS))
    qm_ref[0] = (proj(_G_QM) * ATT_SCALE).astype(BF16)
    gm_ref[0] = jax.nn.silu(proj(_G_GM))


def _inproj(x, w_bf16, tabs):
    batch, seq, _ = x.shape
    tile = TOKEN_TILE
    widths = _GROUP_WIDTHS
    dtypes = [BF16, BF16, BF16, BF16, BF16, F32, F32, F32, F32, BF16, F32]
    tab_spec = pl.BlockSpec((seq, LANES), lambda b, j: (0, 0))
    out_specs = [pl.BlockSpec((1, tile, w), lambda b, j: (b, j, 0)) for w in widths]
    out_shape = [jax.ShapeDtypeStruct((batch, seq, w), d) for w, d in zip(widths, dtypes)]
    out_specs[_G_V] = pl.BlockSpec((1, tile // KEY_CHUNK, VT_ROWS, KEY_CHUNK), lambda b, j: (b, j, 0, 0))
    out_shape[_G_V] = jax.ShapeDtypeStruct((batch, seq // KEY_CHUNK, VT_ROWS, KEY_CHUNK), BF16)
    for group, rows, dtype in ((_G_Q, LANES, BF16), (_G_QI, LANES, BF16), (_G_W, IDX_HEADS, F32)):
        cols = Q_TILE * max(widths[group] // LANES, 1)
        out_specs[group] = pl.BlockSpec((1, tile // Q_TILE, rows, cols), lambda b, j: (b, j, 0, 0))
        out_shape[group] = jax.ShapeDtypeStruct((batch, seq // Q_TILE, rows, cols), dtype)
    out_specs[_G_U] = pl.BlockSpec((tile, SSM_WIDTH), lambda b, j: (j, b))
    out_shape[_G_U] = jax.ShapeDtypeStruct((seq, batch * SSM_WIDTH), F32)
    return pl.pallas_call(
        _inproj_kernel,
        grid=(batch, seq // tile),
        in_specs=[pl.BlockSpec((1, tile, D_MODEL), lambda b, j: (b, j, 0)),
                  pl.BlockSpec((D_MODEL, IN_COLS), lambda b, j: (0, 0)),
                  tab_spec, tab_spec, tab_spec, tab_spec],
        out_specs=out_specs,
        out_shape=out_shape,
        compiler_params=pltpu.CompilerParams(dimension_semantics=("arbitrary", "arbitrary"),
                                             vmem_limit_bytes=VMEM_LIMIT_BYTES),
        name="inproj",
    )(x, w_bf16, *tabs)


def _key_to_float(key):
    bits = jnp.where(key >= 0, key, key ^ 0x7FFFFFFF)
    val = lax.bitcast_convert_type(bits, F32)
    val = jnp.where(key <= KEY_NEG_INF, -jnp.inf, val)
    return jnp.where(key >= KEY_POS_INF, jnp.inf, val)


def _dsa_kernel(q_ref, qi_ref, wi_ref, k_ref, vt_ref, ki_ref, ga_ref, o_ref,
                score_scr, bias_scr, row_scr, s_scr, acc_scr, og_scr, *, topk):
    i = pl.program_id(1)
    n_chunks = ((i + 1) * Q_TILE + KEY_CHUNK - 1) // KEY_CHUNK
    q0 = i * Q_TILE
    kf = float(topk)
    att_cols = ATT_REP * Q_TILE
    groups = KEY_CHUNK // SUBLANES

    def chunk_start(c):
        return pl.multiple_of(c * KEY_CHUNK, KEY_CHUNK)

    def over_chunk_pairs(body, init):
        def pair(c2, carry):
            return body(2 * c2 + 1, body(2 * c2, carry))
        carry = lax.fori_loop(0, n_chunks // 2, pair, init)
        return lax.cond(n_chunks % 2 == 1, lambda cr: body(n_chunks - 1, cr), lambda cr: cr, carry)

    def fold(v, op):
        return op(v.reshape(groups, SUBLANES, v.shape[-1]), axis=0)

    def only_rows(x, lo, n):
        parts = [jnp.zeros((lo, x.shape[1]), x.dtype)] if lo else []
        parts.append(x[lo:lo + n])
        if lo + n < x.shape[0]:
            parts.append(jnp.zeros((x.shape[0] - lo - n, x.shape[1]), x.dtype))
        return jnp.concatenate(parts, axis=0)

    qi_t = qi_ref[0, 0]
    per_tile = LANES // IDX_DIM

    def head_cols(h):
        t = h // per_tile
        return only_rows(qi_t[:, t * Q_TILE:(t + 1) * Q_TILE], (h % per_tile) * IDX_DIM, IDX_DIM)

    rhs_pairs = [jnp.concatenate([head_cols(2 * p), head_cols(2 * p + 1)], axis=1)
                 for p in range(IDX_HEADS // 2)]
    w_rows = [wi_ref[0, 0, h:h + 1, :] for h in range(IDX_HEADS)]
    q_t = q_ref[0, 0]
    qg_t = [only_rows(q_t, g * HEAD_DIM, HEAD_DIM) for g in range(ATT_KV_HEADS)]
    krow = lax.broadcasted_iota(I32, (KEY_CHUNK, Q_TILE), 0)
    kq = q0 + lax.broadcasted_iota(I32, (KEY_CHUNK, Q_TILE), 1)

    def matmul_chunk(c, stats):
        mx_a, mn_a, ge_a, gt_a, top_a = stats
        ks = chunk_start(c)
        kc = ki_ref[0, pl.ds(ks, KEY_CHUNK), :]
        acc = jnp.zeros((KEY_CHUNK, Q_TILE), F32)
        for p in range(IDX_HEADS // 2):
            s2 = jnp.dot(kc, rhs_pairs[p], preferred_element_type=F32)
            acc = acc + w_rows[2 * p] * jnp.maximum(s2[:, :Q_TILE], 0.0)
            acc = acc + w_rows[2 * p + 1] * jnp.maximum(s2[:, Q_TILE:], 0.0)
        causal = ks + krow <= kq
        sc = jnp.where(causal, acc, -jnp.inf)
        score_scr[pl.ds(ks, KEY_CHUNK), :] = sc
        kk = k_ref[0, pl.ds(ks, KEY_CHUNK), :]
        tops = []
        for g in range(ATT_KV_HEADS):
            s = jnp.dot(kk, qg_t[g], preferred_element_type=F32)
            s_scr[g, pl.ds(ks, KEY_CHUNK), :] = s
            tops.append(jnp.maximum(top_a[g], fold(s, jnp.max)))
        return (jnp.maximum(mx_a, fold(sc, jnp.max)),
                jnp.minimum(mn_a, fold(jnp.where(causal, acc, jnp.inf), jnp.min)),
                ge_a + fold(jnp.where(sc >= 0.0, 1.0, 0.0), jnp.sum),
                gt_a + fold(jnp.where(sc > 0.0, 1.0, 0.0), jnp.sum),
                tuple(tops))

    stat0 = lambda v: jnp.full((SUBLANES, Q_TILE), v, F32)
    top0 = tuple(jnp.full((SUBLANES, att_cols), NEG_BIG, F32) for _ in range(ATT_KV_HEADS))
    mx_a, mn_a, ge_a, gt_a, top_a = over_chunk_pairs(
        matmul_chunk, (stat0(-jnp.inf), stat0(jnp.inf), stat0(0.0), stat0(0.0), top0))
    m_top = [jnp.max(top_a[g], axis=0, keepdims=True) for g in range(ATT_KV_HEADS)]
    s_max = jnp.max(mx_a, axis=0, keepdims=True)
    s_min = jnp.min(mn_a, axis=0, keepdims=True)
    n_ge0 = jnp.sum(ge_a, axis=0, keepdims=True)
    n_gt0 = jnp.sum(gt_a, axis=0, keepdims=True)
    n_causal = (q0 + 1 + lax.broadcasted_iota(I32, (1, Q_TILE), 1)).astype(F32)

    def count(pred):
        def body(c, acc):
            ks = chunk_start(c)
            for part in range(KEY_CHUNK // LANES):
                hit = jnp.where(pred(score_scr[pl.ds(ks + part * LANES, LANES), :]), 1.0, 0.0)
                acc = acc + (hit[:COUNT_ROWS] + hit[COUNT_ROWS:])
            return acc
        acc = over_chunk_pairs(body, jnp.zeros((COUNT_ROWS, Q_TILE), F32))
        rows = COUNT_ROWS
        while rows > SUBLANES:
            rows //= 2
            acc = acc[:rows] + acc[rows:]
        return jnp.sum(acc, axis=0, keepdims=True)

    def count_ge(value):
        vb = jnp.broadcast_to(value, (LANES, Q_TILE))
        return count(lambda sc: sc >= vb)

    zero_thr = (n_gt0 < kf) & (n_ge0 >= kf)
    keep_all = n_causal <= kf
    positive = n_gt0 >= kf
    settled = zero_thr | keep_all
    lo0 = jnp.where(settled, 0.0, jnp.where(positive, 0.0, s_min))
    hi0 = jnp.where(settled, 0.0, jnp.where(positive, s_max, 0.0))
    c_lo0 = jnp.where(positive, n_gt0, n_causal)
    c_hi0 = jnp.where(positive, 0.0, n_ge0)
    thr0 = jnp.where(keep_all, -jnp.inf, 0.0)
    done0 = jnp.where(settled, 1.0, 0.0)
    zero_ties = jnp.max(jnp.where(zero_thr & (n_ge0 > kf) & jnp.logical_not(keep_all), 1.0, 0.0)) > 0.0

    def search_pass(bisect, rows):
        lo, hi, c_lo, c_hi, thr, done = rows
        if bisect:
            frac = 0.5
        else:
            frac = jnp.clip((kf - c_hi + 0.5) / (c_lo - c_hi + 1.0), SEARCH_CLIP, 1.0 - SEARCH_CLIP)
        x = hi - (hi - lo) * frac
        c = count_ge(x)
        live = done == 0.0
        above = live & (c >= kf)
        below = live & (c < kf)
        hit = live & (c == kf)
        return (jnp.where(above, x, lo), jnp.where(below, x, hi),
                jnp.where(above, c, c_lo), jnp.where(below, c, c_hi),
                jnp.where(hit, x, thr), jnp.where(hit, 1.0, done))

    def search_cond(state):
        return (state[0] < SEARCH_MAX_PASSES) & (state[1] > 0.0)

    def search_body(state):
        it, _, rows = state
        for _ in range(SEARCH_GROUP):
            rows = search_pass(False, rows)
        return it + SEARCH_GROUP, jnp.sum(1.0 - rows[5]), rows

    def first_passes(rows):
        for p in range(SEARCH_FIRST_PASSES):
            rows = search_pass(p < SEARCH_BISECT_PASSES, rows)
        return rows

    rows0 = (lo0, hi0, c_lo0, c_hi0, thr0, done0)
    rows1 = lax.cond((i + 1) * Q_TILE > topk, first_passes, lambda rows: rows, rows0)
    state = lax.while_loop(search_cond, search_body,
                           (jnp.int32(SEARCH_FIRST_PASSES), jnp.sum(1.0 - rows1[5]), rows1))
    pending, thr_s, done_s = state[1], state[2][4], state[2][5]
    row_scr[0:1, :] = thr_s
    row_scr[1:2, :] = jnp.where(zero_thr, n_gt0, 0.0)
    row_scr[2:3, :] = jnp.where(zero_thr, n_ge0, kf)

    @pl.when(pending > 0.0)
    def _():
        def count_ge_key(key):
            return count_ge(_key_to_float(key))

        key0 = jnp.where(count_ge_key(jnp.zeros((1, Q_TILE), I32)) >= kf, 0, INT_MIN).astype(I32)

        def bit_step(b, key):
            cand = key | jnp.left_shift(jnp.int32(1), 30 - b)
            return jnp.where(count_ge_key(cand) >= kf, cand, key)

        thr_f = _key_to_float(lax.fori_loop(0, 31, bit_step, key0))
        thr_fb = jnp.broadcast_to(thr_f, (LANES, Q_TILE))
        open_row = done_s == 0.0
        row_scr[0:1, :] = jnp.where(open_row, thr_f, thr_s)
        row_scr[1:2, :] = jnp.where(open_row, count(lambda sc: sc > thr_fb), row_scr[1:2, :])
        row_scr[2:3, :] = jnp.where(open_row, count(lambda sc: sc >= thr_fb), row_scr[2:3, :])

    thr = row_scr[0:1, :]
    thr_c = jnp.broadcast_to(thr, (KEY_CHUNK, Q_TILE))
    any_tie = zero_ties | (pending > 0.0)

    def plain_bias(c, carry):
        ks = chunk_start(c)
        keep = (score_scr[pl.ds(ks, KEY_CHUNK), :] >= thr_c) & (ks + krow <= kq)
        return jnp.where(keep, 0.0, NEG_BIG), carry

    def make_tie_bias():
        need = jnp.broadcast_to(kf - row_scr[1:2, :], (KEY_CHUNK, Q_TILE))
        lower = (lax.broadcasted_iota(I32, (KEY_CHUNK, KEY_CHUNK), 1)
                 <= lax.broadcasted_iota(I32, (KEY_CHUNK, KEY_CHUNK), 0))
        prefix_mat = jnp.where(lower, 1.0, 0.0).astype(BF16)

        def tie_bias(c, seen):
            ks = chunk_start(c)
            sc = score_scr[pl.ds(ks, KEY_CHUNK), :]
            tied = sc == thr_c
            rank = seen + jnp.dot(prefix_mat, jnp.where(tied, 1.0, 0.0).astype(BF16),
                                  preferred_element_type=F32)
            chosen = (sc > thr_c) | (tied & (rank <= need))
            return jnp.where(chosen & (ks + krow <= kq), 0.0, NEG_BIG), rank[KEY_CHUNK - 1:KEY_CHUNK, :]

        return tie_bias

    no_ties_seen = jnp.zeros((1, Q_TILE), F32)

    def pv_pass(bias_of, m, carry0):
        def pv_chunk(c, carry, first=False):
            bias, carry = bias_of(c, carry)
            bias4 = jnp.concatenate([bias] * ATT_REP, axis=1)
            for g in range(ATT_KV_HEADS):
                p = jnp.exp2(s_scr[g, pl.ds(chunk_start(c), KEY_CHUNK), :] + bias4 - m[g])
                part = jnp.dot(vt_ref[0, c], p.astype(BF16), preferred_element_type=F32)
                acc_scr[g] = part if first else acc_scr[g] + part
            return carry

        carry = pv_chunk(jnp.int32(0), carry0, first=True)
        rest = n_chunks - 1
        carry = lax.fori_loop(0, rest // 2, lambda c2, cr: pv_chunk(2 * c2 + 2, pv_chunk(2 * c2 + 1, cr)), carry)
        lax.cond(rest % 2 == 1, lambda cr: pv_chunk(n_chunks - 1, cr), lambda cr: cr, carry)

    @pl.when(jnp.logical_not(any_tie))
    def _():
        pv_pass(plain_bias, m_top, jnp.int32(0))

    @pl.when(any_tie)
    def _():
        pv_pass(make_tie_bias(), m_top, no_ties_seen)

    def write_output():
        for g in range(ATT_KV_HEADS):
            denom = acc_scr[g, KV_WIDTH:KV_WIDTH + 1, :]
            og_scr[g * HEAD_DIM:(g + 1) * HEAD_DIM, :] = acc_scr[g, g * HEAD_DIM:(g + 1) * HEAD_DIM, :] / denom
        for j in range(ATT_REP):
            gated = og_scr[:, j * Q_TILE:(j + 1) * Q_TILE].T * ga_ref[0, :, j * LANES:(j + 1) * LANES]
            o_ref[0, :, j * LANES:(j + 1) * LANES] = gated.astype(o_ref.dtype)

    d_min = jnp.min(jnp.minimum(acc_scr[0, KV_WIDTH:KV_WIDTH + 1, :], acc_scr[1, KV_WIDTH:KV_WIDTH + 1, :]))
    write_output()

    @pl.when(jnp.logical_not(d_min >= DENOM_FLOOR))
    def _():
        tie_bias = make_tie_bias()

        def mask_chunk(c, carry):
            seen, m_acc = carry
            ks = chunk_start(c)
            bias, seen = tie_bias(c, seen)
            bias_scr[pl.ds(ks, KEY_CHUNK), :] = bias
            bias4 = jnp.concatenate([bias] * ATT_REP, axis=1)
            return seen, tuple(jnp.maximum(m_acc[g], fold(s_scr[g, pl.ds(ks, KEY_CHUNK), :] + bias4, jnp.max))
                               for g in range(ATT_KV_HEADS))

        m_init = tuple(jnp.full((SUBLANES, att_cols), NEG_BIG, F32) for _ in range(ATT_KV_HEADS))
        _, m_acc = lax.fori_loop(0, n_chunks, mask_chunk, (no_ties_seen, m_init))
        m_sel = [jnp.max(m_acc[g], axis=0, keepdims=True) for g in range(ATT_KV_HEADS)]
        pv_pass(lambda c, carry: (bias_scr[pl.ds(chunk_start(c), KEY_CHUNK), :], carry), m_sel, jnp.int32(0))
        write_output()


def _dsa(q_t, qi_t, wi_t, k, vt, ki, gate):
    batch, seq, _ = k.shape
    topk = min(TOPK_MAX, seq // 4)
    att_cols = ATT_REP * Q_TILE
    per_q = lambda a: pl.BlockSpec((1, 1) + a.shape[2:], lambda b, i: (b, i, 0, 0))
    per_b = lambda w: pl.BlockSpec((1, seq, w), lambda b, i: (b, 0, 0))
    vt_spec = pl.BlockSpec((1, seq // KEY_CHUNK, VT_ROWS, KEY_CHUNK), lambda b, i: (b, 0, 0, 0))
    tok_spec = pl.BlockSpec((1, Q_TILE, ATT_WIDTH), lambda b, i: (b, i, 0))
    return pl.pallas_call(
        functools.partial(_dsa_kernel, topk=topk),
        grid=(batch, seq // Q_TILE),
        in_specs=[per_q(q_t), per_q(qi_t), per_q(wi_t), per_b(KV_WIDTH), vt_spec, per_b(LANES), tok_spec],
        out_specs=tok_spec,
        out_shape=jax.ShapeDtypeStruct((batch, seq, ATT_WIDTH), BF16),
        scratch_shapes=[pltpu.VMEM((seq, Q_TILE), F32),
                        pltpu.VMEM((seq, Q_TILE), F32),
                        pltpu.VMEM((SUBLANES, Q_TILE), F32),
                        pltpu.VMEM((ATT_KV_HEADS, seq, att_cols), F32),
                        pltpu.VMEM((ATT_KV_HEADS, VT_ROWS, att_cols), F32),
                        pltpu.VMEM((KV_WIDTH, att_cols), F32)],
        compiler_params=pltpu.CompilerParams(dimension_semantics=("arbitrary", "arbitrary"),
                                             vmem_limit_bytes=VMEM_LIMIT_BYTES),
        name="dsa",
    )(q_t, qi_t, wi_t, k, vt, ki, gate)


def _s5_kernel(u_ref, bm_ref, cm_ref, are_ref, aim_ref, dsk_ref, wg_ref, bg_ref, o_ref,
               st_scr, xb_scr, carry_scr):
    tc = u_ref.shape[0]
    batch, width = carry_scr.shape[0], dsk_ref.shape[1]
    steps = tc // SCAN_SPLIT
    rows = steps * batch

    @pl.when(pl.program_id(0) == 0)
    def _():
        carry_scr[...] = jnp.zeros(carry_scr.shape, F32)

    u = [u_ref[h * steps:(h + 1) * steps].reshape(steps, batch, width).reshape(rows, width)
         for h in range(SCAN_SPLIT)]
    for h in range(SCAN_SPLIT):
        st_scr[h] = jnp.dot(u[h].astype(BF16), bm_ref[...], preferred_element_type=F32)

    parts = []
    for part in range(SSM_LANES // SCAN_LANES):
        re_lo = part * SCAN_LANES
        im_lo = SSM_LANES + re_lo
        parts.append((slice(re_lo, re_lo + SCAN_LANES), slice(im_lo, im_lo + SCAN_LANES)))
    state = [(carry_scr[:, re], carry_scr[:, im]) for re, im in parts]

    for h in range(SCAN_SPLIT):
        for p, (re, im) in enumerate(parts):
            a_re = jnp.broadcast_to(are_ref[:, re], (batch, SCAN_LANES))
            a_im = jnp.broadcast_to(aim_ref[:, re], (batch, SCAN_LANES))
            x_re, x_im = state[p]
            for t in range(steps):
                r = slice(t * batch, (t + 1) * batch)
                x_re, x_im = (a_re * x_re - a_im * x_im + st_scr[h, r, re],
                              a_re * x_im + a_im * x_re + st_scr[h, r, im])
                xb_scr[h, r, re] = x_re.astype(BF16)
                xb_scr[h, r, im] = x_im.astype(BF16)
            state[p] = (x_re, x_im)
        y = jnp.dot(xb_scr[h], cm_ref[...], preferred_element_type=F32) + dsk_ref[...] * u[h]
        y = jax.nn.gelu(y)
        gate = jax.nn.sigmoid(jnp.dot(y.astype(BF16), wg_ref[...], preferred_element_type=F32) + bg_ref[...])
        o_ref[h * steps:(h + 1) * steps] = (y * gate).reshape(steps, batch, width).reshape(steps, batch * width)

    for (re, im), (x_re, x_im) in zip(parts, state):
        carry_scr[:, re] = x_re
        carry_scr[:, im] = x_im


def _s5(u_tm, batch, bmat, cmat, a_re, a_im, dskip, w_glu, b_glu):
    seq = u_tm.shape[0]
    width = u_tm.shape[1] // batch
    tc = TIME_CHUNK
    full = lambda shape: pl.BlockSpec(shape, lambda t: tuple(0 for _ in shape))
    return pl.pallas_call(
        _s5_kernel,
        grid=(seq // tc,),
        in_specs=[pl.BlockSpec((tc, batch * width), lambda t: (t, 0)),
                  full(bmat.shape), full(cmat.shape), full(a_re.shape), full(a_im.shape),
                  full(dskip.shape), full(w_glu.shape), full(b_glu.shape)],
        out_specs=pl.BlockSpec((tc, batch * width), lambda t: (t, 0)),
        out_shape=jax.ShapeDtypeStruct((seq, batch * width), F32),
        scratch_shapes=[pltpu.VMEM((SCAN_SPLIT, tc // SCAN_SPLIT * batch, 2 * SSM_LANES), F32),
                        pltpu.VMEM((SCAN_SPLIT, tc // SCAN_SPLIT * batch, 2 * SSM_LANES), BF16),
                        pltpu.VMEM((batch, 2 * SSM_LANES), F32)],
        compiler_params=pltpu.CompilerParams(dimension_semantics=("arbitrary",),
                                             vmem_limit_bytes=VMEM_LIMIT_BYTES),
        name="s5",
    )(u_tm, bmat, cmat, a_re, a_im, dskip, w_glu, b_glu)


def _s5_params(lam_re, lam_im, log_dt, b_re, b_im, c_re, c_im):
    dt = jnp.exp(log_dt.astype(F32))[:, None]
    mag = jnp.exp(lam_re.astype(F32) * dt)
    ang = lam_im.astype(F32) * dt
    lb_re, lb_im = mag * jnp.cos(ang), mag * jnp.sin(ang)
    den = lam_re * lam_re + lam_im * lam_im
    k_re = ((lb_re - 1.0) * lam_re + lb_im * lam_im) / den
    k_im = (lb_im * lam_re - (lb_re - 1.0) * lam_im) / den
    bb_re = k_re[:, :, None] * b_re - k_im[:, :, None] * b_im
    bb_im = k_re[:, :, None] * b_im + k_im[:, :, None] * b_re
    same_group = (np.arange(SSM_WIDTH)[:, None] // SSM_GROUP) == (np.arange(SSM_LANES)[None, :] // SSM_STATE)

    def drive(bb):
        per_group = jnp.swapaxes(bb, 1, 2).reshape(SSM_WIDTH, SSM_STATE)
        return jnp.where(same_group, jnp.tile(per_group, (1, SSM_GROUPS)), 0.0)

    def read(cc):
        per_group = jnp.swapaxes(cc, 1, 2).reshape(SSM_LANES, SSM_GROUP)
        return jnp.where(same_group.T, jnp.tile(per_group, (1, SSM_GROUPS)), 0.0)

    bmat = jnp.concatenate([drive(bb_re), drive(bb_im)], axis=1)
    cmat = jnp.concatenate([read(c_re.astype(F32)), read(-c_im.astype(F32))], axis=0)
    return (bmat.astype(BF16), cmat.astype(BF16),
            lb_re.reshape(1, SSM_LANES), lb_im.reshape(1, SSM_LANES))


def _combine_kernel(x_ref, ca_ref, os_ref, gs_ref, qm_ref, gm_ref, mem_ref, wkv_ref,
                    wo_ref, lg_ref, lb_ref, out_ref, mk_scr, mv_scr):
    tile = x_ref.shape[1]
    nt_dims = (((1,), (1,)), ((), ()))
    lane = lax.broadcasted_iota(I32, (tile, LANES), 1)
    low = lane < HEAD_DIM

    @pl.when(pl.program_id(1) == 0)
    def _():
        mkv = jnp.dot(mem_ref[0].astype(BF16), wkv_ref[...], preferred_element_type=F32)
        mk_scr[...] = mkv[:, :MEM_WIDTH].astype(BF16)
        mv_scr[...] = mkv[:, MEM_WIDTH:].astype(BF16)

    qm = qm_ref[0].astype(F32)
    mem_tiles = []
    for t in range(MEM_WIDTH // LANES):
        qt = qm[:, t * LANES:(t + 1) * LANES]
        mk = mk_scr[:, t * LANES:(t + 1) * LANES]
        mv = mv_scr[:, t * LANES:(t + 1) * LANES]
        halves = []
        for first in (True, False):
            qh = jnp.where(low if first else ~low, qt, 0.0).astype(BF16)
            s = lax.dot_general(qh, mk, nt_dims, preferred_element_type=F32)
            p = jnp.exp(s - jnp.max(s, axis=1, keepdims=True))
            o = jnp.dot(p.astype(BF16), mv, preferred_element_type=F32)
            halves.append(o / jnp.sum(p, axis=1, keepdims=True))
        mem_tiles.append(jnp.where(low, halves[0], halves[1]))
    o_mem = jnp.concatenate(mem_tiles, axis=1)

    c_att = ca_ref[0]
    c_ssm = (os_ref[...] * gs_ref[0]).astype(BF16)
    c_mem = (o_mem * gm_ref[0]).astype(BF16)
    a1, a2 = ATT_WIDTH, ATT_WIDTH + SSM_WIDTH
    sub = (jnp.dot(c_att, wo_ref[:a1, :], preferred_element_type=F32)
           + jnp.dot(c_ssm, wo_ref[a1:a2, :], preferred_element_type=F32)
           + jnp.dot(c_mem, wo_ref[a2:, :], preferred_element_type=F32))
    h = DN_ALPHA * x_ref[0] + sub
    mu = jnp.mean(h, axis=1, keepdims=True)
    d = h - mu
    var = jnp.mean(d * d, axis=1, keepdims=True)
    out_ref[0] = d * lax.rsqrt(var + LN_EPS) * lg_ref[...] + lb_ref[...]


def _combine(x, c_att, o_ssm_tm, g_ssm, qm, g_mem, mem, w_mem_kv_bf16, wo_bf16, ln_g, ln_b):
    batch, seq, _ = x.shape
    tile = COMBINE_TILE
    tok = lambda w: pl.BlockSpec((1, tile, w), lambda b, j: (b, j, 0))
    time_major = pl.BlockSpec((tile, SSM_WIDTH), lambda b, j: (j, b))
    per_b = pl.BlockSpec((1, N_MEM, D_MODEL), lambda b, j: (b, 0, 0))
    const = lambda shape: pl.BlockSpec(shape, lambda b, j: tuple(0 for _ in shape))
    return pl.pallas_call(
        _combine_kernel,
        grid=(batch, seq // tile),
        in_specs=[tok(D_MODEL), tok(ATT_WIDTH), time_major, tok(SSM_WIDTH),
                  tok(MEM_WIDTH), tok(MEM_WIDTH), per_b, const((D_MODEL, 2 * MEM_WIDTH)),
                  const((D_MODEL, D_MODEL)), const((1, D_MODEL)), const((1, D_MODEL))],
        out_specs=tok(D_MODEL),
        out_shape=jax.ShapeDtypeStruct((batch, seq, D_MODEL), x.dtype),
        scratch_shapes=[pltpu.VMEM((N_MEM, MEM_WIDTH), BF16),
                        pltpu.VMEM((N_MEM, MEM_WIDTH), BF16)],
        compiler_params=pltpu.CompilerParams(dimension_semantics=("arbitrary", "arbitrary"),
                                             vmem_limit_bytes=VMEM_LIMIT_BYTES),
        name="combine",
    )(x, c_att, o_ssm_tm, g_ssm, qm, g_mem, mem, w_mem_kv_bf16, wo_bf16, ln_g, ln_b)


def _reordered_w_in(w_in):
    offs = np.cumsum([0, ATT_WIDTH, KV_WIDTH, KV_WIDTH, IDX_WIDTH, IDX_DIM, IDX_HEADS,
                      ATT_WIDTH, SSM_WIDTH, SSM_WIDTH, MEM_WIDTH, MEM_WIDTH])
    part = lambda n: w_in[:, int(offs[n]):int(offs[n + 1])]
    k_idx = part(4)
    w_idx = part(5)
    zeros = jnp.zeros((D_MODEL, LANES - IDX_HEADS), w_in.dtype)
    run = lambda first, last: w_in[:, int(offs[first]):int(offs[last + 1])]
    cols = [_pair_heads(part(0), 1), run(1, 3),
            jnp.tile(k_idx, (1, LANES // IDX_DIM)),
            w_idx, zeros,
            _pair_heads(part(6), 1), run(7, 10)]
    return jnp.concatenate(cols, axis=1).astype(BF16)


def kernel(x, mem, w_in, w_mem_kv, lam_re, lam_im, log_dt, b_re, b_im, c_re, c_im, d_skip, w_glu, b_glu,
           w_out, ln_g, ln_b):
    batch, seq, _ = x.shape
    assert seq % TOKEN_TILE == 0 and seq % COMBINE_TILE == 0 and seq % Q_TILE == 0 and seq % TIME_CHUNK == 0

    w_all = _reordered_w_in(w_in)
    wo = jnp.concatenate([_pair_heads(w_out[:ATT_WIDTH], 0), w_out[ATT_WIDTH:]], axis=0).astype(BF16)
    bmat, cmat, a_re, a_im = _s5_params(lam_re, lam_im, log_dt, b_re, b_im, c_re, c_im)
    tabs = _rope_tables(seq, HEAD_DIM) + _rope_tables(seq, IDX_DIM)

    q_t, k, vt, qi_t, ki, wi_t, g_att, u_tm, g_ssm, qm, g_mem = _inproj(x, w_all, tabs)
    c_att = _dsa(q_t, qi_t, wi_t, k, vt, ki, g_att)

    o_ssm_tm = _s5(u_tm, batch, bmat, cmat, a_re, a_im,
                   d_skip.reshape(1, SSM_WIDTH).astype(F32), w_glu.astype(BF16),
                   b_glu.reshape(1, SSM_WIDTH).astype(F32))

    return _combine(x, c_att, o_ssm_tm, g_ssm, qm, g_mem, mem, w_mem_kv.astype(BF16), wo,
                    ln_g.reshape(1, D_MODEL).astype(F32), ln_b.reshape(1, D_MODEL).astype(F32))
```
